```python
import math
import jax, jax.numpy as jnp
from jax import lax
import numpy as np

D_MODEL = 2048
BATCH = 1
SEQ = 8192
DEPTH = 1

D_MIX = D_MODEL
RET_HEADS = 4
RET_HEAD_DIM = D_MIX // 2 // RET_HEADS
RET_WIDTH = RET_HEADS * RET_HEAD_DIM
RET_CHUNK = 128
ROPE_THETA = 10000.0
SSM_WIDTH = D_MIX - RET_WIDTH
SSM_HEAD_DIM = 64
SSM_HEADS = SSM_WIDTH // SSM_HEAD_DIM
SSM_GROUPS = 2
SSM_STATE = 128
SSM_CONV = 4
SSM_CHUNK = 128
SSM_CONV_DIM = SSM_WIDTH + 2 * SSM_GROUPS * SSM_STATE
IN_PROJ_DIM = 4 * RET_WIDTH + SSM_WIDTH + SSM_CONV_DIM + SSM_HEADS
N_EXPERT_GROUPS = 4
EXPERTS_PER_GROUP = 8
N_EXPERTS = N_EXPERT_GROUPS * EXPERTS_PER_GROUP
TOP_K = 2
D_EXPERT = D_MODEL // 4
EPS = 1e-6

kernel_name = 'hymba_retention_ssd_hmoe_block'


def rmsnorm(x, w):
    xf = x.astype(jnp.float32)
    y = xf * lax.rsqrt(jnp.mean(xf * xf, axis=-1, keepdims=True) + EPS)
    return (y * w.astype(jnp.float32)).astype(x.dtype)


def rope(x, positions):
    half = x.shape[-1] // 2
    inv = ROPE_THETA ** (-jnp.arange(half, dtype=jnp.float32) / half)
    ang = positions.astype(jnp.float32)[..., None] * inv
    cos = jnp.cos(ang)[:, :, None, :]
    sin = jnp.sin(ang)[:, :, None, :]
    x1, x2 = x[..., :half], x[..., half:]
    return jnp.concatenate([x1 * cos - x2 * sin, x2 * cos + x1 * sin], axis=-1)


def retention(q, k, v, positions):
    Bsz, T, H, Dh = q.shape
    Lc = RET_CHUNK
    C = T // Lc
    q = rope(q, positions)
    k = rope(k, positions) * (Dh ** -0.5)
    log_gamma = jnp.log1p(-(2.0 ** (-5.0 - jnp.arange(H, dtype=jnp.float32))))
    q = q.reshape(Bsz, C, Lc, H, Dh)
    k = k.reshape(Bsz, C, Lc, H, Dh)
    v = v.reshape(Bsz, C, Lc, H, Dh)
    idx = jnp.arange(Lc, dtype=jnp.float32)
    diff = idx[:, None] - idx[None, :]
    causal = diff >= 0
    decay_intra = jnp.where(causal[None], jnp.exp(jnp.where(causal, diff, 0.0)[None] * log_gamma[:, None, None]), 0.0)
    scores = jnp.einsum('bclhd,bcshd->bchls', q, k) * decay_intra[None, None]
    y_intra = jnp.einsum('bchls,bcshd->bclhd', scores, v)
    w_state = jnp.exp((Lc - 1.0 - idx)[None, :] * log_gamma[:, None])
    v_w = v * w_state.T[None, None, :, :, None]
    chunk_states = jnp.einsum('bcshk,bcshv->bchkv', k, v_w)
    chunk_decay = jnp.exp(Lc * log_gamma)

    def step(carry, s_c):
        return carry * chunk_decay[None, :, None, None] + s_c, carry

    init = jnp.zeros((Bsz, H, Dh, Dh), dtype=chunk_states.dtype)
    _, prev = lax.scan(step, init, jnp.moveaxis(chunk_states, 1, 0))
    prev = jnp.moveaxis(prev, 0, 1)
    w_query = jnp.exp((idx + 1.0)[None, :] * log_gamma[:, None])
    y_cross = jnp.einsum('bclhk,bchkv->bclhv', q, prev) * w_query.T[None, None, :, :, None]
    return (y_intra + y_cross).reshape(Bsz, T, H, Dh)


def causal_conv(u, w, b):
    K = w.shape[0]
    T = u.shape[1]
    up = jnp.pad(u, ((0, 0), (K - 1, 0), (0, 0)))
    out = b[None, None, :]
    for i in range(K):
        out = out + up[:, i:i + T, :] * w[i]
    return out


def ssd(xs, dt, a_log, bm, cm, d_skip):
    Bsz, T, H, P = xs.shape
    G, N = bm.shape[-2], bm.shape[-1]
    J = H // G
    Lc = SSM_CHUNK
    C = T // Lc
    a = dt * (-jnp.exp(a_log))
    x_dt = (xs * dt[..., None]).reshape(Bsz, C, Lc, G, J, P)
    b = bm.reshape(Bsz, C, Lc, G, N)
    c = cm.reshape(Bsz, C, Lc, G, N)
    a_cs = jnp.cumsum(a.reshape(Bsz, C, Lc, H), axis=2)
    causal = jnp.tril(jnp.ones((Lc, Lc), dtype=bool))
    seg = a_cs[:, :, :, None, :] - a_cs[:, :, None, :, :]
    decay = jnp.exp(jnp.where(causal[None, None, :, :, None], seg, -jnp.inf)).reshape(Bsz, C, Lc, Lc, G, J)
    cb = jnp.einsum('bclgn,bcsgn->bclsg', c, b)
    y_diag = jnp.einsum('bclsgj,bcsgjp->bclgjp', cb[..., None] * decay, x_dt)
    to_end = jnp.exp(a_cs[:, :, -1:, :] - a_cs).reshape(Bsz, C, Lc, G, J)
    chunk_states = jnp.einsum('bcsgn,bcsgjp->bcgjpn', b, x_dt * to_end[..., None])
    chunk_decay = jnp.exp(a_cs[:, :, -1, :]).reshape(Bsz, C, G, J)

    def step(carry, inp):
        s_c, d_c = inp
        return carry * d_c[..., None, None] + s_c, carry

    init = jnp.zeros((Bsz, G, J, P, N), dtype=chunk_states.dtype)
    _, prev = lax.scan(step, init, (jnp.moveaxis(chunk_states, 1, 0), jnp.moveaxis(chunk_decay, 1, 0)))
    prev = jnp.moveaxis(prev, 0, 1)
    from_start = jnp.exp(a_cs).reshape(Bsz, C, Lc, G, J)
    y_off = jnp.einsum('bclgn,bcgjpn->bclgjp', c, prev) * from_start[..., None]
    y = (y_diag + y_off).reshape(Bsz, T, H, P)
    return y + xs * d_skip[:, None]


def token_mixer(h, positions, w_in, conv_w, conv_b, dt_bias, a_log, d_skip, ret_norm_w, ssm_norm_w, w_out):
    Bsz, T, _ = h.shape
    f32 = jnp.float32
    proj = jnp.einsum('btd,de->bte', h, w_in).astype(f32)
    cuts = [RET_WIDTH, 2 * RET_WIDTH, 3 * RET_WIDTH, 4 * RET_WIDTH,
            4 * RET_WIDTH + SSM_WIDTH, 4 * RET_WIDTH + SSM_WIDTH + SSM_CONV_DIM]
    q, k, v, g, z, xbc, dt_raw = jnp.split(proj, cuts, axis=-1)
    hs = (Bsz, T, RET_HEADS, RET_HEAD_DIM)
    y_ret = retention(q.reshape(hs), k.reshape(hs), v.reshape(hs), positions)
    mu = jnp.mean(y_ret, axis=-1, keepdims=True)
    var = jnp.mean(jnp.square(y_ret - mu), axis=-1, keepdims=True)
    y_ret = ((y_ret - mu) * lax.rsqrt(var + EPS)).reshape(Bsz, T, RET_WIDTH)
    y_ret = y_ret * ret_norm_w.astype(f32) * jax.nn.silu(g)
    xbc = jax.nn.silu(causal_conv(xbc, conv_w.astype(f32), conv_b.astype(f32)))
    xs, bm, cm = jnp.split(xbc, [SSM_WIDTH, SSM_WIDTH + SSM_GROUPS * SSM_STATE], axis=-1)
    dt = jax.nn.softplus(dt_raw + dt_bias.astype(f32))
    y_ssm = ssd(xs.reshape(Bsz, T, SSM_HEADS, SSM_HEAD_DIM), dt, a_log.astype(f32),
                bm.reshape(Bsz, T, SSM_GROUPS, SSM_STATE), cm.reshape(Bsz, T, SSM_GROUPS, SSM_STATE),
                d_skip.astype(f32))
    y_ssm = y_ssm.reshape(Bsz, T, SSM_WIDTH) * jax.nn.silu(z)
    y_ssm = y_ssm.reshape(Bsz, T, SSM_GROUPS, SSM_WIDTH // SSM_GROUPS)
    y_ssm = y_ssm * lax.rsqrt(jnp.mean(y_ssm * y_ssm, axis=-1, keepdims=True) + EPS)
    y_ssm = y_ssm.reshape(Bsz, T, SSM_WIDTH) * ssm_norm_w.astype(f32)
    y = jnp.concatenate([y_ret, y_ssm], axis=-1).astype(h.dtype)
    return jnp.einsum('bte,ed->btd', y, w_out)


def hier_moe(h, w_rg, b_rg, w_re, b_re, w_gate, w_up, w_down):
    Bsz, T, D = h.shape
    ht = h.reshape(Bsz * T, D)
    f32 = jnp.float32
    logits_g = (ht @ w_rg).astype(f32) + b_rg.astype(f32)
    p_g = jax.nn.softmax(logits_g, axis=-1)
    g_sel = jnp.argmax(logits_g, axis=-1)
    oh_g = jax.nn.one_hot(g_sel, N_EXPERT_GROUPS, dtype=f32)
    p_sel = jnp.sum(p_g * oh_g, axis=-1)
    logits_e = ((ht @ w_re).astype(f32) + b_re.astype(f32)).reshape(-1, N_EXPERT_GROUPS, EXPERTS_PER_GROUP)
    le_sel = jnp.einsum('tge,tg->te', logits_e, oh_g)
    p_e = jax.nn.softmax(le_sel, axis=-1)
    top_w, top_i = lax.top_k(p_e, TOP_K)
    top_w = top_w / jnp.sum(top_w, axis=-1, keepdims=True)
    w_in_group = jnp.sum(jax.nn.one_hot(top_i, EXPERTS_PER_GROUP, dtype=f32) * top_w[..., None], axis=1)
    comb = oh_g[:, :, None] * w_in_group[:, None, :] * p_sel[:, None, None]
    out = jnp.zeros((Bsz * T, D), dtype=f32)
    for grp in range(N_EXPERT_GROUPS):
        sl = slice(grp * EXPERTS_PER_GROUP, (grp + 1) * EXPERTS_PER_GROUP)
        a = jnp.einsum('td,edf->tef', ht, w_gate[sl])
        u = jnp.einsum('td,edf->tef', ht, w_up[sl])
        act = (jax.nn.silu(a.astype(f32)) * u.astype(f32) * comb[:, grp, :, None]).astype(h.dtype)
        out = out + jnp.einsum('tef,efd->td', act, w_down[sl]).astype(f32)
    return out.astype(h.dtype).reshape(Bsz, T, D)


def setup_inputs(seed: int = 0) -> dict:
    key = jax.random.key(seed)
    ks = jax.random.split(key, 24)
    L = DEPTH
    nrm = jax.random.normal
    x = nrm(ks[0], (BATCH, SEQ, D_MODEL), jnp.float32)
    positions = jnp.broadcast_to(jnp.arange(SEQ, dtype=jnp.int32)[None, :], (BATCH, SEQ))
    norm1_w = 1.0 + 0.02 * nrm(ks[1], (L, D_MODEL), jnp.float32)
    w_in = nrm(ks[2], (L, D_MODEL, IN_PROJ_DIM), jnp.float32) * (D_MODEL ** -0.5)
    conv_w = nrm(ks[3], (L, SSM_CONV, SSM_CONV_DIM), jnp.float32) * (SSM_CONV ** -0.5)
    conv_b = 0.01 * nrm(ks[4], (L, SSM_CONV_DIM), jnp.float32)
    dt0 = jnp.exp(jax.random.uniform(ks[5], (L, SSM_HEADS), jnp.float32, math.log(1e-3), math.log(1e-1)))
    dt_bias = dt0 + jnp.log(-jnp.expm1(-dt0))
    a_log = jnp.log(jax.random.uniform(ks[6], (L, SSM_HEADS), jnp.float32, 1.0, 16.0))
    d_skip = 1.0 + 0.1 * nrm(ks[7], (L, SSM_HEADS), jnp.float32)
    ret_norm_w = 1.0 + 0.02 * nrm(ks[8], (L, RET_WIDTH), jnp.float32)
    ssm_norm_w = 1.0 + 0.02 * nrm(ks[9], (L, SSM_WIDTH), jnp.float32)
    w_out = nrm(ks[10], (L, D_MIX, D_MODEL), jnp.float32) * (D_MIX ** -0.5)
    norm2_w = 1.0 + 0.02 * nrm(ks[11], (L, D_MODEL), jnp.float32)
    w_router_group = nrm(ks[12], (L, D_MODEL, N_EXPERT_GROUPS), jnp.float32) * (D_MODEL ** -0.5)
    b_router_group = 0.01 * nrm(ks[13], (L, N_EXPERT_GROUPS), jnp.float32)
    w_router_expert = nrm(ks[14], (L, D_MODEL, N_EXPERTS), jnp.float32) * (D_MODEL ** -0.5)
    b_router_expert = 0.01 * nrm(ks[15], (L, N_EXPERTS), jnp.float32)
    w_expert_gate = nrm(ks[16], (L, N_EXPERTS, D_MODEL, D_EXPERT), jnp.float32) * (D_MODEL ** -0.5)
    w_expert_up = nrm(ks[17], (L, N_EXPERTS, D_MODEL, D_EXPERT), jnp.float32) * (D_MODEL ** -0.5)
    w_expert_down = nrm(ks[18], (L, N_EXPERTS, D_EXPERT, D_MODEL), jnp.float32) * (D_EXPERT ** -0.5)
    final_norm_w = 1.0 + 0.02 * nrm(ks[19], (D_MODEL,), jnp.float32)
    return {'x': x, 'positions': positions, 'norm1_w': norm1_w, 'w_in': w_in,
            'conv_w': conv_w, 'conv_b': conv_b, 'dt_bias': dt_bias, 'a_log': a_log,
            'd_skip': d_skip, 'ret_norm_w': ret_norm_w, 'ssm_norm_w': ssm_norm_w,
            'w_out': w_out, 'norm2_w': norm2_w, 'w_router_group': w_router_group,
            'b_router_group': b_router_group, 'w_router_expert': w_router_expert,
            'b_router_expert': b_router_expert, 'w_expert_gate': w_expert_gate,
            'w_expert_up': w_expert_up, 'w_expert_down': w_expert_down,
            'final_norm_w': final_norm_w}


def reference(x, positions, norm1_w, w_in, conv_w, conv_b, dt_bias, a_log, d_skip,
              ret_norm_w, ssm_norm_w, w_out, norm2_w, w_router_group, b_router_group,
              w_router_expert, b_router_expert, w_expert_gate, w_expert_up, w_expert_down,
              final_norm_w):
    h = x
    for layer in range(DEPTH):
        h = h + token_mixer(rmsnorm(h, norm1_w[layer]), positions, w_in[layer], conv_w[layer],
                            conv_b[layer], dt_bias[layer], a_log[layer], d_skip[layer],
                            ret_norm_w[layer], ssm_norm_w[layer], w_out[layer])
        h = h + hier_moe(rmsnorm(h, norm2_w[layer]), w_router_group[layer], b_router_group[layer],
                         w_router_expert[layer], b_router_expert[layer], w_expert_gate[layer],
                         w_expert_up[layer], w_expert_down[layer])
    return rmsnorm(h, final_norm_w)
```

```python
import functools

import jax
import jax.numpy as jnp
from jax import lax
from jax.experimental import pallas as pl
from jax.experimental.pallas import tpu as pltpu

F32 = jnp.float32
BF16 = jnp.bfloat16
I32 = jnp.int32

D_MODEL = 2048
EPS = 1e-6
CHUNK = 128
RET_HEADS = 4
RET_HEAD_DIM = 256
RET_WIDTH = RET_HEADS * RET_HEAD_DIM
ROPE_THETA = 10000.0
SSM_WIDTH = 1024
SSM_HEAD_DIM = 64
SSM_HEADS = SSM_WIDTH // SSM_HEAD_DIM
SSM_GROUPS = 2
SSM_STATE = 128
SSM_CONV = 4
SSM_CONV_DIM = SSM_WIDTH + 2 * SSM_GROUPS * SSM_STATE
MAIN_PROJ = 4 * RET_WIDTH + SSM_WIDTH + SSM_CONV_DIM
N_GROUPS = 4
GROUP_EXPERTS = 8
N_EXPERTS = N_GROUPS * GROUP_EXPERTS
D_EXPERT = 512
LANES = 128
AUX = LANES
ROW_W = D_MODEL + AUX

VMEM_LIMIT = 56 * 1024 * 1024


def _cparams(sem, vmem=VMEM_LIMIT):
    return pltpu.CompilerParams(dimension_semantics=sem, vmem_limit_bytes=vmem)


def _silu(x):
    return x * (1.0 / (1.0 + jnp.exp(-x)))


def _dot(a, b):
    return jnp.dot(a, b, preferred_element_type=F32)


def _dot_nt(a, b):
    return lax.dot_general(a, b, (((1,), (1,)), ((), ())), preferred_element_type=F32)


def _split3(a):
    a1 = a.astype(BF16)
    r1 = a - a1.astype(F32)
    a2 = r1.astype(BF16)
    a3 = (r1 - a2.astype(F32)).astype(BF16)
    return a1, a2, a3


IN_TM = 1024
IN_TN = 512


def _inproj_body(x_ref, nw_ref, w_ref, wdt_ref, o_ref, odt_ref, hn_ref):
    @pl.when(pl.program_id(1) == 0)
    def _():
        def blk(r, carry):
            sl = pl.ds(pl.multiple_of(r * CHUNK, CHUNK), CHUNK)
            x = x_ref[sl, :]
            ms = jnp.mean(x * x, axis=-1, keepdims=True)
            hn_ref[sl, :] = (x * lax.rsqrt(ms + EPS) * nw_ref[...]).astype(BF16)
            return carry
        lax.fori_loop(0, x_ref.shape[0] // CHUNK, blk, 0)
        odt_ref[...] = _dot(hn_ref[...], wdt_ref[...])

    o_ref[...] = _dot(hn_ref[...], w_ref[...])


def _inproj(x, nw, w_main, w_dt):
    T = x.shape[0]
    tm = min(IN_TM, T)
    return pl.pallas_call(
        _inproj_body,
        grid=(T // tm, MAIN_PROJ // IN_TN),
        in_specs=[
            pl.BlockSpec((tm, D_MODEL), lambda i, j: (i, 0)),
            pl.BlockSpec((1, D_MODEL), lambda i, j: (0, 0)),
            pl.BlockSpec((D_MODEL, IN_TN), lambda i, j: (0, j)),
            pl.BlockSpec((D_MODEL, LANES), lambda i, j: (0, 0)),
        ],
        out_specs=[
            pl.BlockSpec((tm, IN_TN), lambda i, j: (i, j)),
            pl.BlockSpec((tm, LANES), lambda i, j: (i, 0)),
        ],
        out_shape=[
            jax.ShapeDtypeStruct((T, MAIN_PROJ), F32),
            jax.ShapeDtypeStruct((T, LANES), F32),
        ],
        scratch_shapes=[pltpu.VMEM((tm, D_MODEL), BF16)],
        compiler_params=_cparams(("parallel", "arbitrary")),
        name="in_proj",
    )(x, nw, w_main, w_dt)


def _rope_body(pos_ref, inv_ref, cos_ref, sin_ref):
    ang = pos_ref[...] * inv_ref[...]
    cos_ref[...] = jnp.cos(ang)
    sin_ref[...] = jnp.sin(ang)


def _rope_tables(pos, inv):
    T = pos.shape[0]
    tm = min(1024, T)
    half = inv.shape[1]
    return pl.pallas_call(
        _rope_body,
        grid=(T // tm,),
        in_specs=[pl.BlockSpec((tm, 1), lambda i: (i, 0)),
                  pl.BlockSpec((1, half), lambda i: (0, 0))],
        out_specs=[pl.BlockSpec((tm, half), lambda i: (i, 0))] * 2,
        out_shape=[jax.ShapeDtypeStruct((T, half), F32)] * 2,
        compiler_params=_cparams(("parallel",)),
        name="rope_tables",
    )(pos, inv)


RET_CB = 2


def _retention_body(cd_ref, q_ref, k_ref, v_ref, g_ref, cos_ref, sin_ref,
                    dec_ref, wq_ref, ws_ref, nw_ref, o_ref, st_ref, *, cb):
    h = pl.program_id(0)

    @pl.when(pl.program_id(1) == 0)
    def _():
        st_ref[...] = jnp.zeros_like(st_ref)

    chunk_decay = cd_ref[h]
    half = RET_HEAD_DIM // 2
    for i in range(cb):
        sl = slice(i * CHUNK, (i + 1) * CHUNK)
        cos = cos_ref[sl, :]
        sin = sin_ref[sl, :]

        def rope(x):
            x1, x2 = x[:, :half], x[:, half:]
            return jnp.concatenate([x1 * cos - x2 * sin, x2 * cos + x1 * sin], axis=-1)

        q = rope(q_ref[sl, :])
        k = rope(k_ref[sl, :]) * (RET_HEAD_DIM ** -0.5)
        v = v_ref[sl, :]
        qb = q.astype(BF16)
        kb = k.astype(BF16)
        scores = _dot_nt(qb, kb) * dec_ref[0]
        y = _dot(scores.astype(BF16), v.astype(BF16))
        st = st_ref[...]
        y = y + _dot(qb, st.astype(BF16)) * wq_ref[0]
        vw = (v * ws_ref[0]).astype(BF16)
        new = _dot(k.T.astype(BF16), vw)
        st_ref[...] = st * chunk_decay + new
        mu = jnp.mean(y, axis=-1, keepdims=True)
        d = y - mu
        var = jnp.mean(d * d, axis=-1, keepdims=True)
        yn = d * lax.rsqrt(var + EPS)
        o_ref[sl, :] = (yn * nw_ref[...] * _silu(g_ref[sl, :])).astype(BF16)


def _retention(proj, cos, sin, ret_nw, consts):
    T = proj.shape[0]
    cb = min(RET_CB, T // CHUNK)
    rows = cb * CHUNK
    chunk_decay, decay_intra, w_query, w_state = consts
    hd = RET_HEAD_DIM
    half = hd // 2

    def col(base):
        return lambda h, c, cd: (c, base + h)

    grid_spec = pltpu.PrefetchScalarGridSpec(
        num_scalar_prefetch=1,
        grid=(RET_HEADS, T // rows),
        in_specs=[
            pl.BlockSpec((rows, hd), col(0)),
            pl.BlockSpec((rows, hd), col(RET_HEADS)),
            pl.BlockSpec((rows, hd), col(2 * RET_HEADS)),
            pl.BlockSpec((rows, hd), col(3 * RET_HEADS)),
            pl.BlockSpec((rows, half), lambda h, c, cd: (c, 0)),
            pl.BlockSpec((rows, half), lambda h, c, cd: (c, 0)),
            pl.BlockSpec((1, CHUNK, CHUNK), lambda h, c, cd: (h, 0, 0)),
            pl.BlockSpec((1, CHUNK, hd), lambda h, c, cd: (h, 0, 0)),
            pl.BlockSpec((1, CHUNK, hd), lambda h, c, cd: (h, 0, 0)),
            pl.BlockSpec((1, hd), lambda h, c, cd: (0, h)),
        ],
        out_specs=pl.BlockSpec((rows, hd), lambda h, c, cd: (c, h)),
        scratch_shapes=[pltpu.VMEM((hd, hd), F32)],
    )
    return pl.pallas_call(
        functools.partial(_retention_body, cb=cb),
        grid_spec=grid_spec,
        out_shape=jax.ShapeDtypeStruct((T, RET_WIDTH), BF16),
        compiler_params=_cparams(("parallel", "arbitrary")),
        name="retention",
    )(chunk_decay, proj, proj, proj, proj, cos, sin, decay_intra, w_query, w_state, ret_nw)


def _retention_consts():
    H, L = RET_HEADS, CHUNK
    log_gamma = jnp.log1p(-(2.0 ** (-5.0 - jnp.arange(H, dtype=F32))))
    idx = jnp.arange(L, dtype=F32)
    diff = idx[:, None] - idx[None, :]
    causal = diff >= 0
    decay_intra = jnp.where(causal[None], jnp.exp(jnp.where(causal, diff, 0.0)[None] * log_gamma[:, None, None]), 0.0)
    w_state = jnp.exp((L - 1.0 - idx)[None, :] * log_gamma[:, None])
    w_query = jnp.exp((idx + 1.0)[None, :] * log_gamma[:, None])
    chunk_decay = jnp.exp(L * log_gamma)
    bc = lambda w: jnp.broadcast_to(w[:, :, None], (H, L, RET_HEAD_DIM))
    return chunk_decay, decay_intra, bc(w_query), bc(w_state)


SSD_CB = 2
CONV_PAD = 8
XBC_BLK = 512


def _ssd_body(x0_ref, x1_ref, x2_ref, z_ref, dt_ref, cw_ref, cbias_ref, dtb_ref,
              alog_ref, dskip_ref, nw_ref, tri_ref, exp_ref, o_ref, xpad_ref, st_ref, *, cb):
    @pl.when(pl.program_id(0) == 0)
    def _():
        xpad_ref[0:CONV_PAD, :] = jnp.zeros((CONV_PAD, SSM_CONV_DIM), F32)
        st_ref[...] = jnp.zeros_like(st_ref)

    L = CHUNK
    gw = SSM_WIDTH // SSM_GROUPS
    rows_i = lax.broadcasted_iota(I32, (L, L), 0)
    cols_i = lax.broadcasted_iota(I32, (L, L), 1)
    causal = rows_i >= cols_i
    lo_lane = lax.broadcasted_iota(I32, (L, LANES), 1) < SSM_HEAD_DIM
    tri = tri_ref[...]
    expand = exp_ref[...]
    a_neg = -jnp.exp(alog_ref[...])

    for i in range(cb):
        sl = slice(i * L, (i + 1) * L)
        for b, xr in enumerate((x0_ref, x1_ref, x2_ref)):
            xpad_ref[CONV_PAD:CONV_PAD + L, b * XBC_BLK:(b + 1) * XBC_BLK] = xr[sl, :]
        u_parts = []
        for b in range(SSM_CONV_DIM // XBC_BLK):
            cs = slice(b * XBC_BLK, (b + 1) * XBC_BLK)
            acc = cbias_ref[:, cs]
            for t in range(SSM_CONV):
                r0 = CONV_PAD - (SSM_CONV - 1) + t
                acc = acc + xpad_ref[r0:r0 + L, cs] * cw_ref[t:t + 1, cs]
            u_parts.append(_silu(acc))
        xpad_ref[0:CONV_PAD, :] = xpad_ref[L:L + CONV_PAD, :]
        xs = jnp.concatenate(u_parts[:2], axis=-1)
        bm = u_parts[2][:, :SSM_GROUPS * SSM_STATE]
        cm = u_parts[2][:, SSM_GROUPS * SSM_STATE:]

        dt_in = dt_ref[sl, :] + dtb_ref[...]
        dt = jnp.maximum(dt_in, 0.0) + jnp.log1p(jnp.exp(-jnp.abs(dt_in)))
        a = dt * a_neg
        a1, a2, a3 = _split3(a)
        a_cs = _dot(tri, a1) + _dot(tri, a2) + _dot(tri, a3)
        d1, d2, d3 = _split3(dt)
        dt_e = _dot(d1, expand) + _dot(d2, expand) + _dot(d3, expand)
        c1, c2, c3 = _split3(a_cs)
        acs_e = _dot(c1, expand) + _dot(c2, expand) + _dot(c3, expand)
        last = acs_e[L - 1:L, :]
        to_end = jnp.exp(last - acs_e)
        from_start = jnp.exp(acs_e)
        chunk_decay = jnp.exp(last)
        x_dt = xs * dt_e
        xw = (x_dt * to_end).astype(BF16)
        acs_t = a_cs.T
        cmb = cm.astype(BF16)
        bmb = bm.astype(BF16)

        ys = []
        for g in range(SSM_GROUPS):
            ns = slice(g * SSM_STATE, (g + 1) * SSM_STATE)
            gs = slice(g * gw, (g + 1) * gw)
            cg = cmb[:, ns]
            cbg = _dot_nt(cg, bmb[:, ns])
            st = st_ref[g]
            y_off = _dot(cg, st.astype(BF16))
            new = _dot(bm[:, ns].T.astype(BF16), xw[:, gs])
            st_ref[g] = st * chunk_decay[:, gs] + new
            for jp in range(gw // LANES):
                h0 = (g * gw + jp * LANES) // SSM_HEAD_DIM
                ms = []
                for hh in (h0, h0 + 1):
                    seg = a_cs[:, hh:hh + 1] - acs_t[hh:hh + 1, :]
                    dec = jnp.exp(jnp.where(causal, seg, -jnp.inf))
                    ms.append((cbg * dec).astype(BF16))
                lhs = jnp.concatenate(ms, axis=1)
                ls = slice(g * gw + jp * LANES, g * gw + (jp + 1) * LANES)
                xp = x_dt[:, ls]
                rhs = jnp.concatenate([jnp.where(lo_lane, xp, 0.0),
                                       jnp.where(lo_lane, 0.0, xp)], axis=0).astype(BF16)
                y_diag = _dot(lhs, rhs)
                ys.append(y_diag + y_off[:, jp * LANES:(jp + 1) * LANES] * from_start[:, ls])
        y = jnp.concatenate(ys, axis=1) + xs * dskip_ref[...]
        y = y * _silu(z_ref[sl, :])
        outs = []
        for g in range(SSM_GROUPS):
            yg = y[:, g * gw:(g + 1) * gw]
            ms_ = jnp.mean(yg * yg, axis=-1, keepdims=True)
            outs.append(yg * lax.rsqrt(ms_ + EPS))
        o_ref[sl, :] = (jnp.concatenate(outs, axis=1) * nw_ref[...]).astype(BF16)


def _ssd(proj, dt_raw, conv_w, conv_b, dt_bias, a_log, d_skip_e, ssm_nw, tri, expand):
    T = proj.shape[0]
    cb = min(SSD_CB, T // CHUNK)
    rows = cb * CHUNK
    xbc0 = (4 * RET_WIDTH + SSM_WIDTH) // XBC_BLK
    zcol = 4 * RET_WIDTH // SSM_WIDTH
    full = lambda shape: pl.BlockSpec(shape, lambda c: (0,) * len(shape))
    return pl.pallas_call(
        functools.partial(_ssd_body, cb=cb),
        grid=(T // rows,),
        in_specs=[
            pl.BlockSpec((rows, XBC_BLK), lambda c: (c, xbc0)),
            pl.BlockSpec((rows, XBC_BLK), lambda c: (c, xbc0 + 1)),
            pl.BlockSpec((rows, XBC_BLK), lambda c: (c, xbc0 + 2)),
            pl.BlockSpec((rows, SSM_WIDTH), lambda c: (c, zcol)),
            pl.BlockSpec((rows, LANES), lambda c: (c, 0)),
            full((SSM_CONV, SSM_CONV_DIM)),
            full((1, SSM_CONV_DIM)),
            full((1, LANES)),
            full((1, LANES)),
            full((1, SSM_WIDTH)),
            full((1, SSM_WIDTH)),
            full((CHUNK, CHUNK)),
            full((LANES, SSM_WIDTH)),
        ],
        out_specs=pl.BlockSpec((rows, SSM_WIDTH), lambda c: (c, 0)),
        out_shape=jax.ShapeDtypeStruct((T, SSM_WIDTH), BF16),
        scratch_shapes=[
            pltpu.VMEM((CHUNK + CONV_PAD, SSM_CONV_DIM), F32),
            pltpu.VMEM((SSM_GROUPS, SSM_STATE, SSM_WIDTH // SSM_GROUPS), F32),
        ],
        compiler_params=_cparams(("arbitrary",)),
        name="ssd",
    )(proj, proj, proj, proj, dt_raw, conv_w, conv_b, dt_bias, a_log, d_skip_e, ssm_nw, tri, expand)


OUT_TM = 256
ROUTE_ROWS = 8 + N_EXPERTS


def _out_router_body(x_ref, yr_ref, ys_ref, wo_ref, nw_ref, wr_ref, br_ref,
                     h1_ref, h2_ref, ids_ref):
    tm = x_ref.shape[0]
    h1 = x_ref[...] + _dot(yr_ref[...], wo_ref[0:RET_WIDTH, :]) + _dot(ys_ref[...], wo_ref[RET_WIDTH:, :])
    h1_ref[...] = h1
    ms = jnp.mean(h1 * h1, axis=-1, keepdims=True)
    h2 = h1 * lax.rsqrt(ms + EPS) * nw_ref[...]
    h2_ref[:, 0:D_MODEL] = h2

    logits = _dot_nt(wr_ref[...], h2.astype(BF16)) + br_ref[...]
    row = lax.broadcasted_iota(I32, (8, tm), 0)
    lg = jnp.where(row < N_GROUPS, logits[0:8], -jnp.inf)
    m = jnp.max(lg, axis=0, keepdims=True)
    p_sel = 1.0 / jnp.sum(jnp.exp(lg - m), axis=0, keepdims=True)
    g_sel = jnp.min(jnp.where(lg == m, row, 8), axis=0, keepdims=True)
    le = jnp.zeros((GROUP_EXPERTS, tm), F32)
    for g in range(N_GROUPS):
        le = jnp.where(g_sel == g, logits[8 + g * GROUP_EXPERTS:8 + (g + 1) * GROUP_EXPERTS], le)
    m2 = jnp.max(le, axis=0, keepdims=True)
    ee = jnp.exp(le - m2)
    pe = ee / jnp.sum(ee, axis=0, keepdims=True)
    v1 = jnp.max(pe, axis=0, keepdims=True)
    i1 = jnp.min(jnp.where(pe == v1, row, 8), axis=0, keepdims=True)
    pe2 = jnp.where(row == i1, -1.0, pe)
    v2 = jnp.max(pe2, axis=0, keepdims=True)
    i2 = jnp.min(jnp.where(pe2 == v2, row, 8), axis=0, keepdims=True)
    tw = v1 + v2
    c1 = v1 / tw * p_sel
    c2 = v2 / tw * p_sel
    e1 = g_sel * GROUP_EXPERTS + i1
    e2 = g_sel * GROUP_EXPERTS + i2
    ids = jnp.where(row == 0, e1, jnp.where(row == 1, e2, 0))
    for b in range(tm // SORT_BLK):
        ids_ref[b] = ids[:, b * SORT_BLK:(b + 1) * SORT_BLK]
    aux8 = jnp.where(row == 0, c1, jnp.where(row == 1, c2, jnp.where(
        row == 2, e1.astype(F32), jnp.where(row == 3, e2.astype(F32), 0.0))))
    aux = jnp.concatenate([aux8, jnp.zeros((AUX - 8, tm), F32)], axis=0)
    h2_ref[:, D_MODEL:ROW_W] = aux.T


def _out_router(x, y_ret, y_ssm, w_out, nw, wr_t, br):
    T = x.shape[0]
    tm = min(OUT_TM, T)
    full = lambda shape: pl.BlockSpec(shape, lambda i: (0,) * len(shape))
    return pl.pallas_call(
        _out_router_body,
        grid=(T // tm,),
        in_specs=[
            pl.BlockSpec((tm, D_MODEL), lambda i: (i, 0)),
            pl.BlockSpec((tm, RET_WIDTH), lambda i: (i, 0)),
            pl.BlockSpec((tm, SSM_WIDTH), lambda i: (i, 0)),
            full((D_MODEL, D_MODEL)),
            full((1, D_MODEL)),
            full((LANES, D_MODEL)),
            full((LANES, 1)),
        ],
        out_specs=[
            pl.BlockSpec((tm, D_MODEL), lambda i: (i, 0)),
            pl.BlockSpec((tm, ROW_W), lambda i: (i, 0)),
            pl.BlockSpec((tm // SORT_BLK, 8, SORT_BLK), lambda i: (i, 0, 0)),
        ],
        out_shape=[
            jax.ShapeDtypeStruct((T, D_MODEL), F32),
            jax.ShapeDtypeStruct((T, ROW_W), F32),
            jax.ShapeDtypeStruct((T // SORT_BLK, 8, SORT_BLK), I32),
        ],
        compiler_params=_cparams(("parallel",)),
        name="out_router",
    )(x, y_ret, y_ssm, w_out, nw, wr_t, br)


SORT_BLK = 256
MOE_TM = 256


def _sort_index_body(ids_ref, tri_ref, ltri_ref, dest_ref, cnt_ref, rank_ref):
    nblk = ids_ref.shape[0]
    row_e = lax.broadcasted_iota(I32, (N_EXPERTS, SORT_BLK), 0)
    row8 = lax.broadcasted_iota(I32, (8, SORT_BLK), 0)

    def onehots(b):
        ids = ids_ref[b]
        return row_e == ids[0:1], row_e == ids[1:2]

    def rank_blk(b, carry):
        oh1, oh2 = onehots(b)
        ohf = jnp.where(oh1 | oh2, 1.0, 0.0)
        incl = _dot(ohf.astype(BF16), tri_ref[...])
        base = carry + incl - 1.0
        r1 = jnp.sum(jnp.where(oh1, base, 0.0), axis=0, keepdims=True)
        r2 = jnp.sum(jnp.where(oh2, base, 0.0), axis=0, keepdims=True)
        rank_ref[b] = jnp.where(row8 == 0, r1, jnp.where(row8 == 1, r2, 0.0))
        return carry + jnp.sum(ohf, axis=1, keepdims=True)

    cnt = lax.fori_loop(0, nblk, rank_blk, jnp.zeros((N_EXPERTS, 1), F32))
    cnt_ref[...] = jnp.broadcast_to(cnt, cnt_ref.shape)
    tiles = jnp.floor((cnt + (MOE_TM - 1.0)) / MOE_TM)
    tiles_b = jnp.broadcast_to(tiles, (N_EXPERTS, LANES)).astype(BF16)
    off = _dot(ltri_ref[...], tiles_b)[:, 0:1] * MOE_TM

    def dest_blk(b, carry):
        oh1, oh2 = onehots(b)
        o1 = jnp.sum(jnp.where(oh1, off, 0.0), axis=0, keepdims=True)
        o2 = jnp.sum(jnp.where(oh2, off, 0.0), axis=0, keepdims=True)
        d = rank_ref[b] + jnp.where(row8 == 0, o1, jnp.where(row8 == 1, o2, 0.0))
        dest_ref[b] = d.astype(I32)
        return carry

    lax.fori_loop(0, nblk, dest_blk, 0)


def _sort_index(ids, tri_u, ltri):
    nblk = ids.shape[0]
    return pl.pallas_call(
        _sort_index_body,
        out_shape=[jax.ShapeDtypeStruct((nblk, 8, SORT_BLK), I32),
                   jax.ShapeDtypeStruct((N_EXPERTS, LANES), F32)],
        scratch_shapes=[pltpu.VMEM((nblk, 8, SORT_BLK), F32)],
        compiler_params=_cparams(None),
        name="sort_index",
    )(ids, tri_u, ltri)


DISPATCH_BLK = 128


def _dispatch_body(dest_ref, pad_ref, h2_ref, xs_ref, zero_ref, sem):
    T = h2_ref.shape[0]

    def row_copy(t, k):
        return pltpu.make_async_copy(h2_ref.at[pl.ds(t, 1), :],
                                     xs_ref.at[pl.ds(dest_ref[k * T + t], 1), :], sem)

    def pad_copy(i):
        row0 = pl.multiple_of(i * MOE_TM, MOE_TM)
        return pltpu.make_async_copy(zero_ref, xs_ref.at[pl.ds(row0, MOE_TM), :], sem)

    zero_ref[...] = jnp.zeros_like(zero_ref)
    n_tiles = xs_ref.shape[0] // MOE_TM

    def pad_start(i, c):
        @pl.when(pad_ref[i] == 1)
        def _():
            pad_copy(i).start()
        return c

    def pad_wait(i, c):
        @pl.when(pad_ref[i] == 1)
        def _():
            pad_copy(i).wait()
        return c

    lax.fori_loop(0, n_tiles, pad_start, 0)
    lax.fori_loop(0, n_tiles, pad_wait, 0)

    def blk(b, c):
        base = b * DISPATCH_BLK

        def start(i, c2):
            row_copy(base + i, 0).start()
            row_copy(base + i, 1).start()
            return c2

        def wait(i, c2):
            row_copy(base + i, 0).wait()
            row_copy(base + i, 1).wait()
            return c2

        lax.fori_loop(0, DISPATCH_BLK, start, 0)
        lax.fori_loop(0, DISPATCH_BLK, wait, 0)
        return c

    lax.fori_loop(0, T // DISPATCH_BLK, blk, 0)


def _dispatch(dest, pad_rows, h2aug, n_rows):
    grid_spec = pltpu.PrefetchScalarGridSpec(
        num_scalar_prefetch=2,
        grid=(1,),
        in_specs=[pl.BlockSpec(memory_space=pl.ANY)],
        out_specs=pl.BlockSpec(memory_space=pl.ANY),
        scratch_shapes=[pltpu.VMEM((MOE_TM, ROW_W), F32), pltpu.SemaphoreType.DMA(())],
    )
    return pl.pallas_call(
        _dispatch_body,
        grid_spec=grid_spec,
        out_shape=jax.ShapeDtypeStruct((n_rows, ROW_W), F32),
        compiler_params=_cparams(("arbitrary",)),
        name="dispatch",
    )(dest, pad_rows, h2aug)


def _moe_body(te_ref, tb_ref, tf_ref, nu_ref, x_ref, wg_ref, wu_ref, wd_ref, o_ref,
              wgb_ref, wub_ref, wdb_ref):
    i = pl.program_id(0)

    @pl.when(tf_ref[i] == 1)
    def _():
        wgb_ref[...] = wg_ref[0].astype(BF16)
        wub_ref[...] = wu_ref[0].astype(BF16)
        wdb_ref[...] = wd_ref[0].astype(BF16)

    @pl.when(i < nu_ref[0])
    def _():
        xb = x_ref[:, 0:D_MODEL].astype(BF16)
        aux = x_ref[:, D_MODEL:ROW_W]
        first_slot = aux[:, 2:3].astype(I32) == te_ref[i]
        comb = jnp.where(first_slot, aux[:, 0:1], aux[:, 1:2])
        a = _dot(xb, wgb_ref[...])
        u = _dot(xb, wub_ref[...])
        act = (_silu(a) * u * comb).astype(BF16)
        o_ref[...] = _dot(act, wdb_ref[...])

    @pl.when(i >= nu_ref[0])
    def _():
        o_ref[...] = jnp.zeros_like(o_ref)


def _moe(tile_expert, tile_block, tile_first, n_used, x_sorted, w_gate, w_up, w_down):
    n_tiles = x_sorted.shape[0] // MOE_TM
    wspec = lambda shape: pl.BlockSpec((1,) + shape, lambda i, te, tb, tf, nu: (te[i], 0, 0))
    grid_spec = pltpu.PrefetchScalarGridSpec(
        num_scalar_prefetch=4,
        grid=(n_tiles,),
        in_specs=[
            pl.BlockSpec((MOE_TM, ROW_W), lambda i, te, tb, tf, nu: (tb[i], 0)),
            wspec((D_MODEL, D_EXPERT)),
            wspec((D_MODEL, D_EXPERT)),
            wspec((D_EXPERT, D_MODEL)),
        ],
        out_specs=pl.BlockSpec((MOE_TM, D_MODEL), lambda i, te, tb, tf, nu: (i, 0)),
        scratch_shapes=[pltpu.VMEM((D_MODEL, D_EXPERT), BF16),
                        pltpu.VMEM((D_MODEL, D_EXPERT), BF16),
                        pltpu.VMEM((D_EXPERT, D_MODEL), BF16)],
    )
    return pl.pallas_call(
        _moe_body,
        grid_spec=grid_spec,
        out_shape=jax.ShapeDtypeStruct((x_sorted.shape[0], D_MODEL), F32),
        compiler_params=_cparams(("arbitrary",)),
        name="moe",
    )(tile_expert, tile_block, tile_first, n_used, x_sorted, w_gate, w_up, w_down)


COMB_TM = 256


def _combine_body(dest_ref, h1_ref, ys_ref, nw_ref, o_ref, buf_ref, sem):
    tm = h1_ref.shape[0]
    T = dest_ref.shape[0] // 2
    base = pl.program_id(0) * tm

    def row_copy(i, k):
        return pltpu.make_async_copy(ys_ref.at[pl.ds(dest_ref[k * T + base + i], 1), :],
                                     buf_ref.at[k, pl.ds(i, 1), :], sem)

    def start(i, c):
        row_copy(i, 0).start()
        row_copy(i, 1).start()
        return c

    def wait(i, c):
        row_copy(i, 0).wait()
        row_copy(i, 1).wait()
        return c

    lax.fori_loop(0, tm, start, 0)
    lax.fori_loop(0, tm, wait, 0)
    h = h1_ref[...] + (buf_ref[0] + buf_ref[1])
    ms = jnp.mean(h * h, axis=-1, keepdims=True)
    o_ref[...] = h * lax.rsqrt(ms + EPS) * nw_ref[...]


def _combine(dest, h1, y_sorted, nw):
    T = h1.shape[0]
    tm = min(COMB_TM, T)
    grid_spec = pltpu.PrefetchScalarGridSpec(
        num_scalar_prefetch=1,
        grid=(T // tm,),
        in_specs=[
            pl.BlockSpec((tm, D_MODEL), lambda i, d: (i, 0)),
            pl.BlockSpec(memory_space=pl.ANY),
            pl.BlockSpec((1, D_MODEL), lambda i, d: (0, 0)),
        ],
        out_specs=pl.BlockSpec((tm, D_MODEL), lambda i, d: (i, 0)),
        scratch_shapes=[pltpu.VMEM((2, tm, D_MODEL), F32), pltpu.SemaphoreType.DMA(())],
    )
    return pl.pallas_call(
        _combine_body,
        grid_spec=grid_spec,
        out_shape=jax.ShapeDtypeStruct((T, D_MODEL), F32),
        compiler_params=_cparams(("arbitrary",)),
        name="combine",
    )(dest, h1, y_sorted, nw)


def _tile_plan(cnt, n_tiles):
    tiles = (cnt + (MOE_TM - 1)) // MOE_TM
    ends = jnp.cumsum(tiles)
    n_used = ends[-1]
    tile_idx = jnp.arange(n_tiles, dtype=I32)
    last = jnp.maximum(n_used - 1, 0)
    tile_block = jnp.minimum(tile_idx, last)
    tile_expert = jnp.sum((ends[None, :] <= tile_block[:, None]).astype(I32), axis=1)
    starts = ends - tiles
    tile_first = ((tile_idx == starts[tile_expert]) & (tile_idx < n_used)).astype(I32)
    tile_last = tile_idx == (ends[tile_expert] - 1)
    tile_pad = (tile_last | (tile_idx >= n_used)).astype(I32)
    return tile_expert, tile_block.astype(I32), tile_first, n_used.reshape(1).astype(I32), tile_pad


def kernel(x, positions, norm1_w, w_in, conv_w, conv_b, dt_bias, a_log, d_skip, ret_norm_w,
           ssm_norm_w, w_out, norm2_w, w_router_group, b_router_group, w_router_expert,
           b_router_expert, w_expert_gate, w_expert_up, w_expert_down, final_norm_w):
    B, T, D = x.shape
    assert B == 1 and D == D_MODEL and T % CHUNK == 0
    xf = x.reshape(T, D)
    pad_l = lambda v: jnp.pad(v, ((0, 0), (0, LANES - v.shape[-1])))

    w_main = w_in[0][:, :MAIN_PROJ].astype(BF16)
    w_dt = pad_l(w_in[0][:, MAIN_PROJ:]).astype(BF16)
    half = RET_HEAD_DIM // 2
    inv = (ROPE_THETA ** (-jnp.arange(half, dtype=F32) / half)).reshape(1, half)
    tri = (jnp.arange(CHUNK)[:, None] >= jnp.arange(CHUNK)[None, :]).astype(BF16)
    expand = (jnp.arange(LANES)[:, None] == jnp.arange(SSM_WIDTH)[None, :] // SSM_HEAD_DIM).astype(BF16)
    d_skip_e = jnp.repeat(d_skip[0], SSM_HEAD_DIM).reshape(1, SSM_WIDTH)
    wr_t = jnp.zeros((LANES, D), F32)
    wr_t = wr_t.at[0:N_GROUPS].set(w_router_group[0].T).at[8:8 + N_EXPERTS].set(w_router_expert[0].T)
    br = jnp.zeros((LANES,), F32)
    br = br.at[0:N_GROUPS].set(b_router_group[0]).at[8:8 + N_EXPERTS].set(b_router_expert[0])
    tri_u = (jnp.arange(SORT_BLK)[:, None] <= jnp.arange(SORT_BLK)[None, :]).astype(BF16)
    ltri = (jnp.arange(N_EXPERTS)[:, None] > jnp.arange(N_EXPERTS)[None, :]).astype(BF16)

    proj, dt_raw = _inproj(xf, norm1_w[0].reshape(1, D), w_main, w_dt)
    cos, sin = _rope_tables(positions.reshape(T, 1).astype(F32), inv)
    y_ret = _retention(proj, cos, sin, ret_norm_w[0].reshape(1, RET_WIDTH), _retention_consts())
    y_ssm = _ssd(proj, dt_raw, conv_w[0], conv_b[0].reshape(1, -1), pad_l(dt_bias[0].reshape(1, -1)),
                 pad_l(a_log[0].reshape(1, -1)), d_skip_e, ssm_norm_w[0].reshape(1, -1), tri, expand)
    h1, h2aug, ids = _out_router(xf, y_ret, y_ssm, w_out[0].astype(BF16), norm2_w[0].reshape(1, D),
                                 wr_t.astype(BF16), br.reshape(LANES, 1))

    dest_blk, cnt = _sort_index(ids, tri_u, ltri)
    dest = dest_blk[:, 0:2, :].transpose(1, 0, 2).reshape(2 * T)
    n_tiles = (2 * T) // MOE_TM + N_EXPERTS
    tile_expert, tile_block, tile_first, n_used, pad_rows = _tile_plan(cnt[:, 0].astype(I32), n_tiles)
    x_sorted = _dispatch(dest, pad_rows, h2aug, n_tiles * MOE_TM)
    y_sorted = _moe(tile_expert, tile_block, tile_first, n_used, x_sorted,
                    w_expert_gate[0], w_expert_up[0], w_expert_down[0])
    out = _combine(dest, h1, y_sorted, final_norm_w.reshape(1, D))
    return out.reshape(B, T, D)
```

```python
import functools

import jax
import jax.numpy as jnp
from jax import lax
from jax.experimental import pallas as pl
from jax.experimental.pallas import tpu as pltpu

F32 = jnp.float32
BF16 = jnp.bfloat16
I32 = jnp.int32

D_MODEL = 2048
EPS = 1e-6
CHUNK = 128
RET_HEADS = 4
RET_HEAD_DIM = 256
RET_WIDTH = RET_HEADS * RET_HEAD_DIM
ROPE_THETA = 10000.0
SSM_WIDTH = 1024
SSM_HEAD_DIM = 64
SSM_HEADS = SSM_WIDTH // SSM_HEAD_DIM
SSM_GROUPS = 2
SSM_STATE = 128
SSM_CONV = 4
SSM_CONV_DIM = SSM_WIDTH + 2 * SSM_GROUPS * SSM_STATE
MAIN_PROJ = 4 * RET_WIDTH + SSM_WIDTH + SSM_CONV_DIM
N_GROUPS = 4
GROUP_EXPERTS = 8
N_EXPERTS = N_GROUPS * GROUP_EXPERTS
D_EXPERT = 512
LANES = 128
AUX = LANES
ROW_W = D_MODEL + AUX

VMEM_LIMIT = 56 * 1024 * 1024


def _cparams(sem, vmem=VMEM_LIMIT):
    return pltpu.CompilerParams(dimension_semantics=sem, vmem_limit_bytes=vmem)


def _silu(x):
    return x * (1.0 / (1.0 + jnp.exp(-x)))


def _dot(a, b):
    return jnp.dot(a, b, preferred_element_type=F32)


def _dot_nt(a, b):
    return lax.dot_general(a, b, (((1,), (1,)), ((), ())), preferred_element_type=F32)


def _split3(a):
    a1 = a.astype(BF16)
    r1 = a - a1.astype(F32)
    a2 = r1.astype(BF16)
    a3 = (r1 - a2.astype(F32)).astype(BF16)
    return a1, a2, a3


IN_TM = 1024
IN_TN = 512


def _inproj_body(x_ref, nw_ref, w_ref, wdt_ref, o_ref, odt_ref, hn_ref):
    @pl.when(pl.program_id(1) == 0)
    def _():
        def blk(r, carry):
            sl = pl.ds(pl.multiple_of(r * CHUNK, CHUNK), CHUNK)
            x = x_ref[sl, :]
            ms = jnp.mean(x * x, axis=-1, keepdims=True)
            hn_ref[sl, :] = (x * lax.rsqrt(ms + EPS) * nw_ref[...]).astype(BF16)
            return carry
        lax.fori_loop(0, x_ref.shape[0] // CHUNK, blk, 0)
        odt_ref[...] = _dot(hn_ref[...], wdt_ref[...])

    o_ref[...] = _dot(hn_ref[...], w_ref[...])


def _inproj(x, nw, w_main, w_dt):
    T = x.shape[0]
    tm = min(IN_TM, T)
    return pl.pallas_call(
        _inproj_body,
        grid=(T // tm, MAIN_PROJ // IN_TN),
        in_specs=[
            pl.BlockSpec((tm, D_MODEL), lambda i, j: (i, 0)),
            pl.BlockSpec((1, D_MODEL), lambda i, j: (0, 0)),
            pl.BlockSpec((D_MODEL, IN_TN), lambda i, j: (0, j)),
            pl.BlockSpec((D_MODEL, LANES), lambda i, j: (0, 0)),
        ],
        out_specs=[
            pl.BlockSpec((tm, IN_TN), lambda i, j: (i, j)),
            pl.BlockSpec((tm, LANES), lambda i, j: (i, 0)),
        ],
        out_shape=[
            jax.ShapeDtypeStruct((T, MAIN_PROJ), F32),
            jax.ShapeDtypeStruct((T, LANES), F32),
        ],
        scratch_shapes=[pltpu.VMEM((tm, D_MODEL), BF16)],
        compiler_params=_cparams(("parallel", "arbitrary")),
        name="in_proj",
    )(x, nw, w_main, w_dt)


def _rope_body(pos_ref, inv_ref, cos_ref, sin_ref):
    ang = pos_ref[...] * inv_ref[...]
    cos_ref[...] = jnp.cos(ang)
    sin_ref[...] = jnp.sin(ang)


def _rope_tables(pos, inv):
    T = pos.shape[0]
    tm = min(1024, T)
    half = inv.shape[1]
    return pl.pallas_call(
        _rope_body,
        grid=(T // tm,),
        in_specs=[pl.BlockSpec((tm, 1), lambda i: (i, 0)),
                  pl.BlockSpec((1, half), lambda i: (0, 0))],
        out_specs=[pl.BlockSpec((tm, half), lambda i: (i, 0))] * 2,
        out_shape=[jax.ShapeDtypeStruct((T, half), F32)] * 2,
        compiler_params=_cparams(("parallel",)),
        name="rope_tables",
    )(pos, inv)


RET_CB = 2


def _retention_body(cd_ref, q_ref, k_ref, v_ref, g_ref, cos_ref, sin_ref,
                    dec_ref, wq_ref, ws_ref, nw_ref, o_ref, st_ref, *, cb):
    h = pl.program_id(0)

    @pl.when(pl.program_id(1) == 0)
    def _():
        st_ref[...] = jnp.zeros_like(st_ref)

    chunk_decay = cd_ref[h]
    half = RET_HEAD_DIM // 2
    for i in range(cb):
        sl = slice(i * CHUNK, (i + 1) * CHUNK)
        cos = cos_ref[sl, :]
        sin = sin_ref[sl, :]

        def rope(x):
            x1, x2 = x[:, :half], x[:, half:]
            return jnp.concatenate([x1 * cos - x2 * sin, x2 * cos + x1 * sin], axis=-1)

        q = rope(q_ref[sl, :])
        k = rope(k_ref[sl, :]) * (RET_HEAD_DIM ** -0.5)
        v = v_ref[sl, :]
        qb = q.astype(BF16)
        kb = k.astype(BF16)
        scores = _dot_nt(qb, kb) * dec_ref[0]
        y = _dot(scores.astype(BF16), v.astype(BF16))
        st = st_ref[...]
        y = y + _dot(qb, st.astype(BF16)) * wq_ref[0]
        vw = (v * ws_ref[0]).astype(BF16)
        new = _dot(k.T.astype(BF16), vw)
        st_ref[...] = st * chunk_decay + new
        mu = jnp.mean(y, axis=-1, keepdims=True)
        d = y - mu
        var = jnp.mean(d * d, axis=-1, keepdims=True)
        yn = d * lax.rsqrt(var + EPS)
        o_ref[sl, :] = (yn * nw_ref[...] * _silu(g_ref[sl, :])).astype(BF16)


def _retention(proj, cos, sin, ret_nw, consts):
    T = proj.shape[0]
    cb = min(RET_CB, T // CHUNK)
    rows = cb * CHUNK
    chunk_decay, decay_intra, w_query, w_state = consts
    hd = RET_HEAD_DIM
    half = hd // 2

    def col(base):
        return lambda h, c, cd: (c, base + h)

    grid_spec = pltpu.PrefetchScalarGridSpec(
        num_scalar_prefetch=1,
        grid=(RET_HEADS, T // rows),
        in_specs=[
            pl.BlockSpec((rows, hd), col(0)),
            pl.BlockSpec((rows, hd), col(RET_HEADS)),
            pl.BlockSpec((rows, hd), col(2 * RET_HEADS)),
            pl.BlockSpec((rows, hd), col(3 * RET_HEADS)),
            pl.BlockSpec((rows, half), lambda h, c, cd: (c, 0)),
            pl.BlockSpec((rows, half), lambda h, c, cd: (c, 0)),
            pl.BlockSpec((1, CHUNK, CHUNK), lambda h, c, cd: (h, 0, 0)),
            pl.BlockSpec((1, CHUNK, hd), lambda h, c, cd: (h, 0, 0)),
            pl.BlockSpec((1, CHUNK, hd), lambda h, c, cd: (h, 0, 0)),
            pl.BlockSpec((1, hd), lambda h, c, cd: (0, h)),
        ],
        out_specs=pl.BlockSpec((rows, hd), lambda h, c, cd: (c, h)),
        scratch_shapes=[pltpu.VMEM((hd, hd), F32)],
    )
    return pl.pallas_call(
        functools.partial(_retention_body, cb=cb),
        grid_spec=grid_spec,
        out_shape=jax.ShapeDtypeStruct((T, RET_WIDTH), BF16),
        compiler_params=_cparams(("parallel", "arbitrary")),
        name="retention",
    )(chunk_decay, proj, proj, proj, proj, cos, sin, decay_intra, w_query, w_state, ret_nw)


def _retention_consts():
    H, L = RET_HEADS, CHUNK
    log_gamma = jnp.log1p(-(2.0 ** (-5.0 - jnp.arange(H, dtype=F32))))
    idx = jnp.arange(L, dtype=F32)
    diff = idx[:, None] - idx[None, :]
    causal = diff >= 0
    decay_intra = jnp.where(causal[None], jnp.exp(jnp.where(causal, diff, 0.0)[None] * log_gamma[:, None, None]), 0.0)
    w_state = jnp.exp((L - 1.0 - idx)[None, :] * log_gamma[:, None])
    w_query = jnp.exp((idx + 1.0)[None, :] * log_gamma[:, None])
    chunk_decay = jnp.exp(L * log_gamma)
    bc = lambda w: jnp.broadcast_to(w[:, :, None], (H, L, RET_HEAD_DIM))
    return chunk_decay, decay_intra, bc(w_query), bc(w_state)


SSD_CB = 2
CONV_PAD = 8
XBC_BLK = 512


def _ssd_body(x0_ref, x1_ref, x2_ref, z_ref, dt_ref, cw_ref, cbias_ref, dtb_ref,
              alog_ref, dskip_ref, nw_ref, tri_ref, exp_ref, o_ref, xpad_ref, st_ref, *, cb):
    @pl.when(pl.program_id(0) == 0)
    def _():
        xpad_ref[0:CONV_PAD, :] = jnp.zeros((CONV_PAD, SSM_CONV_DIM), F32)
        st_ref[...] = jnp.zeros_like(st_ref)

    L = CHUNK
    gw = SSM_WIDTH // SSM_GROUPS
    rows_i = lax.broadcasted_iota(I32, (L, L), 0)
    cols_i = lax.broadcasted_iota(I32, (L, L), 1)
    causal = rows_i >= cols_i
    lo_lane = lax.broadcasted_iota(I32, (L, LANES), 1) < SSM_HEAD_DIM
    tri = tri_ref[...]
    expand = exp_ref[...]
    a_neg = -jnp.exp(alog_ref[...])

    for i in range(cb):
        sl = slice(i * L, (i + 1) * L)
        for b, xr in enumerate((x0_ref, x1_ref, x2_ref)):
            xpad_ref[CONV_PAD:CONV_PAD + L, b * XBC_BLK:(b + 1) * XBC_BLK] = xr[sl, :]
        u_parts = []
        for b in range(SSM_CONV_DIM // XBC_BLK):
            cs = slice(b * XBC_BLK, (b + 1) * XBC_BLK)
            acc = cbias_ref[:, cs]
            for t in range(SSM_CONV):
                r0 = CONV_PAD - (SSM_CONV - 1) + t
                acc = acc + xpad_ref[r0:r0 + L, cs] * cw_ref[t:t + 1, cs]
            u_parts.append(_silu(acc))
        xpad_ref[0:CONV_PAD, :] = xpad_ref[L:L + CONV_PAD, :]
        xs = jnp.concatenate(u_parts[:2], axis=-1)
        bm = u_parts[2][:, :SSM_GROUPS * SSM_STATE]
        cm = u_parts[2][:, SSM_GROUPS * SSM_STATE:]

        dt_in = dt_ref[sl, :] + dtb_ref[...]
        dt = jnp.maximum(dt_in, 0.0) + jnp.log1p(jnp.exp(-jnp.abs(dt_in)))
        a = dt * a_neg
        a1, a2, a3 = _split3(a)
        a_cs = _dot(tri, a1) + _dot(tri, a2) + _dot(tri, a3)
        d1, d2, d3 = _split3(dt)
        dt_e = _dot(d1, expand) + _dot(d2, expand) + _dot(d3, expand)
        c1, c2, c3 = _split3(a_cs)
        acs_e = _dot(c1, expand) + _dot(c2, expand) + _dot(c3, expand)
        last = acs_e[L - 1:L, :]
        to_end = jnp.exp(last - acs_e)
        from_start = jnp.exp(acs_e)
        chunk_decay = jnp.exp(last)
        x_dt = xs * dt_e
        xw = (x_dt * to_end).astype(BF16)
        acs_t = a_cs.T
        cmb = cm.astype(BF16)
        bmb = bm.astype(BF16)

        ys = []
        for g in range(SSM_GROUPS):
            ns = slice(g * SSM_STATE, (g + 1) * SSM_STATE)
            gs = slice(g * gw, (g + 1) * gw)
            cg = cmb[:, ns]
            cbg = _dot_nt(cg, bmb[:, ns])
            st = st_ref[g]
            y_off = _dot(cg, st.astype(BF16))
            new = _dot(bm[:, ns].T.astype(BF16), xw[:, gs])
            st_ref[g] = st * chunk_decay[:, gs] + new
            for jp in range(gw // LANES):
                h0 = (g * gw + jp * LANES) // SSM_HEAD_DIM
                ms = []
                for hh in (h0, h0 + 1):
                    seg = a_cs[:, hh:hh + 1] - acs_t[hh:hh + 1, :]
                    dec = jnp.exp(jnp.where(causal, seg, -jnp.inf))
                    ms.append((cbg * dec).astype(BF16))
                lhs = jnp.concatenate(ms, axis=1)
                ls = slice(g * gw + jp * LANES, g * gw + (jp + 1) * LANES)
                xp = x_dt[:, ls]
                rhs = jnp.concatenate([jnp.where(lo_lane, xp, 0.0),
                                       jnp.where(lo_lane, 0.0, xp)], axis=0).astype(BF16)
                y_diag = _dot(lhs, rhs)
                ys.append(y_diag + y_off[:, jp * LANES:(jp + 1) * LANES] * from_start[:, ls])
        y = jnp.concatenate(ys, axis=1) + xs * dskip_ref[...]
        y = y * _silu(z_ref[sl, :])
        outs = []
        for g in range(SSM_GROUPS):
            yg = y[:, g * gw:(g + 1) * gw]
            ms_ = jnp.mean(yg * yg, axis=-1, keepdims=True)
            outs.append(yg * lax.rsqrt(ms_ + EPS))
        o_ref[sl, :] = (jnp.concatenate(outs, axis=1) * nw_ref[...]).astype(BF16)


def _ssd(proj, dt_raw, conv_w, conv_b, dt_bias, a_log, d_skip_e, ssm_nw, tri, expand):
    T = proj.shape[0]
    cb = min(SSD_CB, T // CHUNK)
    rows = cb * CHUNK
    xbc0 = (4 * RET_WIDTH + SSM_WIDTH) // XBC_BLK
    zcol = 4 * RET_WIDTH // SSM_WIDTH
    full = lambda shape: pl.BlockSpec(shape, lambda c: (0,) * len(shape))
    return pl.pallas_call(
        functools.partial(_ssd_body, cb=cb),
        grid=(T // rows,),
        in_specs=[
            pl.BlockSpec((rows, XBC_BLK), lambda c: (c, xbc0)),
            pl.BlockSpec((rows, XBC_BLK), lambda c: (c, xbc0 + 1)),
            pl.BlockSpec((rows, XBC_BLK), lambda c: (c, xbc0 + 2)),
            pl.BlockSpec((rows, SSM_WIDTH), lambda c: (c, zcol)),
            pl.BlockSpec((rows, LANES), lambda c: (c, 0)),
            full((SSM_CONV, SSM_CONV_DIM)),
            full((1, SSM_CONV_DIM)),
            full((1, LANES)),
            full((1, LANES)),
            full((1, SSM_WIDTH)),
            full((1, SSM_WIDTH)),
            full((CHUNK, CHUNK)),
            full((LANES, SSM_WIDTH)),
        ],
        out_specs=pl.BlockSpec((rows, SSM_WIDTH), lambda c: (c, 0)),
        out_shape=jax.ShapeDtypeStruct((T, SSM_WIDTH), BF16),
        scratch_shapes=[
            pltpu.VMEM((CHUNK + CONV_PAD, SSM_CONV_DIM), F32),
            pltpu.VMEM((SSM_GROUPS, SSM_STATE, SSM_WIDTH // SSM_GROUPS), F32),
        ],
        compiler_params=_cparams(("arbitrary",)),
        name="ssd",
    )(proj, proj, proj, proj, dt_raw, conv_w, conv_b, dt_bias, a_log, d_skip_e, ssm_nw, tri, expand)


OUT_TM = 256
ROUTE_ROWS = 8 + N_EXPERTS


def _out_router_body(x_ref, yr_ref, ys_ref, wo_ref, nw_ref, wr_ref, br_ref,
                     h1_ref, h2_ref, ids_ref):
    tm = x_ref.shape[0]
    h1 = x_ref[...] + _dot(yr_ref[...], wo_ref[0:RET_WIDTH, :]) + _dot(ys_ref[...], wo_ref[RET_WIDTH:, :])
    h1_ref[...] = h1
    ms = jnp.mean(h1 * h1, axis=-1, keepdims=True)
    h2 = h1 * lax.rsqrt(ms + EPS) * nw_ref[...]
    h2_ref[:, 0:D_MODEL] = h2

    logits = _dot_nt(wr_ref[...], h2.astype(BF16)) + br_ref[...]
    row = lax.broadcasted_iota(I32, (8, tm), 0)
    lg = jnp.where(row < N_GROUPS, logits[0:8], -jnp.inf)
    m = jnp.max(lg, axis=0, keepdims=True)
    p_sel = 1.0 / jnp.sum(jnp.exp(lg - m), axis=0, keepdims=True)
    g_sel = jnp.min(jnp.where(lg == m, row, 8), axis=0, keepdims=True)
    le = jnp.zeros((GROUP_EXPERTS, tm), F32)
    for g in range(N_GROUPS):
        le = jnp.where(g_sel == g, logits[8 + g * GROUP_EXPERTS:8 + (g + 1) * GROUP_EXPERTS], le)
    m2 = jnp.max(le, axis=0, keepdims=True)
    ee = jnp.exp(le - m2)
    pe = ee / jnp.sum(ee, axis=0, keepdims=True)
    v1 = jnp.max(pe, axis=0, keepdims=True)
    i1 = jnp.min(jnp.where(pe == v1, row, 8), axis=0, keepdims=True)
    pe2 = jnp.where(row == i1, -1.0, pe)
    v2 = jnp.max(pe2, axis=0, keepdims=True)
    i2 = jnp.min(jnp.where(pe2 == v2, row, 8), axis=0, keepdims=True)
    tw = v1 + v2
    c1 = v1 / tw * p_sel
    c2 = v2 / tw * p_sel
    e1 = g_sel * GROUP_EXPERTS + i1
    e2 = g_sel * GROUP_EXPERTS + i2
    ids = jnp.where(row == 0, e1, jnp.where(row == 1, e2, 0))
    for b in range(tm // SORT_BLK):
        ids_ref[b] = ids[:, b * SORT_BLK:(b + 1) * SORT_BLK]
    aux8 = jnp.where(row == 0, c1, jnp.where(row == 1, c2, jnp.where(
        row == 2, e1.astype(F32), jnp.where(row == 3, e2.astype(F32), 0.0))))
    aux = jnp.concatenate([aux8, jnp.zeros((AUX - 8, tm), F32)], axis=0)
    h2_ref[:, D_MODEL:ROW_W] = aux.T


def _out_router(x, y_ret, y_ssm, w_out, nw, wr_t, br):
    T = x.shape[0]
    tm = min(OUT_TM, T)
    full = lambda shape: pl.BlockSpec(shape, lambda i: (0,) * len(shape))
    return pl.pallas_call(
        _out_router_body,
        grid=(T // tm,),
        in_specs=[
            pl.BlockSpec((tm, D_MODEL), lambda i: (i, 0)),
            pl.BlockSpec((tm, RET_WIDTH), lambda i: (i, 0)),
            pl.BlockSpec((tm, SSM_WIDTH), lambda i: (i, 0)),
            full((D_MODEL, D_MODEL)),
            full((1, D_MODEL)),
            full((LANES, D_MODEL)),
            full((LANES, 1)),
        ],
        out_specs=[
            pl.BlockSpec((tm, D_MODEL), lambda i: (i, 0)),
            pl.BlockSpec((tm, ROW_W), lambda i: (i, 0)),
            pl.BlockSpec((tm // SORT_BLK, 8, SORT_BLK), lambda i: (i, 0, 0)),
        ],
        out_shape=[
            jax.ShapeDtypeStruct((T, D_MODEL), F32),
            jax.ShapeDtypeStruct((T, ROW_W), F32),
            jax.ShapeDtypeStruct((T // SORT_BLK, 8, SORT_BLK), I32),
        ],
        compiler_params=_cparams(("parallel",)),
        name="out_router",
    )(x, y_ret, y_ssm, w_out, nw, wr_t, br)


SORT_BLK = 256
MOE_TM = 256


def _sort_index_body(ids_ref, tri_ref, ltri_ref, dest_ref, cnt_ref, rank_ref):
    nblk = ids_ref.shape[0]
    row_e = lax.broadcasted_iota(I32, (N_EXPERTS, SORT_BLK), 0)
    row8 = lax.broadcasted_iota(I32, (8, SORT_BLK), 0)

    def onehots(b):
        ids = ids_ref[b]
        return row_e == ids[0:1], row_e == ids[1:2]

    def rank_blk(b, carry):
        oh1, oh2 = onehots(b)
        ohf = jnp.where(oh1 | oh2, 1.0, 0.0)
        incl = _dot(ohf.astype(BF16), tri_ref[...])
        base = carry + incl - 1.0
        r1 = jnp.sum(jnp.where(oh1, base, 0.0), axis=0, keepdims=True)
        r2 = jnp.sum(jnp.where(oh2, base, 0.0), axis=0, keepdims=True)
        rank_ref[b] = jnp.where(row8 == 0, r1, jnp.where(row8 == 1, r2, 0.0))
        return carry + jnp.sum(ohf, axis=1, keepdims=True)

    cnt = lax.fori_loop(0, nblk, rank_blk, jnp.zeros((N_EXPERTS, 1), F32))
    cnt_ref[...] = jnp.broadcast_to(cnt, cnt_ref.shape)
    tiles = jnp.floor((cnt + (MOE_TM - 1.0)) / MOE_TM)
    tiles_b = jnp.broadcast_to(tiles, (N_EXPERTS, LANES)).astype(BF16)
    off = _dot(ltri_ref[...], tiles_b)[:, 0:1] * MOE_TM

    def dest_blk(b, carry):
        oh1, oh2 = onehots(b)
        o1 = jnp.sum(jnp.where(oh1, off, 0.0), axis=0, keepdims=True)
        o2 = jnp.sum(jnp.where(oh2, off, 0.0), axis=0, keepdims=True)
        d = rank_ref[b] + jnp.where(row8 == 0, o1, jnp.where(row8 == 1, o2, 0.0))
        dest_ref[b] = d.astype(I32)
        return carry

    lax.fori_loop(0, nblk, dest_blk, 0)


def _sort_index(ids, tri_u, ltri):
    nblk = ids.shape[0]
    return pl.pallas_call(
        _sort_index_body,
        out_shape=[jax.ShapeDtypeStruct((nblk, 8, SORT_BLK), I32),
                   jax.ShapeDtypeStruct((N_EXPERTS, LANES), F32)],
        scratch_shapes=[pltpu.VMEM((nblk, 8, SORT_BLK), F32)],
        compiler_params=_cparams(None),
        name="sort_index",
    )(ids, tri_u, ltri)


DISPATCH_TM = 256
DMA_UNROLL = 8


def _dispatch_body(dest_ref, pad_ref, h2_ref, xs_ref, zero_ref, sem):
    tm = h2_ref.shape[0]
    T = dest_ref.shape[0] // 2
    base = pl.program_id(0) * tm

    def row_copy(r, k):
        return pltpu.make_async_copy(h2_ref.at[pl.ds(r, 1), :],
                                     xs_ref.at[pl.ds(dest_ref[k * T + base + r], 1), :], sem)

    def pad_copy(i):
        row0 = pl.multiple_of(i * MOE_TM, MOE_TM)
        return pltpu.make_async_copy(zero_ref, xs_ref.at[pl.ds(row0, MOE_TM), :], sem)

    @pl.when(pl.program_id(0) == 0)
    def _():
        zero_ref[...] = jnp.zeros_like(zero_ref)
        n_tiles = xs_ref.shape[0] // MOE_TM

        def pad_start(i, c):
            @pl.when(pad_ref[i] == 1)
            def _():
                pad_copy(i).start()
            return c

        def pad_wait(i, c):
            @pl.when(pad_ref[i] == 1)
            def _():
                pad_copy(i).wait()
            return c

        lax.fori_loop(0, n_tiles, pad_start, 0)
        lax.fori_loop(0, n_tiles, pad_wait, 0)

    def start(r, c):
        row_copy(r, 0).start()
        row_copy(r, 1).start()
        return c

    def wait(r, c):
        row_copy(r, 0).wait()
        row_copy(r, 1).wait()
        return c

    lax.fori_loop(0, tm, start, 0, unroll=DMA_UNROLL)
    lax.fori_loop(0, tm, wait, 0, unroll=DMA_UNROLL)


def _dispatch(dest, tile_pad, h2aug, n_rows):
    T = h2aug.shape[0]
    tm = min(DISPATCH_TM, T)
    grid_spec = pltpu.PrefetchScalarGridSpec(
        num_scalar_prefetch=2,
        grid=(T // tm,),
        in_specs=[pl.BlockSpec((tm, ROW_W), lambda i, d, p: (i, 0))],
        out_specs=pl.BlockSpec(memory_space=pl.ANY),
        scratch_shapes=[pltpu.VMEM((MOE_TM, ROW_W), F32), pltpu.SemaphoreType.DMA(())],
    )
    return pl.pallas_call(
        _dispatch_body,
        grid_spec=grid_spec,
        out_shape=jax.ShapeDtypeStruct((n_rows, ROW_W), F32),
        compiler_params=_cparams(("arbitrary",)),
        name="dispatch",
    )(dest, tile_pad, h2aug)


def _moe_body(te_ref, tb_ref, tf_ref, nu_ref, x_ref, wg_ref, wu_ref, wd_ref, o_ref,
              wgb_ref, wub_ref, wdb_ref):
    i = pl.program_id(0)

    @pl.when(tf_ref[i] == 1)
    def _():
        wgb_ref[...] = wg_ref[0].astype(BF16)
        wub_ref[...] = wu_ref[0].astype(BF16)
        wdb_ref[...] = wd_ref[0].astype(BF16)

    @pl.when(i < nu_ref[0])
    def _():
        xb = x_ref[:, 0:D_MODEL].astype(BF16)
        aux = x_ref[:, D_MODEL:ROW_W]
        first_slot = aux[:, 2:3].astype(I32) == te_ref[i]
        comb = jnp.where(first_slot, aux[:, 0:1], aux[:, 1:2])
        a = _dot(xb, wgb_ref[...])
        u = _dot(xb, wub_ref[...])
        act = (_silu(a) * u * comb).astype(BF16)
        o_ref[...] = _dot(act, wdb_ref[...])

    @pl.when(i >= nu_ref[0])
    def _():
        o_ref[...] = jnp.zeros_like(o_ref)


def _moe(tile_expert, tile_block, tile_first, n_used, x_sorted, w_gate, w_up, w_down):
    n_tiles = x_sorted.shape[0] // MOE_TM
    wspec = lambda shape: pl.BlockSpec((1,) + shape, lambda i, te, tb, tf, nu: (te[i], 0, 0))
    grid_spec = pltpu.PrefetchScalarGridSpec(
        num_scalar_prefetch=4,
        grid=(n_tiles,),
        in_specs=[
            pl.BlockSpec((MOE_TM, ROW_W), lambda i, te, tb, tf, nu: (tb[i], 0)),
            wspec((D_MODEL, D_EXPERT)),
            wspec((D_MODEL, D_EXPERT)),
            wspec((D_EXPERT, D_MODEL)),
        ],
        out_specs=pl.BlockSpec((MOE_TM, D_MODEL), lambda i, te, tb, tf, nu: (i, 0)),
        scratch_shapes=[pltpu.VMEM((D_MODEL, D_EXPERT), BF16),
                        pltpu.VMEM((D_MODEL, D_EXPERT), BF16),
                        pltpu.VMEM((D_EXPERT, D_MODEL), BF16)],
    )
    return pl.pallas_call(
        _moe_body,
        grid_spec=grid_spec,
        out_shape=jax.ShapeDtypeStruct((x_sorted.shape[0], D_MODEL), F32),
        compiler_params=_cparams(("arbitrary",)),
        name="moe",
    )(tile_expert, tile_block, tile_first, n_used, x_sorted, w_gate, w_up, w_down)


COMB_TM = 256


def _combine_body(dest_ref, h1_ref, ys_ref, nw_ref, o_ref, buf_ref, sem):
    tm = h1_ref.shape[0]
    T = dest_ref.shape[0] // 2
    base = pl.program_id(0) * tm

    def row_copy(i, k):
        return pltpu.make_async_copy(ys_ref.at[pl.ds(dest_ref[k * T + base + i], 1), :],
                                     buf_ref.at[k, pl.ds(i, 1), :], sem)

    def start(i, c):
        row_copy(i, 0).start()
        row_copy(i, 1).start()
        return c

    def wait(i, c):
        row_copy(i, 0).wait()
        row_copy(i, 1).wait()
        return c

    lax.fori_loop(0, tm, start, 0, unroll=DMA_UNROLL)
    lax.fori_loop(0, tm, wait, 0, unroll=DMA_UNROLL)
    h = h1_ref[...] + (buf_ref[0] + buf_ref[1])
    ms = jnp.mean(h * h, axis=-1, keepdims=True)
    o_ref[...] = h * lax.rsqrt(ms + EPS) * nw_ref[...]


def _combine(dest, h1, y_sorted, nw):
    T = h1.shape[0]
    tm = min(COMB_TM, T)
    grid_spec = pltpu.PrefetchScalarGridSpec(
        num_scalar_prefetch=1,
        grid=(T // tm,),
        in_specs=[
            pl.BlockSpec((tm, D_MODEL), lambda i, d: (i, 0)),
            pl.BlockSpec(memory_space=pl.ANY),
            pl.BlockSpec((1, D_MODEL), lambda i, d: (0, 0)),
        ],
        out_specs=pl.BlockSpec((tm, D_MODEL), lambda i, d: (i, 0)),
        scratch_shapes=[pltpu.VMEM((2, tm, D_MODEL), F32), pltpu.SemaphoreType.DMA(())],
    )
    return pl.pallas_call(
        _combine_body,
        grid_spec=grid_spec,
        out_shape=jax.ShapeDtypeStruct((T, D_MODEL), F32),
        compiler_params=_cparams(("arbitrary",)),
        name="combine",
    )(dest, h1, y_sorted, nw)


def _tile_plan(cnt, n_tiles):
    tiles = (cnt + (MOE_TM - 1)) // MOE_TM
    ends = jnp.cumsum(tiles)
    n_used = ends[-1]
    tile_idx = jnp.arange(n_tiles, dtype=I32)
    last = jnp.maximum(n_used - 1, 0)
    tile_block = jnp.minimum(tile_idx, last)
    tile_expert = jnp.sum((ends[None, :] <= tile_block[:, None]).astype(I32), axis=1)
    starts = ends - tiles
    tile_first = ((tile_idx == starts[tile_expert]) & (tile_idx < n_used)).astype(I32)
    tile_last = tile_idx == (ends[tile_expert] - 1)
    tile_pad = (tile_last | (tile_idx >= n_used)).astype(I32)
    return tile_expert, tile_block.astype(I32), tile_first, n_used.reshape(1).astype(I32), tile_pad


def kernel(x, positions, norm1_w, w_in, conv_w, conv_b, dt_bias, a_log, d_skip, ret_norm_w,
           ssm_norm_w, w_out, norm2_w, w_router_group, b_router_group, w_router_expert,
           b_router_expert, w_expert_gate, w_expert_up, w_expert_down, final_norm_w):
    B, T, D = x.shape
    assert B == 1 and D == D_MODEL and T % CHUNK == 0
    xf = x.reshape(T, D)
    pad_l = lambda v: jnp.pad(v, ((0, 0), (0, LANES - v.shape[-1])))

    w_main = w_in[0][:, :MAIN_PROJ].astype(BF16)
    w_dt = pad_l(w_in[0][:, MAIN_PROJ:]).astype(BF16)
    half = RET_HEAD_DIM // 2
    inv = (ROPE_THETA ** (-jnp.arange(half, dtype=F32) / half)).reshape(1, half)
    tri = (jnp.arange(CHUNK)[:, None] >= jnp.arange(CHUNK)[None, :]).astype(BF16)
    expand = (jnp.arange(LANES)[:, None] == jnp.arange(SSM_WIDTH)[None, :] // SSM_HEAD_DIM).astype(BF16)
    d_skip_e = jnp.repeat(d_skip[0], SSM_HEAD_DIM).reshape(1, SSM_WIDTH)
    wr_t = jnp.zeros((LANES, D), F32)
    wr_t = wr_t.at[0:N_GROUPS].set(w_router_group[0].T).at[8:8 + N_EXPERTS].set(w_router_expert[0].T)
    br = jnp.zeros((LANES,), F32)
    br = br.at[0:N_GROUPS].set(b_router_group[0]).at[8:8 + N_EXPERTS].set(b_router_expert[0])
    tri_u = (jnp.arange(SORT_BLK)[:, None] <= jnp.arange(SORT_BLK)[None, :]).astype(BF16)
    ltri = (jnp.arange(N_EXPERTS)[:, None] > jnp.arange(N_EXPERTS)[None, :]).astype(BF16)

    proj, dt_raw = _inproj(xf, norm1_w[0].reshape(1, D), w_main, w_dt)
    cos, sin = _rope_tables(positions.reshape(T, 1).astype(F32), inv)
    y_ret = _retention(proj, cos, sin, ret_norm_w[0].reshape(1, RET_WIDTH), _retention_consts())
    y_ssm = _ssd(proj, dt_raw, conv_w[0], conv_b[0].reshape(1, -1), pad_l(dt_bias[0].reshape(1, -1)),
                 pad_l(a_log[0].reshape(1, -1)), d_skip_e, ssm_norm_w[0].reshape(1, -1), tri, expand)
    h1, h2aug, ids = _out_router(xf, y_ret, y_ssm, w_out[0].astype(BF16), norm2_w[0].reshape(1, D),
                                 wr_t.astype(BF16), br.reshape(LANES, 1))

    dest_blk, cnt = _sort_index(ids, tri_u, ltri)
    dest = dest_blk[:, 0:2, :].transpose(1, 0, 2).reshape(2 * T)
    n_tiles = (2 * T) // MOE_TM + N_EXPERTS
    tile_expert, tile_block, tile_first, n_used, tile_pad = _tile_plan(cnt[:, 0].astype(I32), n_tiles)
    x_sorted = _dispatch(dest, tile_pad, h2aug, n_tiles * MOE_TM)
    y_sorted = _moe(tile_expert, tile_block, tile_first, n_used, x_sorted,
                    w_expert_gate[0], w_expert_up[0], w_expert_down[0])
    out = _combine(dest, h1, y_sorted, final_norm_w.reshape(1, D))
    return out.reshape(B, T, D)
```

```python
import functools

import jax
import jax.numpy as jnp
from jax import lax
from jax.experimental import pallas as pl
from jax.experimental.pallas import tpu as pltpu

F32 = jnp.float32
BF16 = jnp.bfloat16
I32 = jnp.int32

D_MODEL = 2048
EPS = 1e-6
CHUNK = 128
RET_HEADS = 4
RET_HEAD_DIM = 256
RET_WIDTH = RET_HEADS * RET_HEAD_DIM
ROPE_THETA = 10000.0
SSM_WIDTH = 1024
SSM_HEAD_DIM = 64
SSM_HEADS = SSM_WIDTH // SSM_HEAD_DIM
SSM_GROUPS = 2
SSM_STATE = 128
SSM_CONV = 4
SSM_CONV_DIM = SSM_WIDTH + 2 * SSM_GROUPS * SSM_STATE
MAIN_PROJ = 4 * RET_WIDTH + SSM_WIDTH + SSM_CONV_DIM
N_GROUPS = 4
GROUP_EXPERTS = 8
N_EXPERTS = N_GROUPS * GROUP_EXPERTS
D_EXPERT = 512
LANES = 128
AUX = LANES
ROW_W = D_MODEL + AUX

VMEM_LIMIT = 56 * 1024 * 1024


def _cparams(sem, vmem=VMEM_LIMIT):
    return pltpu.CompilerParams(dimension_semantics=sem, vmem_limit_bytes=vmem)


def _silu(x):
    return x * (1.0 / (1.0 + jnp.exp(-x)))


def _dot(a, b):
    return jnp.dot(a, b, preferred_element_type=F32)


def _dot_nt(a, b):
    return lax.dot_general(a, b, (((1,), (1,)), ((), ())), preferred_element_type=F32)


def _split3(a):
    a1 = a.astype(BF16)
    r1 = a - a1.astype(F32)
    a2 = r1.astype(BF16)
    a3 = (r1 - a2.astype(F32)).astype(BF16)
    return a1, a2, a3


IN_TM = 1024
IN_TN = 512


def _inproj_body(x_ref, nw_ref, w_ref, wdt_ref, o_ref, odt_ref, hn_ref):
    @pl.when(pl.program_id(1) == 0)
    def _():
        def blk(r, carry):
            sl = pl.ds(pl.multiple_of(r * CHUNK, CHUNK), CHUNK)
            x = x_ref[sl, :]
            ms = jnp.mean(x * x, axis=-1, keepdims=True)
            hn_ref[sl, :] = (x * lax.rsqrt(ms + EPS) * nw_ref[...]).astype(BF16)
            return carry
        lax.fori_loop(0, x_ref.shape[0] // CHUNK, blk, 0)
        odt_ref[...] = _dot(hn_ref[...], wdt_ref[...])

    o_ref[...] = _dot(hn_ref[...], w_ref[...])


def _inproj(x, nw, w_main, w_dt):
    T = x.shape[0]
    tm = min(IN_TM, T)
    return pl.pallas_call(
        _inproj_body,
        grid=(T // tm, MAIN_PROJ // IN_TN),
        in_specs=[
            pl.BlockSpec((tm, D_MODEL), lambda i, j: (i, 0)),
            pl.BlockSpec((1, D_MODEL), lambda i, j: (0, 0)),
            pl.BlockSpec((D_MODEL, IN_TN), lambda i, j: (0, j)),
            pl.BlockSpec((D_MODEL, LANES), lambda i, j: (0, 0)),
        ],
        out_specs=[
            pl.BlockSpec((tm, IN_TN), lambda i, j: (i, j)),
            pl.BlockSpec((tm, LANES), lambda i, j: (i, 0)),
        ],
        out_shape=[
            jax.ShapeDtypeStruct((T, MAIN_PROJ), F32),
            jax.ShapeDtypeStruct((T, LANES), F32),
        ],
        scratch_shapes=[pltpu.VMEM((tm, D_MODEL), BF16)],
        compiler_params=_cparams(("parallel", "arbitrary")),
        name="in_proj",
    )(x, nw, w_main, w_dt)


def _rope_body(pos_ref, inv_ref, cos_ref, sin_ref):
    ang = pos_ref[...] * inv_ref[...]
    cos_ref[...] = jnp.cos(ang)
    sin_ref[...] = jnp.sin(ang)


def _rope_tables(pos, inv):
    T = pos.shape[0]
    tm = min(1024, T)
    half = inv.shape[1]
    return pl.pallas_call(
        _rope_body,
        grid=(T // tm,),
        in_specs=[pl.BlockSpec((tm, 1), lambda i: (i, 0)),
                  pl.BlockSpec((1, half), lambda i: (0, 0))],
        out_specs=[pl.BlockSpec((tm, half), lambda i: (i, 0))] * 2,
        out_shape=[jax.ShapeDtypeStruct((T, half), F32)] * 2,
        compiler_params=_cparams(("parallel",)),
        name="rope_tables",
    )(pos, inv)


RET_CB = 2


def _retention_body(cd_ref, q_ref, k_ref, v_ref, g_ref, cos_ref, sin_ref,
                    dec_ref, wq_ref, ws_ref, nw_ref, o_ref, st_ref, *, cb):
    h = pl.program_id(0)

    @pl.when(pl.program_id(1) == 0)
    def _():
        st_ref[...] = jnp.zeros_like(st_ref)

    chunk_decay = cd_ref[h]
    half = RET_HEAD_DIM // 2
    for i in range(cb):
        sl = slice(i * CHUNK, (i + 1) * CHUNK)
        cos = cos_ref[sl, :]
        sin = sin_ref[sl, :]

        def rope(x):
            x1, x2 = x[:, :half], x[:, half:]
            return jnp.concatenate([x1 * cos - x2 * sin, x2 * cos + x1 * sin], axis=-1)

        q = rope(q_ref[sl, :])
        k = rope(k_ref[sl, :]) * (RET_HEAD_DIM ** -0.5)
        v = v_ref[sl, :]
        qb = q.astype(BF16)
        kb = k.astype(BF16)
        scores = _dot_nt(qb, kb) * dec_ref[0]
        y = _dot(scores.astype(BF16), v.astype(BF16))
        st = st_ref[...]
        y = y + _dot(qb, st.astype(BF16)) * wq_ref[0]
        vw = (v * ws_ref[0]).astype(BF16)
        new = _dot(k.T.astype(BF16), vw)
        st_ref[...] = st * chunk_decay + new
        mu = jnp.mean(y, axis=-1, keepdims=True)
        d = y - mu
        var = jnp.mean(d * d, axis=-1, keepdims=True)
        yn = d * lax.rsqrt(var + EPS)
        o_ref[sl, :] = (yn * nw_ref[...] * _silu(g_ref[sl, :])).astype(BF16)


def _retention(proj, cos, sin, ret_nw, consts):
    T = proj.shape[0]
    cb = min(RET_CB, T // CHUNK)
    rows = cb * CHUNK
    chunk_decay, decay_intra, w_query, w_state = consts
    hd = RET_HEAD_DIM
    half = hd // 2

    def col(base):
        return lambda h, c, cd: (c, base + h)

    grid_spec = pltpu.PrefetchScalarGridSpec(
        num_scalar_prefetch=1,
        grid=(RET_HEADS, T // rows),
        in_specs=[
            pl.BlockSpec((rows, hd), col(0)),
            pl.BlockSpec((rows, hd), col(RET_HEADS)),
            pl.BlockSpec((rows, hd), col(2 * RET_HEADS)),
            pl.BlockSpec((rows, hd), col(3 * RET_HEADS)),
            pl.BlockSpec((rows, half), lambda h, c, cd: (c, 0)),
            pl.BlockSpec((rows, half), lambda h, c, cd: (c, 0)),
            pl.BlockSpec((1, CHUNK, CHUNK), lambda h, c, cd: (h, 0, 0)),
            pl.BlockSpec((1, CHUNK, hd), lambda h, c, cd: (h, 0, 0)),
            pl.BlockSpec((1, CHUNK, hd), lambda h, c, cd: (h, 0, 0)),
            pl.BlockSpec((1, hd), lambda h, c, cd: (0, h)),
        ],
        out_specs=pl.BlockSpec((rows, hd), lambda h, c, cd: (c, h)),
        scratch_shapes=[pltpu.VMEM((hd, hd), F32)],
    )
    return pl.pallas_call(
        functools.partial(_retention_body, cb=cb),
        grid_spec=grid_spec,
        out_shape=jax.ShapeDtypeStruct((T, RET_WIDTH), BF16),
        compiler_params=_cparams(("parallel", "arbitrary")),
        name="retention",
    )(chunk_decay, proj, proj, proj, proj, cos, sin, decay_intra, w_query, w_state, ret_nw)


def _retention_consts():
    H, L = RET_HEADS, CHUNK
    log_gamma = jnp.log1p(-(2.0 ** (-5.0 - jnp.arange(H, dtype=F32))))
    idx = jnp.arange(L, dtype=F32)
    diff = idx[:, None] - idx[None, :]
    causal = diff >= 0
    decay_intra = jnp.where(causal[None], jnp.exp(jnp.where(causal, diff, 0.0)[None] * log_gamma[:, None, None]), 0.0)
    w_state = jnp.exp((L - 1.0 - idx)[None, :] * log_gamma[:, None])
    w_query = jnp.exp((idx + 1.0)[None, :] * log_gamma[:, None])
    chunk_decay = jnp.exp(L * log_gamma)
    bc = lambda w: jnp.broadcast_to(w[:, :, None], (H, L, RET_HEAD_DIM))
    return chunk_decay, decay_intra, bc(w_query), bc(w_state)


SSD_CB = 2
CONV_PAD = 8
XBC_BLK = 512


def _ssd_body(x0_ref, x1_ref, x2_ref, z_ref, dt_ref, cw_ref, cbias_ref, dtb_ref,
              alog_ref, dskip_ref, nw_ref, tri_ref, exp_ref, o_ref, xpad_ref, st_ref, *, cb):
    @pl.when(pl.program_id(0) == 0)
    def _():
        xpad_ref[0:CONV_PAD, :] = jnp.zeros((CONV_PAD, SSM_CONV_DIM), F32)
        st_ref[...] = jnp.zeros_like(st_ref)

    L = CHUNK
    gw = SSM_WIDTH // SSM_GROUPS
    rows_i = lax.broadcasted_iota(I32, (L, L), 0)
    cols_i = lax.broadcasted_iota(I32, (L, L), 1)
    causal = rows_i >= cols_i
    lo_lane = lax.broadcasted_iota(I32, (L, LANES), 1) < SSM_HEAD_DIM
    tri = tri_ref[...]
    expand = exp_ref[...]
    a_neg = -jnp.exp(alog_ref[...])

    for i in range(cb):
        sl = slice(i * L, (i + 1) * L)
        for b, xr in enumerate((x0_ref, x1_ref, x2_ref)):
            xpad_ref[CONV_PAD:CONV_PAD + L, b * XBC_BLK:(b + 1) * XBC_BLK] = xr[sl, :]
        u_parts = []
        for b in range(SSM_CONV_DIM // XBC_BLK):
            cs = slice(b * XBC_BLK, (b + 1) * XBC_BLK)
            acc = cbias_ref[:, cs]
            for t in range(SSM_CONV):
                r0 = CONV_PAD - (SSM_CONV - 1) + t
                acc = acc + xpad_ref[r0:r0 + L, cs] * cw_ref[t:t + 1, cs]
            u_parts.append(_silu(acc))
        xpad_ref[0:CONV_PAD, :] = xpad_ref[L:L + CONV_PAD, :]
        xs = jnp.concatenate(u_parts[:2], axis=-1)
        bm = u_parts[2][:, :SSM_GROUPS * SSM_STATE]
        cm = u_parts[2][:, SSM_GROUPS * SSM_STATE:]

        dt_in = dt_ref[sl, :] + dtb_ref[...]
        dt = jnp.maximum(dt_in, 0.0) + jnp.log1p(jnp.exp(-jnp.abs(dt_in)))
        a = dt * a_neg
        a1, a2, a3 = _split3(a)
        a_cs = _dot(tri, a1) + _dot(tri, a2) + _dot(tri, a3)
        d1, d2, d3 = _split3(dt)
        dt_e = _dot(d1, expand) + _dot(d2, expand) + _dot(d3, expand)
        c1, c2, c3 = _split3(a_cs)
        acs_e = _dot(c1, expand) + _dot(c2, expand) + _dot(c3, expand)
        last = acs_e[L - 1:L, :]
        to_end = jnp.exp(last - acs_e)
        from_start = jnp.exp(acs_e)
        chunk_decay = jnp.exp(last)
        x_dt = xs * dt_e
        xw = (x_dt * to_end).astype(BF16)
        acs_t = a_cs.T
        cmb = cm.astype(BF16)
        bmb = bm.astype(BF16)

        ys = []
        for g in range(SSM_GROUPS):
            ns = slice(g * SSM_STATE, (g + 1) * SSM_STATE)
            gs = slice(g * gw, (g + 1) * gw)
            cg = cmb[:, ns]
            cbg = _dot_nt(cg, bmb[:, ns])
            st = st_ref[g]
            y_off = _dot(cg, st.astype(BF16))
            new = _dot(bm[:, ns].T.astype(BF16), xw[:, gs])
            st_ref[g] = st * chunk_decay[:, gs] + new
            for jp in range(gw // LANES):
                h0 = (g * gw + jp * LANES) // SSM_HEAD_DIM
                ms = []
                for hh in (h0, h0 + 1):
                    seg = a_cs[:, hh:hh + 1] - acs_t[hh:hh + 1, :]
                    dec = jnp.exp(jnp.where(causal, seg, -jnp.inf))
                    ms.append((cbg * dec).astype(BF16))
                lhs = jnp.concatenate(ms, axis=1)
                ls = slice(g * gw + jp * LANES, g * gw + (jp + 1) * LANES)
                xp = x_dt[:, ls]
                rhs = jnp.concatenate([jnp.where(lo_lane, xp, 0.0),
                                       jnp.where(lo_lane, 0.0, xp)], axis=0).astype(BF16)
                y_diag = _dot(lhs, rhs)
                ys.append(y_diag + y_off[:, jp * LANES:(jp + 1) * LANES] * from_start[:, ls])
        y = jnp.concatenate(ys, axis=1) + xs * dskip_ref[...]
        y = y * _silu(z_ref[sl, :])
        outs = []
        for g in range(SSM_GROUPS):
            yg = y[:, g * gw:(g + 1) * gw]
            ms_ = jnp.mean(yg * yg, axis=-1, keepdims=True)
            outs.append(yg * lax.rsqrt(ms_ + EPS))
        o_ref[sl, :] = (jnp.concatenate(outs, axis=1) * nw_ref[...]).astype(BF16)


def _ssd(proj, dt_raw, conv_w, conv_b, dt_bias, a_log, d_skip_e, ssm_nw, tri, expand):
    T = proj.shape[0]
    cb = min(SSD_CB, T // CHUNK)
    rows = cb * CHUNK
    xbc0 = (4 * RET_WIDTH + SSM_WIDTH) // XBC_BLK
    zcol = 4 * RET_WIDTH // SSM_WIDTH
    full = lambda shape: pl.BlockSpec(shape, lambda c: (0,) * len(shape))
    return pl.pallas_call(
        functools.partial(_ssd_body, cb=cb),
        grid=(T // rows,),
        in_specs=[
            pl.BlockSpec((rows, XBC_BLK), lambda c: (c, xbc0)),
            pl.BlockSpec((rows, XBC_BLK), lambda c: (c, xbc0 + 1)),
            pl.BlockSpec((rows, XBC_BLK), lambda c: (c, xbc0 + 2)),
            pl.BlockSpec((rows, SSM_WIDTH), lambda c: (c, zcol)),
            pl.BlockSpec((rows, LANES), lambda c: (c, 0)),
            full((SSM_CONV, SSM_CONV_DIM)),
            full((1, SSM_CONV_DIM)),
            full((1, LANES)),
            full((1, LANES)),
            full((1, SSM_WIDTH)),
            full((1, SSM_WIDTH)),
            full((CHUNK, CHUNK)),
            full((LANES, SSM_WIDTH)),
        ],
        out_specs=pl.BlockSpec((rows, SSM_WIDTH), lambda c: (c, 0)),
        out_shape=jax.ShapeDtypeStruct((T, SSM_WIDTH), BF16),
        scratch_shapes=[
            pltpu.VMEM((CHUNK + CONV_PAD, SSM_CONV_DIM), F32),
            pltpu.VMEM((SSM_GROUPS, SSM_STATE, SSM_WIDTH // SSM_GROUPS), F32),
        ],
        compiler_params=_cparams(("arbitrary",)),
        name="ssd",
    )(proj, proj, proj, proj, dt_raw, conv_w, conv_b, dt_bias, a_log, d_skip_e, ssm_nw, tri, expand)


OUT_TM = 256
ROUTE_ROWS = 8 + N_EXPERTS


def _out_router_body(x_ref, yr_ref, ys_ref, wo_ref, nw_ref, wr_ref, br_ref,
                     h1_ref, h2_ref, ids_ref):
    tm = x_ref.shape[0]
    h1 = x_ref[...] + _dot(yr_ref[...], wo_ref[0:RET_WIDTH, :]) + _dot(ys_ref[...], wo_ref[RET_WIDTH:, :])
    h1_ref[...] = h1
    ms = jnp.mean(h1 * h1, axis=-1, keepdims=True)
    h2 = h1 * lax.rsqrt(ms + EPS) * nw_ref[...]
    h2_ref[:, 0:D_MODEL] = h2

    logits = _dot_nt(wr_ref[...], h2.astype(BF16)) + br_ref[...]
    row = lax.broadcasted_iota(I32, (8, tm), 0)
    lg = jnp.where(row < N_GROUPS, logits[0:8], -jnp.inf)
    m = jnp.max(lg, axis=0, keepdims=True)
    p_sel = 1.0 / jnp.sum(jnp.exp(lg - m), axis=0, keepdims=True)
    g_sel = jnp.min(jnp.where(lg == m, row, 8), axis=0, keepdims=True)
    le = jnp.zeros((GROUP_EXPERTS, tm), F32)
    for g in range(N_GROUPS):
        le = jnp.where(g_sel == g, logits[8 + g * GROUP_EXPERTS:8 + (g + 1) * GROUP_EXPERTS], le)
    m2 = jnp.max(le, axis=0, keepdims=True)
    ee = jnp.exp(le - m2)
    pe = ee / jnp.sum(ee, axis=0, keepdims=True)
    v1 = jnp.max(pe, axis=0, keepdims=True)
    i1 = jnp.min(jnp.where(pe == v1, row, 8), axis=0, keepdims=True)
    pe2 = jnp.where(row == i1, -1.0, pe)
    v2 = jnp.max(pe2, axis=0, keepdims=True)
    i2 = jnp.min(jnp.where(pe2 == v2, row, 8), axis=0, keepdims=True)
    tw = v1 + v2
    c1 = v1 / tw * p_sel
    c2 = v2 / tw * p_sel
    e1 = g_sel * GROUP_EXPERTS + i1
    e2 = g_sel * GROUP_EXPERTS + i2
    ids = jnp.where(row == 0, e1, jnp.where(row == 1, e2, 0))
    for b in range(tm // SORT_BLK):
        ids_ref[b] = ids[:, b * SORT_BLK:(b + 1) * SORT_BLK]
    aux8 = jnp.where(row == 0, c1, jnp.where(row == 1, c2, jnp.where(
        row == 2, e1.astype(F32), jnp.where(row == 3, e2.astype(F32), 0.0))))
    aux = jnp.concatenate([aux8, jnp.zeros((AUX - 8, tm), F32)], axis=0)
    h2_ref[:, D_MODEL:ROW_W] = aux.T


def _out_router(x, y_ret, y_ssm, w_out, nw, wr_t, br):
    T = x.shape[0]
    tm = min(OUT_TM, T)
    full = lambda shape: pl.BlockSpec(shape, lambda i: (0,) * len(shape))
    return pl.pallas_call(
        _out_router_body,
        grid=(T // tm,),
        in_specs=[
            pl.BlockSpec((tm, D_MODEL), lambda i: (i, 0)),
            pl.BlockSpec((tm, RET_WIDTH), lambda i: (i, 0)),
            pl.BlockSpec((tm, SSM_WIDTH), lambda i: (i, 0)),
            full((D_MODEL, D_MODEL)),
            full((1, D_MODEL)),
            full((LANES, D_MODEL)),
            full((LANES, 1)),
        ],
        out_specs=[
            pl.BlockSpec((tm, D_MODEL), lambda i: (i, 0)),
            pl.BlockSpec((tm, ROW_W), lambda i: (i, 0)),
            pl.BlockSpec((tm // SORT_BLK, 8, SORT_BLK), lambda i: (i, 0, 0)),
        ],
        out_shape=[
            jax.ShapeDtypeStruct((T, D_MODEL), F32),
            jax.ShapeDtypeStruct((T, ROW_W), F32),
            jax.ShapeDtypeStruct((T // SORT_BLK, 8, SORT_BLK), I32),
        ],
        compiler_params=_cparams(("parallel",)),
        name="out_router",
    )(x, y_ret, y_ssm, w_out, nw, wr_t, br)


SORT_BLK = 256
MOE_TM = 256


def _sort_index_body(ids_ref, tri_ref, ltri_ref, dest_ref, cnt_ref, rank_ref):
    nblk = ids_ref.shape[0]
    row_e = lax.broadcasted_iota(I32, (N_EXPERTS, SORT_BLK), 0)
    row8 = lax.broadcasted_iota(I32, (8, SORT_BLK), 0)

    def onehots(b):
        ids = ids_ref[b]
        return row_e == ids[0:1], row_e == ids[1:2]

    def rank_blk(b, carry):
        oh1, oh2 = onehots(b)
        ohf = jnp.where(oh1 | oh2, 1.0, 0.0)
        incl = _dot(ohf.astype(BF16), tri_ref[...])
        base = carry + incl - 1.0
        r1 = jnp.sum(jnp.where(oh1, base, 0.0), axis=0, keepdims=True)
        r2 = jnp.sum(jnp.where(oh2, base, 0.0), axis=0, keepdims=True)
        rank_ref[b] = jnp.where(row8 == 0, r1, jnp.where(row8 == 1, r2, 0.0))
        return carry + jnp.sum(ohf, axis=1, keepdims=True)

    cnt = lax.fori_loop(0, nblk, rank_blk, jnp.zeros((N_EXPERTS, 1), F32))
    cnt_ref[...] = jnp.broadcast_to(cnt, cnt_ref.shape)
    tiles = jnp.floor((cnt + (MOE_TM - 1.0)) / MOE_TM)
    tiles_b = jnp.broadcast_to(tiles, (N_EXPERTS, LANES)).astype(BF16)
    off = _dot(ltri_ref[...], tiles_b)[:, 0:1] * MOE_TM

    def dest_blk(b, carry):
        oh1, oh2 = onehots(b)
        o1 = jnp.sum(jnp.where(oh1, off, 0.0), axis=0, keepdims=True)
        o2 = jnp.sum(jnp.where(oh2, off, 0.0), axis=0, keepdims=True)
        d = rank_ref[b] + jnp.where(row8 == 0, o1, jnp.where(row8 == 1, o2, 0.0))
        dest_ref[b] = d.astype(I32)
        return carry

    lax.fori_loop(0, nblk, dest_blk, 0)


def _sort_index(ids, tri_u, ltri):
    nblk = ids.shape[0]
    return pl.pallas_call(
        _sort_index_body,
        out_shape=[jax.ShapeDtypeStruct((nblk, 8, SORT_BLK), I32),
                   jax.ShapeDtypeStruct((N_EXPERTS, LANES), F32)],
        scratch_shapes=[pltpu.VMEM((nblk, 8, SORT_BLK), F32)],
        compiler_params=_cparams(None),
        name="sort_index",
    )(ids, tri_u, ltri)


def _sorted_rows_body(dest_ref, lo_ref, hi_ref, gsrc_ref, sdst_ref):
    T = dest_ref.shape[0] // 2

    def tok(t, c):
        r0 = dest_ref[t]
        r1 = dest_ref[T + t]
        gsrc_ref[r0] = t
        sdst_ref[r0] = t
        gsrc_ref[r1] = t
        sdst_ref[r1] = T + t
        return c

    lax.fori_loop(0, T, tok, 0, unroll=8)

    def seg(e, c):
        def pad(r, c2):
            gsrc_ref[r] = 0
            sdst_ref[r] = 2 * T + (r & (MOE_TM - 1))
            return c2

        lax.fori_loop(lo_ref[e], hi_ref[e], pad, 0)
        return c

    lax.fori_loop(0, lo_ref.shape[0], seg, 0)


def _sorted_rows(dest, pad_lo, pad_hi, n_rows):
    smem = pl.BlockSpec(memory_space=pltpu.SMEM)
    return pl.pallas_call(
        _sorted_rows_body,
        in_specs=[smem, smem, smem],
        out_specs=[smem, smem],
        out_shape=[jax.ShapeDtypeStruct((n_rows,), I32)] * 2,
        compiler_params=_cparams(None),
        name="sorted_rows",
    )(dest, pad_lo, pad_hi)


def _moe_body(te_ref, tf_ref, nu_ref, gsrc_ref, sdst_ref, h2_ref, wg_ref, wu_ref, wd_ref, yt_ref,
              xbuf, ystage, wgb_ref, wub_ref, wdb_ref, gsem, ssem):
    i = pl.program_id(0)
    nu = nu_ref[0]
    spare0 = yt_ref.shape[0] - MOE_TM

    def gather_row(tile, slot, r):
        return pltpu.make_async_copy(h2_ref.at[pl.ds(gsrc_ref[tile * MOE_TM + r], 1), :],
                                     xbuf.at[slot, pl.ds(r, 1), :], gsem.at[slot])

    def scatter_row(tile, slot, r):
        return pltpu.make_async_copy(ystage.at[slot, pl.ds(r, 1), :],
                                     yt_ref.at[pl.ds(sdst_ref[tile * MOE_TM + r], 1), :], ssem.at[slot])

    def gather_tile(slot):
        return pltpu.make_async_copy(h2_ref.at[pl.ds(0, MOE_TM), :], xbuf.at[slot], gsem.at[slot])

    def scatter_tile(slot):
        return pltpu.make_async_copy(ystage.at[slot], yt_ref.at[pl.ds(0, MOE_TM), :], ssem.at[slot])

    @pl.when(i == 0)
    def _():
        ystage[...] = jnp.zeros_like(ystage)
        spare = pltpu.make_async_copy(ystage.at[0], yt_ref.at[pl.ds(spare0, MOE_TM), :], ssem.at[0])
        spare.start()
        spare.wait()
        for r in range(MOE_TM):
            gather_row(0, 0, r).start()

    @pl.when(tf_ref[i] == 1)
    def _():
        wgb_ref[...] = wg_ref[0].astype(BF16)
        wub_ref[...] = wu_ref[0].astype(BF16)
        wdb_ref[...] = wd_ref[0].astype(BF16)

    slot = lax.rem(i, 2)
    other = 1 - slot

    @pl.when(i < nu)
    def _():
        gather_tile(slot).wait()

        @pl.when(i >= 1)
        def _():
            scatter_tile(slot).wait()

        nxt = jnp.minimum(i + 1, nu - 1)
        prv = jnp.maximum(i - 1, 0)
        for r in range(MOE_TM):
            gather_row(nxt, other, r).start()
            scatter_row(prv, other, r).start()

        xb = xbuf[slot, :, 0:D_MODEL].astype(BF16)
        aux = xbuf[slot, :, D_MODEL:ROW_W]
        first_slot = aux[:, 2:3].astype(I32) == te_ref[i]
        comb = jnp.where(first_slot, aux[:, 0:1], aux[:, 1:2])
        a = _dot(xb, wgb_ref[...])
        u = _dot(xb, wub_ref[...])
        act = (_silu(a) * u * comb).astype(BF16)
        ystage[slot] = _dot(act, wdb_ref[...])

    @pl.when(i == nu)
    def _():
        gather_tile(slot).wait()
        scatter_tile(slot).wait()

        def last(r, c):
            scatter_row(nu - 1, other, r).start()
            return c

        lax.fori_loop(0, MOE_TM, last, 0)
        scatter_tile(other).wait()


def _moe(tile_expert, tile_first, n_used, gsrc, sdst, h2aug, w_gate, w_up, w_down):
    T = h2aug.shape[0]
    n_tiles = gsrc.shape[0] // MOE_TM
    wspec = lambda shape: pl.BlockSpec((1,) + shape, lambda i, te, tf, nu, gs, sd: (te[i], 0, 0))
    grid_spec = pltpu.PrefetchScalarGridSpec(
        num_scalar_prefetch=5,
        grid=(n_tiles + 1,),
        in_specs=[
            pl.BlockSpec(memory_space=pl.ANY),
            wspec((D_MODEL, D_EXPERT)),
            wspec((D_MODEL, D_EXPERT)),
            wspec((D_EXPERT, D_MODEL)),
        ],
        out_specs=pl.BlockSpec(memory_space=pl.ANY),
        scratch_shapes=[pltpu.VMEM((2, MOE_TM, ROW_W), F32),
                        pltpu.VMEM((2, MOE_TM, D_MODEL), F32),
                        pltpu.VMEM((D_MODEL, D_EXPERT), BF16),
                        pltpu.VMEM((D_MODEL, D_EXPERT), BF16),
                        pltpu.VMEM((D_EXPERT, D_MODEL), BF16),
                        pltpu.SemaphoreType.DMA((2,)),
                        pltpu.SemaphoreType.DMA((2,))],
    )
    return pl.pallas_call(
        _moe_body,
        grid_spec=grid_spec,
        out_shape=jax.ShapeDtypeStruct((2 * T + MOE_TM, D_MODEL), F32),
        compiler_params=_cparams(("arbitrary",)),
        name="moe",
    )(tile_expert, tile_first, n_used, gsrc, sdst, h2aug, w_gate, w_up, w_down)


COMB_TM = 256


def _combine_body(h1_ref, y0_ref, y1_ref, nw_ref, o_ref):
    h = h1_ref[...] + (y0_ref[...] + y1_ref[...])
    ms = jnp.mean(h * h, axis=-1, keepdims=True)
    o_ref[...] = h * lax.rsqrt(ms + EPS) * nw_ref[...]


def _combine(h1, y_tok, nw):
    T = h1.shape[0]
    tm = min(COMB_TM, T)
    nt = T // tm
    return pl.pallas_call(
        _combine_body,
        grid=(nt,),
        in_specs=[
            pl.BlockSpec((tm, D_MODEL), lambda i: (i, 0)),
            pl.BlockSpec((tm, D_MODEL), lambda i: (i, 0)),
            pl.BlockSpec((tm, D_MODEL), lambda i: (i + nt, 0)),
            pl.BlockSpec((1, D_MODEL), lambda i: (0, 0)),
        ],
        out_specs=pl.BlockSpec((tm, D_MODEL), lambda i: (i, 0)),
        out_shape=jax.ShapeDtypeStruct((T, D_MODEL), F32),
        compiler_params=_cparams(("parallel",)),
        name="combine",
    )(h1, y_tok, y_tok, nw)


def _tile_plan(cnt, n_tiles):
    tiles = (cnt + (MOE_TM - 1)) // MOE_TM
    ends = jnp.cumsum(tiles)
    starts = ends - tiles
    n_used = ends[-1]
    step = jnp.arange(n_tiles + 1, dtype=I32)
    tile = jnp.minimum(step, jnp.maximum(n_used - 1, 0))
    tile_expert = jnp.sum((ends[None, :] <= tile[:, None]).astype(I32), axis=1)
    tile_first = ((step == starts[tile_expert]) & (step < n_used)).astype(I32)
    pad_lo = jnp.concatenate([starts * MOE_TM + cnt, (n_used * MOE_TM).reshape(1)])
    pad_hi = jnp.concatenate([ends * MOE_TM, jnp.full((1,), n_tiles * MOE_TM, I32)])
    return tile_expert, tile_first, n_used.reshape(1).astype(I32), pad_lo.astype(I32), pad_hi.astype(I32)


def kernel(x, positions, norm1_w, w_in, conv_w, conv_b, dt_bias, a_log, d_skip, ret_norm_w,
           ssm_norm_w, w_out, norm2_w, w_router_group, b_router_group, w_router_expert,
           b_router_expert, w_expert_gate, w_expert_up, w_expert_down, final_norm_w):
    B, T, D = x.shape
    assert B == 1 and D == D_MODEL and T % CHUNK == 0
    xf = x.reshape(T, D)
    pad_l = lambda v: jnp.pad(v, ((0, 0), (0, LANES - v.shape[-1])))

    w_main = w_in[0][:, :MAIN_PROJ].astype(BF16)
    w_dt = pad_l(w_in[0][:, MAIN_PROJ:]).astype(BF16)
    half = RET_HEAD_DIM // 2
    inv = (ROPE_THETA ** (-jnp.arange(half, dtype=F32) / half)).reshape(1, half)
    tri = (jnp.arange(CHUNK)[:, None] >= jnp.arange(CHUNK)[None, :]).astype(BF16)
    expand = (jnp.arange(LANES)[:, None] == jnp.arange(SSM_WIDTH)[None, :] // SSM_HEAD_DIM).astype(BF16)
    d_skip_e = jnp.repeat(d_skip[0], SSM_HEAD_DIM).reshape(1, SSM_WIDTH)
    wr_t = jnp.zeros((LANES, D), F32)
    wr_t = wr_t.at[0:N_GROUPS].set(w_router_group[0].T).at[8:8 + N_EXPERTS].set(w_router_expert[0].T)
    br = jnp.zeros((LANES,), F32)
    br = br.at[0:N_GROUPS].set(b_router_group[0]).at[8:8 + N_EXPERTS].set(b_router_expert[0])
    tri_u = (jnp.arange(SORT_BLK)[:, None] <= jnp.arange(SORT_BLK)[None, :]).astype(BF16)
    ltri = (jnp.arange(N_EXPERTS)[:, None] > jnp.arange(N_EXPERTS)[None, :]).astype(BF16)

    proj, dt_raw = _inproj(xf, norm1_w[0].reshape(1, D), w_main, w_dt)
    cos, sin = _rope_tables(positions.reshape(T, 1).astype(F32), inv)
    y_ret = _retention(proj, cos, sin, ret_norm_w[0].reshape(1, RET_WIDTH), _retention_consts())
    y_ssm = _ssd(proj, dt_raw, conv_w[0], conv_b[0].reshape(1, -1), pad_l(dt_bias[0].reshape(1, -1)),
                 pad_l(a_log[0].reshape(1, -1)), d_skip_e, ssm_norm_w[0].reshape(1, -1), tri, expand)
    h1, h2aug, ids = _out_router(xf, y_ret, y_ssm, w_out[0].astype(BF16), norm2_w[0].reshape(1, D),
                                 wr_t.astype(BF16), br.reshape(LANES, 1))

    dest_blk, cnt = _sort_index(ids, tri_u, ltri)
    dest = dest_blk[:, 0:2, :].transpose(1, 0, 2).reshape(2 * T)
    n_tiles = (2 * T) // MOE_TM + N_EXPERTS
    tile_expert, tile_first, n_used, pad_lo, pad_hi = _tile_plan(cnt[:, 0].astype(I32), n_tiles)
    gsrc, sdst = _sorted_rows(dest, pad_lo, pad_hi, n_tiles * MOE_TM)
    y_tok = _moe(tile_expert, tile_first, n_used, gsrc, sdst, h2aug,
                 w_expert_gate[0], w_expert_up[0], w_expert_down[0])
    out = _combine(h1, y_tok, final_norm_w.reshape(1, D))
    return out.reshape(B, T, D)
```

```python
import functools

import jax
import jax.numpy as jnp
from jax import lax
from jax.experimental import pallas as pl
from jax.experimental.pallas import tpu as pltpu

F32 = jnp.float32
BF16 = jnp.bfloat16
I32 = jnp.int32

D_MODEL = 2048
EPS = 1e-6
CHUNK = 128
RET_HEADS = 4
RET_HEAD_DIM = 256
RET_WIDTH = RET_HEADS * RET_HEAD_DIM
ROPE_THETA = 10000.0
SSM_WIDTH = 1024
SSM_HEAD_DIM = 64
SSM_HEADS = SSM_WIDTH // SSM_HEAD_DIM
SSM_GROUPS = 2
SSM_STATE = 128
SSM_CONV = 4
SSM_CONV_DIM = SSM_WIDTH + 2 * SSM_GROUPS * SSM_STATE
MAIN_PROJ = 4 * RET_WIDTH + SSM_WIDTH + SSM_CONV_DIM
N_GROUPS = 4
GROUP_EXPERTS = 8
N_EXPERTS = N_GROUPS * GROUP_EXPERTS
D_EXPERT = 512
LANES = 128
PACK_ROWS = D_MODEL // 2 // LANES
Y_ROWS = D_MODEL // LANES

VMEM_LIMIT = 56 * 1024 * 1024


def _cparams(sem, vmem=VMEM_LIMIT):
    return pltpu.CompilerParams(dimension_semantics=sem, vmem_limit_bytes=vmem)


def _silu(x):
    return x * (1.0 / (1.0 + jnp.exp(-x)))


def _dot(a, b):
    return jnp.dot(a, b, preferred_element_type=F32)


def _dot_nt(a, b):
    return lax.dot_general(a, b, (((1,), (1,)), ((), ())), preferred_element_type=F32)


def _split3(a):
    a1 = a.astype(BF16)
    r1 = a - a1.astype(F32)
    a2 = r1.astype(BF16)
    a3 = (r1 - a2.astype(F32)).astype(BF16)
    return a1, a2, a3


IN_TM = 1024
IN_TN = 512


def _inproj_body(x_ref, nw_ref, w_ref, wdt_ref, o_ref, odt_ref, hn_ref):
    @pl.when(pl.program_id(1) == 0)
    def _():
        def blk(r, carry):
            sl = pl.ds(pl.multiple_of(r * CHUNK, CHUNK), CHUNK)
            x = x_ref[sl, :]
            ms = jnp.mean(x * x, axis=-1, keepdims=True)
            hn_ref[sl, :] = (x * lax.rsqrt(ms + EPS) * nw_ref[...]).astype(BF16)
            return carry
        lax.fori_loop(0, x_ref.shape[0] // CHUNK, blk, 0)
        odt_ref[...] = _dot(hn_ref[...], wdt_ref[...])

    o_ref[...] = _dot(hn_ref[...], w_ref[...])


def _inproj(x, nw, w_main, w_dt):
    T = x.shape[0]
    tm = min(IN_TM, T)
    return pl.pallas_call(
        _inproj_body,
        grid=(T // tm, MAIN_PROJ // IN_TN),
        in_specs=[
            pl.BlockSpec((tm, D_MODEL), lambda i, j: (i, 0)),
            pl.BlockSpec((1, D_MODEL), lambda i, j: (0, 0)),
            pl.BlockSpec((D_MODEL, IN_TN), lambda i, j: (0, j)),
            pl.BlockSpec((D_MODEL, LANES), lambda i, j: (0, 0)),
        ],
        out_specs=[
            pl.BlockSpec((tm, IN_TN), lambda i, j: (i, j)),
            pl.BlockSpec((tm, LANES), lambda i, j: (i, 0)),
        ],
        out_shape=[
            jax.ShapeDtypeStruct((T, MAIN_PROJ), F32),
            jax.ShapeDtypeStruct((T, LANES), F32),
        ],
        scratch_shapes=[pltpu.VMEM((tm, D_MODEL), BF16)],
        compiler_params=_cparams(("parallel", "arbitrary")),
        name="in_proj",
    )(x, nw, w_main, w_dt)


def _rope_body(pos_ref, inv_ref, cos_ref, sin_ref):
    ang = pos_ref[...] * inv_ref[...]
    cos_ref[...] = jnp.cos(ang)
    sin_ref[...] = jnp.sin(ang)


def _rope_tables(pos, inv):
    T = pos.shape[0]
    tm = min(1024, T)
    half = inv.shape[1]
    return pl.pallas_call(
        _rope_body,
        grid=(T // tm,),
        in_specs=[pl.BlockSpec((tm, 1), lambda i: (i, 0)),
                  pl.BlockSpec((1, half), lambda i: (0, 0))],
        out_specs=[pl.BlockSpec((tm, half), lambda i: (i, 0))] * 2,
        out_shape=[jax.ShapeDtypeStruct((T, half), F32)] * 2,
        compiler_params=_cparams(("parallel",)),
        name="rope_tables",
    )(pos, inv)


RET_CB = 2


def _retention_body(cd_ref, q_ref, k_ref, v_ref, g_ref, cos_ref, sin_ref,
                    dec_ref, wq_ref, ws_ref, nw_ref, o_ref, st_ref, *, cb):
    h = pl.program_id(0)

    @pl.when(pl.program_id(1) == 0)
    def _():
        st_ref[...] = jnp.zeros_like(st_ref)

    chunk_decay = cd_ref[h]
    half = RET_HEAD_DIM // 2
    for i in range(cb):
        sl = slice(i * CHUNK, (i + 1) * CHUNK)
        cos = cos_ref[sl, :]
        sin = sin_ref[sl, :]

        def rope(x):
            x1, x2 = x[:, :half], x[:, half:]
            return jnp.concatenate([x1 * cos - x2 * sin, x2 * cos + x1 * sin], axis=-1)

        q = rope(q_ref[sl, :])
        k = rope(k_ref[sl, :]) * (RET_HEAD_DIM ** -0.5)
        v = v_ref[sl, :]
        qb = q.astype(BF16)
        kb = k.astype(BF16)
        scores = _dot_nt(qb, kb) * dec_ref[0]
        y = _dot(scores.astype(BF16), v.astype(BF16))
        st = st_ref[...]
        y = y + _dot(qb, st.astype(BF16)) * wq_ref[0]
        vw = (v * ws_ref[0]).astype(BF16)
        new = _dot(k.T.astype(BF16), vw)
        st_ref[...] = st * chunk_decay + new
        mu = jnp.mean(y, axis=-1, keepdims=True)
        d = y - mu
        var = jnp.mean(d * d, axis=-1, keepdims=True)
        yn = d * lax.rsqrt(var + EPS)
        o_ref[sl, :] = (yn * nw_ref[...] * _silu(g_ref[sl, :])).astype(BF16)


def _retention(proj, cos, sin, ret_nw, consts):
    T = proj.shape[0]
    cb = min(RET_CB, T // CHUNK)
    rows = cb * CHUNK
    chunk_decay, decay_intra, w_query, w_state = consts
    hd = RET_HEAD_DIM
    half = hd // 2

    def col(base):
        return lambda h, c, cd: (c, base + h)

    grid_spec = pltpu.PrefetchScalarGridSpec(
        num_scalar_prefetch=1,
        grid=(RET_HEADS, T // rows),
        in_specs=[
            pl.BlockSpec((rows, hd), col(0)),
            pl.BlockSpec((rows, hd), col(RET_HEADS)),
            pl.BlockSpec((rows, hd), col(2 * RET_HEADS)),
            pl.BlockSpec((rows, hd), col(3 * RET_HEADS)),
            pl.BlockSpec((rows, half), lambda h, c, cd: (c, 0)),
            pl.BlockSpec((rows, half), lambda h, c, cd: (c, 0)),
            pl.BlockSpec((1, CHUNK, CHUNK), lambda h, c, cd: (h, 0, 0)),
            pl.BlockSpec((1, CHUNK, hd), lambda h, c, cd: (h, 0, 0)),
            pl.BlockSpec((1, CHUNK, hd), lambda h, c, cd: (h, 0, 0)),
            pl.BlockSpec((1, hd), lambda h, c, cd: (0, h)),
        ],
        out_specs=pl.BlockSpec((rows, hd), lambda h, c, cd: (c, h)),
        scratch_shapes=[pltpu.VMEM((hd, hd), F32)],
    )
    return pl.pallas_call(
        functools.partial(_retention_body, cb=cb),
        grid_spec=grid_spec,
        out_shape=jax.ShapeDtypeStruct((T, RET_WIDTH), BF16),
        compiler_params=_cparams(("parallel", "arbitrary")),
        name="retention",
    )(chunk_decay, proj, proj, proj, proj, cos, sin, decay_intra, w_query, w_state, ret_nw)


def _retention_consts():
    H, L = RET_HEADS, CHUNK
    log_gamma = jnp.log1p(-(2.0 ** (-5.0 - jnp.arange(H, dtype=F32))))
    idx = jnp.arange(L, dtype=F32)
    diff = idx[:, None] - idx[None, :]
    causal = diff >= 0
    decay_intra = jnp.where(causal[None], jnp.exp(jnp.where(causal, diff, 0.0)[None] * log_gamma[:, None, None]), 0.0)
    w_state = jnp.exp((L - 1.0 - idx)[None, :] * log_gamma[:, None])
    w_query = jnp.exp((idx + 1.0)[None, :] * log_gamma[:, None])
    chunk_decay = jnp.exp(L * log_gamma)
    bc = lambda w: jnp.broadcast_to(w[:, :, None], (H, L, RET_HEAD_DIM))
    return chunk_decay, decay_intra, bc(w_query), bc(w_state)


SSD_CB = 2
CONV_PAD = 8
XBC_BLK = 512


def _ssd_body(x0_ref, x1_ref, x2_ref, z_ref, dt_ref, cw_ref, cbias_ref, dtb_ref,
              alog_ref, dskip_ref, nw_ref, tri_ref, exp_ref, o_ref, xpad_ref, st_ref, *, cb):
    @pl.when(pl.program_id(0) == 0)
    def _():
        xpad_ref[0:CONV_PAD, :] = jnp.zeros((CONV_PAD, SSM_CONV_DIM), F32)
        st_ref[...] = jnp.zeros_like(st_ref)

    L = CHUNK
    gw = SSM_WIDTH // SSM_GROUPS
    rows_i = lax.broadcasted_iota(I32, (L, L), 0)
    cols_i = lax.broadcasted_iota(I32, (L, L), 1)
    causal = rows_i >= cols_i
    lo_lane = lax.broadcasted_iota(I32, (L, LANES), 1) < SSM_HEAD_DIM
    tri = tri_ref[...]
    expand = exp_ref[...]
    a_neg = -jnp.exp(alog_ref[...])

    for i in range(cb):
        sl = slice(i * L, (i + 1) * L)
        for b, xr in enumerate((x0_ref, x1_ref, x2_ref)):
            xpad_ref[CONV_PAD:CONV_PAD + L, b * XBC_BLK:(b + 1) * XBC_BLK] = xr[sl, :]
        u_parts = []
        for b in range(SSM_CONV_DIM // XBC_BLK):
            cs = slice(b * XBC_BLK, (b + 1) * XBC_BLK)
            acc = cbias_ref[:, cs]
            for t in range(SSM_CONV):
                r0 = CONV_PAD - (SSM_CONV - 1) + t
                acc = acc + xpad_ref[r0:r0 + L, cs] * cw_ref[t:t + 1, cs]
            u_parts.append(_silu(acc))
        xpad_ref[0:CONV_PAD, :] = xpad_ref[L:L + CONV_PAD, :]
        xs = jnp.concatenate(u_parts[:2], axis=-1)
        bm = u_parts[2][:, :SSM_GROUPS * SSM_STATE]
        cm = u_parts[2][:, SSM_GROUPS * SSM_STATE:]

        dt_in = dt_ref[sl, :] + dtb_ref[...]
        dt = jnp.maximum(dt_in, 0.0) + jnp.log1p(jnp.exp(-jnp.abs(dt_in)))
        a = dt * a_neg
        a1, a2, a3 = _split3(a)
        a_cs = _dot(tri, a1) + _dot(tri, a2) + _dot(tri, a3)
        d1, d2, d3 = _split3(dt)
        dt_e = _dot(d1, expand) + _dot(d2, expand) + _dot(d3, expand)
        c1, c2, c3 = _split3(a_cs)
        acs_e = _dot(c1, expand) + _dot(c2, expand) + _dot(c3, expand)
        last = acs_e[L - 1:L, :]
        to_end = jnp.exp(last - acs_e)
        from_start = jnp.exp(acs_e)
        chunk_decay = jnp.exp(last)
        x_dt = xs * dt_e
        xw = (x_dt * to_end).astype(BF16)
        acs_t = a_cs.T
        cmb = cm.astype(BF16)
        bmb = bm.astype(BF16)

        ys = []
        for g in range(SSM_GROUPS):
            ns = slice(g * SSM_STATE, (g + 1) * SSM_STATE)
            gs = slice(g * gw, (g + 1) * gw)
            cg = cmb[:, ns]
            cbg = _dot_nt(cg, bmb[:, ns])
            st = st_ref[g]
            y_off = _dot(cg, st.astype(BF16))
            new = _dot(bm[:, ns].T.astype(BF16), xw[:, gs])
            st_ref[g] = st * chunk_decay[:, gs] + new
            for jp in range(gw // LANES):
                h0 = (g * gw + jp * LANES) // SSM_HEAD_DIM
                ms = []
                for hh in (h0, h0 + 1):
                    seg = a_cs[:, hh:hh + 1] - acs_t[hh:hh + 1, :]
                    dec = jnp.exp(jnp.where(causal, seg, -jnp.inf))
                    ms.append((cbg * dec).astype(BF16))
                lhs = jnp.concatenate(ms, axis=1)
                ls = slice(g * gw + jp * LANES, g * gw + (jp + 1) * LANES)
                xp = x_dt[:, ls]
                rhs = jnp.concatenate([jnp.where(lo_lane, xp, 0.0),
                                       jnp.where(lo_lane, 0.0, xp)], axis=0).astype(BF16)
                y_diag = _dot(lhs, rhs)
                ys.append(y_diag + y_off[:, jp * LANES:(jp + 1) * LANES] * from_start[:, ls])
        y = jnp.concatenate(ys, axis=1) + xs * dskip_ref[...]
        y = y * _silu(z_ref[sl, :])
        outs = []
        for g in range(SSM_GROUPS):
            yg = y[:, g * gw:(g + 1) * gw]
            ms_ = jnp.mean(yg * yg, axis=-1, keepdims=True)
            outs.append(yg * lax.rsqrt(ms_ + EPS))
        o_ref[sl, :] = (jnp.concatenate(outs, axis=1) * nw_ref[...]).astype(BF16)


def _ssd(proj, dt_raw, conv_w, conv_b, dt_bias, a_log, d_skip_e, ssm_nw, tri, expand):
    T = proj.shape[0]
    cb = min(SSD_CB, T // CHUNK)
    rows = cb * CHUNK
    xbc0 = (4 * RET_WIDTH + SSM_WIDTH) // XBC_BLK
    zcol = 4 * RET_WIDTH // SSM_WIDTH
    full = lambda shape: pl.BlockSpec(shape, lambda c: (0,) * len(shape))
    return pl.pallas_call(
        functools.partial(_ssd_body, cb=cb),
        grid=(T // rows,),
        in_specs=[
            pl.BlockSpec((rows, XBC_BLK), lambda c: (c, xbc0)),
            pl.BlockSpec((rows, XBC_BLK), lambda c: (c, xbc0 + 1)),
            pl.BlockSpec((rows, XBC_BLK), lambda c: (c, xbc0 + 2)),
            pl.BlockSpec((rows, SSM_WIDTH), lambda c: (c, zcol)),
            pl.BlockSpec((rows, LANES), lambda c: (c, 0)),
            full((SSM_CONV, SSM_CONV_DIM)),
            full((1, SSM_CONV_DIM)),
            full((1, LANES)),
            full((1, LANES)),
            full((1, SSM_WIDTH)),
            full((1, SSM_WIDTH)),
            full((CHUNK, CHUNK)),
            full((LANES, SSM_WIDTH)),
        ],
        out_specs=pl.BlockSpec((rows, SSM_WIDTH), lambda c: (c, 0)),
        out_shape=jax.ShapeDtypeStruct((T, SSM_WIDTH), BF16),
        scratch_shapes=[
            pltpu.VMEM((CHUNK + CONV_PAD, SSM_CONV_DIM), F32),
            pltpu.VMEM((SSM_GROUPS, SSM_STATE, SSM_WIDTH // SSM_GROUPS), F32),
        ],
        compiler_params=_cparams(("arbitrary",)),
        name="ssd",
    )(proj, proj, proj, proj, dt_raw, conv_w, conv_b, dt_bias, a_log, d_skip_e, ssm_nw, tri, expand)


OUT_TM = 256
ROUTE_ROWS = 8 + N_EXPERTS


def _bf16_bits(x):
    return lax.bitcast_convert_type(x.astype(BF16).astype(F32), jnp.uint32)


def _out_router_body(x_ref, yr_ref, ys_ref, wo_ref, nw_ref, wr_ref, br_ref,
                     h1_ref, h2p_ref, cw_ref, ids_ref):
    tm = x_ref.shape[0]
    h1 = x_ref[...] + _dot(yr_ref[...], wo_ref[0:RET_WIDTH, :]) + _dot(ys_ref[...], wo_ref[RET_WIDTH:, :])
    h1_ref[...] = h1
    ms = jnp.mean(h1 * h1, axis=-1, keepdims=True)
    h2 = h1 * lax.rsqrt(ms + EPS) * nw_ref[...]
    word = (_bf16_bits(h2[:, 0:D_MODEL // 2]) >> 16) | _bf16_bits(h2[:, D_MODEL // 2:])
    for s in range(PACK_ROWS):
        h2p_ref[pl.ds(s, tm, stride=PACK_ROWS), :] = word[:, s * LANES:(s + 1) * LANES]

    logits = _dot_nt(wr_ref[...], h2.astype(BF16)) + br_ref[...]
    row = lax.broadcasted_iota(I32, (8, tm), 0)
    lg = jnp.where(row < N_GROUPS, logits[0:8], -jnp.inf)
    m = jnp.max(lg, axis=0, keepdims=True)
    p_sel = 1.0 / jnp.sum(jnp.exp(lg - m), axis=0, keepdims=True)
    g_sel = jnp.min(jnp.where(lg == m, row, 8), axis=0, keepdims=True)
    le = jnp.zeros((GROUP_EXPERTS, tm), F32)
    for g in range(N_GROUPS):
        le = jnp.where(g_sel == g, logits[8 + g * GROUP_EXPERTS:8 + (g + 1) * GROUP_EXPERTS], le)
    m2 = jnp.max(le, axis=0, keepdims=True)
    ee = jnp.exp(le - m2)
    pe = ee / jnp.sum(ee, axis=0, keepdims=True)
    v1 = jnp.max(pe, axis=0, keepdims=True)
    i1 = jnp.min(jnp.where(pe == v1, row, 8), axis=0, keepdims=True)
    pe2 = jnp.where(row == i1, -1.0, pe)
    v2 = jnp.max(pe2, axis=0, keepdims=True)
    i2 = jnp.min(jnp.where(pe2 == v2, row, 8), axis=0, keepdims=True)
    tw = v1 + v2
    c1 = v1 / tw * p_sel
    c2 = v2 / tw * p_sel
    e1 = g_sel * GROUP_EXPERTS + i1
    e2 = g_sel * GROUP_EXPERTS + i2
    ids = jnp.where(row == 0, e1, jnp.where(row == 1, e2, 0))
    for b in range(tm // SORT_BLK):
        ids_ref[b] = ids[:, b * SORT_BLK:(b + 1) * SORT_BLK]
    cw8 = jnp.where(row == 0, c1, jnp.where(row == 1, c2, 0.0))
    cw = jnp.concatenate([cw8, jnp.zeros((LANES - 8, tm), F32)], axis=0)
    cw_ref[...] = cw.T


def _out_router(x, y_ret, y_ssm, w_out, nw, wr_t, br):
    T = x.shape[0]
    tm = min(OUT_TM, T)
    full = lambda shape: pl.BlockSpec(shape, lambda i: (0,) * len(shape))
    return pl.pallas_call(
        _out_router_body,
        grid=(T // tm,),
        in_specs=[
            pl.BlockSpec((tm, D_MODEL), lambda i: (i, 0)),
            pl.BlockSpec((tm, RET_WIDTH), lambda i: (i, 0)),
            pl.BlockSpec((tm, SSM_WIDTH), lambda i: (i, 0)),
            full((D_MODEL, D_MODEL)),
            full((1, D_MODEL)),
            full((LANES, D_MODEL)),
            full((LANES, 1)),
        ],
        out_specs=[
            pl.BlockSpec((tm, D_MODEL), lambda i: (i, 0)),
            pl.BlockSpec((tm * PACK_ROWS, LANES), lambda i: (i, 0)),
            pl.BlockSpec((tm, LANES), lambda i: (i, 0)),
            pl.BlockSpec((tm // SORT_BLK, 8, SORT_BLK), lambda i: (i, 0, 0)),
        ],
        out_shape=[
            jax.ShapeDtypeStruct((T, D_MODEL), F32),
            jax.ShapeDtypeStruct((T * PACK_ROWS, LANES), jnp.uint32),
            jax.ShapeDtypeStruct((T, LANES), F32),
            jax.ShapeDtypeStruct((T // SORT_BLK, 8, SORT_BLK), I32),
        ],
        compiler_params=_cparams(("parallel",)),
        name="out_router",
    )(x, y_ret, y_ssm, w_out, nw, wr_t, br)


SORT_BLK = 256
MOE_TM = 256


def _sort_index_body(ids_ref, tri_ref, ltri_ref, dest_ref, cnt_ref, rank_ref):
    nblk = ids_ref.shape[0]
    row_e = lax.broadcasted_iota(I32, (N_EXPERTS, SORT_BLK), 0)
    row8 = lax.broadcasted_iota(I32, (8, SORT_BLK), 0)

    def onehots(b):
        ids = ids_ref[b]
        return row_e == ids[0:1], row_e == ids[1:2]

    def rank_blk(b, carry):
        oh1, oh2 = onehots(b)
        ohf = jnp.where(oh1 | oh2, 1.0, 0.0)
        incl = _dot(ohf.astype(BF16), tri_ref[...])
        base = carry + incl - 1.0
        r1 = jnp.sum(jnp.where(oh1, base, 0.0), axis=0, keepdims=True)
        r2 = jnp.sum(jnp.where(oh2, base, 0.0), axis=0, keepdims=True)
        rank_ref[b] = jnp.where(row8 == 0, r1, jnp.where(row8 == 1, r2, 0.0))
        return carry + jnp.sum(ohf, axis=1, keepdims=True)

    cnt = lax.fori_loop(0, nblk, rank_blk, jnp.zeros((N_EXPERTS, 1), F32))
    cnt_ref[...] = jnp.broadcast_to(cnt, cnt_ref.shape)
    tiles = jnp.floor((cnt + (MOE_TM - 1.0)) / MOE_TM)
    tiles_b = jnp.broadcast_to(tiles, (N_EXPERTS, LANES)).astype(BF16)
    off = _dot(ltri_ref[...], tiles_b)[:, 0:1] * MOE_TM

    def dest_blk(b, carry):
        oh1, oh2 = onehots(b)
        o1 = jnp.sum(jnp.where(oh1, off, 0.0), axis=0, keepdims=True)
        o2 = jnp.sum(jnp.where(oh2, off, 0.0), axis=0, keepdims=True)
        d = rank_ref[b] + jnp.where(row8 == 0, o1, jnp.where(row8 == 1, o2, 0.0))
        dest_ref[b] = d.astype(I32)
        return carry

    lax.fori_loop(0, nblk, dest_blk, 0)


def _sort_index(ids, tri_u, ltri):
    nblk = ids.shape[0]
    return pl.pallas_call(
        _sort_index_body,
        out_shape=[jax.ShapeDtypeStruct((nblk, 8, SORT_BLK), I32),
                   jax.ShapeDtypeStruct((N_EXPERTS, LANES), F32)],
        scratch_shapes=[pltpu.VMEM((nblk, 8, SORT_BLK), F32)],
        compiler_params=_cparams(None),
        name="sort_index",
    )(ids, tri_u, ltri)


def _sorted_rows_body(dest_ref, lo_ref, hi_ref, srow_ref):
    T = dest_ref.shape[0] // 2

    def tok(t, c):
        srow_ref[dest_ref[t]] = t
        srow_ref[dest_ref[T + t]] = T + t
        return c

    lax.fori_loop(0, T, tok, 0, unroll=8)

    def seg(e, c):
        def pad(r, c2):
            srow_ref[r] = 2 * T + (r & (MOE_TM - 1))
            return c2

        lax.fori_loop(lo_ref[e], hi_ref[e], pad, 0)
        return c

    lax.fori_loop(0, lo_ref.shape[0], seg, 0)


def _sorted_rows(dest, pad_lo, pad_hi, n_rows):
    smem = pl.BlockSpec(memory_space=pltpu.SMEM)
    return pl.pallas_call(
        _sorted_rows_body,
        in_specs=[smem, smem, smem],
        out_specs=smem,
        out_shape=jax.ShapeDtypeStruct((n_rows,), I32),
        compiler_params=_cparams(None),
        name="sorted_rows",
    )(dest, pad_lo, pad_hi)


ROW_DMA_PRIORITY = 1


def _moe_body(te_ref, tf_ref, nu_ref, srow_ref, h2p_ref, wg_ref, wu_ref, wd_ref, yt_ref,
              xbuf, ystage, wgb_ref, wub_ref, wdb_ref, gsem, ssem):
    i = pl.program_id(0)
    nu = nu_ref[0]
    T = h2p_ref.shape[0] // PACK_ROWS
    spare0 = 2 * T * Y_ROWS

    def gather_row(tile, slot, r):
        tok = srow_ref[tile * MOE_TM + r] & (T - 1)
        return pltpu.make_async_copy(h2p_ref.at[pl.ds(pl.multiple_of(tok * PACK_ROWS, PACK_ROWS), PACK_ROWS), :],
                                     xbuf.at[slot, pl.ds(r * PACK_ROWS, PACK_ROWS), :], gsem.at[slot])

    def scatter_row(tile, slot, r):
        row = srow_ref[tile * MOE_TM + r]
        return pltpu.make_async_copy(ystage.at[slot, pl.ds(r * Y_ROWS, Y_ROWS), :],
                                     yt_ref.at[pl.ds(pl.multiple_of(row * Y_ROWS, Y_ROWS), Y_ROWS), :], ssem.at[slot])

    def gather_tile(slot):
        return pltpu.make_async_copy(h2p_ref.at[pl.ds(0, MOE_TM * PACK_ROWS), :], xbuf.at[slot], gsem.at[slot])

    def scatter_tile(slot):
        return pltpu.make_async_copy(ystage.at[slot], yt_ref.at[pl.ds(0, MOE_TM * Y_ROWS), :], ssem.at[slot])

    @pl.when(i == 0)
    def _():
        ystage[...] = jnp.zeros_like(ystage)
        spare = pltpu.make_async_copy(ystage.at[0], yt_ref.at[pl.ds(spare0, MOE_TM * Y_ROWS), :], ssem.at[0])
        spare.start()
        spare.wait()
        for r in range(MOE_TM):
            gather_row(0, 0, r).start(priority=ROW_DMA_PRIORITY)

    @pl.when(tf_ref[i] == 1)
    def _():
        wgb_ref[...] = wg_ref[0].astype(BF16)
        wub_ref[...] = wu_ref[0].astype(BF16)
        wdb_ref[...] = wd_ref[0].astype(BF16)

    slot = lax.rem(i, 2)
    other = 1 - slot

    @pl.when(i < nu)
    def _():
        gather_tile(slot).wait()

        @pl.when(i >= 1)
        def _():
            scatter_tile(slot).wait()

        nxt = jnp.minimum(i + 1, nu - 1)
        prv = jnp.maximum(i - 1, 0)
        for r in range(MOE_TM):
            gather_row(nxt, other, r).start(priority=ROW_DMA_PRIORITY)
            scatter_row(prv, other, r).start(priority=ROW_DMA_PRIORITY)

        lo, hi = [], []
        for s in range(PACK_ROWS):
            w = xbuf[slot, pl.ds(s, MOE_TM, stride=PACK_ROWS), :]
            lo.append(lax.bitcast_convert_type(w << 16, F32).astype(BF16))
            hi.append(lax.bitcast_convert_type(w & jnp.uint32(0xFFFF0000), F32).astype(BF16))
        x_lo = jnp.concatenate(lo, axis=1)
        x_hi = jnp.concatenate(hi, axis=1)
        hd = D_MODEL // 2
        a = _dot(x_lo, wgb_ref[0:hd, :]) + _dot(x_hi, wgb_ref[hd:, :])
        u = _dot(x_lo, wub_ref[0:hd, :]) + _dot(x_hi, wub_ref[hd:, :])
        act = (_silu(a) * u).astype(BF16)
        y = _dot(act, wdb_ref[...])
        for s in range(Y_ROWS):
            ystage[slot, pl.ds(s, MOE_TM, stride=Y_ROWS), :] = y[:, s * LANES:(s + 1) * LANES]

    @pl.when(i == nu)
    def _():
        gather_tile(slot).wait()
        scatter_tile(slot).wait()

        def last(r, c):
            row = srow_ref[(nu - 1) * MOE_TM + r]
            pltpu.make_async_copy(
                ystage.at[other, pl.ds(pl.multiple_of(r * Y_ROWS, Y_ROWS), Y_ROWS), :],
                yt_ref.at[pl.ds(pl.multiple_of(row * Y_ROWS, Y_ROWS), Y_ROWS), :], ssem.at[other]).start()
            return c

        lax.fori_loop(0, MOE_TM, last, 0)
        scatter_tile(other).wait()


def _moe(tile_expert, tile_first, n_used, srow, h2p, w_gate, w_up, w_down):
    T = h2p.shape[0] // PACK_ROWS
    assert T & (T - 1) == 0 and T >= MOE_TM, "token index is recovered from the row table by masking"
    n_tiles = srow.shape[0] // MOE_TM
    wspec = lambda shape: pl.BlockSpec((1,) + shape, lambda i, te, tf, nu, sr: (te[i], 0, 0))
    grid_spec = pltpu.PrefetchScalarGridSpec(
        num_scalar_prefetch=4,
        grid=(n_tiles + 1,),
        in_specs=[
            pl.BlockSpec(memory_space=pl.ANY),
            wspec((D_MODEL, D_EXPERT)),
            wspec((D_MODEL, D_EXPERT)),
            wspec((D_EXPERT, D_MODEL)),
        ],
        out_specs=pl.BlockSpec(memory_space=pl.ANY),
        scratch_shapes=[pltpu.VMEM((2, MOE_TM * PACK_ROWS, LANES), jnp.uint32),
                        pltpu.VMEM((2, MOE_TM * Y_ROWS, LANES), F32),
                        pltpu.VMEM((D_MODEL, D_EXPERT), BF16),
                        pltpu.VMEM((D_MODEL, D_EXPERT), BF16),
                        pltpu.VMEM((D_EXPERT, D_MODEL), BF16),
                        pltpu.SemaphoreType.DMA((2,)),
                        pltpu.SemaphoreType.DMA((2,))],
    )
    return pl.pallas_call(
        _moe_body,
        grid_spec=grid_spec,
        out_shape=jax.ShapeDtypeStruct(((2 * T + MOE_TM) * Y_ROWS, LANES), F32),
        compiler_params=_cparams(("arbitrary",)),
        name="moe",
    )(tile_expert, tile_first, n_used, srow, h2p, w_gate, w_up, w_down)


COMB_TM = 256


def _combine_body(h1_ref, y0_ref, y1_ref, cw_ref, nw_ref, o_ref):
    tm = h1_ref.shape[0]
    c0 = cw_ref[:, 0:1]
    c1 = cw_ref[:, 1:2]
    cols = []
    for s in range(Y_ROWS):
        rows = pl.ds(s, tm, stride=Y_ROWS)
        moe = c0 * y0_ref[rows, :] + c1 * y1_ref[rows, :]
        cols.append(h1_ref[:, s * LANES:(s + 1) * LANES] + moe)
    h = jnp.concatenate(cols, axis=1)
    ms = jnp.mean(h * h, axis=-1, keepdims=True)
    o_ref[...] = h * lax.rsqrt(ms + EPS) * nw_ref[...]


def _combine(h1, y_tok, cw, nw):
    T = h1.shape[0]
    tm = min(COMB_TM, T)
    nt = T // tm
    return pl.pallas_call(
        _combine_body,
        grid=(nt,),
        in_specs=[
            pl.BlockSpec((tm, D_MODEL), lambda i: (i, 0)),
            pl.BlockSpec((tm * Y_ROWS, LANES), lambda i: (i, 0)),
            pl.BlockSpec((tm * Y_ROWS, LANES), lambda i: (i + nt, 0)),
            pl.BlockSpec((tm, LANES), lambda i: (i, 0)),
            pl.BlockSpec((1, D_MODEL), lambda i: (0, 0)),
        ],
        out_specs=pl.BlockSpec((tm, D_MODEL), lambda i: (i, 0)),
        out_shape=jax.ShapeDtypeStruct((T, D_MODEL), F32),
        compiler_params=_cparams(("parallel",)),
        name="combine",
    )(h1, y_tok, y_tok, cw, nw)


def _tile_plan(cnt, n_tiles):
    tiles = (cnt + (MOE_TM - 1)) // MOE_TM
    ends = jnp.cumsum(tiles)
    starts = ends - tiles
    n_used = ends[-1]
    step = jnp.arange(n_tiles + 1, dtype=I32)
    tile = jnp.minimum(step, jnp.maximum(n_used - 1, 0))
    tile_expert = jnp.sum((ends[None, :] <= tile[:, None]).astype(I32), axis=1)
    tile_first = ((step == starts[tile_expert]) & (step < n_used)).astype(I32)
    pad_lo = jnp.concatenate([starts * MOE_TM + cnt, (n_used * MOE_TM).reshape(1)])
    pad_hi = jnp.concatenate([ends * MOE_TM, jnp.full((1,), n_tiles * MOE_TM, I32)])
    return tile_expert, tile_first, n_used.reshape(1).astype(I32), pad_lo.astype(I32), pad_hi.astype(I32)


def kernel(x, positions, norm1_w, w_in, conv_w, conv_b, dt_bias, a_log, d_skip, ret_norm_w,
           ssm_norm_w, w_out, norm2_w, w_router_group, b_router_group, w_router_expert,
           b_router_expert, w_expert_gate, w_expert_up, w_expert_down, final_norm_w):
    B, T, D = x.shape
    assert B == 1 and D == D_MODEL and T % CHUNK == 0
    xf = x.reshape(T, D)
    pad_l = lambda v: jnp.pad(v, ((0, 0), (0, LANES - v.shape[-1])))

    w_main = w_in[0][:, :MAIN_PROJ].astype(BF16)
    w_dt = pad_l(w_in[0][:, MAIN_PROJ:]).astype(BF16)
    half = RET_HEAD_DIM // 2
    inv = (ROPE_THETA ** (-jnp.arange(half, dtype=F32) / half)).reshape(1, half)
    tri = (jnp.arange(CHUNK)[:, None] >= jnp.arange(CHUNK)[None, :]).astype(BF16)
    expand = (jnp.arange(LANES)[:, None] == jnp.arange(SSM_WIDTH)[None, :] // SSM_HEAD_DIM).astype(BF16)
    d_skip_e = jnp.repeat(d_skip[0], SSM_HEAD_DIM).reshape(1, SSM_WIDTH)
    wr_t = jnp.zeros((LANES, D), F32)
    wr_t = wr_t.at[0:N_GROUPS].set(w_router_group[0].T).at[8:8 + N_EXPERTS].set(w_router_expert[0].T)
    br = jnp.zeros((LANES,), F32)
    br = br.at[0:N_GROUPS].set(b_router_group[0]).at[8:8 + N_EXPERTS].set(b_router_expert[0])
    tri_u = (jnp.arange(SORT_BLK)[:, None] <= jnp.arange(SORT_BLK)[None, :]).astype(BF16)
    ltri = (jnp.arange(N_EXPERTS)[:, None] > jnp.arange(N_EXPERTS)[None, :]).astype(BF16)

    proj, dt_raw = _inproj(xf, norm1_w[0].reshape(1, D), w_main, w_dt)
    cos, sin = _rope_tables(positions.reshape(T, 1).astype(F32), inv)
    y_ret = _retention(proj, cos, sin, ret_norm_w[0].reshape(1, RET_WIDTH), _retention_consts())
    y_ssm = _ssd(proj, dt_raw, conv_w[0], conv_b[0].reshape(1, -1), pad_l(dt_bias[0].reshape(1, -1)),
                 pad_l(a_log[0].reshape(1, -1)), d_skip_e, ssm_norm_w[0].reshape(1, -1), tri, expand)
    h1, h2p, cw, ids = _out_router(xf, y_ret, y_ssm, w_out[0].astype(BF16), norm2_w[0].reshape(1, D),
                                   wr_t.astype(BF16), br.reshape(LANES, 1))

    dest_blk, cnt = _sort_index(ids, tri_u, ltri)
    dest = dest_blk[:, 0:2, :].transpose(1, 0, 2).reshape(2 * T)
    n_tiles = (2 * T) // MOE_TM + N_EXPERTS
    tile_expert, tile_first, n_used, pad_lo, pad_hi = _tile_plan(cnt[:, 0].astype(I32), n_tiles)
    srow = _sorted_rows(dest, pad_lo, pad_hi, n_tiles * MOE_TM)
    y_tok = _moe(tile_expert, tile_first, n_used, srow, h2p,
                 w_expert_gate[0], w_expert_up[0], w_expert_down[0])
    out = _combine(h1, y_tok, cw, final_norm_w.reshape(1, D))
    return out.reshape(B, T, D)
```

```python
import functools

import jax
import jax.numpy as jnp
from jax import lax
from jax.experimental import pallas as pl
from jax.experimental.pallas import tpu as pltpu

F32 = jnp.float32
BF16 = jnp.bfloat16
I32 = jnp.int32

D_MODEL = 2048
EPS = 1e-6
CHUNK = 128
RET_HEADS = 4
RET_HEAD_DIM = 256
RET_WIDTH = RET_HEADS * RET_HEAD_DIM
ROPE_THETA = 10000.0
SSM_WIDTH = 1024
SSM_HEAD_DIM = 64
SSM_HEADS = SSM_WIDTH // SSM_HEAD_DIM
SSM_GROUPS = 2
SSM_STATE = 128
SSM_CONV = 4
SSM_CONV_DIM = SSM_WIDTH + 2 * SSM_GROUPS * SSM_STATE
MAIN_PROJ = 4 * RET_WIDTH + SSM_WIDTH + SSM_CONV_DIM
N_GROUPS = 4
GROUP_EXPERTS = 8
N_EXPERTS = N_GROUPS * GROUP_EXPERTS
D_EXPERT = 512
LANES = 128
Y_ROWS = D_MODEL // LANES

VMEM_LIMIT = 56 * 1024 * 1024


def _cparams(sem, vmem=VMEM_LIMIT):
    return pltpu.CompilerParams(dimension_semantics=sem, vmem_limit_bytes=vmem)


def _silu(x):
    return x * (1.0 / (1.0 + jnp.exp(-x)))


def _dot(a, b):
    return jnp.dot(a, b, preferred_element_type=F32)


def _dot_nt(a, b):
    return lax.dot_general(a, b, (((1,), (1,)), ((), ())), preferred_element_type=F32)


def _split3(a):
    a1 = a.astype(BF16)
    r1 = a - a1.astype(F32)
    a2 = r1.astype(BF16)
    a3 = (r1 - a2.astype(F32)).astype(BF16)
    return a1, a2, a3


IN_TM = 1024
IN_TN = 512


def _inproj_body(x_ref, nw_ref, w_ref, wdt_ref, o_ref, odt_ref, hn_ref):
    @pl.when(pl.program_id(1) == 0)
    def _():
        def blk(r, carry):
            sl = pl.ds(pl.multiple_of(r * CHUNK, CHUNK), CHUNK)
            x = x_ref[sl, :]
            ms = jnp.mean(x * x, axis=-1, keepdims=True)
            hn_ref[sl, :] = (x * lax.rsqrt(ms + EPS) * nw_ref[...]).astype(BF16)
            return carry
        lax.fori_loop(0, x_ref.shape[0] // CHUNK, blk, 0)
        odt_ref[...] = _dot(hn_ref[...], wdt_ref[...])

    o_ref[...] = _dot(hn_ref[...], w_ref[...])


def _inproj(x, nw, w_main, w_dt):
    T = x.shape[0]
    tm = min(IN_TM, T)
    return pl.pallas_call(
        _inproj_body,
        grid=(T // tm, MAIN_PROJ // IN_TN),
        in_specs=[
            pl.BlockSpec((tm, D_MODEL), lambda i, j: (i, 0)),
            pl.BlockSpec((1, D_MODEL), lambda i, j: (0, 0)),
            pl.BlockSpec((D_MODEL, IN_TN), lambda i, j: (0, j)),
            pl.BlockSpec((D_MODEL, LANES), lambda i, j: (0, 0)),
        ],
        out_specs=[
            pl.BlockSpec((tm, IN_TN), lambda i, j: (i, j)),
            pl.BlockSpec((tm, LANES), lambda i, j: (i, 0)),
        ],
        out_shape=[
            jax.ShapeDtypeStruct((T, MAIN_PROJ), F32),
            jax.ShapeDtypeStruct((T, LANES), F32),
        ],
        scratch_shapes=[pltpu.VMEM((tm, D_MODEL), BF16)],
        compiler_params=_cparams(("parallel", "arbitrary")),
        name="in_proj",
    )(x, nw, w_main, w_dt)


def _rope_body(pos_ref, inv_ref, cos_ref, sin_ref):
    ang = pos_ref[...] * inv_ref[...]
    cos_ref[...] = jnp.cos(ang)
    sin_ref[...] = jnp.sin(ang)


def _rope_tables(pos, inv):
    T = pos.shape[0]
    tm = min(1024, T)
    half = inv.shape[1]
    return pl.pallas_call(
        _rope_body,
        grid=(T // tm,),
        in_specs=[pl.BlockSpec((tm, 1), lambda i: (i, 0)),
                  pl.BlockSpec((1, half), lambda i: (0, 0))],
        out_specs=[pl.BlockSpec((tm, half), lambda i: (i, 0))] * 2,
        out_shape=[jax.ShapeDtypeStruct((T, half), F32)] * 2,
        compiler_params=_cparams(("parallel",)),
        name="rope_tables",
    )(pos, inv)


RET_CB = 2


def _retention_body(cd_ref, q_ref, k_ref, v_ref, g_ref, cos_ref, sin_ref,
                    dec_ref, wq_ref, ws_ref, nw_ref, o_ref, st_ref, *, cb):
    h = pl.program_id(0)

    @pl.when(pl.program_id(1) == 0)
    def _():
        st_ref[...] = jnp.zeros_like(st_ref)

    chunk_decay = cd_ref[h]
    half = RET_HEAD_DIM // 2
    for i in range(cb):
        sl = slice(i * CHUNK, (i + 1) * CHUNK)
        cos = cos_ref[sl, :]
        sin = sin_ref[sl, :]

        def rope(x):
            x1, x2 = x[:, :half], x[:, half:]
            return jnp.concatenate([x1 * cos - x2 * sin, x2 * cos + x1 * sin], axis=-1)

        q = rope(q_ref[sl, :])
        k = rope(k_ref[sl, :]) * (RET_HEAD_DIM ** -0.5)
        v = v_ref[sl, :]
        qb = q.astype(BF16)
        kb = k.astype(BF16)
        scores = _dot_nt(qb, kb) * dec_ref[0]
        y = _dot(scores.astype(BF16), v.astype(BF16))
        st = st_ref[...]
        y = y + _dot(qb, st.astype(BF16)) * wq_ref[0]
        vw = (v * ws_ref[0]).astype(BF16)
        new = _dot(k.T.astype(BF16), vw)
        st_ref[...] = st * chunk_decay + new
        mu = jnp.mean(y, axis=-1, keepdims=True)
        d = y - mu
        var = jnp.mean(d * d, axis=-1, keepdims=True)
        yn = d * lax.rsqrt(var + EPS)
        o_ref[sl, :] = (yn * nw_ref[...] * _silu(g_ref[sl, :])).astype(BF16)


def _retention(proj, cos, sin, ret_nw, consts):
    T = proj.shape[0]
    cb = min(RET_CB, T // CHUNK)
    rows = cb * CHUNK
    chunk_decay, decay_intra, w_query, w_state = consts
    hd = RET_HEAD_DIM
    half = hd // 2

    def col(base):
        return lambda h, c, cd: (c, base + h)

    grid_spec = pltpu.PrefetchScalarGridSpec(
        num_scalar_prefetch=1,
        grid=(RET_HEADS, T // rows),
        in_specs=[
            pl.BlockSpec((rows, hd), col(0)),
            pl.BlockSpec((rows, hd), col(RET_HEADS)),
            pl.BlockSpec((rows, hd), col(2 * RET_HEADS)),
            pl.BlockSpec((rows, hd), col(3 * RET_HEADS)),
            pl.BlockSpec((rows, half), lambda h, c, cd: (c, 0)),
            pl.BlockSpec((rows, half), lambda h, c, cd: (c, 0)),
            pl.BlockSpec((1, CHUNK, CHUNK), lambda h, c, cd: (h, 0, 0)),
            pl.BlockSpec((1, CHUNK, hd), lambda h, c, cd: (h, 0, 0)),
            pl.BlockSpec((1, CHUNK, hd), lambda h, c, cd: (h, 0, 0)),
            pl.BlockSpec((1, hd), lambda h, c, cd: (0, h)),
        ],
        out_specs=pl.BlockSpec((rows, hd), lambda h, c, cd: (c, h)),
        scratch_shapes=[pltpu.VMEM((hd, hd), F32)],
    )
    return pl.pallas_call(
        functools.partial(_retention_body, cb=cb),
        grid_spec=grid_spec,
        out_shape=jax.ShapeDtypeStruct((T, RET_WIDTH), BF16),
        compiler_params=_cparams(("parallel", "arbitrary")),
        name="retention",
    )(chunk_decay, proj, proj, proj, proj, cos, sin, decay_intra, w_query, w_state, ret_nw)


def _retention_consts():
    H, L = RET_HEADS, CHUNK
    log_gamma = jnp.log1p(-(2.0 ** (-5.0 - jnp.arange(H, dtype=F32))))
    idx = jnp.arange(L, dtype=F32)
    diff = idx[:, None] - idx[None, :]
    causal = diff >= 0
    decay_intra = jnp.where(causal[None], jnp.exp(jnp.where(causal, diff, 0.0)[None] * log_gamma[:, None, None]), 0.0)
    w_state = jnp.exp((L - 1.0 - idx)[None, :] * log_gamma[:, None])
    w_query = jnp.exp((idx + 1.0)[None, :] * log_gamma[:, None])
    chunk_decay = jnp.exp(L * log_gamma)
    bc = lambda w: jnp.broadcast_to(w[:, :, None], (H, L, RET_HEAD_DIM))
    return chunk_decay, decay_intra, bc(w_query), bc(w_state)


SSD_CB = 2
CONV_PAD = 8
XBC_BLK = 512


def _ssd_body(x0_ref, x1_ref, x2_ref, z_ref, dt_ref, cw_ref, cbias_ref, dtb_ref,
              alog_ref, dskip_ref, nw_ref, tri_ref, exp_ref, o_ref, xpad_ref, st_ref, *, cb):
    @pl.when(pl.program_id(0) == 0)
    def _():
        xpad_ref[0:CONV_PAD, :] = jnp.zeros((CONV_PAD, SSM_CONV_DIM), F32)
        st_ref[...] = jnp.zeros_like(st_ref)

    L = CHUNK
    gw = SSM_WIDTH // SSM_GROUPS
    rows_i = lax.broadcasted_iota(I32, (L, L), 0)
    cols_i = lax.broadcasted_iota(I32, (L, L), 1)
    causal = rows_i >= cols_i
    lo_lane = lax.broadcasted_iota(I32, (L, LANES), 1) < SSM_HEAD_DIM
    tri = tri_ref[...]
    expand = exp_ref[...]
    a_neg = -jnp.exp(alog_ref[...])

    for i in range(cb):
        sl = slice(i * L, (i + 1) * L)
        for b, xr in enumerate((x0_ref, x1_ref, x2_ref)):
            xpad_ref[CONV_PAD:CONV_PAD + L, b * XBC_BLK:(b + 1) * XBC_BLK] = xr[sl, :]
        u_parts = []
        for b in range(SSM_CONV_DIM // XBC_BLK):
            cs = slice(b * XBC_BLK, (b + 1) * XBC_BLK)
            acc = cbias_ref[:, cs]
            for t in range(SSM_CONV):
                r0 = CONV_PAD - (SSM_CONV - 1) + t
                acc = acc + xpad_ref[r0:r0 + L, cs] * cw_ref[t:t + 1, cs]
            u_parts.append(_silu(acc))
        xpad_ref[0:CONV_PAD, :] = xpad_ref[L:L + CONV_PAD, :]
        xs = jnp.concatenate(u_parts[:2], axis=-1)
        bm = u_parts[2][:, :SSM_GROUPS * SSM_STATE]
        cm = u_parts[2][:, SSM_GROUPS * SSM_STATE:]

        dt_in = dt_ref[sl, :] + dtb_ref[...]
        dt = jnp.maximum(dt_in, 0.0) + jnp.log1p(jnp.exp(-jnp.abs(dt_in)))
        a = dt * a_neg
        a1, a2, a3 = _split3(a)
        a_cs = _dot(tri, a1) + _dot(tri, a2) + _dot(tri, a3)
        d1, d2, d3 = _split3(dt)
        dt_e = _dot(d1, expand) + _dot(d2, expand) + _dot(d3, expand)
        c1, c2, c3 = _split3(a_cs)
        acs_e = _dot(c1, expand) + _dot(c2, expand) + _dot(c3, expand)
        last = acs_e[L - 1:L, :]
        to_end = jnp.exp(last - acs_e)
        from_start = jnp.exp(acs_e)
        chunk_decay = jnp.exp(last)
        x_dt = xs * dt_e
        xw = (x_dt * to_end).astype(BF16)
        acs_t = a_cs.T
        cmb = cm.astype(BF16)
        bmb = bm.astype(BF16)

        ys = []
        for g in range(SSM_GROUPS):
            ns = slice(g * SSM_STATE, (g + 1) * SSM_STATE)
            gs = slice(g * gw, (g + 1) * gw)
            cg = cmb[:, ns]
            cbg = _dot_nt(cg, bmb[:, ns])
            st = st_ref[g]
            y_off = _dot(cg, st.astype(BF16))
            new = _dot(bm[:, ns].T.astype(BF16), xw[:, gs])
            st_ref[g] = st * chunk_decay[:, gs] + new
            for jp in range(gw // LANES):
                h0 = (g * gw + jp * LANES) // SSM_HEAD_DIM
                ms = []
                for hh in (h0, h0 + 1):
                    seg = a_cs[:, hh:hh + 1] - acs_t[hh:hh + 1, :]
                    dec = jnp.exp(jnp.where(causal, seg, -jnp.inf))
                    ms.append((cbg * dec).astype(BF16))
                lhs = jnp.concatenate(ms, axis=1)
                ls = slice(g * gw + jp * LANES, g * gw + (jp + 1) * LANES)
                xp = x_dt[:, ls]
                rhs = jnp.concatenate([jnp.where(lo_lane, xp, 0.0),
                                       jnp.where(lo_lane, 0.0, xp)], axis=0).astype(BF16)
                y_diag = _dot(lhs, rhs)
                ys.append(y_diag + y_off[:, jp * LANES:(jp + 1) * LANES] * from_start[:, ls])
        y = jnp.concatenate(ys, axis=1) + xs * dskip_ref[...]
        y = y * _silu(z_ref[sl, :])
        outs = []
        for g in range(SSM_GROUPS):
            yg = y[:, g * gw:(g + 1) * gw]
            ms_ = jnp.mean(yg * yg, axis=-1, keepdims=True)
            outs.append(yg * lax.rsqrt(ms_ + EPS))
        o_ref[sl, :] = (jnp.concatenate(outs, axis=1) * nw_ref[...]).astype(BF16)


def _ssd(proj, dt_raw, conv_w, conv_b, dt_bias, a_log, d_skip_e, ssm_nw, tri, expand):
    T = proj.shape[0]
    cb = min(SSD_CB, T // CHUNK)
    rows = cb * CHUNK
    xbc0 = (4 * RET_WIDTH + SSM_WIDTH) // XBC_BLK
    zcol = 4 * RET_WIDTH // SSM_WIDTH
    full = lambda shape: pl.BlockSpec(shape, lambda c: (0,) * len(shape))
    return pl.pallas_call(
        functools.partial(_ssd_body, cb=cb),
        grid=(T // rows,),
        in_specs=[
            pl.BlockSpec((rows, XBC_BLK), lambda c: (c, xbc0)),
            pl.BlockSpec((rows, XBC_BLK), lambda c: (c, xbc0 + 1)),
            pl.BlockSpec((rows, XBC_BLK), lambda c: (c, xbc0 + 2)),
            pl.BlockSpec((rows, SSM_WIDTH), lambda c: (c, zcol)),
            pl.BlockSpec((rows, LANES), lambda c: (c, 0)),
            full((SSM_CONV, SSM_CONV_DIM)),
            full((1, SSM_CONV_DIM)),
            full((1, LANES)),
            full((1, LANES)),
            full((1, SSM_WIDTH)),
            full((1, SSM_WIDTH)),
            full((CHUNK, CHUNK)),
            full((LANES, SSM_WIDTH)),
        ],
        out_specs=pl.BlockSpec((rows, SSM_WIDTH), lambda c: (c, 0)),
        out_shape=jax.ShapeDtypeStruct((T, SSM_WIDTH), BF16),
        scratch_shapes=[
            pltpu.VMEM((CHUNK + CONV_PAD, SSM_CONV_DIM), F32),
            pltpu.VMEM((SSM_GROUPS, SSM_STATE, SSM_WIDTH // SSM_GROUPS), F32),
        ],
        compiler_params=_cparams(("arbitrary",)),
        name="ssd",
    )(proj, proj, proj, proj, dt_raw, conv_w, conv_b, dt_bias, a_log, d_skip_e, ssm_nw, tri, expand)


OUT_TM = 256
ROUTE_ROWS = 8 + N_EXPERTS


def _out_router_body(x_ref, yr_ref, ys_ref, wo_ref, nw_ref, wr_ref, br_ref,
                     h1_ref, h2t_ref, cw_ref, ids_ref):
    tm = x_ref.shape[0]
    h1 = x_ref[...] + _dot(yr_ref[...], wo_ref[0:RET_WIDTH, :]) + _dot(ys_ref[...], wo_ref[RET_WIDTH:, :])
    h1_ref[...] = h1
    ms = jnp.mean(h1 * h1, axis=-1, keepdims=True)
    h2 = h1 * lax.rsqrt(ms + EPS) * nw_ref[...]
    for s in range(Y_ROWS):
        h2t_ref[:, s, :] = h2[:, s * LANES:(s + 1) * LANES]

    logits = _dot_nt(wr_ref[...], h2.astype(BF16)) + br_ref[...]
    row = lax.broadcasted_iota(I32, (8, tm), 0)
    lg = jnp.where(row < N_GROUPS, logits[0:8], -jnp.inf)
    m = jnp.max(lg, axis=0, keepdims=True)
    p_sel = 1.0 / jnp.sum(jnp.exp(lg - m), axis=0, keepdims=True)
    g_sel = jnp.min(jnp.where(lg == m, row, 8), axis=0, keepdims=True)
    le = jnp.zeros((GROUP_EXPERTS, tm), F32)
    for g in range(N_GROUPS):
        le = jnp.where(g_sel == g, logits[8 + g * GROUP_EXPERTS:8 + (g + 1) * GROUP_EXPERTS], le)
    m2 = jnp.max(le, axis=0, keepdims=True)
    ee = jnp.exp(le - m2)
    pe = ee / jnp.sum(ee, axis=0, keepdims=True)
    v1 = jnp.max(pe, axis=0, keepdims=True)
    i1 = jnp.min(jnp.where(pe == v1, row, 8), axis=0, keepdims=True)
    pe2 = jnp.where(row == i1, -1.0, pe)
    v2 = jnp.max(pe2, axis=0, keepdims=True)
    i2 = jnp.min(jnp.where(pe2 == v2, row, 8), axis=0, keepdims=True)
    tw = v1 + v2
    c1 = v1 / tw * p_sel
    c2 = v2 / tw * p_sel
    e1 = g_sel * GROUP_EXPERTS + i1
    e2 = g_sel * GROUP_EXPERTS + i2
    ids = jnp.where(row == 0, e1, jnp.where(row == 1, e2, 0))
    for b in range(tm // SORT_BLK):
        ids_ref[b] = ids[:, b * SORT_BLK:(b + 1) * SORT_BLK]
    cw8 = jnp.where(row == 0, c1, jnp.where(row == 1, c2, 0.0))
    cw = jnp.concatenate([cw8, jnp.zeros((LANES - 8, tm), F32)], axis=0)
    cw_ref[...] = cw.T


def _out_router(x, y_ret, y_ssm, w_out, nw, wr_t, br):
    T = x.shape[0]
    tm = min(OUT_TM, T)
    full = lambda shape: pl.BlockSpec(shape, lambda i: (0,) * len(shape))
    return pl.pallas_call(
        _out_router_body,
        grid=(T // tm,),
        in_specs=[
            pl.BlockSpec((tm, D_MODEL), lambda i: (i, 0)),
            pl.BlockSpec((tm, RET_WIDTH), lambda i: (i, 0)),
            pl.BlockSpec((tm, SSM_WIDTH), lambda i: (i, 0)),
            full((D_MODEL, D_MODEL)),
            full((1, D_MODEL)),
            full((LANES, D_MODEL)),
            full((LANES, 1)),
        ],
        out_specs=[
            pl.BlockSpec((tm, D_MODEL), lambda i: (i, 0)),
            pl.BlockSpec((tm, Y_ROWS, LANES), lambda i: (i, 0, 0)),
            pl.BlockSpec((tm, LANES), lambda i: (i, 0)),
            pl.BlockSpec((tm // SORT_BLK, 8, SORT_BLK), lambda i: (i, 0, 0)),
        ],
        out_shape=[
            jax.ShapeDtypeStruct((T, D_MODEL), F32),
            jax.ShapeDtypeStruct((T, Y_ROWS, LANES), F32),
            jax.ShapeDtypeStruct((T, LANES), F32),
            jax.ShapeDtypeStruct((T // SORT_BLK, 8, SORT_BLK), I32),
        ],
        compiler_params=_cparams(("parallel",)),
        name="out_router",
    )(x, y_ret, y_ssm, w_out, nw, wr_t, br)


SORT_BLK = 256
MOE_TM = 256


def _sort_index_body(ids_ref, tri_ref, ltri_ref, dest_ref, cnt_ref, rank_ref):
    nblk = ids_ref.shape[0]
    row_e = lax.broadcasted_iota(I32, (N_EXPERTS, SORT_BLK), 0)
    row8 = lax.broadcasted_iota(I32, (8, SORT_BLK), 0)

    def onehots(b):
        ids = ids_ref[b]
        return row_e == ids[0:1], row_e == ids[1:2]

    def rank_blk(b, carry):
        oh1, oh2 = onehots(b)
        ohf = jnp.where(oh1 | oh2, 1.0, 0.0)
        incl = _dot(ohf.astype(BF16), tri_ref[...])
        base = carry + incl - 1.0
        r1 = jnp.sum(jnp.where(oh1, base, 0.0), axis=0, keepdims=True)
        r2 = jnp.sum(jnp.where(oh2, base, 0.0), axis=0, keepdims=True)
        rank_ref[b] = jnp.where(row8 == 0, r1, jnp.where(row8 == 1, r2, 0.0))
        return carry + jnp.sum(ohf, axis=1, keepdims=True)

    cnt = lax.fori_loop(0, nblk, rank_blk, jnp.zeros((N_EXPERTS, 1), F32))
    cnt_ref[...] = jnp.broadcast_to(cnt, cnt_ref.shape)
    tiles = jnp.floor((cnt + (MOE_TM - 1.0)) / MOE_TM)
    tiles_b = jnp.broadcast_to(tiles, (N_EXPERTS, LANES)).astype(BF16)
    off = _dot(ltri_ref[...], tiles_b)[:, 0:1] * MOE_TM

    def dest_blk(b, carry):
        oh1, oh2 = onehots(b)
        o1 = jnp.sum(jnp.where(oh1, off, 0.0), axis=0, keepdims=True)
        o2 = jnp.sum(jnp.where(oh2, off, 0.0), axis=0, keepdims=True)
        d = rank_ref[b] + jnp.where(row8 == 0, o1, jnp.where(row8 == 1, o2, 0.0))
        dest_ref[b] = d.astype(I32)
        return carry

    lax.fori_loop(0, nblk, dest_blk, 0)


def _sort_index(ids, tri_u, ltri):
    nblk = ids.shape[0]
    return pl.pallas_call(
        _sort_index_body,
        out_shape=[jax.ShapeDtypeStruct((nblk, 8, SORT_BLK), I32),
                   jax.ShapeDtypeStruct((N_EXPERTS, LANES), F32)],
        scratch_shapes=[pltpu.VMEM((nblk, 8, SORT_BLK), F32)],
        compiler_params=_cparams(None),
        name="sort_index",
    )(ids, tri_u, ltri)


ROW_DMA_PRIORITY = 1


def _moe_body(te_ref, tf_ref, nx_ref, nu_ref, dest_ref, lo_ref, hi_ref,
              h2t_ref, wg_ref, wu_ref, wd_ref, yt_ref,
              srow_ref, xbuf, ystage, wsg, wsu, wsd, wgb_ref, wub_ref, wdb_ref, gsem, ssem, wsem):
    i = pl.program_id(0)
    nu = nu_ref[0]
    T = h2t_ref.shape[0]

    def gather_row(tile, slot, r):
        tok = srow_ref[tile * MOE_TM + r] & (T - 1)
        return pltpu.make_async_copy(h2t_ref.at[tok], xbuf.at[slot, :, r, :], gsem.at[slot])

    def scatter_row(tile, slot, r):
        return pltpu.make_async_copy(ystage.at[slot, :, r, :], yt_ref.at[srow_ref[tile * MOE_TM + r]],
                                     ssem.at[slot])

    def gather_tile(slot):
        return pltpu.make_async_copy(h2t_ref.at[pl.ds(0, MOE_TM)], h2t_ref.at[pl.ds(0, MOE_TM)], gsem.at[slot])

    def scatter_tile(slot):
        return pltpu.make_async_copy(yt_ref.at[pl.ds(0, MOE_TM)], yt_ref.at[pl.ds(0, MOE_TM)], ssem.at[slot])

    def weight_copies(e):
        return (pltpu.make_async_copy(wg_ref.at[e], wsg, wsem.at[0]),
                pltpu.make_async_copy(wu_ref.at[e], wsu, wsem.at[1]),
                pltpu.make_async_copy(wd_ref.at[e], wsd, wsem.at[2]))

    @pl.when(i == 0)
    def _():
        for c in weight_copies(te_ref[0]):
            c.start()

        def tok(t, c):
            srow_ref[dest_ref[t]] = t
            srow_ref[dest_ref[T + t]] = T + t
            return c

        lax.fori_loop(0, T, tok, 0, unroll=8)

        def seg(e, c):
            def pad(r, c2):
                srow_ref[r] = 2 * T + (r & (MOE_TM - 1))
                return c2

            lax.fori_loop(lo_ref[e], hi_ref[e], pad, 0)
            return c

        lax.fori_loop(0, lo_ref.shape[0], seg, 0)

        ystage[...] = jnp.zeros_like(ystage)

        def spare(r, c):
            pltpu.make_async_copy(ystage.at[0, :, r, :], yt_ref.at[2 * T + r], ssem.at[0]).start()
            return c

        lax.fori_loop(0, MOE_TM, spare, 0)
        scatter_tile(0).wait()
        for r in range(MOE_TM):
            gather_row(0, 0, r).start(priority=ROW_DMA_PRIORITY)

    @pl.when(tf_ref[i] == 1)
    def _():
        for c in weight_copies(te_ref[i]):
            c.wait()
        wgb_ref[...] = wsg[...].astype(BF16)
        wub_ref[...] = wsu[...].astype(BF16)
        wdb_ref[...] = wsd[...].astype(BF16)

        @pl.when(nx_ref[i] >= 0)
        def _():
            for c in weight_copies(nx_ref[i]):
                c.start()

    slot = lax.rem(i, 2)
    other = 1 - slot

    @pl.when(i < nu)
    def _():
        gather_tile(slot).wait()

        @pl.when(i >= 1)
        def _():
            scatter_tile(slot).wait()

        nxt = jnp.minimum(i + 1, nu - 1)
        prv = jnp.maximum(i - 1, 0)
        for r in range(MOE_TM):
            gather_row(nxt, other, r).start(priority=ROW_DMA_PRIORITY)
            scatter_row(prv, other, r).start(priority=ROW_DMA_PRIORITY)

        x = jnp.concatenate([xbuf[slot, s].astype(BF16) for s in range(Y_ROWS)], axis=1)
        a = _dot(x, wgb_ref[...])
        u = _dot(x, wub_ref[...])
        act = (_silu(a) * u).astype(BF16)
        y = _dot(act, wdb_ref[...])
        for s in range(Y_ROWS):
            ystage[slot, s] = y[:, s * LANES:(s + 1) * LANES]

    @pl.when(i == nu)
    def _():
        gather_tile(slot).wait()
        scatter_tile(slot).wait()

        def last(r, c):
            scatter_row(nu - 1, other, r).start()
            return c

        lax.fori_loop(0, MOE_TM, last, 0)
        scatter_tile(other).wait()


def _moe(tile_expert, tile_first, next_expert, n_used, dest, pad_lo, pad_hi, h2t, w_gate, w_up, w_down):
    T = h2t.shape[0]
    assert T & (T - 1) == 0 and T >= MOE_TM, "token index is recovered from the row table by masking"
    n_tiles = tile_expert.shape[0] - 1
    hbm = pl.BlockSpec(memory_space=pl.ANY)
    grid_spec = pltpu.PrefetchScalarGridSpec(
        num_scalar_prefetch=7,
        grid=(n_tiles + 1,),
        in_specs=[hbm, hbm, hbm, hbm],
        out_specs=hbm,
        scratch_shapes=[pltpu.SMEM((n_tiles * MOE_TM,), I32),
                        pltpu.VMEM((2, Y_ROWS, MOE_TM, LANES), F32),
                        pltpu.VMEM((2, Y_ROWS, MOE_TM, LANES), F32),
                        pltpu.VMEM((D_MODEL, D_EXPERT), F32),
                        pltpu.VMEM((D_MODEL, D_EXPERT), F32),
                        pltpu.VMEM((D_EXPERT, D_MODEL), F32),
                        pltpu.VMEM((D_MODEL, D_EXPERT), BF16),
                        pltpu.VMEM((D_MODEL, D_EXPERT), BF16),
                        pltpu.VMEM((D_EXPERT, D_MODEL), BF16),
                        pltpu.SemaphoreType.DMA((2,)),
                        pltpu.SemaphoreType.DMA((2,)),
                        pltpu.SemaphoreType.DMA((3,))],
    )
    return pl.pallas_call(
        _moe_body,
        grid_spec=grid_spec,
        out_shape=jax.ShapeDtypeStruct((2 * T + MOE_TM, Y_ROWS, LANES), F32),
        compiler_params=_cparams(("arbitrary",)),
        name="moe",
    )(tile_expert, tile_first, next_expert, n_used, dest, pad_lo, pad_hi, h2t, w_gate, w_up, w_down)


COMB_TM = 256


def _combine_body(h1_ref, y0_ref, y1_ref, cw_ref, nw_ref, o_ref):
    tm = h1_ref.shape[0]
    c0 = cw_ref[:, 0:1]
    c1 = cw_ref[:, 1:2]
    cols = []
    for s in range(Y_ROWS):
        moe = c0 * y0_ref[:, s, :] + c1 * y1_ref[:, s, :]
        cols.append(h1_ref[:, s * LANES:(s + 1) * LANES] + moe)
    h = jnp.concatenate(cols, axis=1)
    ms = jnp.mean(h * h, axis=-1, keepdims=True)
    o_ref[...] = h * lax.rsqrt(ms + EPS) * nw_ref[...]


def _combine(h1, y_tok, cw, nw):
    T = h1.shape[0]
    tm = min(COMB_TM, T)
    nt = T // tm
    return pl.pallas_call(
        _combine_body,
        grid=(nt,),
        in_specs=[
            pl.BlockSpec((tm, D_MODEL), lambda i: (i, 0)),
            pl.BlockSpec((tm, Y_ROWS, LANES), lambda i: (i, 0, 0)),
            pl.BlockSpec((tm, Y_ROWS, LANES), lambda i: (i + nt, 0, 0)),
            pl.BlockSpec((tm, LANES), lambda i: (i, 0)),
            pl.BlockSpec((1, D_MODEL), lambda i: (0, 0)),
        ],
        out_specs=pl.BlockSpec((tm, D_MODEL), lambda i: (i, 0)),
        out_shape=jax.ShapeDtypeStruct((T, D_MODEL), F32),
        compiler_params=_cparams(("parallel",)),
        name="combine",
    )(h1, y_tok, y_tok, cw, nw)


def _tile_plan(cnt, n_tiles):
    tiles = (cnt + (MOE_TM - 1)) // MOE_TM
    ends = jnp.cumsum(tiles)
    starts = ends - tiles
    n_used = ends[-1]
    step = jnp.arange(n_tiles + 1, dtype=I32)
    tile = jnp.minimum(step, jnp.maximum(n_used - 1, 0))
    tile_expert = jnp.sum((ends[None, :] <= tile[:, None]).astype(I32), axis=1)
    tile_first = ((step == starts[tile_expert]) & (step < n_used)).astype(I32)
    nxt_tile = ends[tile_expert]
    nxt_expert = jnp.sum((ends[None, :] <= nxt_tile[:, None]).astype(I32), axis=1)
    next_expert = jnp.where(nxt_tile < n_used, nxt_expert, -1).astype(I32)
    pad_lo = starts * MOE_TM + cnt
    pad_hi = ends * MOE_TM
    return (tile_expert, tile_first, next_expert, n_used.reshape(1).astype(I32),
            pad_lo.astype(I32), pad_hi.astype(I32))


def kernel(x, positions, norm1_w, w_in, conv_w, conv_b, dt_bias, a_log, d_skip, ret_norm_w,
           ssm_norm_w, w_out, norm2_w, w_router_group, b_router_group, w_router_expert,
           b_router_expert, w_expert_gate, w_expert_up, w_expert_down, final_norm_w):
    B, T, D = x.shape
    assert B == 1 and D == D_MODEL and T % CHUNK == 0
    xf = x.reshape(T, D)
    pad_l = lambda v: jnp.pad(v, ((0, 0), (0, LANES - v.shape[-1])))

    w_main = w_in[0][:, :MAIN_PROJ].astype(BF16)
    w_dt = pad_l(w_in[0][:, MAIN_PROJ:]).astype(BF16)
    half = RET_HEAD_DIM // 2
    inv = (ROPE_THETA ** (-jnp.arange(half, dtype=F32) / half)).reshape(1, half)
    tri = (jnp.arange(CHUNK)[:, None] >= jnp.arange(CHUNK)[None, :]).astype(BF16)
    expand = (jnp.arange(LANES)[:, None] == jnp.arange(SSM_WIDTH)[None, :] // SSM_HEAD_DIM).astype(BF16)
    d_skip_e = jnp.repeat(d_skip[0], SSM_HEAD_DIM).reshape(1, SSM_WIDTH)
    wr_t = jnp.zeros((LANES, D), F32)
    wr_t = wr_t.at[0:N_GROUPS].set(w_router_group[0].T).at[8:8 + N_EXPERTS].set(w_router_expert[0].T)
    br = jnp.zeros((LANES,), F32)
    br = br.at[0:N_GROUPS].set(b_router_group[0]).at[8:8 + N_EXPERTS].set(b_router_expert[0])
    tri_u = (jnp.arange(SORT_BLK)[:, None] <= jnp.arange(SORT_BLK)[None, :]).astype(BF16)
    ltri = (jnp.arange(N_EXPERTS)[:, None] > jnp.arange(N_EXPERTS)[None, :]).astype(BF16)

    proj, dt_raw = _inproj(xf, norm1_w[0].reshape(1, D), w_main, w_dt)
    cos, sin = _rope_tables(positions.reshape(T, 1).astype(F32), inv)
    y_ret = _retention(proj, cos, sin, ret_norm_w[0].reshape(1, RET_WIDTH), _retention_consts())
    y_ssm = _ssd(proj, dt_raw, conv_w[0], conv_b[0].reshape(1, -1), pad_l(dt_bias[0].reshape(1, -1)),
                 pad_l(a_log[0].reshape(1, -1)), d_skip_e, ssm_norm_w[0].reshape(1, -1), tri, expand)
    h1, h2t, cw, ids = _out_router(xf, y_ret, y_ssm, w_out[0].astype(BF16), norm2_w[0].reshape(1, D),
                                   wr_t.astype(BF16), br.reshape(LANES, 1))

    dest_blk, cnt = _sort_index(ids, tri_u, ltri)
    dest = dest_blk[:, 0:2, :].transpose(1, 0, 2).reshape(2 * T)
    n_tiles = (2 * T) // MOE_TM + N_EXPERTS
    tile_expert, tile_first, next_expert, n_used, pad_lo, pad_hi = _tile_plan(cnt[:, 0].astype(I32), n_tiles)
    y_tok = _moe(tile_expert, tile_first, next_expert, n_used, dest, pad_lo, pad_hi, h2t,
                 w_expert_gate[0], w_expert_up[0], w_expert_down[0])
    out = _combine(h1, y_tok, cw, final_norm_w.reshape(1, D))
    return out.reshape(B, T, D)
```

```python
import functools

import jax
import jax.numpy as jnp
from jax import lax
from jax.experimental import pallas as pl
from jax.experimental.pallas import tpu as pltpu

F32 = jnp.float32
BF16 = jnp.bfloat16
I32 = jnp.int32

D_MODEL = 2048
EPS = 1e-6
CHUNK = 128
RET_HEADS = 4
RET_HEAD_DIM = 256
RET_WIDTH = RET_HEADS * RET_HEAD_DIM
ROPE_THETA = 10000.0
SSM_WIDTH = 1024
SSM_HEAD_DIM = 64
SSM_HEADS = SSM_WIDTH // SSM_HEAD_DIM
SSM_GROUPS = 2
SSM_STATE = 128
SSM_CONV = 4
SSM_CONV_DIM = SSM_WIDTH + 2 * SSM_GROUPS * SSM_STATE
MAIN_PROJ = 4 * RET_WIDTH + SSM_WIDTH + SSM_CONV_DIM
N_GROUPS = 4
GROUP_EXPERTS = 8
N_EXPERTS = N_GROUPS * GROUP_EXPERTS
D_EXPERT = 512
LANES = 128
Y_ROWS = D_MODEL // LANES

VMEM_LIMIT = 56 * 1024 * 1024


def _cparams(sem, vmem=VMEM_LIMIT):
    return pltpu.CompilerParams(dimension_semantics=sem, vmem_limit_bytes=vmem)


def _silu(x):
    return x * (1.0 / (1.0 + jnp.exp(-x)))


def _dot(a, b):
    return jnp.dot(a, b, preferred_element_type=F32)


def _dot_nt(a, b):
    return lax.dot_general(a, b, (((1,), (1,)), ((), ())), preferred_element_type=F32)


def _split3(a):
    a1 = a.astype(BF16)
    r1 = a - a1.astype(F32)
    a2 = r1.astype(BF16)
    a3 = (r1 - a2.astype(F32)).astype(BF16)
    return a1, a2, a3


IN_TM = 1024
IN_TN = 512


def _inproj_body(x_ref, nw_ref, w_ref, wdt_ref, o_ref, odt_ref, hn_ref):
    @pl.when(pl.program_id(1) == 0)
    def _():
        def blk(r, carry):
            sl = pl.ds(pl.multiple_of(r * CHUNK, CHUNK), CHUNK)
            x = x_ref[sl, :]
            ms = jnp.mean(x * x, axis=-1, keepdims=True)
            hn_ref[sl, :] = (x * lax.rsqrt(ms + EPS) * nw_ref[...]).astype(BF16)
            return carry
        lax.fori_loop(0, x_ref.shape[0] // CHUNK, blk, 0)
        odt_ref[...] = _dot(hn_ref[...], wdt_ref[...])

    o_ref[...] = _dot(hn_ref[...], w_ref[...])


def _inproj(x, nw, w_main, w_dt):
    T = x.shape[0]
    tm = min(IN_TM, T)
    return pl.pallas_call(
        _inproj_body,
        grid=(T // tm, MAIN_PROJ // IN_TN),
        in_specs=[
            pl.BlockSpec((tm, D_MODEL), lambda i, j: (i, 0)),
            pl.BlockSpec((1, D_MODEL), lambda i, j: (0, 0)),
            pl.BlockSpec((D_MODEL, IN_TN), lambda i, j: (0, j)),
            pl.BlockSpec((D_MODEL, LANES), lambda i, j: (0, 0)),
        ],
        out_specs=[
            pl.BlockSpec((tm, IN_TN), lambda i, j: (i, j)),
            pl.BlockSpec((tm, LANES), lambda i, j: (i, 0)),
        ],
        out_shape=[
            jax.ShapeDtypeStruct((T, MAIN_PROJ), F32),
            jax.ShapeDtypeStruct((T, LANES), F32),
        ],
        scratch_shapes=[pltpu.VMEM((tm, D_MODEL), BF16)],
        compiler_params=_cparams(("parallel", "arbitrary")),
        name="in_proj",
    )(x, nw, w_main, w_dt)


def _rope_body(pos_ref, inv_ref, cos_ref, sin_ref):
    ang = pos_ref[...] * inv_ref[...]
    cos_ref[...] = jnp.cos(ang)
    sin_ref[...] = jnp.sin(ang)


def _rope_tables(pos, inv):
    T = pos.shape[0]
    tm = min(1024, T)
    half = inv.shape[1]
    return pl.pallas_call(
        _rope_body,
        grid=(T // tm,),
        in_specs=[pl.BlockSpec((tm, 1), lambda i: (i, 0)),
                  pl.BlockSpec((1, half), lambda i: (0, 0))],
        out_specs=[pl.BlockSpec((tm, half), lambda i: (i, 0))] * 2,
        out_shape=[jax.ShapeDtypeStruct((T, half), F32)] * 2,
        compiler_params=_cparams(("parallel",)),
        name="rope_tables",
    )(pos, inv)


RET_CB = 2


def _retention_body(cd_ref, q_ref, k_ref, v_ref, g_ref, cos_ref, sin_ref,
                    dec_ref, wq_ref, ws_ref, nw_ref, o_ref, st_ref, *, cb):
    h = pl.program_id(0)

    @pl.when(pl.program_id(1) == 0)
    def _():
        st_ref[...] = jnp.zeros_like(st_ref)

    chunk_decay = cd_ref[h]
    half = RET_HEAD_DIM // 2
    for i in range(cb):
        sl = slice(i * CHUNK, (i + 1) * CHUNK)
        cos = cos_ref[sl, :]
        sin = sin_ref[sl, :]

        def rope(x):
            x1, x2 = x[:, :half], x[:, half:]
            return jnp.concatenate([x1 * cos - x2 * sin, x2 * cos + x1 * sin], axis=-1)

        q = rope(q_ref[sl, :])
        k = rope(k_ref[sl, :]) * (RET_HEAD_DIM ** -0.5)
        v = v_ref[sl, :]
        qb = q.astype(BF16)
        kb = k.astype(BF16)
        scores = _dot_nt(qb, kb) * dec_ref[0]
        y = _dot(scores.astype(BF16), v.astype(BF16))
        st = st_ref[...]
        y = y + _dot(qb, st.astype(BF16)) * wq_ref[0]
        vw = (v * ws_ref[0]).astype(BF16)
        new = _dot(k.T.astype(BF16), vw)
        st_ref[...] = st * chunk_decay + new
        mu = jnp.mean(y, axis=-1, keepdims=True)
        d = y - mu
        var = jnp.mean(d * d, axis=-1, keepdims=True)
        yn = d * lax.rsqrt(var + EPS)
        o_ref[sl, :] = (yn * nw_ref[...] * _silu(g_ref[sl, :])).astype(BF16)


def _retention(proj, cos, sin, ret_nw, consts):
    T = proj.shape[0]
    cb = min(RET_CB, T // CHUNK)
    rows = cb * CHUNK
    chunk_decay, decay_intra, w_query, w_state = consts
    hd = RET_HEAD_DIM
    half = hd // 2

    def col(base):
        return lambda h, c, cd: (c, base + h)

    grid_spec = pltpu.PrefetchScalarGridSpec(
        num_scalar_prefetch=1,
        grid=(RET_HEADS, T // rows),
        in_specs=[
            pl.BlockSpec((rows, hd), col(0)),
            pl.BlockSpec((rows, hd), col(RET_HEADS)),
            pl.BlockSpec((rows, hd), col(2 * RET_HEADS)),
            pl.BlockSpec((rows, hd), col(3 * RET_HEADS)),
            pl.BlockSpec((rows, half), lambda h, c, cd: (c, 0)),
            pl.BlockSpec((rows, half), lambda h, c, cd: (c, 0)),
            pl.BlockSpec((1, CHUNK, CHUNK), lambda h, c, cd: (h, 0, 0)),
            pl.BlockSpec((1, CHUNK, hd), lambda h, c, cd: (h, 0, 0)),
            pl.BlockSpec((1, CHUNK, hd), lambda h, c, cd: (h, 0, 0)),
            pl.BlockSpec((1, hd), lambda h, c, cd: (0, h)),
        ],
        out_specs=pl.BlockSpec((rows, hd), lambda h, c, cd: (c, h)),
        scratch_shapes=[pltpu.VMEM((hd, hd), F32)],
    )
    return pl.pallas_call(
        functools.partial(_retention_body, cb=cb),
        grid_spec=grid_spec,
        out_shape=jax.ShapeDtypeStruct((T, RET_WIDTH), BF16),
        compiler_params=_cparams(("parallel", "arbitrary")),
        name="retention",
    )(chunk_decay, proj, proj, proj, proj, cos, sin, decay_intra, w_query, w_state, ret_nw)


def _retention_consts():
    H, L = RET_HEADS, CHUNK
    log_gamma = jnp.log1p(-(2.0 ** (-5.0 - jnp.arange(H, dtype=F32))))
    idx = jnp.arange(L, dtype=F32)
    diff = idx[:, None] - idx[None, :]
    causal = diff >= 0
    decay_intra = jnp.where(causal[None], jnp.exp(jnp.where(causal, diff, 0.0)[None] * log_gamma[:, None, None]), 0.0)
    w_state = jnp.exp((L - 1.0 - idx)[None, :] * log_gamma[:, None])
    w_query = jnp.exp((idx + 1.0)[None, :] * log_gamma[:, None])
    chunk_decay = jnp.exp(L * log_gamma)
    bc = lambda w: jnp.broadcast_to(w[:, :, None], (H, L, RET_HEAD_DIM))
    return chunk_decay, decay_intra, bc(w_query), bc(w_state)


SSD_CB = 2
CONV_PAD = 8
XBC_BLK = 512


def _ssd_body(x0_ref, x1_ref, x2_ref, z_ref, dt_ref, cw_ref, cbias_ref, dtb_ref,
              alog_ref, dskip_ref, nw_ref, tri_ref, exp_ref, o_ref, xpad_ref, st_ref, *, cb):
    @pl.when(pl.program_id(0) == 0)
    def _():
        xpad_ref[0:CONV_PAD, :] = jnp.zeros((CONV_PAD, SSM_CONV_DIM), F32)
        st_ref[...] = jnp.zeros_like(st_ref)

    L = CHUNK
    gw = SSM_WIDTH // SSM_GROUPS
    rows_i = lax.broadcasted_iota(I32, (L, L), 0)
    cols_i = lax.broadcasted_iota(I32, (L, L), 1)
    causal = rows_i >= cols_i
    lo_lane = lax.broadcasted_iota(I32, (L, LANES), 1) < SSM_HEAD_DIM
    tri = tri_ref[...]
    expand = exp_ref[...]
    a_neg = -jnp.exp(alog_ref[...])

    for i in range(cb):
        sl = slice(i * L, (i + 1) * L)
        for b, xr in enumerate((x0_ref, x1_ref, x2_ref)):
            xpad_ref[CONV_PAD:CONV_PAD + L, b * XBC_BLK:(b + 1) * XBC_BLK] = xr[sl, :]
        u_parts = []
        for b in range(SSM_CONV_DIM // XBC_BLK):
            cs = slice(b * XBC_BLK, (b + 1) * XBC_BLK)
            acc = cbias_ref[:, cs]
            for t in range(SSM_CONV):
                r0 = CONV_PAD - (SSM_CONV - 1) + t
                acc = acc + xpad_ref[r0:r0 + L, cs] * cw_ref[t:t + 1, cs]
            u_parts.append(_silu(acc))
        xpad_ref[0:CONV_PAD, :] = xpad_ref[L:L + CONV_PAD, :]
        xs = jnp.concatenate(u_parts[:2], axis=-1)
        bm = u_parts[2][:, :SSM_GROUPS * SSM_STATE]
        cm = u_parts[2][:, SSM_GROUPS * SSM_STATE:]

        dt_in = dt_ref[sl, :] + dtb_ref[...]
        dt = jnp.maximum(dt_in, 0.0) + jnp.log1p(jnp.exp(-jnp.abs(dt_in)))
        a = dt * a_neg
        a1, a2, a3 = _split3(a)
        a_cs = _dot(tri, a1) + _dot(tri, a2) + _dot(tri, a3)
        d1, d2, d3 = _split3(dt)
        dt_e = _dot(d1, expand) + _dot(d2, expand) + _dot(d3, expand)
        c1, c2, c3 = _split3(a_cs)
        acs_e = _dot(c1, expand) + _dot(c2, expand) + _dot(c3, expand)
        last = acs_e[L - 1:L, :]
        to_end = jnp.exp(last - acs_e)
        from_start = jnp.exp(acs_e)
        chunk_decay = jnp.exp(last)
        x_dt = xs * dt_e
        xw = (x_dt * to_end).astype(BF16)
        acs_t = a_cs.T
        cmb = cm.astype(BF16)
        bmb = bm.astype(BF16)

        ys = []
        for g in range(SSM_GROUPS):
            ns = slice(g * SSM_STATE, (g + 1) * SSM_STATE)
            gs = slice(g * gw, (g + 1) * gw)
            cg = cmb[:, ns]
            cbg = _dot_nt(cg, bmb[:, ns])
            st = st_ref[g]
            y_off = _dot(cg, st.astype(BF16))
            new = _dot(bm[:, ns].T.astype(BF16), xw[:, gs])
            st_ref[g] = st * chunk_decay[:, gs] + new
            for jp in range(gw // LANES):
                h0 = (g * gw + jp * LANES) // SSM_HEAD_DIM
                ms = []
                for hh in (h0, h0 + 1):
                    seg = a_cs[:, hh:hh + 1] - acs_t[hh:hh + 1, :]
                    dec = jnp.exp(jnp.where(causal, seg, -jnp.inf))
                    ms.append((cbg * dec).astype(BF16))
                lhs = jnp.concatenate(ms, axis=1)
                ls = slice(g * gw + jp * LANES, g * gw + (jp + 1) * LANES)
                xp = x_dt[:, ls]
                rhs = jnp.concatenate([jnp.where(lo_lane, xp, 0.0),
                                       jnp.where(lo_lane, 0.0, xp)], axis=0).astype(BF16)
                y_diag = _dot(lhs, rhs)
                ys.append(y_diag + y_off[:, jp * LANES:(jp + 1) * LANES] * from_start[:, ls])
        y = jnp.concatenate(ys, axis=1) + xs * dskip_ref[...]
        y = y * _silu(z_ref[sl, :])
        outs = []
        for g in range(SSM_GROUPS):
            yg = y[:, g * gw:(g + 1) * gw]
            ms_ = jnp.mean(yg * yg, axis=-1, keepdims=True)
            outs.append(yg * lax.rsqrt(ms_ + EPS))
        o_ref[sl, :] = (jnp.concatenate(outs, axis=1) * nw_ref[...]).astype(BF16)


def _ssd(proj, dt_raw, conv_w, conv_b, dt_bias, a_log, d_skip_e, ssm_nw, tri, expand):
    T = proj.shape[0]
    cb = min(SSD_CB, T // CHUNK)
    rows = cb * CHUNK
    xbc0 = (4 * RET_WIDTH + SSM_WIDTH) // XBC_BLK
    zcol = 4 * RET_WIDTH // SSM_WIDTH
    full = lambda shape: pl.BlockSpec(shape, lambda c: (0,) * len(shape))
    return pl.pallas_call(
        functools.partial(_ssd_body, cb=cb),
        grid=(T // rows,),
        in_specs=[
            pl.BlockSpec((rows, XBC_BLK), lambda c: (c, xbc0)),
            pl.BlockSpec((rows, XBC_BLK), lambda c: (c, xbc0 + 1)),
            pl.BlockSpec((rows, XBC_BLK), lambda c: (c, xbc0 + 2)),
            pl.BlockSpec((rows, SSM_WIDTH), lambda c: (c, zcol)),
            pl.BlockSpec((rows, LANES), lambda c: (c, 0)),
            full((SSM_CONV, SSM_CONV_DIM)),
            full((1, SSM_CONV_DIM)),
            full((1, LANES)),
            full((1, LANES)),
            full((1, SSM_WIDTH)),
            full((1, SSM_WIDTH)),
            full((CHUNK, CHUNK)),
            full((LANES, SSM_WIDTH)),
        ],
        out_specs=pl.BlockSpec((rows, SSM_WIDTH), lambda c: (c, 0)),
        out_shape=jax.ShapeDtypeStruct((T, SSM_WIDTH), BF16),
        scratch_shapes=[
            pltpu.VMEM((CHUNK + CONV_PAD, SSM_CONV_DIM), F32),
            pltpu.VMEM((SSM_GROUPS, SSM_STATE, SSM_WIDTH // SSM_GROUPS), F32),
        ],
        compiler_params=_cparams(("arbitrary",)),
        name="ssd",
    )(proj, proj, proj, proj, dt_raw, conv_w, conv_b, dt_bias, a_log, d_skip_e, ssm_nw, tri, expand)


OUT_TM = 256
ROUTE_ROWS = 8 + N_EXPERTS


def _out_router_body(x_ref, yr_ref, ys_ref, wo_ref, nw_ref, wr_ref, br_ref,
                     h1_ref, h2t_ref, cw_ref, ids_ref):
    tm = x_ref.shape[0]
    h1 = x_ref[...] + _dot(yr_ref[...], wo_ref[0:RET_WIDTH, :]) + _dot(ys_ref[...], wo_ref[RET_WIDTH:, :])
    h1_ref[...] = h1
    ms = jnp.mean(h1 * h1, axis=-1, keepdims=True)
    h2 = h1 * lax.rsqrt(ms + EPS) * nw_ref[...]
    for s in range(Y_ROWS):
        h2t_ref[:, s, :] = h2[:, s * LANES:(s + 1) * LANES]

    logits = _dot_nt(wr_ref[...], h2.astype(BF16)) + br_ref[...]
    row = lax.broadcasted_iota(I32, (8, tm), 0)
    lg = jnp.where(row < N_GROUPS, logits[0:8], -jnp.inf)
    m = jnp.max(lg, axis=0, keepdims=True)
    p_sel = 1.0 / jnp.sum(jnp.exp(lg - m), axis=0, keepdims=True)
    g_sel = jnp.min(jnp.where(lg == m, row, 8), axis=0, keepdims=True)
    le = jnp.zeros((GROUP_EXPERTS, tm), F32)
    for g in range(N_GROUPS):
        le = jnp.where(g_sel == g, logits[8 + g * GROUP_EXPERTS:8 + (g + 1) * GROUP_EXPERTS], le)
    m2 = jnp.max(le, axis=0, keepdims=True)
    ee = jnp.exp(le - m2)
    pe = ee / jnp.sum(ee, axis=0, keepdims=True)
    v1 = jnp.max(pe, axis=0, keepdims=True)
    i1 = jnp.min(jnp.where(pe == v1, row, 8), axis=0, keepdims=True)
    pe2 = jnp.where(row == i1, -1.0, pe)
    v2 = jnp.max(pe2, axis=0, keepdims=True)
    i2 = jnp.min(jnp.where(pe2 == v2, row, 8), axis=0, keepdims=True)
    tw = v1 + v2
    c1 = v1 / tw * p_sel
    c2 = v2 / tw * p_sel
    e1 = g_sel * GROUP_EXPERTS + i1
    e2 = g_sel * GROUP_EXPERTS + i2
    ids = jnp.where(row == 0, e1, jnp.where(row == 1, e2, 0))
    for b in range(tm // SORT_BLK):
        ids_ref[b] = ids[:, b * SORT_BLK:(b + 1) * SORT_BLK]
    cw8 = jnp.where(row == 0, c1, jnp.where(row == 1, c2, 0.0))
    cw = jnp.concatenate([cw8, jnp.zeros((LANES - 8, tm), F32)], axis=0)
    cw_ref[...] = cw.T


def _out_router(x, y_ret, y_ssm, w_out, nw, wr_t, br):
    T = x.shape[0]
    tm = min(OUT_TM, T)
    full = lambda shape: pl.BlockSpec(shape, lambda i: (0,) * len(shape))
    return pl.pallas_call(
        _out_router_body,
        grid=(T // tm,),
        in_specs=[
            pl.BlockSpec((tm, D_MODEL), lambda i: (i, 0)),
            pl.BlockSpec((tm, RET_WIDTH), lambda i: (i, 0)),
            pl.BlockSpec((tm, SSM_WIDTH), lambda i: (i, 0)),
            full((D_MODEL, D_MODEL)),
            full((1, D_MODEL)),
            full((LANES, D_MODEL)),
            full((LANES, 1)),
        ],
        out_specs=[
            pl.BlockSpec((tm, D_MODEL), lambda i: (i, 0)),
            pl.BlockSpec((tm, Y_ROWS, LANES), lambda i: (i, 0, 0)),
            pl.BlockSpec((tm, LANES), lambda i: (i, 0)),
            pl.BlockSpec((tm // SORT_BLK, 8, SORT_BLK), lambda i: (i, 0, 0)),
        ],
        out_shape=[
            jax.ShapeDtypeStruct((T, D_MODEL), F32),
            jax.ShapeDtypeStruct((T, Y_ROWS, LANES), F32),
            jax.ShapeDtypeStruct((T, LANES), F32),
            jax.ShapeDtypeStruct((T // SORT_BLK, 8, SORT_BLK), I32),
        ],
        compiler_params=_cparams(("parallel",)),
        name="out_router",
    )(x, y_ret, y_ssm, w_out, nw, wr_t, br)


SORT_BLK = 256
MOE_TM = 256


def _sort_index_body(ids_ref, tri_ref, ltri_ref, dest_ref, cnt_ref, rank_ref):
    nblk = ids_ref.shape[0]
    row_e = lax.broadcasted_iota(I32, (N_EXPERTS, SORT_BLK), 0)
    row8 = lax.broadcasted_iota(I32, (8, SORT_BLK), 0)

    def onehots(b):
        ids = ids_ref[b]
        return row_e == ids[0:1], row_e == ids[1:2]

    def rank_blk(b, carry):
        oh1, oh2 = onehots(b)
        ohf = jnp.where(oh1 | oh2, 1.0, 0.0)
        incl = _dot(ohf.astype(BF16), tri_ref[...])
        base = carry + incl - 1.0
        r1 = jnp.sum(jnp.where(oh1, base, 0.0), axis=0, keepdims=True)
        r2 = jnp.sum(jnp.where(oh2, base, 0.0), axis=0, keepdims=True)
        rank_ref[b] = jnp.where(row8 == 0, r1, jnp.where(row8 == 1, r2, 0.0))
        return carry + jnp.sum(ohf, axis=1, keepdims=True)

    cnt = lax.fori_loop(0, nblk, rank_blk, jnp.zeros((N_EXPERTS, 1), F32))
    cnt_ref[...] = jnp.broadcast_to(cnt, cnt_ref.shape)
    tiles = jnp.floor((cnt + (MOE_TM - 1.0)) / MOE_TM)
    tiles_b = jnp.broadcast_to(tiles, (N_EXPERTS, LANES)).astype(BF16)
    off = _dot(ltri_ref[...], tiles_b)[:, 0:1] * MOE_TM

    def dest_blk(b, carry):
        oh1, oh2 = onehots(b)
        o1 = jnp.sum(jnp.where(oh1, off, 0.0), axis=0, keepdims=True)
        o2 = jnp.sum(jnp.where(oh2, off, 0.0), axis=0, keepdims=True)
        d = rank_ref[b] + jnp.where(row8 == 0, o1, jnp.where(row8 == 1, o2, 0.0))
        dest_ref[b] = d.astype(I32)
        return carry

    lax.fori_loop(0, nblk, dest_blk, 0)


def _sort_index(ids, tri_u, ltri):
    nblk = ids.shape[0]
    return pl.pallas_call(
        _sort_index_body,
        out_shape=[jax.ShapeDtypeStruct((nblk, 8, SORT_BLK), I32),
                   jax.ShapeDtypeStruct((N_EXPERTS, LANES), F32)],
        scratch_shapes=[pltpu.VMEM((nblk, 8, SORT_BLK), F32)],
        compiler_params=_cparams(None),
        name="sort_index",
    )(ids, tri_u, ltri)


ROW_DMA_PRIORITY = 1


def _moe_body(te_ref, tf_ref, nx_ref, nu_ref, dest_ref, lo_ref, hi_ref,
              h2t_ref, wg_ref, wu_ref, wd_ref, yt_ref,
              srow_ref, xbuf, ystage, wsg, wsu, wsd, wgb_ref, wub_ref, wdb_ref, gsem, ssem, wsem):
    i = pl.program_id(0)
    nu = nu_ref[0]
    T = h2t_ref.shape[0]

    def gather_row(tile, slot, r):
        tok = srow_ref[tile * MOE_TM + r] & (T - 1)
        return pltpu.make_async_copy(h2t_ref.at[tok], xbuf.at[slot, :, r, :], gsem.at[slot])

    def scatter_row(tile, slot, r):
        return pltpu.make_async_copy(ystage.at[slot, :, r, :], yt_ref.at[srow_ref[tile * MOE_TM + r]],
                                     ssem.at[slot])

    def gather_tile(slot):
        return pltpu.make_async_copy(h2t_ref.at[pl.ds(0, MOE_TM)], h2t_ref.at[pl.ds(0, MOE_TM)], gsem.at[slot])

    def scatter_tile(slot):
        return pltpu.make_async_copy(yt_ref.at[pl.ds(0, MOE_TM)], yt_ref.at[pl.ds(0, MOE_TM)], ssem.at[slot])

    def weight_copies(e):
        return (pltpu.make_async_copy(wg_ref.at[e], wsg, wsem.at[0]),
                pltpu.make_async_copy(wu_ref.at[e], wsu, wsem.at[1]),
                pltpu.make_async_copy(wd_ref.at[e], wsd, wsem.at[2]))

    @pl.when(i == 0)
    def _():
        for c in weight_copies(te_ref[0]):
            c.start()

        def tok(t, c):
            srow_ref[dest_ref[t]] = t
            srow_ref[dest_ref[T + t]] = T + t
            return c

        lax.fori_loop(0, T, tok, 0, unroll=8)

        def seg(e, c):
            def pad(r, c2):
                srow_ref[r] = 2 * T + (r & (MOE_TM - 1))
                return c2

            lax.fori_loop(lo_ref[e], hi_ref[e], pad, 0)
            return c

        lax.fori_loop(0, lo_ref.shape[0], seg, 0)

        ystage[...] = jnp.zeros_like(ystage)

        def spare(r, c):
            pltpu.make_async_copy(ystage.at[0, :, r, :], yt_ref.at[2 * T + r], ssem.at[0]).start()
            return c

        lax.fori_loop(0, MOE_TM, spare, 0)
        scatter_tile(0).wait()
        for r in range(MOE_TM):
            gather_row(0, 0, r).start(priority=ROW_DMA_PRIORITY)

    @pl.when(tf_ref[i] == 1)
    def _():
        for c in weight_copies(te_ref[i]):
            c.wait()
        wgb_ref[...] = wsg[...].astype(BF16)
        wub_ref[...] = wsu[...].astype(BF16)
        wdb_ref[...] = wsd[...].astype(BF16)

        @pl.when(nx_ref[i] >= 0)
        def _():
            for c in weight_copies(nx_ref[i]):
                c.start()

    slot = lax.rem(i, 2)
    other = 1 - slot

    @pl.when(i < nu)
    def _():
        gather_tile(slot).wait()

        @pl.when(i >= 1)
        def _():
            scatter_tile(slot).wait()

        nxt = jnp.minimum(i + 1, nu - 1)
        prv = jnp.maximum(i - 1, 0)
        for r in range(MOE_TM):
            gather_row(nxt, other, r).start(priority=r % 2)
            scatter_row(prv, other, r).start(priority=(r + 1) % 2)

        x = jnp.concatenate([xbuf[slot, s].astype(BF16) for s in range(Y_ROWS)], axis=1)
        a = _dot(x, wgb_ref[...])
        u = _dot(x, wub_ref[...])
        act = (_silu(a) * u).astype(BF16)
        y = _dot(act, wdb_ref[...])
        for s in range(Y_ROWS):
            ystage[slot, s] = y[:, s * LANES:(s + 1) * LANES]

    @pl.when(i == nu)
    def _():
        gather_tile(slot).wait()
        scatter_tile(slot).wait()

        def last(r, c):
            scatter_row(nu - 1, other, r).start()
            return c

        lax.fori_loop(0, MOE_TM, last, 0)
        scatter_tile(other).wait()


def _moe(tile_expert, tile_first, next_expert, n_used, dest, pad_lo, pad_hi, h2t, w_gate, w_up, w_down):
    T = h2t.shape[0]
    assert T & (T - 1) == 0 and T >= MOE_TM, "token index is recovered from the row table by masking"
    n_tiles = tile_expert.shape[0] - 1
    hbm = pl.BlockSpec(memory_space=pl.ANY)
    grid_spec = pltpu.PrefetchScalarGridSpec(
        num_scalar_prefetch=7,
        grid=(n_tiles + 1,),
        in_specs=[hbm, hbm, hbm, hbm],
        out_specs=hbm,
        scratch_shapes=[pltpu.SMEM((n_tiles * MOE_TM,), I32),
                        pltpu.VMEM((2, Y_ROWS, MOE_TM, LANES), F32),
                        pltpu.VMEM((2, Y_ROWS, MOE_TM, LANES), F32),
                        pltpu.VMEM((D_MODEL, D_EXPERT), F32),
                        pltpu.VMEM((D_MODEL, D_EXPERT), F32),
                        pltpu.VMEM((D_EXPERT, D_MODEL), F32),
                        pltpu.VMEM((D_MODEL, D_EXPERT), BF16),
                        pltpu.VMEM((D_MODEL, D_EXPERT), BF16),
                        pltpu.VMEM((D_EXPERT, D_MODEL), BF16),
                        pltpu.SemaphoreType.DMA((2,)),
                        pltpu.SemaphoreType.DMA((2,)),
                        pltpu.SemaphoreType.DMA((3,))],
    )
    return pl.pallas_call(
        _moe_body,
        grid_spec=grid_spec,
        out_shape=jax.ShapeDtypeStruct((2 * T + MOE_TM, Y_ROWS, LANES), F32),
        compiler_params=_cparams(("arbitrary",)),
        name="moe",
    )(tile_expert, tile_first, next_expert, n_used, dest, pad_lo, pad_hi, h2t, w_gate, w_up, w_down)


COMB_TM = 256


def _combine_body(h1_ref, y0_ref, y1_ref, cw_ref, nw_ref, o_ref):
    tm = h1_ref.shape[0]
    c0 = cw_ref[:, 0:1]
    c1 = cw_ref[:, 1:2]
    cols = []
    for s in range(Y_ROWS):
        moe = c0 * y0_ref[:, s, :] + c1 * y1_ref[:, s, :]
        cols.append(h1_ref[:, s * LANES:(s + 1) * LANES] + moe)
    h = jnp.concatenate(cols, axis=1)
    ms = jnp.mean(h * h, axis=-1, keepdims=True)
    o_ref[...] = h * lax.rsqrt(ms + EPS) * nw_ref[...]


def _combine(h1, y_tok, cw, nw):
    T = h1.shape[0]
    tm = min(COMB_TM, T)
    nt = T // tm
    return pl.pallas_call(
        _combine_body,
        grid=(nt,),
        in_specs=[
            pl.BlockSpec((tm, D_MODEL), lambda i: (i, 0)),
            pl.BlockSpec((tm, Y_ROWS, LANES), lambda i: (i, 0, 0)),
            pl.BlockSpec((tm, Y_ROWS, LANES), lambda i: (i + nt, 0, 0)),
            pl.BlockSpec((tm, LANES), lambda i: (i, 0)),
            pl.BlockSpec((1, D_MODEL), lambda i: (0, 0)),
        ],
        out_specs=pl.BlockSpec((tm, D_MODEL), lambda i: (i, 0)),
        out_shape=jax.ShapeDtypeStruct((T, D_MODEL), F32),
        compiler_params=_cparams(("parallel",)),
        name="combine",
    )(h1, y_tok, y_tok, cw, nw)


def _tile_plan(cnt, n_tiles):
    tiles = (cnt + (MOE_TM - 1)) // MOE_TM
    ends = jnp.cumsum(tiles)
    starts = ends - tiles
    n_used = ends[-1]
    step = jnp.arange(n_tiles + 1, dtype=I32)
    tile = jnp.minimum(step, jnp.maximum(n_used - 1, 0))
    tile_expert = jnp.sum((ends[None, :] <= tile[:, None]).astype(I32), axis=1)
    tile_first = ((step == starts[tile_expert]) & (step < n_used)).astype(I32)
    nxt_tile = ends[tile_expert]
    nxt_expert = jnp.sum((ends[None, :] <= nxt_tile[:, None]).astype(I32), axis=1)
    next_expert = jnp.where(nxt_tile < n_used, nxt_expert, -1).astype(I32)
    pad_lo = starts * MOE_TM + cnt
    pad_hi = ends * MOE_TM
    return (tile_expert, tile_first, next_expert, n_used.reshape(1).astype(I32),
            pad_lo.astype(I32), pad_hi.astype(I32))


def kernel(x, positions, norm1_w, w_in, conv_w, conv_b, dt_bias, a_log, d_skip, ret_norm_w,
           ssm_norm_w, w_out, norm2_w, w_router_group, b_router_group, w_router_expert,
           b_router_expert, w_expert_gate, w_expert_up, w_expert_down, final_norm_w):
    B, T, D = x.shape
    assert B == 1 and D == D_MODEL and T % CHUNK == 0
    xf = x.reshape(T, D)
    pad_l = lambda v: jnp.pad(v, ((0, 0), (0, LANES - v.shape[-1])))

    w_main = w_in[0][:, :MAIN_PROJ].astype(BF16)
    w_dt = pad_l(w_in[0][:, MAIN_PROJ:]).astype(BF16)
    half = RET_HEAD_DIM // 2
    inv = (ROPE_THETA ** (-jnp.arange(half, dtype=F32) / half)).reshape(1, half)
    tri = (jnp.arange(CHUNK)[:, None] >= jnp.arange(CHUNK)[None, :]).astype(BF16)
    expand = (jnp.arange(LANES)[:, None] == jnp.arange(SSM_WIDTH)[None, :] // SSM_HEAD_DIM).astype(BF16)
    d_skip_e = jnp.repeat(d_skip[0], SSM_HEAD_DIM).reshape(1, SSM_WIDTH)
    wr_t = jnp.zeros((LANES, D), F32)
    wr_t = wr_t.at[0:N_GROUPS].set(w_router_group[0].T).at[8:8 + N_EXPERTS].set(w_router_expert[0].T)
    br = jnp.zeros((LANES,), F32)
    br = br.at[0:N_GROUPS].set(b_router_group[0]).at[8:8 + N_EXPERTS].set(b_router_expert[0])
    tri_u = (jnp.arange(SORT_BLK)[:, None] <= jnp.arange(SORT_BLK)[None, :]).astype(BF16)
    ltri = (jnp.arange(N_EXPERTS)[:, None] > jnp.arange(N_EXPERTS)[None, :]).astype(BF16)

    proj, dt_raw = _inproj(xf, norm1_w[0].reshape(1, D), w_main, w_dt)
    cos, sin = _rope_tables(positions.reshape(T, 1).astype(F32), inv)
    y_ret = _retention(proj, cos, sin, ret_norm_w[0].reshape(1, RET_WIDTH), _retention_consts())
    y_ssm = _ssd(proj, dt_raw, conv_w[0], conv_b[0].reshape(1, -1), pad_l(dt_bias[0].reshape(1, -1)),
                 pad_l(a_log[0].reshape(1, -1)), d_skip_e, ssm_norm_w[0].reshape(1, -1), tri, expand)
    h1, h2t, cw, ids = _out_router(xf, y_ret, y_ssm, w_out[0].astype(BF16), norm2_w[0].reshape(1, D),
                                   wr_t.astype(BF16), br.reshape(LANES, 1))

    dest_blk, cnt = _sort_index(ids, tri_u, ltri)
    dest = dest_blk[:, 0:2, :].transpose(1, 0, 2).reshape(2 * T)
    n_tiles = (2 * T) // MOE_TM + N_EXPERTS
    tile_expert, tile_first, next_expert, n_used, pad_lo, pad_hi = _tile_plan(cnt[:, 0].astype(I32), n_tiles)
    y_tok = _moe(tile_expert, tile_first, next_expert, n_used, dest, pad_lo, pad_hi, h2t,
                 w_expert_gate[0], w_expert_up[0], w_expert_down[0])
    out = _combine(h1, y_tok, cw, final_norm_w.reshape(1, D))
    return out.reshape(B, T, D)
```

```python
import functools

import jax
import jax.numpy as jnp
from jax import lax
from jax.experimental import pallas as pl
from jax.experimental.pallas import tpu as pltpu

F32 = jnp.float32
BF16 = jnp.bfloat16
I32 = jnp.int32

D_MODEL = 2048
EPS = 1e-6
CHUNK = 128
RET_HEADS = 4
RET_HEAD_DIM = 256
RET_WIDTH = RET_HEADS * RET_HEAD_DIM
ROPE_THETA = 10000.0
SSM_WIDTH = 1024
SSM_HEAD_DIM = 64
SSM_HEADS = SSM_WIDTH // SSM_HEAD_DIM
SSM_GROUPS = 2
SSM_STATE = 128
SSM_CONV = 4
SSM_CONV_DIM = SSM_WIDTH + 2 * SSM_GROUPS * SSM_STATE
MAIN_PROJ = 4 * RET_WIDTH + SSM_WIDTH + SSM_CONV_DIM
N_GROUPS = 4
GROUP_EXPERTS = 8
N_EXPERTS = N_GROUPS * GROUP_EXPERTS
D_EXPERT = 512
LANES = 128
Y_ROWS = D_MODEL // LANES

VMEM_LIMIT = 56 * 1024 * 1024


def _cparams(sem, vmem=VMEM_LIMIT):
    return pltpu.CompilerParams(dimension_semantics=sem, vmem_limit_bytes=vmem)


def _silu(x):
    return x * (1.0 / (1.0 + jnp.exp(-x)))


def _dot(a, b):
    return jnp.dot(a, b, preferred_element_type=F32)


def _dot_nt(a, b):
    return lax.dot_general(a, b, (((1,), (1,)), ((), ())), preferred_element_type=F32)


def _split3(a):
    a1 = a.astype(BF16)
    r1 = a - a1.astype(F32)
    a2 = r1.astype(BF16)
    a3 = (r1 - a2.astype(F32)).astype(BF16)
    return a1, a2, a3


IN_TM = 1024
IN_TN = 512


def _inproj_body(x_ref, nw_ref, w_ref, wdt_ref, o_ref, odt_ref, hn_ref):
    @pl.when(pl.program_id(1) == 0)
    def _():
        def blk(r, carry):
            sl = pl.ds(pl.multiple_of(r * CHUNK, CHUNK), CHUNK)
            x = x_ref[sl, :]
            ms = jnp.mean(x * x, axis=-1, keepdims=True)
            hn_ref[sl, :] = (x * lax.rsqrt(ms + EPS) * nw_ref[...]).astype(BF16)
            return carry
        lax.fori_loop(0, x_ref.shape[0] // CHUNK, blk, 0)
        odt_ref[...] = _dot(hn_ref[...], wdt_ref[...])

    o_ref[...] = _dot(hn_ref[...], w_ref[...])


def _inproj(x, nw, w_main, w_dt):
    T = x.shape[0]
    tm = min(IN_TM, T)
    return pl.pallas_call(
        _inproj_body,
        grid=(T // tm, MAIN_PROJ // IN_TN),
        in_specs=[
            pl.BlockSpec((tm, D_MODEL), lambda i, j: (i, 0)),
            pl.BlockSpec((1, D_MODEL), lambda i, j: (0, 0)),
            pl.BlockSpec((D_MODEL, IN_TN), lambda i, j: (0, j)),
            pl.BlockSpec((D_MODEL, LANES), lambda i, j: (0, 0)),
        ],
        out_specs=[
            pl.BlockSpec((tm, IN_TN), lambda i, j: (i, j)),
            pl.BlockSpec((tm, LANES), lambda i, j: (i, 0)),
        ],
        out_shape=[
            jax.ShapeDtypeStruct((T, MAIN_PROJ), F32),
            jax.ShapeDtypeStruct((T, LANES), F32),
        ],
        scratch_shapes=[pltpu.VMEM((tm, D_MODEL), BF16)],
        compiler_params=_cparams(("parallel", "arbitrary")),
        name="in_proj",
    )(x, nw, w_main, w_dt)


def _rope_body(pos_ref, inv_ref, cos_ref, sin_ref):
    ang = pos_ref[...] * inv_ref[...]
    cos_ref[...] = jnp.cos(ang)
    sin_ref[...] = jnp.sin(ang)


def _rope_tables(pos, inv):
    T = pos.shape[0]
    tm = min(1024, T)
    half = inv.shape[1]
    return pl.pallas_call(
        _rope_body,
        grid=(T // tm,),
        in_specs=[pl.BlockSpec((tm, 1), lambda i: (i, 0)),
                  pl.BlockSpec((1, half), lambda i: (0, 0))],
        out_specs=[pl.BlockSpec((tm, half), lambda i: (i, 0))] * 2,
        out_shape=[jax.ShapeDtypeStruct((T, half), F32)] * 2,
        compiler_params=_cparams(("parallel",)),
        name="rope_tables",
    )(pos, inv)


RET_CB = 2


def _retention_body(cd_ref, q_ref, k_ref, v_ref, g_ref, cos_ref, sin_ref,
                    dec_ref, wq_ref, ws_ref, nw_ref, o_ref, st_ref, *, cb):
    h = pl.program_id(0)

    @pl.when(pl.program_id(1) == 0)
    def _():
        st_ref[...] = jnp.zeros_like(st_ref)

    chunk_decay = cd_ref[h]
    half = RET_HEAD_DIM // 2
    for i in range(cb):
        sl = slice(i * CHUNK, (i + 1) * CHUNK)
        cos = cos_ref[sl, :]
        sin = sin_ref[sl, :]

        def rope(x):
            x1, x2 = x[:, :half], x[:, half:]
            return jnp.concatenate([x1 * cos - x2 * sin, x2 * cos + x1 * sin], axis=-1)

        q = rope(q_ref[sl, :])
        k = rope(k_ref[sl, :]) * (RET_HEAD_DIM ** -0.5)
        v = v_ref[sl, :]
        qb = q.astype(BF16)
        kb = k.astype(BF16)
        scores = _dot_nt(qb, kb) * dec_ref[0]
        y = _dot(scores.astype(BF16), v.astype(BF16))
        st = st_ref[...]
        y = y + _dot(qb, st.astype(BF16)) * wq_ref[0]
        vw = (v * ws_ref[0]).astype(BF16)
        new = _dot(k.T.astype(BF16), vw)
        st_ref[...] = st * chunk_decay + new
        mu = jnp.mean(y, axis=-1, keepdims=True)
        d = y - mu
        var = jnp.mean(d * d, axis=-1, keepdims=True)
        yn = d * lax.rsqrt(var + EPS)
        o_ref[sl, :] = (yn * nw_ref[...] * _silu(g_ref[sl, :])).astype(BF16)


def _retention(proj, cos, sin, ret_nw, consts):
    T = proj.shape[0]
    cb = min(RET_CB, T // CHUNK)
    rows = cb * CHUNK
    chunk_decay, decay_intra, w_query, w_state = consts
    hd = RET_HEAD_DIM
    half = hd // 2

    def col(base):
        return lambda h, c, cd: (c, base + h)

    grid_spec = pltpu.PrefetchScalarGridSpec(
        num_scalar_prefetch=1,
        grid=(RET_HEADS, T // rows),
        in_specs=[
            pl.BlockSpec((rows, hd), col(0)),
            pl.BlockSpec((rows, hd), col(RET_HEADS)),
            pl.BlockSpec((rows, hd), col(2 * RET_HEADS)),
            pl.BlockSpec((rows, hd), col(3 * RET_HEADS)),
            pl.BlockSpec((rows, half), lambda h, c, cd: (c, 0)),
            pl.BlockSpec((rows, half), lambda h, c, cd: (c, 0)),
            pl.BlockSpec((1, CHUNK, CHUNK), lambda h, c, cd: (h, 0, 0)),
            pl.BlockSpec((1, CHUNK, hd), lambda h, c, cd: (h, 0, 0)),
            pl.BlockSpec((1, CHUNK, hd), lambda h, c, cd: (h, 0, 0)),
            pl.BlockSpec((1, hd), lambda h, c, cd: (0, h)),
        ],
        out_specs=pl.BlockSpec((rows, hd), lambda h, c, cd: (c, h)),
        scratch_shapes=[pltpu.VMEM((hd, hd), F32)],
    )
    return pl.pallas_call(
        functools.partial(_retention_body, cb=cb),
        grid_spec=grid_spec,
        out_shape=jax.ShapeDtypeStruct((T, RET_WIDTH), BF16),
        compiler_params=_cparams(("parallel", "arbitrary")),
        name="retention",
    )(chunk_decay, proj, proj, proj, proj, cos, sin, decay_intra, w_query, w_state, ret_nw)


def _retention_consts():
    H, L = RET_HEADS, CHUNK
    log_gamma = jnp.log1p(-(2.0 ** (-5.0 - jnp.arange(H, dtype=F32))))
    idx = jnp.arange(L, dtype=F32)
    diff = idx[:, None] - idx[None, :]
    causal = diff >= 0
    decay_intra = jnp.where(causal[None], jnp.exp(jnp.where(causal, diff, 0.0)[None] * log_gamma[:, None, None]), 0.0)
    w_state = jnp.exp((L - 1.0 - idx)[None, :] * log_gamma[:, None])
    w_query = jnp.exp((idx + 1.0)[None, :] * log_gamma[:, None])
    chunk_decay = jnp.exp(L * log_gamma)
    bc = lambda w: jnp.broadcast_to(w[:, :, None], (H, L, RET_HEAD_DIM))
    return chunk_decay, decay_intra, bc(w_query), bc(w_state)


SSD_CB = 2
CONV_PAD = 8
XBC_BLK = 512


def _ssd_body(x0_ref, x1_ref, x2_ref, z_ref, dt_ref, cw_ref, cbias_ref, dtb_ref,
              alog_ref, dskip_ref, nw_ref, tri_ref, exp_ref, o_ref, xpad_ref, st_ref, *, cb):
    @pl.when(pl.program_id(0) == 0)
    def _():
        xpad_ref[0:CONV_PAD, :] = jnp.zeros((CONV_PAD, SSM_CONV_DIM), F32)
        st_ref[...] = jnp.zeros_like(st_ref)

    L = CHUNK
    gw = SSM_WIDTH // SSM_GROUPS
    rows_i = lax.broadcasted_iota(I32, (L, L), 0)
    cols_i = lax.broadcasted_iota(I32, (L, L), 1)
    causal = rows_i >= cols_i
    lo_lane = lax.broadcasted_iota(I32, (L, LANES), 1) < SSM_HEAD_DIM
    tri = tri_ref[...]
    expand = exp_ref[...]
    a_neg = -jnp.exp(alog_ref[...])

    for i in range(cb):
        sl = slice(i * L, (i + 1) * L)
        for b, xr in enumerate((x0_ref, x1_ref, x2_ref)):
            xpad_ref[CONV_PAD:CONV_PAD + L, b * XBC_BLK:(b + 1) * XBC_BLK] = xr[sl, :]
        u_parts = []
        for b in range(SSM_CONV_DIM // XBC_BLK):
            cs = slice(b * XBC_BLK, (b + 1) * XBC_BLK)
            acc = cbias_ref[:, cs]
            for t in range(SSM_CONV):
                r0 = CONV_PAD - (SSM_CONV - 1) + t
                acc = acc + xpad_ref[r0:r0 + L, cs] * cw_ref[t:t + 1, cs]
            u_parts.append(_silu(acc))
        xpad_ref[0:CONV_PAD, :] = xpad_ref[L:L + CONV_PAD, :]
        xs = jnp.concatenate(u_parts[:2], axis=-1)
        bm = u_parts[2][:, :SSM_GROUPS * SSM_STATE]
        cm = u_parts[2][:, SSM_GROUPS * SSM_STATE:]

        dt_in = dt_ref[sl, :] + dtb_ref[...]
        dt = jnp.maximum(dt_in, 0.0) + jnp.log1p(jnp.exp(-jnp.abs(dt_in)))
        a = dt * a_neg
        a1, a2, a3 = _split3(a)
        a_cs = _dot(tri, a1) + _dot(tri, a2) + _dot(tri, a3)
        d1, d2, d3 = _split3(dt)
        dt_e = _dot(d1, expand) + _dot(d2, expand) + _dot(d3, expand)
        c1, c2, c3 = _split3(a_cs)
        acs_e = _dot(c1, expand) + _dot(c2, expand) + _dot(c3, expand)
        last = acs_e[L - 1:L, :]
        to_end = jnp.exp(last - acs_e)
        from_start = jnp.exp(acs_e)
        chunk_decay = jnp.exp(last)
        x_dt = xs * dt_e
        xw = (x_dt * to_end).astype(BF16)
        acs_t = a_cs.T
        cmb = cm.astype(BF16)
        bmb = bm.astype(BF16)

        ys = []
        for g in range(SSM_GROUPS):
            ns = slice(g * SSM_STATE, (g + 1) * SSM_STATE)
            gs = slice(g * gw, (g + 1) * gw)
            cg = cmb[:, ns]
            cbg = _dot_nt(cg, bmb[:, ns])
            st = st_ref[g]
            y_off = _dot(cg, st.astype(BF16))
            new = _dot(bm[:, ns].T.astype(BF16), xw[:, gs])
            st_ref[g] = st * chunk_decay[:, gs] + new
            for jp in range(gw // LANES):
                h0 = (g * gw + jp * LANES) // SSM_HEAD_DIM
                ms = []
                for hh in (h0, h0 + 1):
                    seg = a_cs[:, hh:hh + 1] - acs_t[hh:hh + 1, :]
                    dec = jnp.exp(jnp.where(causal, seg, -jnp.inf))
                    ms.append((cbg * dec).astype(BF16))
                lhs = jnp.concatenate(ms, axis=1)
                ls = slice(g * gw + jp * LANES, g * gw + (jp + 1) * LANES)
                xp = x_dt[:, ls]
                rhs = jnp.concatenate([jnp.where(lo_lane, xp, 0.0),
                                       jnp.where(lo_lane, 0.0, xp)], axis=0).astype(BF16)
                y_diag = _dot(lhs, rhs)
                ys.append(y_diag + y_off[:, jp * LANES:(jp + 1) * LANES] * from_start[:, ls])
        y = jnp.concatenate(ys, axis=1) + xs * dskip_ref[...]
        y = y * _silu(z_ref[sl, :])
        outs = []
        for g in range(SSM_GROUPS):
            yg = y[:, g * gw:(g + 1) * gw]
            ms_ = jnp.mean(yg * yg, axis=-1, keepdims=True)
            outs.append(yg * lax.rsqrt(ms_ + EPS))
        o_ref[sl, :] = (jnp.concatenate(outs, axis=1) * nw_ref[...]).astype(BF16)


def _ssd(proj, dt_raw, conv_w, conv_b, dt_bias, a_log, d_skip_e, ssm_nw, tri, expand):
    T = proj.shape[0]
    cb = min(SSD_CB, T // CHUNK)
    rows = cb * CHUNK
    xbc0 = (4 * RET_WIDTH + SSM_WIDTH) // XBC_BLK
    zcol = 4 * RET_WIDTH // SSM_WIDTH
    full = lambda shape: pl.BlockSpec(shape, lambda c: (0,) * len(shape))
    return pl.pallas_call(
        functools.partial(_ssd_body, cb=cb),
        grid=(T // rows,),
        in_specs=[
            pl.BlockSpec((rows, XBC_BLK), lambda c: (c, xbc0)),
            pl.BlockSpec((rows, XBC_BLK), lambda c: (c, xbc0 + 1)),
            pl.BlockSpec((rows, XBC_BLK), lambda c: (c, xbc0 + 2)),
            pl.BlockSpec((rows, SSM_WIDTH), lambda c: (c, zcol)),
            pl.BlockSpec((rows, LANES), lambda c: (c, 0)),
            full((SSM_CONV, SSM_CONV_DIM)),
            full((1, SSM_CONV_DIM)),
            full((1, LANES)),
            full((1, LANES)),
            full((1, SSM_WIDTH)),
            full((1, SSM_WIDTH)),
            full((CHUNK, CHUNK)),
            full((LANES, SSM_WIDTH)),
        ],
        out_specs=pl.BlockSpec((rows, SSM_WIDTH), lambda c: (c, 0)),
        out_shape=jax.ShapeDtypeStruct((T, SSM_WIDTH), BF16),
        scratch_shapes=[
            pltpu.VMEM((CHUNK + CONV_PAD, SSM_CONV_DIM), F32),
            pltpu.VMEM((SSM_GROUPS, SSM_STATE, SSM_WIDTH // SSM_GROUPS), F32),
        ],
        compiler_params=_cparams(("arbitrary",)),
        name="ssd",
    )(proj, proj, proj, proj, dt_raw, conv_w, conv_b, dt_bias, a_log, d_skip_e, ssm_nw, tri, expand)


OUT_TM = 256
ROUTE_ROWS = 8 + N_EXPERTS


def _out_router_body(x_ref, yr_ref, ys_ref, wo_ref, nw_ref, wr_ref, br_ref,
                     h1_ref, h2t_ref, cw_ref, ids_ref):
    tm = x_ref.shape[0]
    h1 = x_ref[...] + _dot(yr_ref[...], wo_ref[0:RET_WIDTH, :]) + _dot(ys_ref[...], wo_ref[RET_WIDTH:, :])
    h1_ref[...] = h1
    ms = jnp.mean(h1 * h1, axis=-1, keepdims=True)
    h2 = h1 * lax.rsqrt(ms + EPS) * nw_ref[...]
    for s in range(Y_ROWS):
        h2t_ref[:, s, :] = h2[:, s * LANES:(s + 1) * LANES]

    logits = _dot_nt(wr_ref[...], h2.astype(BF16)) + br_ref[...]
    row = lax.broadcasted_iota(I32, (8, tm), 0)
    lg = jnp.where(row < N_GROUPS, logits[0:8], -jnp.inf)
    m = jnp.max(lg, axis=0, keepdims=True)
    p_sel = 1.0 / jnp.sum(jnp.exp(lg - m), axis=0, keepdims=True)
    g_sel = jnp.min(jnp.where(lg == m, row, 8), axis=0, keepdims=True)
    le = jnp.zeros((GROUP_EXPERTS, tm), F32)
    for g in range(N_GROUPS):
        le = jnp.where(g_sel == g, logits[8 + g * GROUP_EXPERTS:8 + (g + 1) * GROUP_EXPERTS], le)
    m2 = jnp.max(le, axis=0, keepdims=True)
    ee = jnp.exp(le - m2)
    pe = ee / jnp.sum(ee, axis=0, keepdims=True)
    v1 = jnp.max(pe, axis=0, keepdims=True)
    i1 = jnp.min(jnp.where(pe == v1, row, 8), axis=0, keepdims=True)
    pe2 = jnp.where(row == i1, -1.0, pe)
    v2 = jnp.max(pe2, axis=0, keepdims=True)
    i2 = jnp.min(jnp.where(pe2 == v2, row, 8), axis=0, keepdims=True)
    tw = v1 + v2
    c1 = v1 / tw * p_sel
    c2 = v2 / tw * p_sel
    e1 = g_sel * GROUP_EXPERTS + i1
    e2 = g_sel * GROUP_EXPERTS + i2
    ids = jnp.where(row == 0, e1, jnp.where(row == 1, e2, 0))
    for b in range(tm // SORT_BLK):
        ids_ref[b] = ids[:, b * SORT_BLK:(b + 1) * SORT_BLK]
    cw8 = jnp.where(row == 0, c1, jnp.where(row == 1, c2, 0.0))
    cw = jnp.concatenate([cw8, jnp.zeros((LANES - 8, tm), F32)], axis=0)
    cw_ref[...] = cw.T


def _out_router(x, y_ret, y_ssm, w_out, nw, wr_t, br):
    T = x.shape[0]
    tm = min(OUT_TM, T)
    full = lambda shape: pl.BlockSpec(shape, lambda i: (0,) * len(shape))
    return pl.pallas_call(
        _out_router_body,
        grid=(T // tm,),
        in_specs=[
            pl.BlockSpec((tm, D_MODEL), lambda i: (i, 0)),
            pl.BlockSpec((tm, RET_WIDTH), lambda i: (i, 0)),
            pl.BlockSpec((tm, SSM_WIDTH), lambda i: (i, 0)),
            full((D_MODEL, D_MODEL)),
            full((1, D_MODEL)),
            full((LANES, D_MODEL)),
            full((LANES, 1)),
        ],
        out_specs=[
            pl.BlockSpec((tm, D_MODEL), lambda i: (i, 0)),
            pl.BlockSpec((tm, Y_ROWS, LANES), lambda i: (i, 0, 0)),
            pl.BlockSpec((tm, LANES), lambda i: (i, 0)),
            pl.BlockSpec((tm // SORT_BLK, 8, SORT_BLK), lambda i: (i, 0, 0)),
        ],
        out_shape=[
            jax.ShapeDtypeStruct((T, D_MODEL), F32),
            jax.ShapeDtypeStruct((T, Y_ROWS, LANES), F32),
            jax.ShapeDtypeStruct((T, LANES), F32),
            jax.ShapeDtypeStruct((T // SORT_BLK, 8, SORT_BLK), I32),
        ],
        compiler_params=_cparams(("parallel",)),
        name="out_router",
    )(x, y_ret, y_ssm, w_out, nw, wr_t, br)


SORT_BLK = 256
MOE_TM = 512


def _sort_index_body(ids_ref, tri_ref, ltri_ref, dest_ref, cnt_ref, rank_ref):
    nblk = ids_ref.shape[0]
    row_e = lax.broadcasted_iota(I32, (N_EXPERTS, SORT_BLK), 0)
    row8 = lax.broadcasted_iota(I32, (8, SORT_BLK), 0)

    def onehots(b):
        ids = ids_ref[b]
        return row_e == ids[0:1], row_e == ids[1:2]

    def rank_blk(b, carry):
        oh1, oh2 = onehots(b)
        ohf = jnp.where(oh1 | oh2, 1.0, 0.0)
        incl = _dot(ohf.astype(BF16), tri_ref[...])
        base = carry + incl - 1.0
        r1 = jnp.sum(jnp.where(oh1, base, 0.0), axis=0, keepdims=True)
        r2 = jnp.sum(jnp.where(oh2, base, 0.0), axis=0, keepdims=True)
        rank_ref[b] = jnp.where(row8 == 0, r1, jnp.where(row8 == 1, r2, 0.0))
        return carry + jnp.sum(ohf, axis=1, keepdims=True)

    cnt = lax.fori_loop(0, nblk, rank_blk, jnp.zeros((N_EXPERTS, 1), F32))
    cnt_ref[...] = jnp.broadcast_to(cnt, cnt_ref.shape)
    tiles = jnp.floor((cnt + (MOE_TM - 1.0)) / MOE_TM)
    tiles_b = jnp.broadcast_to(tiles, (N_EXPERTS, LANES)).astype(BF16)
    off = _dot(ltri_ref[...], tiles_b)[:, 0:1] * MOE_TM

    def dest_blk(b, carry):
        oh1, oh2 = onehots(b)
        o1 = jnp.sum(jnp.where(oh1, off, 0.0), axis=0, keepdims=True)
        o2 = jnp.sum(jnp.where(oh2, off, 0.0), axis=0, keepdims=True)
        d = rank_ref[b] + jnp.where(row8 == 0, o1, jnp.where(row8 == 1, o2, 0.0))
        dest_ref[b] = d.astype(I32)
        return carry

    lax.fori_loop(0, nblk, dest_blk, 0)


def _sort_index(ids, tri_u, ltri):
    nblk = ids.shape[0]
    return pl.pallas_call(
        _sort_index_body,
        out_shape=[jax.ShapeDtypeStruct((nblk, 8, SORT_BLK), I32),
                   jax.ShapeDtypeStruct((N_EXPERTS, LANES), F32)],
        scratch_shapes=[pltpu.VMEM((nblk, 8, SORT_BLK), F32)],
        compiler_params=_cparams(None),
        name="sort_index",
    )(ids, tri_u, ltri)


ROW_DMA_PRIORITY = 1


def _moe_body(te_ref, tf_ref, nx_ref, nu_ref, dest_ref, lo_ref, hi_ref,
              h2t_ref, wg_ref, wu_ref, wd_ref, yt_ref,
              srow_ref, xbuf, ystage, wsg, wsu, wsd, wgb_ref, wub_ref, wdb_ref, gsem, ssem, wsem):
    i = pl.program_id(0)
    nu = nu_ref[0]
    T = h2t_ref.shape[0]

    def gather_row(tile, slot, r):
        tok = srow_ref[tile * MOE_TM + r] & (T - 1)
        return pltpu.make_async_copy(h2t_ref.at[tok], xbuf.at[slot, :, r, :], gsem.at[slot])

    def scatter_row(tile, slot, r):
        return pltpu.make_async_copy(ystage.at[slot, :, r, :], yt_ref.at[srow_ref[tile * MOE_TM + r]],
                                     ssem.at[slot])

    def gather_tile(slot):
        return pltpu.make_async_copy(h2t_ref.at[pl.ds(0, MOE_TM)], h2t_ref.at[pl.ds(0, MOE_TM)], gsem.at[slot])

    def scatter_tile(slot):
        return pltpu.make_async_copy(yt_ref.at[pl.ds(0, MOE_TM)], yt_ref.at[pl.ds(0, MOE_TM)], ssem.at[slot])

    def weight_copies(e):
        return (pltpu.make_async_copy(wg_ref.at[e], wsg, wsem.at[0]),
                pltpu.make_async_copy(wu_ref.at[e], wsu, wsem.at[1]),
                pltpu.make_async_copy(wd_ref.at[e], wsd, wsem.at[2]))

    @pl.when(i == 0)
    def _():
        for c in weight_copies(te_ref[0]):
            c.start()

        def tok(t, c):
            srow_ref[dest_ref[t]] = t
            srow_ref[dest_ref[T + t]] = T + t
            return c

        lax.fori_loop(0, T, tok, 0, unroll=8)

        def seg(e, c):
            def pad(r, c2):
                srow_ref[r] = 2 * T + (r & (MOE_TM - 1))
                return c2

            lax.fori_loop(lo_ref[e], hi_ref[e], pad, 0)
            return c

        lax.fori_loop(0, lo_ref.shape[0], seg, 0)

        ystage[...] = jnp.zeros_like(ystage)

        def spare(r, c):
            pltpu.make_async_copy(ystage.at[0, :, r, :], yt_ref.at[2 * T + r], ssem.at[0]).start()
            return c

        lax.fori_loop(0, MOE_TM, spare, 0)
        scatter_tile(0).wait()
        for r in range(MOE_TM):
            gather_row(0, 0, r).start(priority=ROW_DMA_PRIORITY)

    @pl.when(tf_ref[i] == 1)
    def _():
        for c in weight_copies(te_ref[i]):
            c.wait()
        wgb_ref[...] = wsg[...].astype(BF16)
        wub_ref[...] = wsu[...].astype(BF16)
        wdb_ref[...] = wsd[...].astype(BF16)

        @pl.when(nx_ref[i] >= 0)
        def _():
            for c in weight_copies(nx_ref[i]):
                c.start()

    slot = lax.rem(i, 2)
    other = 1 - slot

    @pl.when(i < nu)
    def _():
        gather_tile(slot).wait()

        @pl.when(i >= 1)
        def _():
            scatter_tile(slot).wait()

        nxt = jnp.minimum(i + 1, nu - 1)
        prv = jnp.maximum(i - 1, 0)
        for r in range(MOE_TM):
            gather_row(nxt, other, r).start(priority=r % 2)
            scatter_row(prv, other, r).start(priority=(r + 1) % 2)

        x = jnp.concatenate([xbuf[slot, s].astype(BF16) for s in range(Y_ROWS)], axis=1)
        a = _dot(x, wgb_ref[...])
        u = _dot(x, wub_ref[...])
        act = (_silu(a) * u).astype(BF16)
        y = _dot(act, wdb_ref[...])
        for s in range(Y_ROWS):
            ystage[slot, s] = y[:, s * LANES:(s + 1) * LANES]

    @pl.when(i == nu)
    def _():
        gather_tile(slot).wait()
        scatter_tile(slot).wait()

        def last(r, c):
            scatter_row(nu - 1, other, r).start()
            return c

        lax.fori_loop(0, MOE_TM, last, 0)
        scatter_tile(other).wait()


def _moe(tile_expert, tile_first, next_expert, n_used, dest, pad_lo, pad_hi, h2t, w_gate, w_up, w_down):
    T = h2t.shape[0]
    assert T & (T - 1) == 0 and T >= MOE_TM, "token index is recovered from the row table by masking"
    n_tiles = tile_expert.shape[0] - 1
    hbm = pl.BlockSpec(memory_space=pl.ANY)
    grid_spec = pltpu.PrefetchScalarGridSpec(
        num_scalar_prefetch=7,
        grid=(n_tiles + 1,),
        in_specs=[hbm, hbm, hbm, hbm],
        out_specs=hbm,
        scratch_shapes=[pltpu.SMEM((n_tiles * MOE_TM,), I32),
                        pltpu.VMEM((2, Y_ROWS, MOE_TM, LANES), F32),
                        pltpu.VMEM((2, Y_ROWS, MOE_TM, LANES), F32),
                        pltpu.VMEM((D_MODEL, D_EXPERT), F32),
                        pltpu.VMEM((D_MODEL, D_EXPERT), F32),
                        pltpu.VMEM((D_EXPERT, D_MODEL), F32),
                        pltpu.VMEM((D_MODEL, D_EXPERT), BF16),
                        pltpu.VMEM((D_MODEL, D_EXPERT), BF16),
                        pltpu.VMEM((D_EXPERT, D_MODEL), BF16),
                        pltpu.SemaphoreType.DMA((2,)),
                        pltpu.SemaphoreType.DMA((2,)),
                        pltpu.SemaphoreType.DMA((3,))],
    )
    return pl.pallas_call(
        _moe_body,
        grid_spec=grid_spec,
        out_shape=jax.ShapeDtypeStruct((2 * T + MOE_TM, Y_ROWS, LANES), F32),
        compiler_params=_cparams(("arbitrary",)),
        name="moe",
    )(tile_expert, tile_first, next_expert, n_used, dest, pad_lo, pad_hi, h2t, w_gate, w_up, w_down)


COMB_TM = 256


def _combine_body(h1_ref, y0_ref, y1_ref, cw_ref, nw_ref, o_ref):
    tm = h1_ref.shape[0]
    c0 = cw_ref[:, 0:1]
    c1 = cw_ref[:, 1:2]
    cols = []
    for s in range(Y_ROWS):
        moe = c0 * y0_ref[:, s, :] + c1 * y1_ref[:, s, :]
        cols.append(h1_ref[:, s * LANES:(s + 1) * LANES] + moe)
    h = jnp.concatenate(cols, axis=1)
    ms = jnp.mean(h * h, axis=-1, keepdims=True)
    o_ref[...] = h * lax.rsqrt(ms + EPS) * nw_ref[...]


def _combine(h1, y_tok, cw, nw):
    T = h1.shape[0]
    tm = min(COMB_TM, T)
    nt = T // tm
    return pl.pallas_call(
        _combine_body,
        grid=(nt,),
        in_specs=[
            pl.BlockSpec((tm, D_MODEL), lambda i: (i, 0)),
            pl.BlockSpec((tm, Y_ROWS, LANES), lambda i: (i, 0, 0)),
            pl.BlockSpec((tm, Y_ROWS, LANES), lambda i: (i + nt, 0, 0)),
            pl.BlockSpec((tm, LANES), lambda i: (i, 0)),
            pl.BlockSpec((1, D_MODEL), lambda i: (0, 0)),
        ],
        out_specs=pl.BlockSpec((tm, D_MODEL), lambda i: (i, 0)),
        out_shape=jax.ShapeDtypeStruct((T, D_MODEL), F32),
        compiler_params=_cparams(("parallel",)),
        name="combine",
    )(h1, y_tok, y_tok, cw, nw)


def _tile_plan(cnt, n_tiles):
    tiles = (cnt + (MOE_TM - 1)) // MOE_TM
    ends = jnp.cumsum(tiles)
    starts = ends - tiles
    n_used = ends[-1]
    step = jnp.arange(n_tiles + 1, dtype=I32)
    tile = jnp.minimum(step, jnp.maximum(n_used - 1, 0))
    tile_expert = jnp.sum((ends[None, :] <= tile[:, None]).astype(I32), axis=1)
    tile_first = ((step == starts[tile_expert]) & (step < n_used)).astype(I32)
    nxt_tile = ends[tile_expert]
    nxt_expert = jnp.sum((ends[None, :] <= nxt_tile[:, None]).astype(I32), axis=1)
    next_expert = jnp.where(nxt_tile < n_used, nxt_expert, -1).astype(I32)
    pad_lo = starts * MOE_TM + cnt
    pad_hi = ends * MOE_TM
    return (tile_expert, tile_first, next_expert, n_used.reshape(1).astype(I32),
            pad_lo.astype(I32), pad_hi.astype(I32))


def kernel(x, positions, norm1_w, w_in, conv_w, conv_b, dt_bias, a_log, d_skip, ret_norm_w,
           ssm_norm_w, w_out, norm2_w, w_router_group, b_router_group, w_router_expert,
           b_router_expert, w_expert_gate, w_expert_up, w_expert_down, final_norm_w):
    B, T, D = x.shape
    assert B == 1 and D == D_MODEL and T % CHUNK == 0
    xf = x.reshape(T, D)
    pad_l = lambda v: jnp.pad(v, ((0, 0), (0, LANES - v.shape[-1])))

    w_main = w_in[0][:, :MAIN_PROJ].astype(BF16)
    w_dt = pad_l(w_in[0][:, MAIN_PROJ:]).astype(BF16)
    half = RET_HEAD_DIM // 2
    inv = (ROPE_THETA ** (-jnp.arange(half, dtype=F32) / half)).reshape(1, half)
    tri = (jnp.arange(CHUNK)[:, None] >= jnp.arange(CHUNK)[None, :]).astype(BF16)
    expand = (jnp.arange(LANES)[:, None] == jnp.arange(SSM_WIDTH)[None, :] // SSM_HEAD_DIM).astype(BF16)
    d_skip_e = jnp.repeat(d_skip[0], SSM_HEAD_DIM).reshape(1, SSM_WIDTH)
    wr_t = jnp.zeros((LANES, D), F32)
    wr_t = wr_t.at[0:N_GROUPS].set(w_router_group[0].T).at[8:8 + N_EXPERTS].set(w_router_expert[0].T)
    br = jnp.zeros((LANES,), F32)
    br = br.at[0:N_GROUPS].set(b_router_group[0]).at[8:8 + N_EXPERTS].set(b_router_expert[0])
    tri_u = (jnp.arange(SORT_BLK)[:, None] <= jnp.arange(SORT_BLK)[None, :]).astype(BF16)
    ltri = (jnp.arange(N_EXPERTS)[:, None] > jnp.arange(N_EXPERTS)[None, :]).astype(BF16)

    proj, dt_raw = _inproj(xf, norm1_w[0].reshape(1, D), w_main, w_dt)
    cos, sin = _rope_tables(positions.reshape(T, 1).astype(F32), inv)
    y_ret = _retention(proj, cos, sin, ret_norm_w[0].reshape(1, RET_WIDTH), _retention_consts())
    y_ssm = _ssd(proj, dt_raw, conv_w[0], conv_b[0].reshape(1, -1), pad_l(dt_bias[0].reshape(1, -1)),
                 pad_l(a_log[0].reshape(1, -1)), d_skip_e, ssm_norm_w[0].reshape(1, -1), tri, expand)
    h1, h2t, cw, ids = _out_router(xf, y_ret, y_ssm, w_out[0].astype(BF16), norm2_w[0].reshape(1, D),
                                   wr_t.astype(BF16), br.reshape(LANES, 1))

    dest_blk, cnt = _sort_index(ids, tri_u, ltri)
    dest = dest_blk[:, 0:2, :].transpose(1, 0, 2).reshape(2 * T)
    n_tiles = (2 * T) // MOE_TM + N_EXPERTS
    tile_expert, tile_first, next_expert, n_used, pad_lo, pad_hi = _tile_plan(cnt[:, 0].astype(I32), n_tiles)
    y_tok = _moe(tile_expert, tile_first, next_expert, n_used, dest, pad_lo, pad_hi, h2t,
                 w_expert_gate[0], w_expert_up[0], w_expert_down[0])
    out = _combine(h1, y_tok, cw, final_norm_w.reshape(1, D))
    return out.reshape(B, T, D)
```

```python
import functools

import jax
import jax.numpy as jnp
from jax import lax
from jax.experimental import pallas as pl
from jax.experimental.pallas import tpu as pltpu

F32 = jnp.float32
BF16 = jnp.bfloat16
I32 = jnp.int32

D_MODEL = 2048
EPS = 1e-6
CHUNK = 128
RET_HEADS = 4
RET_HEAD_DIM = 256
RET_WIDTH = RET_HEADS * RET_HEAD_DIM
ROPE_THETA = 10000.0
SSM_WIDTH = 1024
SSM_HEAD_DIM = 64
SSM_HEADS = SSM_WIDTH // SSM_HEAD_DIM
SSM_GROUPS = 2
SSM_STATE = 128
SSM_CONV = 4
SSM_CONV_DIM = SSM_WIDTH + 2 * SSM_GROUPS * SSM_STATE
MAIN_PROJ = 4 * RET_WIDTH + SSM_WIDTH + SSM_CONV_DIM
N_GROUPS = 4
GROUP_EXPERTS = 8
N_EXPERTS = N_GROUPS * GROUP_EXPERTS
D_EXPERT = 512
LANES = 128
Y_ROWS = D_MODEL // LANES

VMEM_LIMIT = 56 * 1024 * 1024


def _cparams(sem, vmem=VMEM_LIMIT):
    return pltpu.CompilerParams(dimension_semantics=sem, vmem_limit_bytes=vmem)


def _silu(x):
    return x * (1.0 / (1.0 + jnp.exp(-x)))


def _dot(a, b):
    return jnp.dot(a, b, preferred_element_type=F32)


def _dot_nt(a, b):
    return lax.dot_general(a, b, (((1,), (1,)), ((), ())), preferred_element_type=F32)


def _split3(a):
    a1 = a.astype(BF16)
    r1 = a - a1.astype(F32)
    a2 = r1.astype(BF16)
    a3 = (r1 - a2.astype(F32)).astype(BF16)
    return a1, a2, a3


def _cast_body(w_ref, o_ref):
    o_ref[...] = w_ref[...].astype(BF16)


def _cast_cols_bf16(w, n_cols, tn):
    rows = w.shape[0]
    return pl.pallas_call(
        _cast_body,
        grid=(n_cols // tn,),
        in_specs=[pl.BlockSpec((rows, tn), lambda j: (0, j))],
        out_specs=pl.BlockSpec((rows, tn), lambda j: (0, j)),
        out_shape=jax.ShapeDtypeStruct((rows, n_cols), BF16),
        compiler_params=_cparams(("parallel",)),
        name="cast_bf16",
    )(w)


IN_TM = 1024
IN_TN = 512


def _inproj_body(x_ref, nw_ref, w_ref, wdt_ref, o_ref, odt_ref, hn_ref):
    @pl.when(pl.program_id(1) == 0)
    def _():
        def blk(r, carry):
            sl = pl.ds(pl.multiple_of(r * CHUNK, CHUNK), CHUNK)
            x = x_ref[sl, :]
            ms = jnp.mean(x * x, axis=-1, keepdims=True)
            hn_ref[sl, :] = (x * lax.rsqrt(ms + EPS) * nw_ref[...]).astype(BF16)
            return carry
        lax.fori_loop(0, x_ref.shape[0] // CHUNK, blk, 0)
        odt_ref[...] = _dot(hn_ref[...], wdt_ref[...])

    o_ref[...] = _dot(hn_ref[...], w_ref[...])


def _inproj(x, nw, w_main, w_dt):
    T = x.shape[0]
    tm = min(IN_TM, T)
    return pl.pallas_call(
        _inproj_body,
        grid=(T // tm, MAIN_PROJ // IN_TN),
        in_specs=[
            pl.BlockSpec((tm, D_MODEL), lambda i, j: (i, 0)),
            pl.BlockSpec((1, D_MODEL), lambda i, j: (0, 0)),
            pl.BlockSpec((D_MODEL, IN_TN), lambda i, j: (0, j)),
            pl.BlockSpec((D_MODEL, LANES), lambda i, j: (0, 0)),
        ],
        out_specs=[
            pl.BlockSpec((tm, IN_TN), lambda i, j: (i, j)),
            pl.BlockSpec((tm, LANES), lambda i, j: (i, 0)),
        ],
        out_shape=[
            jax.ShapeDtypeStruct((T, MAIN_PROJ), F32),
            jax.ShapeDtypeStruct((T, LANES), F32),
        ],
        scratch_shapes=[pltpu.VMEM((tm, D_MODEL), BF16)],
        compiler_params=_cparams(("parallel", "arbitrary")),
        name="in_proj",
    )(x, nw, w_main, w_dt)


def _rope_body(pos_ref, inv_ref, cos_ref, sin_ref):
    ang = pos_ref[...] * inv_ref[...]
    cos_ref[...] = jnp.cos(ang)
    sin_ref[...] = jnp.sin(ang)


def _rope_tables(pos, inv):
    T = pos.shape[0]
    tm = min(1024, T)
    half = inv.shape[1]
    return pl.pallas_call(
        _rope_body,
        grid=(T // tm,),
        in_specs=[pl.BlockSpec((tm, 1), lambda i: (i, 0)),
                  pl.BlockSpec((1, half), lambda i: (0, 0))],
        out_specs=[pl.BlockSpec((tm, half), lambda i: (i, 0))] * 2,
        out_shape=[jax.ShapeDtypeStruct((T, half), F32)] * 2,
        compiler_params=_cparams(("parallel",)),
        name="rope_tables",
    )(pos, inv)


RET_CB = 2


def _retention_body(cd_ref, q_ref, k_ref, v_ref, g_ref, cos_ref, sin_ref,
                    dec_ref, wq_ref, ws_ref, nw_ref, o_ref, st_ref, *, cb):
    @pl.when(pl.program_id(0) == 0)
    def _():
        st_ref[...] = jnp.zeros_like(st_ref)

    hd = RET_HEAD_DIM
    half = hd // 2
    for i in range(cb):
        sl = slice(i * CHUNK, (i + 1) * CHUNK)
        cos = cos_ref[sl, :]
        sin = sin_ref[sl, :]

        def rope(x):
            x1, x2 = x[:, :half], x[:, half:]
            return jnp.concatenate([x1 * cos - x2 * sin, x2 * cos + x1 * sin], axis=-1)

        for h in range(RET_HEADS):
            hs = slice(h * hd, (h + 1) * hd)
            q = rope(q_ref[sl, hs])
            k = rope(k_ref[sl, hs]) * (hd ** -0.5)
            v = v_ref[sl, hs]
            qb = q.astype(BF16)
            kb = k.astype(BF16)
            scores = _dot_nt(qb, kb) * dec_ref[h]
            y = _dot(scores.astype(BF16), v.astype(BF16))
            st = st_ref[h]
            y = y + _dot(qb, st.astype(BF16)) * wq_ref[h]
            vw = (v * ws_ref[h]).astype(BF16)
            new = _dot(k.T.astype(BF16), vw)
            st_ref[h] = st * cd_ref[h] + new
            mu = jnp.mean(y, axis=-1, keepdims=True)
            d = y - mu
            var = jnp.mean(d * d, axis=-1, keepdims=True)
            yn = d * lax.rsqrt(var + EPS)
            o_ref[sl, hs] = (yn * nw_ref[:, hs] * _silu(g_ref[sl, hs])).astype(BF16)


def _retention(proj, cos, sin, ret_nw, consts):
    T = proj.shape[0]
    cb = min(RET_CB, T // CHUNK)
    rows = cb * CHUNK
    chunk_decay, decay_intra, w_query, w_state = consts
    hd = RET_HEAD_DIM
    half = hd // 2
    full = lambda shape: pl.BlockSpec(shape, lambda c, cd: (0,) * len(shape))
    col = lambda j: pl.BlockSpec((rows, RET_WIDTH), lambda c, cd: (c, j))

    grid_spec = pltpu.PrefetchScalarGridSpec(
        num_scalar_prefetch=1,
        grid=(T // rows,),
        in_specs=[
            col(0), col(1), col(2), col(3),
            pl.BlockSpec((rows, half), lambda c, cd: (c, 0)),
            pl.BlockSpec((rows, half), lambda c, cd: (c, 0)),
            full((RET_HEADS, CHUNK, CHUNK)),
            full((RET_HEADS, CHUNK, hd)),
            full((RET_HEADS, CHUNK, hd)),
            full((1, RET_WIDTH)),
        ],
        out_specs=pl.BlockSpec((rows, RET_WIDTH), lambda c, cd: (c, 0)),
        scratch_shapes=[pltpu.VMEM((RET_HEADS, hd, hd), F32)],
    )
    return pl.pallas_call(
        functools.partial(_retention_body, cb=cb),
        grid_spec=grid_spec,
        out_shape=jax.ShapeDtypeStruct((T, RET_WIDTH), BF16),
        compiler_params=_cparams(("arbitrary",)),
        name="retention",
    )(chunk_decay, proj, proj, proj, proj, cos, sin, decay_intra, w_query, w_state, ret_nw)


def _retention_consts():
    H, L = RET_HEADS, CHUNK
    log_gamma = jnp.log1p(-(2.0 ** (-5.0 - jnp.arange(H, dtype=F32))))
    idx = jnp.arange(L, dtype=F32)
    diff = idx[:, None] - idx[None, :]
    causal = diff >= 0
    decay_intra = jnp.where(causal[None], jnp.exp(jnp.where(causal, diff, 0.0)[None] * log_gamma[:, None, None]), 0.0)
    w_state = jnp.exp((L - 1.0 - idx)[None, :] * log_gamma[:, None])
    w_query = jnp.exp((idx + 1.0)[None, :] * log_gamma[:, None])
    chunk_decay = jnp.exp(L * log_gamma)
    bc = lambda w: jnp.broadcast_to(w[:, :, None], (H, L, RET_HEAD_DIM))
    return chunk_decay, decay_intra, bc(w_query), bc(w_state)


SSD_CB = 2
CONV_PAD = 8
XBC_BLK = 512


def _ssd_body(x0_ref, x1_ref, x2_ref, z_ref, dt_ref, cw_ref, cbias_ref, dtb_ref,
              alog_ref, dskip_ref, nw_ref, tri_ref, exp_ref, o_ref, xpad_ref, st_ref, *, cb):
    @pl.when(pl.program_id(0) == 0)
    def _():
        xpad_ref[0:CONV_PAD, :] = jnp.zeros((CONV_PAD, SSM_CONV_DIM), F32)
        st_ref[...] = jnp.zeros_like(st_ref)

    L = CHUNK
    gw = SSM_WIDTH // SSM_GROUPS
    rows_i = lax.broadcasted_iota(I32, (L, L), 0)
    cols_i = lax.broadcasted_iota(I32, (L, L), 1)
    causal = rows_i >= cols_i
    lo_lane = lax.broadcasted_iota(I32, (L, LANES), 1) < SSM_HEAD_DIM
    tri = tri_ref[...]
    expand = exp_ref[...]
    a_neg = -jnp.exp(alog_ref[...])

    for i in range(cb):
        sl = slice(i * L, (i + 1) * L)
        for b, xr in enumerate((x0_ref, x1_ref, x2_ref)):
            xpad_ref[CONV_PAD:CONV_PAD + L, b * XBC_BLK:(b + 1) * XBC_BLK] = xr[sl, :]
        u_parts = []
        for b in range(SSM_CONV_DIM // XBC_BLK):
            cs = slice(b * XBC_BLK, (b + 1) * XBC_BLK)
            acc = cbias_ref[:, cs]
            for t in range(SSM_CONV):
                r0 = CONV_PAD - (SSM_CONV - 1) + t
                acc = acc + xpad_ref[r0:r0 + L, cs] * cw_ref[t:t + 1, cs]
            u_parts.append(_silu(acc))
        xpad_ref[0:CONV_PAD, :] = xpad_ref[L:L + CONV_PAD, :]
        xs = jnp.concatenate(u_parts[:2], axis=-1)
        bm = u_parts[2][:, :SSM_GROUPS * SSM_STATE]
        cm = u_parts[2][:, SSM_GROUPS * SSM_STATE:]

        dt_in = dt_ref[sl, :] + dtb_ref[...]
        dt = jnp.maximum(dt_in, 0.0) + jnp.log1p(jnp.exp(-jnp.abs(dt_in)))
        a = dt * a_neg
        a1, a2, a3 = _split3(a)
        a_cs = _dot(tri, a1) + _dot(tri, a2) + _dot(tri, a3)
        d1, d2, d3 = _split3(dt)
        dt_e = _dot(d1, expand) + _dot(d2, expand) + _dot(d3, expand)
        c1, c2, c3 = _split3(a_cs)
        acs_e = _dot(c1, expand) + _dot(c2, expand) + _dot(c3, expand)
        last = acs_e[L - 1:L, :]
        to_end = jnp.exp(last - acs_e)
        from_start = jnp.exp(acs_e)
        chunk_decay = jnp.exp(last)
        x_dt = xs * dt_e
        xw = (x_dt * to_end).astype(BF16)
        acs_t = a_cs.T
        cmb = cm.astype(BF16)
        bmb = bm.astype(BF16)

        ys = []
        for g in range(SSM_GROUPS):
            ns = slice(g * SSM_STATE, (g + 1) * SSM_STATE)
            gs = slice(g * gw, (g + 1) * gw)
            cg = cmb[:, ns]
            cbg = _dot_nt(cg, bmb[:, ns])
            st = st_ref[g]
            y_off = _dot(cg, st.astype(BF16))
            new = _dot(bm[:, ns].T.astype(BF16), xw[:, gs])
            st_ref[g] = st * chunk_decay[:, gs] + new
            for jp in range(gw // LANES):
                h0 = (g * gw + jp * LANES) // SSM_HEAD_DIM
                ms = []
                for hh in (h0, h0 + 1):
                    seg = a_cs[:, hh:hh + 1] - acs_t[hh:hh + 1, :]
                    dec = jnp.exp(jnp.where(causal, seg, -jnp.inf))
                    ms.append((cbg * dec).astype(BF16))
                lhs = jnp.concatenate(ms, axis=1)
                ls = slice(g * gw + jp * LANES, g * gw + (jp + 1) * LANES)
                xp = x_dt[:, ls]
                rhs = jnp.concatenate([jnp.where(lo_lane, xp, 0.0),
                                       jnp.where(lo_lane, 0.0, xp)], axis=0).astype(BF16)
                y_diag = _dot(lhs, rhs)
                ys.append(y_diag + y_off[:, jp * LANES:(jp + 1) * LANES] * from_start[:, ls])
        y = jnp.concatenate(ys, axis=1) + xs * dskip_ref[...]
        y = y * _silu(z_ref[sl, :])
        outs = []
        for g in range(SSM_GROUPS):
            yg = y[:, g * gw:(g + 1) * gw]
            ms_ = jnp.mean(yg * yg, axis=-1, keepdims=True)
            outs.append(yg * lax.rsqrt(ms_ + EPS))
        o_ref[sl, :] = (jnp.concatenate(outs, axis=1) * nw_ref[...]).astype(BF16)


def _ssd(proj, dt_raw, conv_w, conv_b, dt_bias, a_log, d_skip_e, ssm_nw, tri, expand):
    T = proj.shape[0]
    cb = min(SSD_CB, T // CHUNK)
    rows = cb * CHUNK
    xbc0 = (4 * RET_WIDTH + SSM_WIDTH) // XBC_BLK
    zcol = 4 * RET_WIDTH // SSM_WIDTH
    full = lambda shape: pl.BlockSpec(shape, lambda c: (0,) * len(shape))
    return pl.pallas_call(
        functools.partial(_ssd_body, cb=cb),
        grid=(T // rows,),
        in_specs=[
            pl.BlockSpec((rows, XBC_BLK), lambda c: (c, xbc0)),
            pl.BlockSpec((rows, XBC_BLK), lambda c: (c, xbc0 + 1)),
            pl.BlockSpec((rows, XBC_BLK), lambda c: (c, xbc0 + 2)),
            pl.BlockSpec((rows, SSM_WIDTH), lambda c: (c, zcol)),
            pl.BlockSpec((rows, LANES), lambda c: (c, 0)),
            full((SSM_CONV, SSM_CONV_DIM)),
            full((1, SSM_CONV_DIM)),
            full((1, LANES)),
            full((1, LANES)),
            full((1, SSM_WIDTH)),
            full((1, SSM_WIDTH)),
            full((CHUNK, CHUNK)),
            full((LANES, SSM_WIDTH)),
        ],
        out_specs=pl.BlockSpec((rows, SSM_WIDTH), lambda c: (c, 0)),
        out_shape=jax.ShapeDtypeStruct((T, SSM_WIDTH), BF16),
        scratch_shapes=[
            pltpu.VMEM((CHUNK + CONV_PAD, SSM_CONV_DIM), F32),
            pltpu.VMEM((SSM_GROUPS, SSM_STATE, SSM_WIDTH // SSM_GROUPS), F32),
        ],
        compiler_params=_cparams(("arbitrary",)),
        name="ssd",
    )(proj, proj, proj, proj, dt_raw, conv_w, conv_b, dt_bias, a_log, d_skip_e, ssm_nw, tri, expand)


OUT_TM = 256
ROUTE_ROWS = 8 + N_EXPERTS


def _out_router_body(x_ref, yr_ref, ys_ref, wo_ref, nw_ref, wr_ref, br_ref,
                     h1_ref, h2t_ref, cw_ref, ids_ref):
    tm = x_ref.shape[0]
    h1 = x_ref[...] + _dot(yr_ref[...], wo_ref[0:RET_WIDTH, :]) + _dot(ys_ref[...], wo_ref[RET_WIDTH:, :])
    h1_ref[...] = h1
    ms = jnp.mean(h1 * h1, axis=-1, keepdims=True)
    h2 = h1 * lax.rsqrt(ms + EPS) * nw_ref[...]
    for s in range(Y_ROWS):
        h2t_ref[:, s, :] = h2[:, s * LANES:(s + 1) * LANES]

    logits = _dot_nt(wr_ref[...], h2.astype(BF16)) + br_ref[...]
    row = lax.broadcasted_iota(I32, (8, tm), 0)
    lg = jnp.where(row < N_GROUPS, logits[0:8], -jnp.inf)
    m = jnp.max(lg, axis=0, keepdims=True)
    p_sel = 1.0 / jnp.sum(jnp.exp(lg - m), axis=0, keepdims=True)
    g_sel = jnp.min(jnp.where(lg == m, row, 8), axis=0, keepdims=True)
    le = jnp.zeros((GROUP_EXPERTS, tm), F32)
    for g in range(N_GROUPS):
        le = jnp.where(g_sel == g, logits[8 + g * GROUP_EXPERTS:8 + (g + 1) * GROUP_EXPERTS], le)
    m2 = jnp.max(le, axis=0, keepdims=True)
    ee = jnp.exp(le - m2)
    pe = ee / jnp.sum(ee, axis=0, keepdims=True)
    v1 = jnp.max(pe, axis=0, keepdims=True)
    i1 = jnp.min(jnp.where(pe == v1, row, 8), axis=0, keepdims=True)
    pe2 = jnp.where(row == i1, -1.0, pe)
    v2 = jnp.max(pe2, axis=0, keepdims=True)
    i2 = jnp.min(jnp.where(pe2 == v2, row, 8), axis=0, keepdims=True)
    tw = v1 + v2
    c1 = v1 / tw * p_sel
    c2 = v2 / tw * p_sel
    e1 = g_sel * GROUP_EXPERTS + i1
    e2 = g_sel * GROUP_EXPERTS + i2
    ids = jnp.where(row == 0, e1, jnp.where(row == 1, e2, 0))
    for b in range(tm // SORT_BLK):
        ids_ref[b] = ids[:, b * SORT_BLK:(b + 1) * SORT_BLK]
    cw8 = jnp.where(row == 0, c1, jnp.where(row == 1, c2, 0.0))
    cw = jnp.concatenate([cw8, jnp.zeros((LANES - 8, tm), F32)], axis=0)
    cw_ref[...] = cw.T


def _out_router(x, y_ret, y_ssm, w_out, nw, wr_t, br):
    T = x.shape[0]
    tm = min(OUT_TM, T)
    full = lambda shape: pl.BlockSpec(shape, lambda i: (0,) * len(shape))
    return pl.pallas_call(
        _out_router_body,
        grid=(T // tm,),
        in_specs=[
            pl.BlockSpec((tm, D_MODEL), lambda i: (i, 0)),
            pl.BlockSpec((tm, RET_WIDTH), lambda i: (i, 0)),
            pl.BlockSpec((tm, SSM_WIDTH), lambda i: (i, 0)),
            full((D_MODEL, D_MODEL)),
            full((1, D_MODEL)),
            full((LANES, D_MODEL)),
            full((LANES, 1)),
        ],
        out_specs=[
            pl.BlockSpec((tm, D_MODEL), lambda i: (i, 0)),
            pl.BlockSpec((tm, Y_ROWS, LANES), lambda i: (i, 0, 0)),
            pl.BlockSpec((tm, LANES), lambda i: (i, 0)),
            pl.BlockSpec((tm // SORT_BLK, 8, SORT_BLK), lambda i: (i, 0, 0)),
        ],
        out_shape=[
            jax.ShapeDtypeStruct((T, D_MODEL), F32),
            jax.ShapeDtypeStruct((T, Y_ROWS, LANES), F32),
            jax.ShapeDtypeStruct((T, LANES), F32),
            jax.ShapeDtypeStruct((T // SORT_BLK, 8, SORT_BLK), I32),
        ],
        compiler_params=_cparams(("parallel",)),
        name="out_router",
    )(x, y_ret, y_ssm, w_out, nw, wr_t, br)


SORT_BLK = 256
MOE_TM = 256


def _sort_index_body(ids_ref, tri_ref, ltri_ref, dest_ref, cnt_ref, rank_ref):
    nblk = ids_ref.shape[0]
    row_e = lax.broadcasted_iota(I32, (N_EXPERTS, SORT_BLK), 0)
    row8 = lax.broadcasted_iota(I32, (8, SORT_BLK), 0)

    def onehots(b):
        ids = ids_ref[b]
        return row_e == ids[0:1], row_e == ids[1:2]

    def rank_blk(b, carry):
        oh1, oh2 = onehots(b)
        ohf = jnp.where(oh1 | oh2, 1.0, 0.0)
        incl = _dot(ohf.astype(BF16), tri_ref[...])
        base = carry + incl - 1.0
        r1 = jnp.sum(jnp.where(oh1, base, 0.0), axis=0, keepdims=True)
        r2 = jnp.sum(jnp.where(oh2, base, 0.0), axis=0, keepdims=True)
        rank_ref[b] = jnp.where(row8 == 0, r1, jnp.where(row8 == 1, r2, 0.0))
        return carry + jnp.sum(ohf, axis=1, keepdims=True)

    cnt = lax.fori_loop(0, nblk, rank_blk, jnp.zeros((N_EXPERTS, 1), F32))
    cnt_ref[...] = jnp.broadcast_to(cnt, cnt_ref.shape)
    tiles = jnp.floor((cnt + (MOE_TM - 1.0)) / MOE_TM)
    tiles_b = jnp.broadcast_to(tiles, (N_EXPERTS, LANES)).astype(BF16)
    off = _dot(ltri_ref[...], tiles_b)[:, 0:1] * MOE_TM

    def dest_blk(b, carry):
        oh1, oh2 = onehots(b)
        o1 = jnp.sum(jnp.where(oh1, off, 0.0), axis=0, keepdims=True)
        o2 = jnp.sum(jnp.where(oh2, off, 0.0), axis=0, keepdims=True)
        d = rank_ref[b] + jnp.where(row8 == 0, o1, jnp.where(row8 == 1, o2, 0.0))
        dest_ref[b] = d.astype(I32)
        return carry

    lax.fori_loop(0, nblk, dest_blk, 0)


def _sort_index(ids, tri_u, ltri):
    nblk = ids.shape[0]
    return pl.pallas_call(
        _sort_index_body,
        out_shape=[jax.ShapeDtypeStruct((nblk, 8, SORT_BLK), I32),
                   jax.ShapeDtypeStruct((N_EXPERTS, LANES), F32)],
        scratch_shapes=[pltpu.VMEM((nblk, 8, SORT_BLK), F32)],
        compiler_params=_cparams(None),
        name="sort_index",
    )(ids, tri_u, ltri)


ROW_DMA_PRIORITY = 1


def _moe_body(te_ref, tf_ref, nx_ref, nu_ref, dest_ref, lo_ref, hi_ref,
              h2t_ref, wg_ref, wu_ref, wd_ref, yt_ref,
              srow_ref, xbuf, ystage, wsg, wsu, wsd, wgb_ref, wub_ref, wdb_ref, gsem, ssem, wsem):
    i = pl.program_id(0)
    nu = nu_ref[0]
    T = h2t_ref.shape[0]

    def gather_row(tile, slot, r):
        tok = srow_ref[tile * MOE_TM + r] & (T - 1)
        return pltpu.make_async_copy(h2t_ref.at[tok], xbuf.at[slot, :, r, :], gsem.at[slot])

    def scatter_row(tile, slot, r):
        return pltpu.make_async_copy(ystage.at[slot, :, r, :], yt_ref.at[srow_ref[tile * MOE_TM + r]],
                                     ssem.at[slot])

    def gather_tile(slot):
        return pltpu.make_async_copy(h2t_ref.at[pl.ds(0, MOE_TM)], h2t_ref.at[pl.ds(0, MOE_TM)], gsem.at[slot])

    def scatter_tile(slot):
        return pltpu.make_async_copy(yt_ref.at[pl.ds(0, MOE_TM)], yt_ref.at[pl.ds(0, MOE_TM)], ssem.at[slot])

    def weight_copies(e):
        return (pltpu.make_async_copy(wg_ref.at[e], wsg, wsem.at[0]),
                pltpu.make_async_copy(wu_ref.at[e], wsu, wsem.at[1]),
                pltpu.make_async_copy(wd_ref.at[e], wsd, wsem.at[2]))

    @pl.when(i == 0)
    def _():
        for c in weight_copies(te_ref[0]):
            c.start()

        def tok(t, c):
            srow_ref[dest_ref[t]] = t
            srow_ref[dest_ref[T + t]] = T + t
            return c

        lax.fori_loop(0, T, tok, 0, unroll=8)

        def seg(e, c):
            def pad(r, c2):
                srow_ref[r] = 2 * T + (r & (MOE_TM - 1))
                return c2

            lax.fori_loop(lo_ref[e], hi_ref[e], pad, 0)
            return c

        lax.fori_loop(0, lo_ref.shape[0], seg, 0)

        ystage[...] = jnp.zeros_like(ystage)

        def spare(r, c):
            pltpu.make_async_copy(ystage.at[0, :, r, :], yt_ref.at[2 * T + r], ssem.at[0]).start()
            return c

        lax.fori_loop(0, MOE_TM, spare, 0)
        scatter_tile(0).wait()
        for r in range(MOE_TM):
            gather_row(0, 0, r).start(priority=ROW_DMA_PRIORITY)

    @pl.when(tf_ref[i] == 1)
    def _():
        for c in weight_copies(te_ref[i]):
            c.wait()
        wgb_ref[...] = wsg[...].astype(BF16)
        wub_ref[...] = wsu[...].astype(BF16)
        wdb_ref[...] = wsd[...].astype(BF16)

        @pl.when(nx_ref[i] >= 0)
        def _():
            for c in weight_copies(nx_ref[i]):
                c.start()

    slot = lax.rem(i, 2)
    other = 1 - slot

    @pl.when(i < nu)
    def _():
        gather_tile(slot).wait()

        @pl.when(i >= 1)
        def _():
            scatter_tile(slot).wait()

        nxt = jnp.minimum(i + 1, nu - 1)
        prv = jnp.maximum(i - 1, 0)
        for r in range(MOE_TM):
            gather_row(nxt, other, r).start(priority=r % 2)
            scatter_row(prv, other, r).start(priority=(r + 1) % 2)

        x = jnp.concatenate([xbuf[slot, s].astype(BF16) for s in range(Y_ROWS)], axis=1)
        a = _dot(x, wgb_ref[...])
        u = _dot(x, wub_ref[...])
        act = (_silu(a) * u).astype(BF16)
        y = _dot(act, wdb_ref[...])
        for s in range(Y_ROWS):
            ystage[slot, s] = y[:, s * LANES:(s + 1) * LANES]

    @pl.when(i == nu)
    def _():
        gather_tile(slot).wait()
        scatter_tile(slot).wait()

        def last(r, c):
            scatter_row(nu - 1, other, r).start()
            return c

        lax.fori_loop(0, MOE_TM, last, 0)
        scatter_tile(other).wait()


def _moe(tile_expert, tile_first, next_expert, n_used, dest, pad_lo, pad_hi, h2t, w_gate, w_up, w_down):
    T = h2t.shape[0]
    assert T & (T - 1) == 0 and T >= MOE_TM, "token index is recovered from the row table by masking"
    n_tiles = tile_expert.shape[0] - 1
    hbm = pl.BlockSpec(memory_space=pl.ANY)
    grid_spec = pltpu.PrefetchScalarGridSpec(
        num_scalar_prefetch=7,
        grid=(n_tiles + 1,),
        in_specs=[hbm, hbm, hbm, hbm],
        out_specs=hbm,
        scratch_shapes=[pltpu.SMEM((n_tiles * MOE_TM,), I32),
                        pltpu.VMEM((2, Y_ROWS, MOE_TM, LANES), F32),
                        pltpu.VMEM((2, Y_ROWS, MOE_TM, LANES), F32),
                        pltpu.VMEM((D_MODEL, D_EXPERT), F32),
                        pltpu.VMEM((D_MODEL, D_EXPERT), F32),
                        pltpu.VMEM((D_EXPERT, D_MODEL), F32),
                        pltpu.VMEM((D_MODEL, D_EXPERT), BF16),
                        pltpu.VMEM((D_MODEL, D_EXPERT), BF16),
                        pltpu.VMEM((D_EXPERT, D_MODEL), BF16),
                        pltpu.SemaphoreType.DMA((2,)),
                        pltpu.SemaphoreType.DMA((2,)),
                        pltpu.SemaphoreType.DMA((3,))],
    )
    return pl.pallas_call(
        _moe_body,
        grid_spec=grid_spec,
        out_shape=jax.ShapeDtypeStruct((2 * T + MOE_TM, Y_ROWS, LANES), F32),
        compiler_params=_cparams(("arbitrary",)),
        name="moe",
    )(tile_expert, tile_first, next_expert, n_used, dest, pad_lo, pad_hi, h2t, w_gate, w_up, w_down)


COMB_TM = 256


def _combine_body(h1_ref, y0_ref, y1_ref, cw_ref, nw_ref, o_ref):
    tm = h1_ref.shape[0]
    c0 = cw_ref[:, 0:1]
    c1 = cw_ref[:, 1:2]
    cols = []
    for s in range(Y_ROWS):
        moe = c0 * y0_ref[:, s, :] + c1 * y1_ref[:, s, :]
        cols.append(h1_ref[:, s * LANES:(s + 1) * LANES] + moe)
    h = jnp.concatenate(cols, axis=1)
    ms = jnp.mean(h * h, axis=-1, keepdims=True)
    o_ref[...] = h * lax.rsqrt(ms + EPS) * nw_ref[...]


def _combine(h1, y_tok, cw, nw):
    T = h1.shape[0]
    tm = min(COMB_TM, T)
    nt = T // tm
    return pl.pallas_call(
        _combine_body,
        grid=(nt,),
        in_specs=[
            pl.BlockSpec((tm, D_MODEL), lambda i: (i, 0)),
            pl.BlockSpec((tm, Y_ROWS, LANES), lambda i: (i, 0, 0)),
            pl.BlockSpec((tm, Y_ROWS, LANES), lambda i: (i + nt, 0, 0)),
            pl.BlockSpec((tm, LANES), lambda i: (i, 0)),
            pl.BlockSpec((1, D_MODEL), lambda i: (0, 0)),
        ],
        out_specs=pl.BlockSpec((tm, D_MODEL), lambda i: (i, 0)),
        out_shape=jax.ShapeDtypeStruct((T, D_MODEL), F32),
        compiler_params=_cparams(("parallel",)),
        name="combine",
    )(h1, y_tok, y_tok, cw, nw)


def _tile_plan(cnt, n_tiles):
    tiles = (cnt + (MOE_TM - 1)) // MOE_TM
    ends = jnp.cumsum(tiles)
    starts = ends - tiles
    n_used = ends[-1]
    step = jnp.arange(n_tiles + 1, dtype=I32)
    tile = jnp.minimum(step, jnp.maximum(n_used - 1, 0))
    tile_expert = jnp.sum((ends[None, :] <= tile[:, None]).astype(I32), axis=1)
    tile_first = ((step == starts[tile_expert]) & (step < n_used)).astype(I32)
    nxt_tile = ends[tile_expert]
    nxt_expert = jnp.sum((ends[None, :] <= nxt_tile[:, None]).astype(I32), axis=1)
    next_expert = jnp.where(nxt_tile < n_used, nxt_expert, -1).astype(I32)
    pad_lo = starts * MOE_TM + cnt
    pad_hi = ends * MOE_TM
    return (tile_expert, tile_first, next_expert, n_used.reshape(1).astype(I32),
            pad_lo.astype(I32), pad_hi.astype(I32))


def kernel(x, positions, norm1_w, w_in, conv_w, conv_b, dt_bias, a_log, d_skip, ret_norm_w,
           ssm_norm_w, w_out, norm2_w, w_router_group, b_router_group, w_router_expert,
           b_router_expert, w_expert_gate, w_expert_up, w_expert_down, final_norm_w):
    B, T, D = x.shape
    assert B == 1 and D == D_MODEL and T % CHUNK == 0
    xf = x.reshape(T, D)
    pad_l = lambda v: jnp.pad(v, ((0, 0), (0, LANES - v.shape[-1])))

    w_main = _cast_cols_bf16(w_in[0], MAIN_PROJ, IN_TN)
    w_dt = pad_l(w_in[0][:, MAIN_PROJ:]).astype(BF16)
    half = RET_HEAD_DIM // 2
    inv = (ROPE_THETA ** (-jnp.arange(half, dtype=F32) / half)).reshape(1, half)
    tri = (jnp.arange(CHUNK)[:, None] >= jnp.arange(CHUNK)[None, :]).astype(BF16)
    expand = (jnp.arange(LANES)[:, None] == jnp.arange(SSM_WIDTH)[None, :] // SSM_HEAD_DIM).astype(BF16)
    d_skip_e = jnp.repeat(d_skip[0], SSM_HEAD_DIM).reshape(1, SSM_WIDTH)
    wr_t = jnp.zeros((LANES, D), F32)
    wr_t = wr_t.at[0:N_GROUPS].set(w_router_group[0].T).at[8:8 + N_EXPERTS].set(w_router_expert[0].T)
    br = jnp.zeros((LANES,), F32)
    br = br.at[0:N_GROUPS].set(b_router_group[0]).at[8:8 + N_EXPERTS].set(b_router_expert[0])
    tri_u = (jnp.arange(SORT_BLK)[:, None] <= jnp.arange(SORT_BLK)[None, :]).astype(BF16)
    ltri = (jnp.arange(N_EXPERTS)[:, None] > jnp.arange(N_EXPERTS)[None, :]).astype(BF16)

    proj, dt_raw = _inproj(xf, norm1_w[0].reshape(1, D), w_main, w_dt)
    cos, sin = _rope_tables(positions.reshape(T, 1).astype(F32), inv)
    y_ret = _retention(proj, cos, sin, ret_norm_w[0].reshape(1, RET_WIDTH), _retention_consts())
    y_ssm = _ssd(proj, dt_raw, conv_w[0], conv_b[0].reshape(1, -1), pad_l(dt_bias[0].reshape(1, -1)),
                 pad_l(a_log[0].reshape(1, -1)), d_skip_e, ssm_norm_w[0].reshape(1, -1), tri, expand)
    h1, h2t, cw, ids = _out_router(xf, y_ret, y_ssm, w_out[0].astype(BF16), norm2_w[0].reshape(1, D),
                                   wr_t.astype(BF16), br.reshape(LANES, 1))

    dest_blk, cnt = _sort_index(ids, tri_u, ltri)
    dest = dest_blk[:, 0:2, :].transpose(1, 0, 2).reshape(2 * T)
    n_tiles = (2 * T) // MOE_TM + N_EXPERTS
    tile_expert, tile_first, next_expert, n_used, pad_lo, pad_hi = _tile_plan(cnt[:, 0].astype(I32), n_tiles)
    y_tok = _moe(tile_expert, tile_first, next_expert, n_used, dest, pad_lo, pad_hi, h2t,
                 w_expert_gate[0], w_expert_up[0], w_expert_down[0])
    out = _combine(h1, y_tok, cw, final_norm_w.reshape(1, D))
    return out.reshape(B, T, D)
```

```python
import functools

import jax
import jax.numpy as jnp
from jax import lax
from jax.experimental import pallas as pl
from jax.experimental.pallas import tpu as pltpu

F32 = jnp.float32
BF16 = jnp.bfloat16
I32 = jnp.int32

D_MODEL = 2048
EPS = 1e-6
CHUNK = 128
RET_HEADS = 4
RET_HEAD_DIM = 256
RET_WIDTH = RET_HEADS * RET_HEAD_DIM
ROPE_THETA = 10000.0
SSM_WIDTH = 1024
SSM_HEAD_DIM = 64
SSM_HEADS = SSM_WIDTH // SSM_HEAD_DIM
SSM_GROUPS = 2
SSM_STATE = 128
SSM_CONV = 4
SSM_CONV_DIM = SSM_WIDTH + 2 * SSM_GROUPS * SSM_STATE
MAIN_PROJ = 4 * RET_WIDTH + SSM_WIDTH + SSM_CONV_DIM
N_GROUPS = 4
GROUP_EXPERTS = 8
N_EXPERTS = N_GROUPS * GROUP_EXPERTS
D_EXPERT = 512
LANES = 128
Y_ROWS = D_MODEL // LANES

VMEM_LIMIT = 56 * 1024 * 1024


def _cparams(sem, vmem=VMEM_LIMIT):
    return pltpu.CompilerParams(dimension_semantics=sem, vmem_limit_bytes=vmem)


def _silu(x):
    return x * (1.0 / (1.0 + jnp.exp(-x)))


def _dot(a, b):
    return jnp.dot(a, b, preferred_element_type=F32)


def _dot_nt(a, b):
    return lax.dot_general(a, b, (((1,), (1,)), ((), ())), preferred_element_type=F32)


def _split3(a):
    a1 = a.astype(BF16)
    r1 = a - a1.astype(F32)
    a2 = r1.astype(BF16)
    a3 = (r1 - a2.astype(F32)).astype(BF16)
    return a1, a2, a3


def _cast_body(w_ref, o_ref):
    o_ref[...] = w_ref[...].astype(BF16)


def _cast_rows_bf16(w, n_rows, tr):
    cols = w.shape[1]
    return pl.pallas_call(
        _cast_body,
        grid=(n_rows // tr,),
        in_specs=[pl.BlockSpec((tr, cols), lambda j: (j, 0))],
        out_specs=pl.BlockSpec((tr, cols), lambda j: (j, 0)),
        out_shape=jax.ShapeDtypeStruct((n_rows, cols), BF16),
        compiler_params=_cparams(("parallel",)),
        name="cast_bf16",
    )(w)


IN_TM = 1024
IN_TN = 512


def _inproj_body(x_ref, nw_ref, w_ref, wdt_ref, o_ref, odt_ref, hn_ref):
    @pl.when(pl.program_id(1) == 0)
    def _():
        def blk(r, carry):
            sl = pl.ds(pl.multiple_of(r * CHUNK, CHUNK), CHUNK)
            x = x_ref[sl, :]
            ms = jnp.mean(x * x, axis=-1, keepdims=True)
            hn_ref[sl, :] = (x * lax.rsqrt(ms + EPS) * nw_ref[...]).astype(BF16)
            return carry
        lax.fori_loop(0, x_ref.shape[0] // CHUNK, blk, 0)
        odt_ref[...] = _dot_nt(hn_ref[...], wdt_ref[...])

    o_ref[...] = _dot_nt(hn_ref[...], w_ref[...])


def _inproj(x, nw, w_main_t, w_dt_t):
    T = x.shape[0]
    tm = min(IN_TM, T)
    return pl.pallas_call(
        _inproj_body,
        grid=(T // tm, MAIN_PROJ // IN_TN),
        in_specs=[
            pl.BlockSpec((tm, D_MODEL), lambda i, j: (i, 0)),
            pl.BlockSpec((1, D_MODEL), lambda i, j: (0, 0)),
            pl.BlockSpec((IN_TN, D_MODEL), lambda i, j: (j, 0)),
            pl.BlockSpec((LANES, D_MODEL), lambda i, j: (0, 0)),
        ],
        out_specs=[
            pl.BlockSpec((tm, IN_TN), lambda i, j: (i, j)),
            pl.BlockSpec((tm, LANES), lambda i, j: (i, 0)),
        ],
        out_shape=[
            jax.ShapeDtypeStruct((T, MAIN_PROJ), F32),
            jax.ShapeDtypeStruct((T, LANES), F32),
        ],
        scratch_shapes=[pltpu.VMEM((tm, D_MODEL), BF16)],
        compiler_params=_cparams(("parallel", "arbitrary")),
        name="in_proj",
    )(x, nw, w_main_t, w_dt_t)


def _rope_body(pos_ref, inv_ref, cos_ref, sin_ref):
    ang = pos_ref[...] * inv_ref[...]
    cos_ref[...] = jnp.cos(ang)
    sin_ref[...] = jnp.sin(ang)


def _rope_tables(pos, inv):
    T = pos.shape[0]
    tm = min(1024, T)
    half = inv.shape[1]
    return pl.pallas_call(
        _rope_body,
        grid=(T // tm,),
        in_specs=[pl.BlockSpec((tm, 1), lambda i: (i, 0)),
                  pl.BlockSpec((1, half), lambda i: (0, 0))],
        out_specs=[pl.BlockSpec((tm, half), lambda i: (i, 0))] * 2,
        out_shape=[jax.ShapeDtypeStruct((T, half), F32)] * 2,
        compiler_params=_cparams(("parallel",)),
        name="rope_tables",
    )(pos, inv)


RET_CB = 2


def _retention_body(cd_ref, q_ref, k_ref, v_ref, g_ref, cos_ref, sin_ref,
                    dec_ref, wq_ref, ws_ref, nw_ref, o_ref, st_ref, *, cb):
    @pl.when(pl.program_id(0) == 0)
    def _():
        st_ref[...] = jnp.zeros_like(st_ref)

    hd = RET_HEAD_DIM
    half = hd // 2
    for i in range(cb):
        sl = slice(i * CHUNK, (i + 1) * CHUNK)
        cos = cos_ref[sl, :]
        sin = sin_ref[sl, :]

        def rope(x):
            x1, x2 = x[:, :half], x[:, half:]
            return jnp.concatenate([x1 * cos - x2 * sin, x2 * cos + x1 * sin], axis=-1)

        for h in range(RET_HEADS):
            hs = slice(h * hd, (h + 1) * hd)
            q = rope(q_ref[sl, hs])
            k = rope(k_ref[sl, hs]) * (hd ** -0.5)
            v = v_ref[sl, hs]
            qb = q.astype(BF16)
            kb = k.astype(BF16)
            scores = _dot_nt(qb, kb) * dec_ref[h]
            y = _dot(scores.astype(BF16), v.astype(BF16))
            st = st_ref[h]
            y = y + _dot(qb, st.astype(BF16)) * wq_ref[h]
            vw = (v * ws_ref[h]).astype(BF16)
            new = _dot(k.T.astype(BF16), vw)
            st_ref[h] = st * cd_ref[h] + new
            mu = jnp.mean(y, axis=-1, keepdims=True)
            d = y - mu
            var = jnp.mean(d * d, axis=-1, keepdims=True)
            yn = d * lax.rsqrt(var + EPS)
            o_ref[sl, hs] = (yn * nw_ref[:, hs] * _silu(g_ref[sl, hs])).astype(BF16)


def _retention(proj, cos, sin, ret_nw, consts):
    T = proj.shape[0]
    cb = min(RET_CB, T // CHUNK)
    rows = cb * CHUNK
    chunk_decay, decay_intra, w_query, w_state = consts
    hd = RET_HEAD_DIM
    half = hd // 2
    full = lambda shape: pl.BlockSpec(shape, lambda c, cd: (0,) * len(shape))
    col = lambda j: pl.BlockSpec((rows, RET_WIDTH), lambda c, cd: (c, j))

    grid_spec = pltpu.PrefetchScalarGridSpec(
        num_scalar_prefetch=1,
        grid=(T // rows,),
        in_specs=[
            col(0), col(1), col(2), col(3),
            pl.BlockSpec((rows, half), lambda c, cd: (c, 0)),
            pl.BlockSpec((rows, half), lambda c, cd: (c, 0)),
            full((RET_HEADS, CHUNK, CHUNK)),
            full((RET_HEADS, CHUNK, hd)),
            full((RET_HEADS, CHUNK, hd)),
            full((1, RET_WIDTH)),
        ],
        out_specs=pl.BlockSpec((rows, RET_WIDTH), lambda c, cd: (c, 0)),
        scratch_shapes=[pltpu.VMEM((RET_HEADS, hd, hd), F32)],
    )
    return pl.pallas_call(
        functools.partial(_retention_body, cb=cb),
        grid_spec=grid_spec,
        out_shape=jax.ShapeDtypeStruct((T, RET_WIDTH), BF16),
        compiler_params=_cparams(("arbitrary",)),
        name="retention",
    )(chunk_decay, proj, proj, proj, proj, cos, sin, decay_intra, w_query, w_state, ret_nw)


def _retention_consts():
    H, L = RET_HEADS, CHUNK
    log_gamma = jnp.log1p(-(2.0 ** (-5.0 - jnp.arange(H, dtype=F32))))
    idx = jnp.arange(L, dtype=F32)
    diff = idx[:, None] - idx[None, :]
    causal = diff >= 0
    decay_intra = jnp.where(causal[None], jnp.exp(jnp.where(causal, diff, 0.0)[None] * log_gamma[:, None, None]), 0.0)
    w_state = jnp.exp((L - 1.0 - idx)[None, :] * log_gamma[:, None])
    w_query = jnp.exp((idx + 1.0)[None, :] * log_gamma[:, None])
    chunk_decay = jnp.exp(L * log_gamma)
    bc = lambda w: jnp.broadcast_to(w[:, :, None], (H, L, RET_HEAD_DIM))
    return chunk_decay, decay_intra, bc(w_query), bc(w_state)


SSD_CB = 2
CONV_PAD = 8
XBC_BLK = 512


def _ssd_body(x0_ref, x1_ref, x2_ref, z_ref, dt_ref, cw_ref, cbias_ref, dtb_ref,
              alog_ref, dskip_ref, nw_ref, tri_ref, exp_ref, o_ref, xpad_ref, st_ref, *, cb):
    @pl.when(pl.program_id(0) == 0)
    def _():
        xpad_ref[0:CONV_PAD, :] = jnp.zeros((CONV_PAD, SSM_CONV_DIM), F32)
        st_ref[...] = jnp.zeros_like(st_ref)

    L = CHUNK
    gw = SSM_WIDTH // SSM_GROUPS
    rows_i = lax.broadcasted_iota(I32, (L, L), 0)
    cols_i = lax.broadcasted_iota(I32, (L, L), 1)
    causal = rows_i >= cols_i
    lo_lane = lax.broadcasted_iota(I32, (L, LANES), 1) < SSM_HEAD_DIM
    tri = tri_ref[...]
    expand = exp_ref[...]
    a_neg = -jnp.exp(alog_ref[...])

    for i in range(cb):
        sl = slice(i * L, (i + 1) * L)
        for b, xr in enumerate((x0_ref, x1_ref, x2_ref)):
            xpad_ref[CONV_PAD:CONV_PAD + L, b * XBC_BLK:(b + 1) * XBC_BLK] = xr[sl, :]
        u_parts = []
        for b in range(SSM_CONV_DIM // XBC_BLK):
            cs = slice(b * XBC_BLK, (b + 1) * XBC_BLK)
            acc = cbias_ref[:, cs]
            for t in range(SSM_CONV):
                r0 = CONV_PAD - (SSM_CONV - 1) + t
                acc = acc + xpad_ref[r0:r0 + L, cs] * cw_ref[t:t + 1, cs]
            u_parts.append(_silu(acc))
        xpad_ref[0:CONV_PAD, :] = xpad_ref[L:L + CONV_PAD, :]
        xs = jnp.concatenate(u_parts[:2], axis=-1)
        bm = u_parts[2][:, :SSM_GROUPS * SSM_STATE]
        cm = u_parts[2][:, SSM_GROUPS * SSM_STATE:]

        dt_in = dt_ref[sl, :] + dtb_ref[...]
        dt = jnp.maximum(dt_in, 0.0) + jnp.log1p(jnp.exp(-jnp.abs(dt_in)))
        a = dt * a_neg
        a1, a2, a3 = _split3(a)
        a_cs = _dot(tri, a1) + _dot(tri, a2) + _dot(tri, a3)
        d1, d2, d3 = _split3(dt)
        dt_e = _dot(d1, expand) + _dot(d2, expand) + _dot(d3, expand)
        c1, c2, c3 = _split3(a_cs)
        acs_e = _dot(c1, expand) + _dot(c2, expand) + _dot(c3, expand)
        last = acs_e[L - 1:L, :]
        to_end = jnp.exp(last - acs_e)
        from_start = jnp.exp(acs_e)
        chunk_decay = jnp.exp(last)
        x_dt = xs * dt_e
        xw = (x_dt * to_end).astype(BF16)
        acs_t = a_cs.T
        cmb = cm.astype(BF16)
        bmb = bm.astype(BF16)

        ys = []
        for g in range(SSM_GROUPS):
            ns = slice(g * SSM_STATE, (g + 1) * SSM_STATE)
            gs = slice(g * gw, (g + 1) * gw)
            cg = cmb[:, ns]
            cbg = _dot_nt(cg, bmb[:, ns])
            st = st_ref[g]
            y_off = _dot(cg, st.astype(BF16))
            new = _dot(bm[:, ns].T.astype(BF16), xw[:, gs])
            st_ref[g] = st * chunk_decay[:, gs] + new
            for jp in range(gw // LANES):
                h0 = (g * gw + jp * LANES) // SSM_HEAD_DIM
                ms = []
                for hh in (h0, h0 + 1):
                    seg = a_cs[:, hh:hh + 1] - acs_t[hh:hh + 1, :]
                    dec = jnp.exp(jnp.where(causal, seg, -jnp.inf))
                    ms.append((cbg * dec).astype(BF16))
                lhs = jnp.concatenate(ms, axis=1)
                ls = slice(g * gw + jp * LANES, g * gw + (jp + 1) * LANES)
                xp = x_dt[:, ls]
                rhs = jnp.concatenate([jnp.where(lo_lane, xp, 0.0),
                                       jnp.where(lo_lane, 0.0, xp)], axis=0).astype(BF16)
                y_diag = _dot(lhs, rhs)
                ys.append(y_diag + y_off[:, jp * LANES:(jp + 1) * LANES] * from_start[:, ls])
        y = jnp.concatenate(ys, axis=1) + xs * dskip_ref[...]
        y = y * _silu(z_ref[sl, :])
        outs = []
        for g in range(SSM_GROUPS):
            yg = y[:, g * gw:(g + 1) * gw]
            ms_ = jnp.mean(yg * yg, axis=-1, keepdims=True)
            outs.append(yg * lax.rsqrt(ms_ + EPS))
        o_ref[sl, :] = (jnp.concatenate(outs, axis=1) * nw_ref[...]).astype(BF16)


def _ssd(proj, dt_raw, conv_w, conv_b, dt_bias, a_log, d_skip_e, ssm_nw, tri, expand):
    T = proj.shape[0]
    cb = min(SSD_CB, T // CHUNK)
    rows = cb * CHUNK
    xbc0 = (4 * RET_WIDTH + SSM_WIDTH) // XBC_BLK
    zcol = 4 * RET_WIDTH // SSM_WIDTH
    full = lambda shape: pl.BlockSpec(shape, lambda c: (0,) * len(shape))
    return pl.pallas_call(
        functools.partial(_ssd_body, cb=cb),
        grid=(T // rows,),
        in_specs=[
            pl.BlockSpec((rows, XBC_BLK), lambda c: (c, xbc0)),
            pl.BlockSpec((rows, XBC_BLK), lambda c: (c, xbc0 + 1)),
            pl.BlockSpec((rows, XBC_BLK), lambda c: (c, xbc0 + 2)),
            pl.BlockSpec((rows, SSM_WIDTH), lambda c: (c, zcol)),
            pl.BlockSpec((rows, LANES), lambda c: (c, 0)),
            full((SSM_CONV, SSM_CONV_DIM)),
            full((1, SSM_CONV_DIM)),
            full((1, LANES)),
            full((1, LANES)),
            full((1, SSM_WIDTH)),
            full((1, SSM_WIDTH)),
            full((CHUNK, CHUNK)),
            full((LANES, SSM_WIDTH)),
        ],
        out_specs=pl.BlockSpec((rows, SSM_WIDTH), lambda c: (c, 0)),
        out_shape=jax.ShapeDtypeStruct((T, SSM_WIDTH), BF16),
        scratch_shapes=[
            pltpu.VMEM((CHUNK + CONV_PAD, SSM_CONV_DIM), F32),
            pltpu.VMEM((SSM_GROUPS, SSM_STATE, SSM_WIDTH // SSM_GROUPS), F32),
        ],
        compiler_params=_cparams(("arbitrary",)),
        name="ssd",
    )(proj, proj, proj, proj, dt_raw, conv_w, conv_b, dt_bias, a_log, d_skip_e, ssm_nw, tri, expand)


OUT_TM = 256
ROUTE_ROWS = 8 + N_EXPERTS


def _out_router_body(x_ref, yr_ref, ys_ref, wo_ref, nw_ref, wr_ref, br_ref,
                     h1_ref, h2t_ref, cw_ref, ids_ref):
    tm = x_ref.shape[0]
    h1 = x_ref[...] + _dot(yr_ref[...], wo_ref[0:RET_WIDTH, :]) + _dot(ys_ref[...], wo_ref[RET_WIDTH:, :])
    h1_ref[...] = h1
    ms = jnp.mean(h1 * h1, axis=-1, keepdims=True)
    h2 = h1 * lax.rsqrt(ms + EPS) * nw_ref[...]
    for s in range(Y_ROWS):
        h2t_ref[:, s, :] = h2[:, s * LANES:(s + 1) * LANES]

    logits = _dot_nt(wr_ref[...], h2.astype(BF16)) + br_ref[...]
    row = lax.broadcasted_iota(I32, (8, tm), 0)
    lg = jnp.where(row < N_GROUPS, logits[0:8], -jnp.inf)
    m = jnp.max(lg, axis=0, keepdims=True)
    p_sel = 1.0 / jnp.sum(jnp.exp(lg - m), axis=0, keepdims=True)
    g_sel = jnp.min(jnp.where(lg == m, row, 8), axis=0, keepdims=True)
    le = jnp.zeros((GROUP_EXPERTS, tm), F32)
    for g in range(N_GROUPS):
        le = jnp.where(g_sel == g, logits[8 + g * GROUP_EXPERTS:8 + (g + 1) * GROUP_EXPERTS], le)
    m2 = jnp.max(le, axis=0, keepdims=True)
    ee = jnp.exp(le - m2)
    pe = ee / jnp.sum(ee, axis=0, keepdims=True)
    v1 = jnp.max(pe, axis=0, keepdims=True)
    i1 = jnp.min(jnp.where(pe == v1, row, 8), axis=0, keepdims=True)
    pe2 = jnp.where(row == i1, -1.0, pe)
    v2 = jnp.max(pe2, axis=0, keepdims=True)
    i2 = jnp.min(jnp.where(pe2 == v2, row, 8), axis=0, keepdims=True)
    tw = v1 + v2
    c1 = v1 / tw * p_sel
    c2 = v2 / tw * p_sel
    e1 = g_sel * GROUP_EXPERTS + i1
    e2 = g_sel * GROUP_EXPERTS + i2
    ids = jnp.where(row == 0, e1, jnp.where(row == 1, e2, 0))
    for b in range(tm // SORT_BLK):
        ids_ref[b] = ids[:, b * SORT_BLK:(b + 1) * SORT_BLK]
    cw8 = jnp.where(row == 0, c1, jnp.where(row == 1, c2, 0.0))
    cw = jnp.concatenate([cw8, jnp.zeros((LANES - 8, tm), F32)], axis=0)
    cw_ref[...] = cw.T


def _out_router(x, y_ret, y_ssm, w_out, nw, wr_t, br):
    T = x.shape[0]
    tm = min(OUT_TM, T)
    full = lambda shape: pl.BlockSpec(shape, lambda i: (0,) * len(shape))
    return pl.pallas_call(
        _out_router_body,
        grid=(T // tm,),
        in_specs=[
            pl.BlockSpec((tm, D_MODEL), lambda i: (i, 0)),
            pl.BlockSpec((tm, RET_WIDTH), lambda i: (i, 0)),
            pl.BlockSpec((tm, SSM_WIDTH), lambda i: (i, 0)),
            full((D_MODEL, D_MODEL)),
            full((1, D_MODEL)),
            full((LANES, D_MODEL)),
            full((LANES, 1)),
        ],
        out_specs=[
            pl.BlockSpec((tm, D_MODEL), lambda i: (i, 0)),
            pl.BlockSpec((tm, Y_ROWS, LANES), lambda i: (i, 0, 0)),
            pl.BlockSpec((tm, LANES), lambda i: (i, 0)),
            pl.BlockSpec((tm // SORT_BLK, 8, SORT_BLK), lambda i: (i, 0, 0)),
        ],
        out_shape=[
            jax.ShapeDtypeStruct((T, D_MODEL), F32),
            jax.ShapeDtypeStruct((T, Y_ROWS, LANES), F32),
            jax.ShapeDtypeStruct((T, LANES), F32),
            jax.ShapeDtypeStruct((T // SORT_BLK, 8, SORT_BLK), I32),
        ],
        compiler_params=_cparams(("parallel",)),
        name="out_router",
    )(x, y_ret, y_ssm, w_out, nw, wr_t, br)


SORT_BLK = 256
MOE_TM = 256


def _sort_index_body(ids_ref, tri_ref, ltri_ref, dest_ref, cnt_ref, rank_ref):
    nblk = ids_ref.shape[0]
    row_e = lax.broadcasted_iota(I32, (N_EXPERTS, SORT_BLK), 0)
    row8 = lax.broadcasted_iota(I32, (8, SORT_BLK), 0)

    def onehots(b):
        ids = ids_ref[b]
        return row_e == ids[0:1], row_e == ids[1:2]

    def rank_blk(b, carry):
        oh1, oh2 = onehots(b)
        ohf = jnp.where(oh1 | oh2, 1.0, 0.0)
        incl = _dot(ohf.astype(BF16), tri_ref[...])
        base = carry + incl - 1.0
        r1 = jnp.sum(jnp.where(oh1, base, 0.0), axis=0, keepdims=True)
        r2 = jnp.sum(jnp.where(oh2, base, 0.0), axis=0, keepdims=True)
        rank_ref[b] = jnp.where(row8 == 0, r1, jnp.where(row8 == 1, r2, 0.0))
        return carry + jnp.sum(ohf, axis=1, keepdims=True)

    cnt = lax.fori_loop(0, nblk, rank_blk, jnp.zeros((N_EXPERTS, 1), F32))
    cnt_ref[...] = jnp.broadcast_to(cnt, cnt_ref.shape)
    tiles = jnp.floor((cnt + (MOE_TM - 1.0)) / MOE_TM)
    tiles_b = jnp.broadcast_to(tiles, (N_EXPERTS, LANES)).astype(BF16)
    off = _dot(ltri_ref[...], tiles_b)[:, 0:1] * MOE_TM

    def dest_blk(b, carry):
        oh1, oh2 = onehots(b)
        o1 = jnp.sum(jnp.where(oh1, off, 0.0), axis=0, keepdims=True)
        o2 = jnp.sum(jnp.where(oh2, off, 0.0), axis=0, keepdims=True)
        d = rank_ref[b] + jnp.where(row8 == 0, o1, jnp.where(row8 == 1, o2, 0.0))
        dest_ref[b] = d.astype(I32)
        return carry

    lax.fori_loop(0, nblk, dest_blk, 0)


def _sort_index(ids, tri_u, ltri):
    nblk = ids.shape[0]
    return pl.pallas_call(
        _sort_index_body,
        out_shape=[jax.ShapeDtypeStruct((nblk, 8, SORT_BLK), I32),
                   jax.ShapeDtypeStruct((N_EXPERTS, LANES), F32)],
        scratch_shapes=[pltpu.VMEM((nblk, 8, SORT_BLK), F32)],
        compiler_params=_cparams(None),
        name="sort_index",
    )(ids, tri_u, ltri)


ROW_DMA_PRIORITY = 1


def _moe_body(te_ref, tf_ref, nx_ref, nu_ref, dest_ref, lo_ref, hi_ref,
              h2t_ref, wg_ref, wu_ref, wd_ref, yt_ref,
              srow_ref, xbuf, ystage, wsg, wsu, wsd, wgb_ref, wub_ref, wdb_ref, gsem, ssem, wsem):
    i = pl.program_id(0)
    nu = nu_ref[0]
    T = h2t_ref.shape[0]

    def gather_row(tile, slot, r):
        tok = srow_ref[tile * MOE_TM + r] & (T - 1)
        return pltpu.make_async_copy(h2t_ref.at[tok], xbuf.at[slot, :, r, :], gsem.at[slot])

    def scatter_row(tile, slot, r):
        return pltpu.make_async_copy(ystage.at[slot, :, r, :], yt_ref.at[srow_ref[tile * MOE_TM + r]],
                                     ssem.at[slot])

    def gather_tile(slot):
        return pltpu.make_async_copy(h2t_ref.at[pl.ds(0, MOE_TM)], h2t_ref.at[pl.ds(0, MOE_TM)], gsem.at[slot])

    def scatter_tile(slot):
        return pltpu.make_async_copy(yt_ref.at[pl.ds(0, MOE_TM)], yt_ref.at[pl.ds(0, MOE_TM)], ssem.at[slot])

    def weight_copies(e):
        return (pltpu.make_async_copy(wg_ref.at[e], wsg, wsem.at[0]),
                pltpu.make_async_copy(wu_ref.at[e], wsu, wsem.at[1]),
                pltpu.make_async_copy(wd_ref.at[e], wsd, wsem.at[2]))

    @pl.when(i == 0)
    def _():
        for c in weight_copies(te_ref[0]):
            c.start()

        def tok(t, c):
            srow_ref[dest_ref[t]] = t
            srow_ref[dest_ref[T + t]] = T + t
            return c

        lax.fori_loop(0, T, tok, 0, unroll=8)

        def seg(e, c):
            def pad(r, c2):
                srow_ref[r] = 2 * T + (r & (MOE_TM - 1))
                return c2

            lax.fori_loop(lo_ref[e], hi_ref[e], pad, 0)
            return c

        lax.fori_loop(0, lo_ref.shape[0], seg, 0)

        ystage[...] = jnp.zeros_like(ystage)

        def spare(r, c):
            pltpu.make_async_copy(ystage.at[0, :, r, :], yt_ref.at[2 * T + r], ssem.at[0]).start()
            return c

        lax.fori_loop(0, MOE_TM, spare, 0)
        scatter_tile(0).wait()
        for r in range(MOE_TM):
            gather_row(0, 0, r).start(priority=ROW_DMA_PRIORITY)

    @pl.when(tf_ref[i] == 1)
    def _():
        for c in weight_copies(te_ref[i]):
            c.wait()
        wgb_ref[...] = wsg[...].astype(BF16)
        wub_ref[...] = wsu[...].astype(BF16)
        wdb_ref[...] = wsd[...].astype(BF16)

        @pl.when(nx_ref[i] >= 0)
        def _():
            for c in weight_copies(nx_ref[i]):
                c.start()

    slot = lax.rem(i, 2)
    other = 1 - slot

    @pl.when(i < nu)
    def _():
        gather_tile(slot).wait()

        @pl.when(i >= 1)
        def _():
            scatter_tile(slot).wait()

        nxt = jnp.minimum(i + 1, nu - 1)
        prv = jnp.maximum(i - 1, 0)
        for r in range(MOE_TM):
            gather_row(nxt, other, r).start(priority=r % 2)
            scatter_row(prv, other, r).start(priority=(r + 1) % 2)

        x = jnp.concatenate([xbuf[slot, s].astype(BF16) for s in range(Y_ROWS)], axis=1)
        a = _dot(x, wgb_ref[...])
        u = _dot(x, wub_ref[...])
        act = (_silu(a) * u).astype(BF16)
        y = _dot(act, wdb_ref[...])
        for s in range(Y_ROWS):
            ystage[slot, s] = y[:, s * LANES:(s + 1) * LANES]

    @pl.when(i == nu)
    def _():
        gather_tile(slot).wait()
        scatter_tile(slot).wait()

        def last(r, c):
            scatter_row(nu - 1, other, r).start()
            return c

        lax.fori_loop(0, MOE_TM, last, 0)
        scatter_tile(other).wait()


def _moe(tile_expert, tile_first, next_expert, n_used, dest, pad_lo, pad_hi, h2t, w_gate, w_up, w_down):
    T = h2t.shape[0]
    assert T & (T - 1) == 0 and T >= MOE_TM, "token index is recovered from the row table by masking"
    n_tiles = tile_expert.shape[0] - 1
    hbm = pl.BlockSpec(memory_space=pl.ANY)
    grid_spec = pltpu.PrefetchScalarGridSpec(
        num_scalar_prefetch=7,
        grid=(n_tiles + 1,),
        in_specs=[hbm, hbm, hbm, hbm],
        out_specs=hbm,
        scratch_shapes=[pltpu.SMEM((n_tiles * MOE_TM,), I32),
                        pltpu.VMEM((2, Y_ROWS, MOE_TM, LANES), F32),
                        pltpu.VMEM((2, Y_ROWS, MOE_TM, LANES), F32),
                        pltpu.VMEM((D_MODEL, D_EXPERT), F32),
                        pltpu.VMEM((D_MODEL, D_EXPERT), F32),
                        pltpu.VMEM((D_EXPERT, D_MODEL), F32),
                        pltpu.VMEM((D_MODEL, D_EXPERT), BF16),
                        pltpu.VMEM((D_MODEL, D_EXPERT), BF16),
                        pltpu.VMEM((D_EXPERT, D_MODEL), BF16),
                        pltpu.SemaphoreType.DMA((2,)),
                        pltpu.SemaphoreType.DMA((2,)),
                        pltpu.SemaphoreType.DMA((3,))],
    )
    return pl.pallas_call(
        _moe_body,
        grid_spec=grid_spec,
        out_shape=jax.ShapeDtypeStruct((2 * T + MOE_TM, Y_ROWS, LANES), F32),
        compiler_params=_cparams(("arbitrary",)),
        name="moe",
    )(tile_expert, tile_first, next_expert, n_used, dest, pad_lo, pad_hi, h2t, w_gate, w_up, w_down)


COMB_TM = 256


def _combine_body(h1_ref, y0_ref, y1_ref, cw_ref, nw_ref, o_ref):
    tm = h1_ref.shape[0]
    c0 = cw_ref[:, 0:1]
    c1 = cw_ref[:, 1:2]
    cols = []
    for s in range(Y_ROWS):
        moe = c0 * y0_ref[:, s, :] + c1 * y1_ref[:, s, :]
        cols.append(h1_ref[:, s * LANES:(s + 1) * LANES] + moe)
    h = jnp.concatenate(cols, axis=1)
    ms = jnp.mean(h * h, axis=-1, keepdims=True)
    o_ref[...] = h * lax.rsqrt(ms + EPS) * nw_ref[...]


def _combine(h1, y_tok, cw, nw):
    T = h1.shape[0]
    tm = min(COMB_TM, T)
    nt = T // tm
    return pl.pallas_call(
        _combine_body,
        grid=(nt,),
        in_specs=[
            pl.BlockSpec((tm, D_MODEL), lambda i: (i, 0)),
            pl.BlockSpec((tm, Y_ROWS, LANES), lambda i: (i, 0, 0)),
            pl.BlockSpec((tm, Y_ROWS, LANES), lambda i: (i + nt, 0, 0)),
            pl.BlockSpec((tm, LANES), lambda i: (i, 0)),
            pl.BlockSpec((1, D_MODEL), lambda i: (0, 0)),
        ],
        out_specs=pl.BlockSpec((tm, D_MODEL), lambda i: (i, 0)),
        out_shape=jax.ShapeDtypeStruct((T, D_MODEL), F32),
        compiler_params=_cparams(("parallel",)),
        name="combine",
    )(h1, y_tok, y_tok, cw, nw)


def _tile_plan(cnt, n_tiles):
    tiles = (cnt + (MOE_TM - 1)) // MOE_TM
    ends = jnp.cumsum(tiles)
    starts = ends - tiles
    n_used = ends[-1]
    step = jnp.arange(n_tiles + 1, dtype=I32)
    tile = jnp.minimum(step, jnp.maximum(n_used - 1, 0))
    tile_expert = jnp.sum((ends[None, :] <= tile[:, None]).astype(I32), axis=1)
    tile_first = ((step == starts[tile_expert]) & (step < n_used)).astype(I32)
    nxt_tile = ends[tile_expert]
    nxt_expert = jnp.sum((ends[None, :] <= nxt_tile[:, None]).astype(I32), axis=1)
    next_expert = jnp.where(nxt_tile < n_used, nxt_expert, -1).astype(I32)
    pad_lo = starts * MOE_TM + cnt
    pad_hi = ends * MOE_TM
    return (tile_expert, tile_first, next_expert, n_used.reshape(1).astype(I32),
            pad_lo.astype(I32), pad_hi.astype(I32))


def kernel(x, positions, norm1_w, w_in, conv_w, conv_b, dt_bias, a_log, d_skip, ret_norm_w,
           ssm_norm_w, w_out, norm2_w, w_router_group, b_router_group, w_router_expert,
           b_router_expert, w_expert_gate, w_expert_up, w_expert_down, final_norm_w):
    B, T, D = x.shape
    assert B == 1 and D == D_MODEL and T % CHUNK == 0
    xf = x.reshape(T, D)
    pad_l = lambda v: jnp.pad(v, ((0, 0), (0, LANES - v.shape[-1])))

    w_in_t = jnp.swapaxes(w_in[0], 0, 1)
    w_main = _cast_rows_bf16(w_in_t, MAIN_PROJ, IN_TN)
    w_dt = jnp.pad(w_in_t[MAIN_PROJ:], ((0, LANES - (w_in_t.shape[0] - MAIN_PROJ)), (0, 0))).astype(BF16)
    half = RET_HEAD_DIM // 2
    inv = (ROPE_THETA ** (-jnp.arange(half, dtype=F32) / half)).reshape(1, half)
    tri = (jnp.arange(CHUNK)[:, None] >= jnp.arange(CHUNK)[None, :]).astype(BF16)
    expand = (jnp.arange(LANES)[:, None] == jnp.arange(SSM_WIDTH)[None, :] // SSM_HEAD_DIM).astype(BF16)
    d_skip_e = jnp.repeat(d_skip[0], SSM_HEAD_DIM).reshape(1, SSM_WIDTH)
    wr_t = jnp.zeros((LANES, D), F32)
    wr_t = wr_t.at[0:N_GROUPS].set(w_router_group[0].T).at[8:8 + N_EXPERTS].set(w_router_expert[0].T)
    br = jnp.zeros((LANES,), F32)
    br = br.at[0:N_GROUPS].set(b_router_group[0]).at[8:8 + N_EXPERTS].set(b_router_expert[0])
    tri_u = (jnp.arange(SORT_BLK)[:, None] <= jnp.arange(SORT_BLK)[None, :]).astype(BF16)
    ltri = (jnp.arange(N_EXPERTS)[:, None] > jnp.arange(N_EXPERTS)[None, :]).astype(BF16)

    proj, dt_raw = _inproj(xf, norm1_w[0].reshape(1, D), w_main, w_dt)
    cos, sin = _rope_tables(positions.reshape(T, 1).astype(F32), inv)
    y_ret = _retention(proj, cos, sin, ret_norm_w[0].reshape(1, RET_WIDTH), _retention_consts())
    y_ssm = _ssd(proj, dt_raw, conv_w[0], conv_b[0].reshape(1, -1), pad_l(dt_bias[0].reshape(1, -1)),
                 pad_l(a_log[0].reshape(1, -1)), d_skip_e, ssm_norm_w[0].reshape(1, -1), tri, expand)
    h1, h2t, cw, ids = _out_router(xf, y_ret, y_ssm, w_out[0].astype(BF16), norm2_w[0].reshape(1, D),
                                   wr_t.astype(BF16), br.reshape(LANES, 1))

    dest_blk, cnt = _sort_index(ids, tri_u, ltri)
    dest = dest_blk[:, 0:2, :].transpose(1, 0, 2).reshape(2 * T)
    n_tiles = (2 * T) // MOE_TM + N_EXPERTS
    tile_expert, tile_first, next_expert, n_used, pad_lo, pad_hi = _tile_plan(cnt[:, 0].astype(I32), n_tiles)
    y_tok = _moe(tile_expert, tile_first, next_expert, n_used, dest, pad_lo, pad_hi, h2t,
                 w_expert_gate[0], w_expert_up[0], w_expert_down[0])
    out = _combine(h1, y_tok, cw, final_norm_w.reshape(1, D))
    return out.reshape(B, T, D)
```

```python
import functools

import jax
import jax.numpy as jnp
from jax import lax
from jax.experimental import pallas as pl
from jax.experimental.pallas import tpu as pltpu

F32 = jnp.float32
BF16 = jnp.bfloat16
I32 = jnp.int32

D_MODEL = 2048
EPS = 1e-6
CHUNK = 128
RET_HEADS = 4
RET_HEAD_DIM = 256
RET_WIDTH = RET_HEADS * RET_HEAD_DIM
ROPE_THETA = 10000.0
SSM_WIDTH = 1024
SSM_HEAD_DIM = 64
SSM_HEADS = SSM_WIDTH // SSM_HEAD_DIM
SSM_GROUPS = 2
SSM_STATE = 128
SSM_CONV = 4
SSM_CONV_DIM = SSM_WIDTH + 2 * SSM_GROUPS * SSM_STATE
MAIN_PROJ = 4 * RET_WIDTH + SSM_WIDTH + SSM_CONV_DIM
N_GROUPS = 4
GROUP_EXPERTS = 8
N_EXPERTS = N_GROUPS * GROUP_EXPERTS
D_EXPERT = 512
LANES = 128
Y_ROWS = D_MODEL // LANES

VMEM_LIMIT = 56 * 1024 * 1024


def _cparams(sem, vmem=VMEM_LIMIT):
    return pltpu.CompilerParams(dimension_semantics=sem, vmem_limit_bytes=vmem)


def _silu(x):
    return x * (1.0 / (1.0 + jnp.exp(-x)))


def _dot(a, b):
    return jnp.dot(a, b, preferred_element_type=F32)


def _dot_nt(a, b):
    return lax.dot_general(a, b, (((1,), (1,)), ((), ())), preferred_element_type=F32)


def _split3(a):
    a1 = a.astype(BF16)
    r1 = a - a1.astype(F32)
    a2 = r1.astype(BF16)
    a3 = (r1 - a2.astype(F32)).astype(BF16)
    return a1, a2, a3


def _cast_body(w_ref, o_ref):
    o_ref[...] = w_ref[...].astype(BF16)


def _cast_rows_bf16(w, n_rows, tr):
    cols = w.shape[1]
    return pl.pallas_call(
        _cast_body,
        grid=(n_rows // tr,),
        in_specs=[pl.BlockSpec((tr, cols), lambda j: (j, 0))],
        out_specs=pl.BlockSpec((tr, cols), lambda j: (j, 0)),
        out_shape=jax.ShapeDtypeStruct((n_rows, cols), BF16),
        compiler_params=_cparams(("parallel",)),
        name="cast_bf16",
    )(w)


CAST_ROWS = 512


def _rope_body(pos_ref, inv_ref, cos_ref, sin_ref):
    ang = pos_ref[...] * inv_ref[...]
    cos_ref[...] = jnp.cos(ang)
    sin_ref[...] = jnp.sin(ang)


def _rope_tables(pos, inv):
    T = pos.shape[0]
    tm = min(1024, T)
    half = inv.shape[1]
    return pl.pallas_call(
        _rope_body,
        grid=(T // tm,),
        in_specs=[pl.BlockSpec((tm, 1), lambda i: (i, 0)),
                  pl.BlockSpec((1, half), lambda i: (0, 0))],
        out_specs=[pl.BlockSpec((tm, half), lambda i: (i, 0))] * 2,
        out_shape=[jax.ShapeDtypeStruct((T, half), F32)] * 2,
        compiler_params=_cparams(("parallel",)),
        name="rope_tables",
    )(pos, inv)


def _retention_chunks(cd_ref, q_ref, k_ref, v_ref, g_ref, cos_ref, sin_ref,
                      dec_ref, wq_ref, ws_ref, nw_ref, o_ref, st_ref, cb):
    hd = RET_HEAD_DIM
    half = hd // 2
    for i in range(cb):
        sl = slice(i * CHUNK, (i + 1) * CHUNK)
        cos = cos_ref[sl, :]
        sin = sin_ref[sl, :]

        def rope(x):
            x1, x2 = x[:, :half], x[:, half:]
            return jnp.concatenate([x1 * cos - x2 * sin, x2 * cos + x1 * sin], axis=-1)

        for h in range(RET_HEADS):
            hs = slice(h * hd, (h + 1) * hd)
            q = rope(q_ref[sl, hs])
            k = rope(k_ref[sl, hs]) * (hd ** -0.5)
            v = v_ref[sl, hs]
            qb = q.astype(BF16)
            kb = k.astype(BF16)
            scores = _dot_nt(qb, kb) * dec_ref[h]
            y = _dot(scores.astype(BF16), v.astype(BF16))
            st = st_ref[h]
            y = y + _dot(qb, st.astype(BF16)) * wq_ref[h]
            vw = (v * ws_ref[h]).astype(BF16)
            new = _dot(k.T.astype(BF16), vw)
            st_ref[h] = st * cd_ref[h] + new
            mu = jnp.mean(y, axis=-1, keepdims=True)
            d = y - mu
            var = jnp.mean(d * d, axis=-1, keepdims=True)
            yn = d * lax.rsqrt(var + EPS)
            o_ref[sl, hs] = (yn * nw_ref[:, hs] * _silu(g_ref[sl, hs])).astype(BF16)


def _retention_consts():
    H, L = RET_HEADS, CHUNK
    log_gamma = jnp.log1p(-(2.0 ** (-5.0 - jnp.arange(H, dtype=F32))))
    idx = jnp.arange(L, dtype=F32)
    diff = idx[:, None] - idx[None, :]
    causal = diff >= 0
    decay_intra = jnp.where(causal[None], jnp.exp(jnp.where(causal, diff, 0.0)[None] * log_gamma[:, None, None]), 0.0)
    w_state = jnp.exp((L - 1.0 - idx)[None, :] * log_gamma[:, None])
    w_query = jnp.exp((idx + 1.0)[None, :] * log_gamma[:, None])
    chunk_decay = jnp.exp(L * log_gamma)
    bc = lambda w: jnp.broadcast_to(w[:, :, None], (H, L, RET_HEAD_DIM))
    return chunk_decay, decay_intra, bc(w_query), bc(w_state)


CONV_PAD = 8
XBC_BLK = 512


def _ssd_chunks(xbc_ref, z_ref, dt_ref, cw_ref, cbias_ref, dtb_ref,
                alog_ref, dskip_ref, nw_ref, tri_ref, exp_ref, o_ref, xpad_ref, st_ref, cb):
    L = CHUNK
    gw = SSM_WIDTH // SSM_GROUPS
    rows_i = lax.broadcasted_iota(I32, (L, L), 0)
    cols_i = lax.broadcasted_iota(I32, (L, L), 1)
    causal = rows_i >= cols_i
    lo_lane = lax.broadcasted_iota(I32, (L, LANES), 1) < SSM_HEAD_DIM
    tri = tri_ref[...]
    expand = exp_ref[...]
    a_neg = -jnp.exp(alog_ref[...])

    for i in range(cb):
        sl = slice(i * L, (i + 1) * L)
        xpad_ref[CONV_PAD:CONV_PAD + L, :] = xbc_ref[sl, :]
        u_parts = []
        for b in range(SSM_CONV_DIM // XBC_BLK):
            cs = slice(b * XBC_BLK, (b + 1) * XBC_BLK)
            acc = cbias_ref[:, cs]
            for t in range(SSM_CONV):
                r0 = CONV_PAD - (SSM_CONV - 1) + t
                acc = acc + xpad_ref[r0:r0 + L, cs] * cw_ref[t:t + 1, cs]
            u_parts.append(_silu(acc))
        xpad_ref[0:CONV_PAD, :] = xpad_ref[L:L + CONV_PAD, :]
        xs = jnp.concatenate(u_parts[:2], axis=-1)
        bm = u_parts[2][:, :SSM_GROUPS * SSM_STATE]
        cm = u_parts[2][:, SSM_GROUPS * SSM_STATE:]

        dt_in = dt_ref[sl, :] + dtb_ref[...]
        dt = jnp.maximum(dt_in, 0.0) + jnp.log1p(jnp.exp(-jnp.abs(dt_in)))
        a = dt * a_neg
        a1, a2, a3 = _split3(a)
        a_cs = _dot(tri, a1) + _dot(tri, a2) + _dot(tri, a3)
        d1, d2, d3 = _split3(dt)
        dt_e = _dot(d1, expand) + _dot(d2, expand) + _dot(d3, expand)
        c1, c2, c3 = _split3(a_cs)
        acs_e = _dot(c1, expand) + _dot(c2, expand) + _dot(c3, expand)
        last = acs_e[L - 1:L, :]
        to_end = jnp.exp(last - acs_e)
        from_start = jnp.exp(acs_e)
        chunk_decay = jnp.exp(last)
        x_dt = xs * dt_e
        xw = (x_dt * to_end).astype(BF16)
        acs_t = a_cs.T
        cmb = cm.astype(BF16)
        bmb = bm.astype(BF16)

        ys = []
        for g in range(SSM_GROUPS):
            ns = slice(g * SSM_STATE, (g + 1) * SSM_STATE)
            gs = slice(g * gw, (g + 1) * gw)
            cg = cmb[:, ns]
            cbg = _dot_nt(cg, bmb[:, ns])
            st = st_ref[g]
            y_off = _dot(cg, st.astype(BF16))
            new = _dot(bm[:, ns].T.astype(BF16), xw[:, gs])
            st_ref[g] = st * chunk_decay[:, gs] + new
            for jp in range(gw // LANES):
                h0 = (g * gw + jp * LANES) // SSM_HEAD_DIM
                ms = []
                for hh in (h0, h0 + 1):
                    seg = a_cs[:, hh:hh + 1] - acs_t[hh:hh + 1, :]
                    dec = jnp.exp(jnp.where(causal, seg, -jnp.inf))
                    ms.append((cbg * dec).astype(BF16))
                lhs = jnp.concatenate(ms, axis=1)
                ls = slice(g * gw + jp * LANES, g * gw + (jp + 1) * LANES)
                xp = x_dt[:, ls]
                rhs = jnp.concatenate([jnp.where(lo_lane, xp, 0.0),
                                       jnp.where(lo_lane, 0.0, xp)], axis=0).astype(BF16)
                y_diag = _dot(lhs, rhs)
                ys.append(y_diag + y_off[:, jp * LANES:(jp + 1) * LANES] * from_start[:, ls])
        y = jnp.concatenate(ys, axis=1) + xs * dskip_ref[...]
        y = y * _silu(z_ref[sl, :])
        outs = []
        for g in range(SSM_GROUPS):
            yg = y[:, g * gw:(g + 1) * gw]
            ms_ = jnp.mean(yg * yg, axis=-1, keepdims=True)
            outs.append(yg * lax.rsqrt(ms_ + EPS))
        o_ref[sl, :] = (jnp.concatenate(outs, axis=1) * nw_ref[...]).astype(BF16)


MIX_CB = 2
MIX_VMEM = 60 * 1024 * 1024
COL_Z = 4 * RET_WIDTH
COL_XBC = COL_Z + SSM_WIDTH


def _mixer_body(cd_ref, x_ref, n1w_ref, wt_hbm, wdt_ref, cos_ref, sin_ref, dec_ref, wq_ref, ws_ref, rnw_ref,
                cw_ref, cbias_ref, dtb_ref, alog_ref, dskip_ref, snw_ref, tri_ref, exp_ref,
                yret_ref, yssm_ref,
                w_ref, hn_ref, pq, pk, pv, pg, pz, pxbc, pdt, st_ret, xpad_ref, st_ssd, sem, *, cb):
    @pl.when(pl.program_id(0) == 0)
    def _():
        cp = pltpu.make_async_copy(wt_hbm, w_ref, sem)
        cp.start()
        st_ret[...] = jnp.zeros_like(st_ret)
        xpad_ref[0:CONV_PAD, :] = jnp.zeros((CONV_PAD, SSM_CONV_DIM), F32)
        st_ssd[...] = jnp.zeros_like(st_ssd)
        cp.wait()

    x = x_ref[...]
    ms = jnp.mean(x * x, axis=-1, keepdims=True)
    hn_ref[...] = (x * lax.rsqrt(ms + EPS) * n1w_ref[...]).astype(BF16)

    def project(dst, col0):
        width = dst.shape[1]
        dst[...] = _dot_nt(hn_ref[...], w_ref[col0:col0 + width, :])

    project(pxbc, COL_XBC)
    project(pz, COL_Z)
    pdt[...] = _dot_nt(hn_ref[...], wdt_ref[...])
    _ssd_chunks(pxbc, pz, pdt, cw_ref, cbias_ref, dtb_ref, alog_ref, dskip_ref, snw_ref, tri_ref, exp_ref,
                yssm_ref, xpad_ref, st_ssd, cb)
    for j, dst in enumerate((pq, pk, pv, pg)):
        project(dst, j * RET_WIDTH)
    _retention_chunks(cd_ref, pq, pk, pv, pg, cos_ref, sin_ref, dec_ref, wq_ref, ws_ref, rnw_ref,
                      yret_ref, st_ret, cb)


def _mixer(x, n1w, w_main_t, w_dt_t, cos, sin, ret_nw, ret_consts,
           conv_w, conv_b, dt_bias, a_log, d_skip_e, ssm_nw, tri, expand):
    T = x.shape[0]
    cb = min(MIX_CB, T // CHUNK)
    rows = cb * CHUNK
    chunk_decay, decay_intra, w_query, w_state = ret_consts
    hd = RET_HEAD_DIM
    half = hd // 2
    full = lambda shape: pl.BlockSpec(shape, lambda c, cd: (0,) * len(shape))
    rowblk = lambda width: pl.BlockSpec((rows, width), lambda c, cd: (c, 0))
    scratch = lambda width: pltpu.VMEM((rows, width), F32)
    grid_spec = pltpu.PrefetchScalarGridSpec(
        num_scalar_prefetch=1,
        grid=(T // rows,),
        in_specs=[
            rowblk(D_MODEL),
            full((1, D_MODEL)),
            pl.BlockSpec(memory_space=pl.ANY),
            full((LANES, D_MODEL)),
            rowblk(half), rowblk(half),
            full((RET_HEADS, CHUNK, CHUNK)),
            full((RET_HEADS, CHUNK, hd)),
            full((RET_HEADS, CHUNK, hd)),
            full((1, RET_WIDTH)),
            full((SSM_CONV, SSM_CONV_DIM)),
            full((1, SSM_CONV_DIM)),
            full((1, LANES)),
            full((1, LANES)),
            full((1, SSM_WIDTH)),
            full((1, SSM_WIDTH)),
            full((CHUNK, CHUNK)),
            full((LANES, SSM_WIDTH)),
        ],
        out_specs=[rowblk(RET_WIDTH), rowblk(SSM_WIDTH)],
        scratch_shapes=[
            pltpu.VMEM((MAIN_PROJ, D_MODEL), BF16),
            pltpu.VMEM((rows, D_MODEL), BF16),
            scratch(RET_WIDTH), scratch(RET_WIDTH), scratch(RET_WIDTH), scratch(RET_WIDTH),
            scratch(SSM_WIDTH), scratch(SSM_CONV_DIM), scratch(LANES),
            pltpu.VMEM((RET_HEADS, hd, hd), F32),
            pltpu.VMEM((CHUNK + CONV_PAD, SSM_CONV_DIM), F32),
            pltpu.VMEM((SSM_GROUPS, SSM_STATE, SSM_WIDTH // SSM_GROUPS), F32),
            pltpu.SemaphoreType.DMA(()),
        ],
    )
    return pl.pallas_call(
        functools.partial(_mixer_body, cb=cb),
        grid_spec=grid_spec,
        out_shape=[jax.ShapeDtypeStruct((T, RET_WIDTH), BF16),
                   jax.ShapeDtypeStruct((T, SSM_WIDTH), BF16)],
        compiler_params=_cparams(("arbitrary",), MIX_VMEM),
        name="mixer",
    )(chunk_decay, x, n1w, w_main_t, w_dt_t, cos, sin, decay_intra, w_query, w_state, ret_nw,
      conv_w, conv_b, dt_bias, a_log, d_skip_e, ssm_nw, tri, expand)


OUT_TM = 256
ROUTE_ROWS = 8 + N_EXPERTS


def _out_router_body(x_ref, yr_ref, ys_ref, wo_ref, nw_ref, wr_ref, br_ref,
                     h1_ref, h2t_ref, cw_ref, ids_ref):
    tm = x_ref.shape[0]
    h1 = x_ref[...] + _dot(yr_ref[...], wo_ref[0:RET_WIDTH, :]) + _dot(ys_ref[...], wo_ref[RET_WIDTH:, :])
    h1_ref[...] = h1
    ms = jnp.mean(h1 * h1, axis=-1, keepdims=True)
    h2 = h1 * lax.rsqrt(ms + EPS) * nw_ref[...]
    for s in range(Y_ROWS):
        h2t_ref[:, s, :] = h2[:, s * LANES:(s + 1) * LANES]

    logits = _dot_nt(wr_ref[...], h2.astype(BF16)) + br_ref[...]
    row = lax.broadcasted_iota(I32, (8, tm), 0)
    lg = jnp.where(row < N_GROUPS, logits[0:8], -jnp.inf)
    m = jnp.max(lg, axis=0, keepdims=True)
    p_sel = 1.0 / jnp.sum(jnp.exp(lg - m), axis=0, keepdims=True)
    g_sel = jnp.min(jnp.where(lg == m, row, 8), axis=0, keepdims=True)
    le = jnp.zeros((GROUP_EXPERTS, tm), F32)
    for g in range(N_GROUPS):
        le = jnp.where(g_sel == g, logits[8 + g * GROUP_EXPERTS:8 + (g + 1) * GROUP_EXPERTS], le)
    m2 = jnp.max(le, axis=0, keepdims=True)
    ee = jnp.exp(le - m2)
    pe = ee / jnp.sum(ee, axis=0, keepdims=True)
    v1 = jnp.max(pe, axis=0, keepdims=True)
    i1 = jnp.min(jnp.where(pe == v1, row, 8), axis=0, keepdims=True)
    pe2 = jnp.where(row == i1, -1.0, pe)
    v2 = jnp.max(pe2, axis=0, keepdims=True)
    i2 = jnp.min(jnp.where(pe2 == v2, row, 8), axis=0, keepdims=True)
    tw = v1 + v2
    c1 = v1 / tw * p_sel
    c2 = v2 / tw * p_sel
    e1 = g_sel * GROUP_EXPERTS + i1
    e2 = g_sel * GROUP_EXPERTS + i2
    ids = jnp.where(row == 0, e1, jnp.where(row == 1, e2, 0))
    for b in range(tm // SORT_BLK):
        ids_ref[b] = ids[:, b * SORT_BLK:(b + 1) * SORT_BLK]
    cw8 = jnp.where(row == 0, c1, jnp.where(row == 1, c2, 0.0))
    cw = jnp.concatenate([cw8, jnp.zeros((LANES - 8, tm), F32)], axis=0)
    cw_ref[...] = cw.T


def _out_router(x, y_ret, y_ssm, w_out, nw, wr_t, br):
    T = x.shape[0]
    tm = min(OUT_TM, T)
    full = lambda shape: pl.BlockSpec(shape, lambda i: (0,) * len(shape))
    return pl.pallas_call(
        _out_router_body,
        grid=(T // tm,),
        in_specs=[
            pl.BlockSpec((tm, D_MODEL), lambda i: (i, 0)),
            pl.BlockSpec((tm, RET_WIDTH), lambda i: (i, 0)),
            pl.BlockSpec((tm, SSM_WIDTH), lambda i: (i, 0)),
            full((D_MODEL, D_MODEL)),
            full((1, D_MODEL)),
            full((LANES, D_MODEL)),
            full((LANES, 1)),
        ],
        out_specs=[
            pl.BlockSpec((tm, D_MODEL), lambda i: (i, 0)),
            pl.BlockSpec((tm, Y_ROWS, LANES), lambda i: (i, 0, 0)),
            pl.BlockSpec((tm, LANES), lambda i: (i, 0)),
            pl.BlockSpec((tm // SORT_BLK, 8, SORT_BLK), lambda i: (i, 0, 0)),
        ],
        out_shape=[
            jax.ShapeDtypeStruct((T, D_MODEL), F32),
            jax.ShapeDtypeStruct((T, Y_ROWS, LANES), F32),
            jax.ShapeDtypeStruct((T, LANES), F32),
            jax.ShapeDtypeStruct((T // SORT_BLK, 8, SORT_BLK), I32),
        ],
        compiler_params=_cparams(("parallel",)),
        name="out_router",
    )(x, y_ret, y_ssm, w_out, nw, wr_t, br)


SORT_BLK = 256
MOE_TM = 256


def _sort_index_body(ids_ref, tri_ref, ltri_ref, dest_ref, cnt_ref, rank_ref):
    nblk = ids_ref.shape[0]
    row_e = lax.broadcasted_iota(I32, (N_EXPERTS, SORT_BLK), 0)
    row8 = lax.broadcasted_iota(I32, (8, SORT_BLK), 0)

    def onehots(b):
        ids = ids_ref[b]
        return row_e == ids[0:1], row_e == ids[1:2]

    def rank_blk(b, carry):
        oh1, oh2 = onehots(b)
        ohf = jnp.where(oh1 | oh2, 1.0, 0.0)
        incl = _dot(ohf.astype(BF16), tri_ref[...])
        base = carry + incl - 1.0
        r1 = jnp.sum(jnp.where(oh1, base, 0.0), axis=0, keepdims=True)
        r2 = jnp.sum(jnp.where(oh2, base, 0.0), axis=0, keepdims=True)
        rank_ref[b] = jnp.where(row8 == 0, r1, jnp.where(row8 == 1, r2, 0.0))
        return carry + jnp.sum(ohf, axis=1, keepdims=True)

    cnt = lax.fori_loop(0, nblk, rank_blk, jnp.zeros((N_EXPERTS, 1), F32))
    cnt_ref[...] = jnp.broadcast_to(cnt, cnt_ref.shape)
    tiles = jnp.floor((cnt + (MOE_TM - 1.0)) / MOE_TM)
    tiles_b = jnp.broadcast_to(tiles, (N_EXPERTS, LANES)).astype(BF16)
    off = _dot(ltri_ref[...], tiles_b)[:, 0:1] * MOE_TM

    def dest_blk(b, carry):
        oh1, oh2 = onehots(b)
        o1 = jnp.sum(jnp.where(oh1, off, 0.0), axis=0, keepdims=True)
        o2 = jnp.sum(jnp.where(oh2, off, 0.0), axis=0, keepdims=True)
        d = rank_ref[b] + jnp.where(row8 == 0, o1, jnp.where(row8 == 1, o2, 0.0))
        dest_ref[b] = d.astype(I32)
        return carry

    lax.fori_loop(0, nblk, dest_blk, 0)


def _sort_index(ids, tri_u, ltri):
    nblk = ids.shape[0]
    return pl.pallas_call(
        _sort_index_body,
        out_shape=[jax.ShapeDtypeStruct((nblk, 8, SORT_BLK), I32),
                   jax.ShapeDtypeStruct((N_EXPERTS, LANES), F32)],
        scratch_shapes=[pltpu.VMEM((nblk, 8, SORT_BLK), F32)],
        compiler_params=_cparams(None),
        name="sort_index",
    )(ids, tri_u, ltri)


ROW_DMA_PRIORITY = 1


def _moe_body(te_ref, tf_ref, nx_ref, nu_ref, dest_ref, lo_ref, hi_ref,
              h2t_ref, wg_ref, wu_ref, wd_ref, yt_ref,
              srow_ref, xbuf, ystage, wsg, wsu, wsd, wgb_ref, wub_ref, wdb_ref, gsem, ssem, wsem):
    i = pl.program_id(0)
    nu = nu_ref[0]
    T = h2t_ref.shape[0]

    def gather_row(tile, slot, r):
        tok = srow_ref[tile * MOE_TM + r] & (T - 1)
        return pltpu.make_async_copy(h2t_ref.at[tok], xbuf.at[slot, :, r, :], gsem.at[slot])

    def scatter_row(tile, slot, r):
        return pltpu.make_async_copy(ystage.at[slot, :, r, :], yt_ref.at[srow_ref[tile * MOE_TM + r]],
                                     ssem.at[slot])

    def gather_tile(slot):
        return pltpu.make_async_copy(h2t_ref.at[pl.ds(0, MOE_TM)], h2t_ref.at[pl.ds(0, MOE_TM)], gsem.at[slot])

    def scatter_tile(slot):
        return pltpu.make_async_copy(yt_ref.at[pl.ds(0, MOE_TM)], yt_ref.at[pl.ds(0, MOE_TM)], ssem.at[slot])

    def weight_copies(e):
        return (pltpu.make_async_copy(wg_ref.at[e], wsg, wsem.at[0]),
                pltpu.make_async_copy(wu_ref.at[e], wsu, wsem.at[1]),
                pltpu.make_async_copy(wd_ref.at[e], wsd, wsem.at[2]))

    @pl.when(i == 0)
    def _():
        for c in weight_copies(te_ref[0]):
            c.start()

        def tok(t, c):
            srow_ref[dest_ref[t]] = t
            srow_ref[dest_ref[T + t]] = T + t
            return c

        lax.fori_loop(0, T, tok, 0, unroll=8)

        def seg(e, c):
            def pad(r, c2):
                srow_ref[r] = 2 * T + (r & (MOE_TM - 1))
                return c2

            lax.fori_loop(lo_ref[e], hi_ref[e], pad, 0)
            return c

        lax.fori_loop(0, lo_ref.shape[0], seg, 0)

        ystage[...] = jnp.zeros_like(ystage)

        def spare(r, c):
            pltpu.make_async_copy(ystage.at[0, :, r, :], yt_ref.at[2 * T + r], ssem.at[0]).start()
            return c

        lax.fori_loop(0, MOE_TM, spare, 0)
        scatter_tile(0).wait()
        for r in range(MOE_TM):
            gather_row(0, 0, r).start(priority=ROW_DMA_PRIORITY)

    @pl.when(tf_ref[i] == 1)
    def _():
        for c in weight_copies(te_ref[i]):
            c.wait()
        wgb_ref[...] = wsg[...].astype(BF16)
        wub_ref[...] = wsu[...].astype(BF16)
        wdb_ref[...] = wsd[...].astype(BF16)

        @pl.when(nx_ref[i] >= 0)
        def _():
            for c in weight_copies(nx_ref[i]):
                c.start()

    slot = lax.rem(i, 2)
    other = 1 - slot

    @pl.when(i < nu)
    def _():
        gather_tile(slot).wait()

        @pl.when(i >= 1)
        def _():
            scatter_tile(slot).wait()

        nxt = jnp.minimum(i + 1, nu - 1)
        prv = jnp.maximum(i - 1, 0)
        for r in range(MOE_TM):
            gather_row(nxt, other, r).start(priority=r % 2)
            scatter_row(prv, other, r).start(priority=(r + 1) % 2)

        x = jnp.concatenate([xbuf[slot, s].astype(BF16) for s in range(Y_ROWS)], axis=1)
        a = _dot(x, wgb_ref[...])
        u = _dot(x, wub_ref[...])
        act = (_silu(a) * u).astype(BF16)
        y = _dot(act, wdb_ref[...])
        for s in range(Y_ROWS):
            ystage[slot, s] = y[:, s * LANES:(s + 1) * LANES]

    @pl.when(i == nu)
    def _():
        gather_tile(slot).wait()
        scatter_tile(slot).wait()

        def last(r, c):
            scatter_row(nu - 1, other, r).start()
            return c

        lax.fori_loop(0, MOE_TM, last, 0)
        scatter_tile(other).wait()


def _moe(tile_expert, tile_first, next_expert, n_used, dest, pad_lo, pad_hi, h2t, w_gate, w_up, w_down):
    T = h2t.shape[0]
    assert T & (T - 1) == 0 and T >= MOE_TM, "token index is recovered from the row table by masking"
    n_tiles = tile_expert.shape[0] - 1
    hbm = pl.BlockSpec(memory_space=pl.ANY)
    grid_spec = pltpu.PrefetchScalarGridSpec(
        num_scalar_prefetch=7,
        grid=(n_tiles + 1,),
        in_specs=[hbm, hbm, hbm, hbm],
        out_specs=hbm,
        scratch_shapes=[pltpu.SMEM((n_tiles * MOE_TM,), I32),
                        pltpu.VMEM((2, Y_ROWS, MOE_TM, LANES), F32),
                        pltpu.VMEM((2, Y_ROWS, MOE_TM, LANES), F32),
                        pltpu.VMEM((D_MODEL, D_EXPERT), F32),
                        pltpu.VMEM((D_MODEL, D_EXPERT), F32),
                        pltpu.VMEM((D_EXPERT, D_MODEL), F32),
                        pltpu.VMEM((D_MODEL, D_EXPERT), BF16),
                        pltpu.VMEM((D_MODEL, D_EXPERT), BF16),
                        pltpu.VMEM((D_EXPERT, D_MODEL), BF16),
                        pltpu.SemaphoreType.DMA((2,)),
                        pltpu.SemaphoreType.DMA((2,)),
                        pltpu.SemaphoreType.DMA((3,))],
    )
    return pl.pallas_call(
        _moe_body,
        grid_spec=grid_spec,
        out_shape=jax.ShapeDtypeStruct((2 * T + MOE_TM, Y_ROWS, LANES), F32),
        compiler_params=_cparams(("arbitrary",)),
        name="moe",
    )(tile_expert, tile_first, next_expert, n_used, dest, pad_lo, pad_hi, h2t, w_gate, w_up, w_down)


COMB_TM = 256


def _combine_body(h1_ref, y0_ref, y1_ref, cw_ref, nw_ref, o_ref):
    tm = h1_ref.shape[0]
    c0 = cw_ref[:, 0:1]
    c1 = cw_ref[:, 1:2]
    cols = []
    for s in range(Y_ROWS):
        moe = c0 * y0_ref[:, s, :] + c1 * y1_ref[:, s, :]
        cols.append(h1_ref[:, s * LANES:(s + 1) * LANES] + moe)
    h = jnp.concatenate(cols, axis=1)
    ms = jnp.mean(h * h, axis=-1, keepdims=True)
    o_ref[...] = h * lax.rsqrt(ms + EPS) * nw_ref[...]


def _combine(h1, y_tok, cw, nw):
    T = h1.shape[0]
    tm = min(COMB_TM, T)
    nt = T // tm
    return pl.pallas_call(
        _combine_body,
        grid=(nt,),
        in_specs=[
            pl.BlockSpec((tm, D_MODEL), lambda i: (i, 0)),
            pl.BlockSpec((tm, Y_ROWS, LANES), lambda i: (i, 0, 0)),
            pl.BlockSpec((tm, Y_ROWS, LANES), lambda i: (i + nt, 0, 0)),
            pl.BlockSpec((tm, LANES), lambda i: (i, 0)),
            pl.BlockSpec((1, D_MODEL), lambda i: (0, 0)),
        ],
        out_specs=pl.BlockSpec((tm, D_MODEL), lambda i: (i, 0)),
        out_shape=jax.ShapeDtypeStruct((T, D_MODEL), F32),
        compiler_params=_cparams(("parallel",)),
        name="combine",
    )(h1, y_tok, y_tok, cw, nw)


def _tile_plan(cnt, n_tiles):
    tiles = (cnt + (MOE_TM - 1)) // MOE_TM
    ends = jnp.cumsum(tiles)
    starts = ends - tiles
    n_used = ends[-1]
    step = jnp.arange(n_tiles + 1, dtype=I32)
    tile = jnp.minimum(step, jnp.maximum(n_used - 1, 0))
    tile_expert = jnp.sum((ends[None, :] <= tile[:, None]).astype(I32), axis=1)
    tile_first = ((step == starts[tile_expert]) & (step < n_used)).astype(I32)
    nxt_tile = ends[tile_expert]
    nxt_expert = jnp.sum((ends[None, :] <= nxt_tile[:, None]).astype(I32), axis=1)
    next_expert = jnp.where(nxt_tile < n_used, nxt_expert, -1).astype(I32)
    pad_lo = starts * MOE_TM + cnt
    pad_hi = ends * MOE_TM
    return (tile_expert, tile_first, next_expert, n_used.reshape(1).astype(I32),
            pad_lo.astype(I32), pad_hi.astype(I32))


def kernel(x, positions, norm1_w, w_in, conv_w, conv_b, dt_bias, a_log, d_skip, ret_norm_w,
           ssm_norm_w, w_out, norm2_w, w_router_group, b_router_group, w_router_expert,
           b_router_expert, w_expert_gate, w_expert_up, w_expert_down, final_norm_w):
    B, T, D = x.shape
    assert B == 1 and D == D_MODEL and T % CHUNK == 0
    xf = x.reshape(T, D)
    pad_l = lambda v: jnp.pad(v, ((0, 0), (0, LANES - v.shape[-1])))

    w_in_t = jnp.swapaxes(w_in[0], 0, 1)
    w_main = _cast_rows_bf16(w_in_t, MAIN_PROJ, CAST_ROWS)
    w_dt = jnp.pad(w_in_t[MAIN_PROJ:], ((0, LANES - (w_in_t.shape[0] - MAIN_PROJ)), (0, 0))).astype(BF16)
    half = RET_HEAD_DIM // 2
    inv = (ROPE_THETA ** (-jnp.arange(half, dtype=F32) / half)).reshape(1, half)
    tri = (jnp.arange(CHUNK)[:, None] >= jnp.arange(CHUNK)[None, :]).astype(BF16)
    expand = (jnp.arange(LANES)[:, None] == jnp.arange(SSM_WIDTH)[None, :] // SSM_HEAD_DIM).astype(BF16)
    d_skip_e = jnp.repeat(d_skip[0], SSM_HEAD_DIM).reshape(1, SSM_WIDTH)
    wr_t = jnp.zeros((LANES, D), F32)
    wr_t = wr_t.at[0:N_GROUPS].set(w_router_group[0].T).at[8:8 + N_EXPERTS].set(w_router_expert[0].T)
    br = jnp.zeros((LANES,), F32)
    br = br.at[0:N_GROUPS].set(b_router_group[0]).at[8:8 + N_EXPERTS].set(b_router_expert[0])
    tri_u = (jnp.arange(SORT_BLK)[:, None] <= jnp.arange(SORT_BLK)[None, :]).astype(BF16)
    ltri = (jnp.arange(N_EXPERTS)[:, None] > jnp.arange(N_EXPERTS)[None, :]).astype(BF16)

    cos, sin = _rope_tables(positions.reshape(T, 1).astype(F32), inv)
    y_ret, y_ssm = _mixer(xf, norm1_w[0].reshape(1, D), w_main, w_dt, cos, sin,
                          ret_norm_w[0].reshape(1, RET_WIDTH), _retention_consts(),
                          conv_w[0], conv_b[0].reshape(1, -1), pad_l(dt_bias[0].reshape(1, -1)),
                          pad_l(a_log[0].reshape(1, -1)), d_skip_e, ssm_norm_w[0].reshape(1, -1), tri, expand)
    h1, h2t, cw, ids = _out_router(xf, y_ret, y_ssm, w_out[0].astype(BF16), norm2_w[0].reshape(1, D),
                                   wr_t.astype(BF16), br.reshape(LANES, 1))

    dest_blk, cnt = _sort_index(ids, tri_u, ltri)
    dest = dest_blk[:, 0:2, :].transpose(1, 0, 2).reshape(2 * T)
    n_tiles = (2 * T) // MOE_TM + N_EXPERTS
    tile_expert, tile_first, next_expert, n_used, pad_lo, pad_hi = _tile_plan(cnt[:, 0].astype(I32), n_tiles)
    y_tok = _moe(tile_expert, tile_first, next_expert, n_used, dest, pad_lo, pad_hi, h2t,
                 w_expert_gate[0], w_expert_up[0], w_expert_down[0])
    out = _combine(h1, y_tok, cw, final_norm_w.reshape(1, D))
    return out.reshape(B, T, D)
```

```python
import functools

import jax
import jax.numpy as jnp
from jax import lax
from jax.experimental import pallas as pl
from jax.experimental.pallas import tpu as pltpu

F32 = jnp.float32
BF16 = jnp.bfloat16
I32 = jnp.int32

D_MODEL = 2048
EPS = 1e-6
CHUNK = 128
RET_HEADS = 4
RET_HEAD_DIM = 256
RET_WIDTH = RET_HEADS * RET_HEAD_DIM
ROPE_THETA = 10000.0
SSM_WIDTH = 1024
SSM_HEAD_DIM = 64
SSM_HEADS = SSM_WIDTH // SSM_HEAD_DIM
SSM_GROUPS = 2
SSM_STATE = 128
SSM_CONV = 4
SSM_CONV_DIM = SSM_WIDTH + 2 * SSM_GROUPS * SSM_STATE
MAIN_PROJ = 4 * RET_WIDTH + SSM_WIDTH + SSM_CONV_DIM
N_GROUPS = 4
GROUP_EXPERTS = 8
N_EXPERTS = N_GROUPS * GROUP_EXPERTS
D_EXPERT = 512
LANES = 128
Y_ROWS = D_MODEL // LANES

VMEM_LIMIT = 56 * 1024 * 1024


def _cparams(sem, vmem=VMEM_LIMIT):
    return pltpu.CompilerParams(dimension_semantics=sem, vmem_limit_bytes=vmem)


def _silu(x):
    return x * (1.0 / (1.0 + jnp.exp(-x)))


def _dot(a, b):
    return jnp.dot(a, b, preferred_element_type=F32)


def _dot_nt(a, b):
    return lax.dot_general(a, b, (((1,), (1,)), ((), ())), preferred_element_type=F32)


def _split3(a):
    a1 = a.astype(BF16)
    r1 = a - a1.astype(F32)
    a2 = r1.astype(BF16)
    a3 = (r1 - a2.astype(F32)).astype(BF16)
    return a1, a2, a3


def _cast_body(w_ref, o_ref):
    o_ref[...] = w_ref[...].astype(BF16)


def _cast_rows_bf16(w, n_rows, tr):
    cols = w.shape[1]
    return pl.pallas_call(
        _cast_body,
        grid=(n_rows // tr,),
        in_specs=[pl.BlockSpec((tr, cols), lambda j: (j, 0))],
        out_specs=pl.BlockSpec((tr, cols), lambda j: (j, 0)),
        out_shape=jax.ShapeDtypeStruct((n_rows, cols), BF16),
        compiler_params=_cparams(("parallel",)),
        name="cast_bf16",
    )(w)


CAST_ROWS = 512


def _rope_body(pos_ref, inv_ref, cos_ref, sin_ref):
    ang = pos_ref[...] * inv_ref[...]
    cos_ref[...] = jnp.cos(ang)
    sin_ref[...] = jnp.sin(ang)


def _rope_tables(pos, inv):
    T = pos.shape[0]
    tm = min(1024, T)
    half = inv.shape[1]
    return pl.pallas_call(
        _rope_body,
        grid=(T // tm,),
        in_specs=[pl.BlockSpec((tm, 1), lambda i: (i, 0)),
                  pl.BlockSpec((1, half), lambda i: (0, 0))],
        out_specs=[pl.BlockSpec((tm, half), lambda i: (i, 0))] * 2,
        out_shape=[jax.ShapeDtypeStruct((T, half), F32)] * 2,
        compiler_params=_cparams(("parallel",)),
        name="rope_tables",
    )(pos, inv)


def _retention_chunks(cd_ref, q_ref, k_ref, v_ref, g_ref, cos_ref, sin_ref,
                      dec_ref, wq_ref, ws_ref, nw_ref, o_ref, st_ref, cb, tick):
    hd = RET_HEAD_DIM
    half = hd // 2
    for i in range(cb):
        sl = slice(i * CHUNK, (i + 1) * CHUNK)
        cos = cos_ref[sl, :]
        sin = sin_ref[sl, :]

        def rope(x):
            x1, x2 = x[:, :half], x[:, half:]
            return jnp.concatenate([x1 * cos - x2 * sin, x2 * cos + x1 * sin], axis=-1)

        for h in range(RET_HEADS):
            hs = slice(h * hd, (h + 1) * hd)
            q = rope(q_ref[sl, hs])
            k = rope(k_ref[sl, hs]) * (hd ** -0.5)
            v = v_ref[sl, hs]
            qb = q.astype(BF16)
            kb = k.astype(BF16)
            scores = _dot_nt(qb, kb) * dec_ref[h]
            y = _dot(scores.astype(BF16), v.astype(BF16))
            st = st_ref[h]
            y = y + _dot(qb, st.astype(BF16)) * wq_ref[h]
            vw = (v * ws_ref[h]).astype(BF16)
            new = _dot(k.T.astype(BF16), vw)
            st_ref[h] = st * cd_ref[h] + new
            mu = jnp.mean(y, axis=-1, keepdims=True)
            d = y - mu
            var = jnp.mean(d * d, axis=-1, keepdims=True)
            yn = d * lax.rsqrt(var + EPS)
            o_ref[sl, hs] = (yn * nw_ref[:, hs] * _silu(g_ref[sl, hs])).astype(BF16)
            tick()


def _retention_consts():
    H, L = RET_HEADS, CHUNK
    log_gamma = jnp.log1p(-(2.0 ** (-5.0 - jnp.arange(H, dtype=F32))))
    idx = jnp.arange(L, dtype=F32)
    diff = idx[:, None] - idx[None, :]
    causal = diff >= 0
    decay_intra = jnp.where(causal[None], jnp.exp(jnp.where(causal, diff, 0.0)[None] * log_gamma[:, None, None]), 0.0)
    w_state = jnp.exp((L - 1.0 - idx)[None, :] * log_gamma[:, None])
    w_query = jnp.exp((idx + 1.0)[None, :] * log_gamma[:, None])
    chunk_decay = jnp.exp(L * log_gamma)
    bc = lambda w: jnp.broadcast_to(w[:, :, None], (H, L, RET_HEAD_DIM))
    return chunk_decay, decay_intra, bc(w_query), bc(w_state)


CONV_PAD = 8
XBC_BLK = 512


def _ssd_chunks(xbc_ref, z_ref, dt_ref, cw_ref, cbias_ref, dtb_ref,
                alog_ref, dskip_ref, nw_ref, tri_ref, exp_ref, o_ref, xpad_ref, st_ref, cb, tick):
    L = CHUNK
    gw = SSM_WIDTH // SSM_GROUPS
    rows_i = lax.broadcasted_iota(I32, (L, L), 0)
    cols_i = lax.broadcasted_iota(I32, (L, L), 1)
    causal = rows_i >= cols_i
    lo_lane = lax.broadcasted_iota(I32, (L, LANES), 1) < SSM_HEAD_DIM
    tri = tri_ref[...]
    expand = exp_ref[...]
    a_neg = -jnp.exp(alog_ref[...])

    for i in range(cb):
        sl = slice(i * L, (i + 1) * L)
        xpad_ref[CONV_PAD:CONV_PAD + L, :] = xbc_ref[sl, :]
        u_parts = []
        for b in range(SSM_CONV_DIM // XBC_BLK):
            cs = slice(b * XBC_BLK, (b + 1) * XBC_BLK)
            acc = cbias_ref[:, cs]
            for t in range(SSM_CONV):
                r0 = CONV_PAD - (SSM_CONV - 1) + t
                acc = acc + xpad_ref[r0:r0 + L, cs] * cw_ref[t:t + 1, cs]
            u_parts.append(_silu(acc))
            tick()
        xpad_ref[0:CONV_PAD, :] = xpad_ref[L:L + CONV_PAD, :]
        xs = jnp.concatenate(u_parts[:2], axis=-1)
        bm = u_parts[2][:, :SSM_GROUPS * SSM_STATE]
        cm = u_parts[2][:, SSM_GROUPS * SSM_STATE:]

        dt_in = dt_ref[sl, :] + dtb_ref[...]
        dt = jnp.maximum(dt_in, 0.0) + jnp.log1p(jnp.exp(-jnp.abs(dt_in)))
        a = dt * a_neg
        a1, a2, a3 = _split3(a)
        a_cs = _dot(tri, a1) + _dot(tri, a2) + _dot(tri, a3)
        d1, d2, d3 = _split3(dt)
        dt_e = _dot(d1, expand) + _dot(d2, expand) + _dot(d3, expand)
        c1, c2, c3 = _split3(a_cs)
        acs_e = _dot(c1, expand) + _dot(c2, expand) + _dot(c3, expand)
        tick()
        last = acs_e[L - 1:L, :]
        to_end = jnp.exp(last - acs_e)
        from_start = jnp.exp(acs_e)
        chunk_decay = jnp.exp(last)
        x_dt = xs * dt_e
        xw = (x_dt * to_end).astype(BF16)
        acs_t = a_cs.T
        cmb = cm.astype(BF16)
        bmb = bm.astype(BF16)

        ys = []
        for g in range(SSM_GROUPS):
            ns = slice(g * SSM_STATE, (g + 1) * SSM_STATE)
            gs = slice(g * gw, (g + 1) * gw)
            cg = cmb[:, ns]
            cbg = _dot_nt(cg, bmb[:, ns])
            st = st_ref[g]
            y_off = _dot(cg, st.astype(BF16))
            new = _dot(bm[:, ns].T.astype(BF16), xw[:, gs])
            st_ref[g] = st * chunk_decay[:, gs] + new
            tick()
            for jp in range(gw // LANES):
                h0 = (g * gw + jp * LANES) // SSM_HEAD_DIM
                ms = []
                for hh in (h0, h0 + 1):
                    seg = a_cs[:, hh:hh + 1] - acs_t[hh:hh + 1, :]
                    dec = jnp.exp(jnp.where(causal, seg, -jnp.inf))
                    ms.append((cbg * dec).astype(BF16))
                lhs = jnp.concatenate(ms, axis=1)
                ls = slice(g * gw + jp * LANES, g * gw + (jp + 1) * LANES)
                xp = x_dt[:, ls]
                rhs = jnp.concatenate([jnp.where(lo_lane, xp, 0.0),
                                       jnp.where(lo_lane, 0.0, xp)], axis=0).astype(BF16)
                y_diag = _dot(lhs, rhs)
                ys.append(y_diag + y_off[:, jp * LANES:(jp + 1) * LANES] * from_start[:, ls])
                tick()
        y = jnp.concatenate(ys, axis=1) + xs * dskip_ref[...]
        y = y * _silu(z_ref[sl, :])
        outs = []
        for g in range(SSM_GROUPS):
            yg = y[:, g * gw:(g + 1) * gw]
            ms_ = jnp.mean(yg * yg, axis=-1, keepdims=True)
            outs.append(yg * lax.rsqrt(ms_ + EPS))
        o_ref[sl, :] = (jnp.concatenate(outs, axis=1) * nw_ref[...]).astype(BF16)


MIX_CB = 2
MIX_VMEM = 60 * 1024 * 1024
COL_Z = 4 * RET_WIDTH
COL_XBC = COL_Z + SSM_WIDTH
PROJ_WIDTHS = (RET_WIDTH, RET_WIDTH, RET_WIDTH, RET_WIDTH, SSM_WIDTH, SSM_CONV_DIM, LANES)
PROJ_SLAB = 256


def _mixer_body(cd_ref, x_ref, n1w_ref, wt_hbm, wdt_ref, cos_ref, sin_ref, dec_ref, wq_ref, ws_ref, rnw_ref,
                cw_ref, cbias_ref, dtb_ref, alog_ref, dskip_ref, snw_ref, tri_ref, exp_ref,
                yret_ref, yssm_ref, w_ref, hn_ref, pq, pk, pv, pg, pz, pxbc, pdt,
                st_ret, xpad_ref, st_ssd, sem, *, cb):
    @pl.when(pl.program_id(0) == 0)
    def _():
        cp = pltpu.make_async_copy(wt_hbm, w_ref, sem)
        cp.start()
        st_ret[...] = jnp.zeros_like(st_ret)
        xpad_ref[0:CONV_PAD, :] = jnp.zeros((CONV_PAD, SSM_CONV_DIM), F32)
        st_ssd[...] = jnp.zeros_like(st_ssd)
        cp.wait()

    def slabs(ref, col0):
        def slab(lo):
            def go():
                ref[:, lo:lo + PROJ_SLAB] = _dot_nt(hn_ref[...], w_ref[col0 + lo:col0 + lo + PROJ_SLAB, :])
            return go
        return [slab(lo) for lo in range(0, ref.shape[1], PROJ_SLAB)]

    x = x_ref[...]
    ms = jnp.mean(x * x, axis=-1, keepdims=True)
    hn_ref[...] = (x * lax.rsqrt(ms + EPS) * n1w_ref[...]).astype(BF16)
    for piece in slabs(pxbc, COL_XBC) + slabs(pz, COL_Z):
        piece()
    pdt[...] = _dot_nt(hn_ref[...], wdt_ref[...])

    pieces = slabs(pq, 0) + slabs(pk, RET_WIDTH) + slabs(pv, 2 * RET_WIDTH) + slabs(pg, 3 * RET_WIDTH)
    n_pieces = len(pieces)
    n_ticks = cb * (SSM_CONV_DIM // XBC_BLK + 1 + SSM_GROUPS + SSM_WIDTH // LANES)
    calls = [0]

    def tick():
        calls[0] += 1
        while pieces and (n_pieces - len(pieces)) * n_ticks < calls[0] * n_pieces:
            pieces.pop(0)()

    _ssd_chunks(pxbc, pz, pdt, cw_ref, cbias_ref, dtb_ref, alog_ref, dskip_ref, snw_ref, tri_ref,
                exp_ref, yssm_ref, xpad_ref, st_ssd, cb, tick)
    while pieces:
        pieces.pop(0)()
    _retention_chunks(cd_ref, pq, pk, pv, pg, cos_ref, sin_ref, dec_ref, wq_ref, ws_ref, rnw_ref,
                      yret_ref, st_ret, cb, lambda: None)


def _mixer(x, n1w, w_main_t, w_dt_t, cos, sin, ret_nw, ret_consts,
           conv_w, conv_b, dt_bias, a_log, d_skip_e, ssm_nw, tri, expand):
    T = x.shape[0]
    cb = min(MIX_CB, T // CHUNK)
    rows = cb * CHUNK
    chunk_decay, decay_intra, w_query, w_state = ret_consts
    hd = RET_HEAD_DIM
    half = hd // 2
    full = lambda shape: pl.BlockSpec(shape, lambda c, cd: (0,) * len(shape))
    rowblk = lambda width: pl.BlockSpec((rows, width), lambda c, cd: (c, 0))
    scratch = lambda width: pltpu.VMEM((rows, width), F32)
    grid_spec = pltpu.PrefetchScalarGridSpec(
        num_scalar_prefetch=1,
        grid=(T // rows,),
        in_specs=[
            rowblk(D_MODEL),
            full((1, D_MODEL)),
            pl.BlockSpec(memory_space=pl.ANY),
            full((LANES, D_MODEL)),
            rowblk(half), rowblk(half),
            full((RET_HEADS, CHUNK, CHUNK)),
            full((RET_HEADS, CHUNK, hd)),
            full((RET_HEADS, CHUNK, hd)),
            full((1, RET_WIDTH)),
            full((SSM_CONV, SSM_CONV_DIM)),
            full((1, SSM_CONV_DIM)),
            full((1, LANES)),
            full((1, LANES)),
            full((1, SSM_WIDTH)),
            full((1, SSM_WIDTH)),
            full((CHUNK, CHUNK)),
            full((LANES, SSM_WIDTH)),
        ],
        out_specs=[rowblk(RET_WIDTH), rowblk(SSM_WIDTH)],
        scratch_shapes=[
            pltpu.VMEM((MAIN_PROJ, D_MODEL), BF16),
            pltpu.VMEM((rows, D_MODEL), BF16),
            *[scratch(w) for w in PROJ_WIDTHS],
            pltpu.VMEM((RET_HEADS, hd, hd), F32),
            pltpu.VMEM((CHUNK + CONV_PAD, SSM_CONV_DIM), F32),
            pltpu.VMEM((SSM_GROUPS, SSM_STATE, SSM_WIDTH // SSM_GROUPS), F32),
            pltpu.SemaphoreType.DMA(()),
        ],
    )
    return pl.pallas_call(
        functools.partial(_mixer_body, cb=cb),
        grid_spec=grid_spec,
        out_shape=[jax.ShapeDtypeStruct((T, RET_WIDTH), BF16),
                   jax.ShapeDtypeStruct((T, SSM_WIDTH), BF16)],
        compiler_params=_cparams(("arbitrary",), MIX_VMEM),
        name="mixer",
    )(chunk_decay, x, n1w, w_main_t, w_dt_t, cos, sin, decay_intra, w_query, w_state, ret_nw,
      conv_w, conv_b, dt_bias, a_log, d_skip_e, ssm_nw, tri, expand)


OUT_TM = 256
ROUTE_ROWS = 8 + N_EXPERTS


def _out_router_body(x_ref, yr_ref, ys_ref, wo_ref, nw_ref, wr_ref, br_ref,
                     h1_ref, h2t_ref, cw_ref, ids_ref):
    tm = x_ref.shape[0]
    h1 = x_ref[...] + _dot(yr_ref[...], wo_ref[0:RET_WIDTH, :]) + _dot(ys_ref[...], wo_ref[RET_WIDTH:, :])
    h1_ref[...] = h1
    ms = jnp.mean(h1 * h1, axis=-1, keepdims=True)
    h2 = h1 * lax.rsqrt(ms + EPS) * nw_ref[...]
    for s in range(Y_ROWS):
        h2t_ref[:, s, :] = h2[:, s * LANES:(s + 1) * LANES]

    logits = _dot_nt(wr_ref[...], h2.astype(BF16)) + br_ref[...]
    row = lax.broadcasted_iota(I32, (8, tm), 0)
    lg = jnp.where(row < N_GROUPS, logits[0:8], -jnp.inf)
    m = jnp.max(lg, axis=0, keepdims=True)
    p_sel = 1.0 / jnp.sum(jnp.exp(lg - m), axis=0, keepdims=True)
    g_sel = jnp.min(jnp.where(lg == m, row, 8), axis=0, keepdims=True)
    le = jnp.zeros((GROUP_EXPERTS, tm), F32)
    for g in range(N_GROUPS):
        le = jnp.where(g_sel == g, logits[8 + g * GROUP_EXPERTS:8 + (g + 1) * GROUP_EXPERTS], le)
    m2 = jnp.max(le, axis=0, keepdims=True)
    ee = jnp.exp(le - m2)
    pe = ee / jnp.sum(ee, axis=0, keepdims=True)
    v1 = jnp.max(pe, axis=0, keepdims=True)
    i1 = jnp.min(jnp.where(pe == v1, row, 8), axis=0, keepdims=True)
    pe2 = jnp.where(row == i1, -1.0, pe)
    v2 = jnp.max(pe2, axis=0, keepdims=True)
    i2 = jnp.min(jnp.where(pe2 == v2, row, 8), axis=0, keepdims=True)
    tw = v1 + v2
    c1 = v1 / tw * p_sel
    c2 = v2 / tw * p_sel
    e1 = g_sel * GROUP_EXPERTS + i1
    e2 = g_sel * GROUP_EXPERTS + i2
    ids = jnp.where(row == 0, e1, jnp.where(row == 1, e2, 0))
    for b in range(tm // SORT_BLK):
        ids_ref[b] = ids[:, b * SORT_BLK:(b + 1) * SORT_BLK]
    cw8 = jnp.where(row == 0, c1, jnp.where(row == 1, c2, 0.0))
    cw = jnp.concatenate([cw8, jnp.zeros((LANES - 8, tm), F32)], axis=0)
    cw_ref[...] = cw.T


def _out_router(x, y_ret, y_ssm, w_out, nw, wr_t, br):
    T = x.shape[0]
    tm = min(OUT_TM, T)
    full = lambda shape: pl.BlockSpec(shape, lambda i: (0,) * len(shape))
    return pl.pallas_call(
        _out_router_body,
        grid=(T // tm,),
        in_specs=[
            pl.BlockSpec((tm, D_MODEL), lambda i: (i, 0)),
            pl.BlockSpec((tm, RET_WIDTH), lambda i: (i, 0)),
            pl.BlockSpec((tm, SSM_WIDTH), lambda i: (i, 0)),
            full((D_MODEL, D_MODEL)),
            full((1, D_MODEL)),
            full((LANES, D_MODEL)),
            full((LANES, 1)),
        ],
        out_specs=[
            pl.BlockSpec((tm, D_MODEL), lambda i: (i, 0)),
            pl.BlockSpec((tm, Y_ROWS, LANES), lambda i: (i, 0, 0)),
            pl.BlockSpec((tm, LANES), lambda i: (i, 0)),
            pl.BlockSpec((tm // SORT_BLK, 8, SORT_BLK), lambda i: (i, 0, 0)),
        ],
        out_shape=[
            jax.ShapeDtypeStruct((T, D_MODEL), F32),
            jax.ShapeDtypeStruct((T, Y_ROWS, LANES), F32),
            jax.ShapeDtypeStruct((T, LANES), F32),
            jax.ShapeDtypeStruct((T // SORT_BLK, 8, SORT_BLK), I32),
        ],
        compiler_params=_cparams(("parallel",)),
        name="out_router",
    )(x, y_ret, y_ssm, w_out, nw, wr_t, br)


SORT_BLK = 256
MOE_TM = 256


def _sort_index_body(ids_ref, tri_ref, ltri_ref, dest_ref, cnt_ref, rank_ref):
    nblk = ids_ref.shape[0]
    row_e = lax.broadcasted_iota(I32, (N_EXPERTS, SORT_BLK), 0)
    row8 = lax.broadcasted_iota(I32, (8, SORT_BLK), 0)

    def onehots(b):
        ids = ids_ref[b]
        return row_e == ids[0:1], row_e == ids[1:2]

    def rank_blk(b, carry):
        oh1, oh2 = onehots(b)
        ohf = jnp.where(oh1 | oh2, 1.0, 0.0)
        incl = _dot(ohf.astype(BF16), tri_ref[...])
        base = carry + incl - 1.0
        r1 = jnp.sum(jnp.where(oh1, base, 0.0), axis=0, keepdims=True)
        r2 = jnp.sum(jnp.where(oh2, base, 0.0), axis=0, keepdims=True)
        rank_ref[b] = jnp.where(row8 == 0, r1, jnp.where(row8 == 1, r2, 0.0))
        return carry + jnp.sum(ohf, axis=1, keepdims=True)

    cnt = lax.fori_loop(0, nblk, rank_blk, jnp.zeros((N_EXPERTS, 1), F32))
    cnt_ref[...] = jnp.broadcast_to(cnt, cnt_ref.shape)
    tiles = jnp.floor((cnt + (MOE_TM - 1.0)) / MOE_TM)
    tiles_b = jnp.broadcast_to(tiles, (N_EXPERTS, LANES)).astype(BF16)
    off = _dot(ltri_ref[...], tiles_b)[:, 0:1] * MOE_TM

    def dest_blk(b, carry):
        oh1, oh2 = onehots(b)
        o1 = jnp.sum(jnp.where(oh1, off, 0.0), axis=0, keepdims=True)
        o2 = jnp.sum(jnp.where(oh2, off, 0.0), axis=0, keepdims=True)
        d = rank_ref[b] + jnp.where(row8 == 0, o1, jnp.where(row8 == 1, o2, 0.0))
        dest_ref[b] = d.astype(I32)
        return carry

    lax.fori_loop(0, nblk, dest_blk, 0)


def _sort_index(ids, tri_u, ltri):
    nblk = ids.shape[0]
    return pl.pallas_call(
        _sort_index_body,
        out_shape=[jax.ShapeDtypeStruct((nblk, 8, SORT_BLK), I32),
                   jax.ShapeDtypeStruct((N_EXPERTS, LANES), F32)],
        scratch_shapes=[pltpu.VMEM((nblk, 8, SORT_BLK), F32)],
        compiler_params=_cparams(None),
        name="sort_index",
    )(ids, tri_u, ltri)


ROW_DMA_PRIORITY = 1


def _moe_body(te_ref, tf_ref, nx_ref, nu_ref, dest_ref, lo_ref, hi_ref,
              h2t_ref, wg_ref, wu_ref, wd_ref, yt_ref,
              srow_ref, wcount, xbuf, ystage, wsg, wsu, wsd, wgb_ref, wub_ref, wdb_ref, gsem, ssem, wsem):
    i = pl.program_id(0)
    nu = nu_ref[0]
    T = h2t_ref.shape[0]

    def gather_row(tile, slot, r):
        tok = srow_ref[tile * MOE_TM + r] & (T - 1)
        return pltpu.make_async_copy(h2t_ref.at[tok], xbuf.at[slot, :, r, :], gsem.at[slot])

    def scatter_row(tile, slot, r):
        return pltpu.make_async_copy(ystage.at[slot, :, r, :], yt_ref.at[srow_ref[tile * MOE_TM + r]],
                                     ssem.at[slot])

    def gather_tile(slot):
        return pltpu.make_async_copy(h2t_ref.at[pl.ds(0, MOE_TM)], h2t_ref.at[pl.ds(0, MOE_TM)], gsem.at[slot])

    def scatter_tile(slot):
        return pltpu.make_async_copy(yt_ref.at[pl.ds(0, MOE_TM)], yt_ref.at[pl.ds(0, MOE_TM)], ssem.at[slot])

    def weight_copies(e, ws):
        return (pltpu.make_async_copy(wg_ref.at[e], wsg.at[ws], wsem.at[ws, 0]),
                pltpu.make_async_copy(wu_ref.at[e], wsu.at[ws], wsem.at[ws, 1]),
                pltpu.make_async_copy(wd_ref.at[e], wsd.at[ws], wsem.at[ws, 2]))

    @pl.when(i == 0)
    def _():
        wcount[0] = 0
        for c in weight_copies(te_ref[0], 0):
            c.start()

        def tok(t, c):
            srow_ref[dest_ref[t]] = t
            srow_ref[dest_ref[T + t]] = T + t
            return c

        lax.fori_loop(0, T, tok, 0, unroll=8)

        def seg(e, c):
            def pad(r, c2):
                srow_ref[r] = 2 * T + (r & (MOE_TM - 1))
                return c2

            lax.fori_loop(lo_ref[e], hi_ref[e], pad, 0)
            return c

        lax.fori_loop(0, lo_ref.shape[0], seg, 0)

        ystage[...] = jnp.zeros_like(ystage)

        def spare(r, c):
            pltpu.make_async_copy(ystage.at[0, :, r, :], yt_ref.at[2 * T + r], ssem.at[0]).start()
            return c

        lax.fori_loop(0, MOE_TM, spare, 0)
        scatter_tile(0).wait()
        for r in range(MOE_TM):
            gather_row(0, 0, r).start(priority=ROW_DMA_PRIORITY)

    @pl.when(tf_ref[i] == 1)
    def _():
        ws = lax.rem(wcount[0], 2)
        wcount[0] = wcount[0] + 1
        for c in weight_copies(te_ref[i], ws):
            c.wait()

        @pl.when(nx_ref[i] >= 0)
        def _():
            for c in weight_copies(nx_ref[i], 1 - ws):
                c.start()

        wgb_ref[...] = wsg[ws].astype(BF16)
        wub_ref[...] = wsu[ws].astype(BF16)
        wdb_ref[...] = wsd[ws].astype(BF16)

    slot = lax.rem(i, 2)
    other = 1 - slot

    @pl.when(i < nu)
    def _():
        gather_tile(slot).wait()

        @pl.when(i >= 1)
        def _():
            scatter_tile(slot).wait()

        nxt = jnp.minimum(i + 1, nu - 1)
        prv = jnp.maximum(i - 1, 0)
        for r in range(MOE_TM):
            gather_row(nxt, other, r).start(priority=r % 2)
            scatter_row(prv, other, r).start(priority=(r + 1) % 2)

        x = jnp.concatenate([xbuf[slot, s].astype(BF16) for s in range(Y_ROWS)], axis=1)
        a = _dot(x, wgb_ref[...])
        u = _dot(x, wub_ref[...])
        act = (_silu(a) * u).astype(BF16)
        y = _dot(act, wdb_ref[...])
        for s in range(Y_ROWS):
            ystage[slot, s] = y[:, s * LANES:(s + 1) * LANES]

    @pl.when(i == nu)
    def _():
        gather_tile(slot).wait()
        scatter_tile(slot).wait()

        def last(r, c):
            scatter_row(nu - 1, other, r).start()
            return c

        lax.fori_loop(0, MOE_TM, last, 0)
        scatter_tile(other).wait()


def _moe(tile_expert, tile_first, next_expert, n_used, dest, pad_lo, pad_hi, h2t, w_gate, w_up, w_down):
    T = h2t.shape[0]
    assert T & (T - 1) == 0 and T >= MOE_TM, "token index is recovered from the row table by masking"
    n_tiles = tile_expert.shape[0] - 1
    hbm = pl.BlockSpec(memory_space=pl.ANY)
    grid_spec = pltpu.PrefetchScalarGridSpec(
        num_scalar_prefetch=7,
        grid=(n_tiles + 1,),
        in_specs=[hbm, hbm, hbm, hbm],
        out_specs=hbm,
        scratch_shapes=[pltpu.SMEM((n_tiles * MOE_TM,), I32),
                        pltpu.SMEM((1,), I32),
                        pltpu.VMEM((2, Y_ROWS, MOE_TM, LANES), F32),
                        pltpu.VMEM((2, Y_ROWS, MOE_TM, LANES), F32),
                        pltpu.VMEM((2, D_MODEL, D_EXPERT), F32),
                        pltpu.VMEM((2, D_MODEL, D_EXPERT), F32),
                        pltpu.VMEM((2, D_EXPERT, D_MODEL), F32),
                        pltpu.VMEM((D_MODEL, D_EXPERT), BF16),
                        pltpu.VMEM((D_MODEL, D_EXPERT), BF16),
                        pltpu.VMEM((D_EXPERT, D_MODEL), BF16),
                        pltpu.SemaphoreType.DMA((2,)),
                        pltpu.SemaphoreType.DMA((2,)),
                        pltpu.SemaphoreType.DMA((2, 3))],
    )
    return pl.pallas_call(
        _moe_body,
        grid_spec=grid_spec,
        out_shape=jax.ShapeDtypeStruct((2 * T + MOE_TM, Y_ROWS, LANES), F32),
        compiler_params=_cparams(("arbitrary",)),
        name="moe",
    )(tile_expert, tile_first, next_expert, n_used, dest, pad_lo, pad_hi, h2t, w_gate, w_up, w_down)


COMB_TM = 256


def _combine_body(h1_ref, y0_ref, y1_ref, cw_ref, nw_ref, o_ref):
    tm = h1_ref.shape[0]
    c0 = cw_ref[:, 0:1]
    c1 = cw_ref[:, 1:2]
    cols = []
    for s in range(Y_ROWS):
        moe = c0 * y0_ref[:, s, :] + c1 * y1_ref[:, s, :]
        cols.append(h1_ref[:, s * LANES:(s + 1) * LANES] + moe)
    h = jnp.concatenate(cols, axis=1)
    ms = jnp.mean(h * h, axis=-1, keepdims=True)
    o_ref[...] = h * lax.rsqrt(ms + EPS) * nw_ref[...]


def _combine(h1, y_tok, cw, nw):
    T = h1.shape[0]
    tm = min(COMB_TM, T)
    nt = T // tm
    return pl.pallas_call(
        _combine_body,
        grid=(nt,),
        in_specs=[
            pl.BlockSpec((tm, D_MODEL), lambda i: (i, 0)),
            pl.BlockSpec((tm, Y_ROWS, LANES), lambda i: (i, 0, 0)),
            pl.BlockSpec((tm, Y_ROWS, LANES), lambda i: (i + nt, 0, 0)),
            pl.BlockSpec((tm, LANES), lambda i: (i, 0)),
            pl.BlockSpec((1, D_MODEL), lambda i: (0, 0)),
        ],
        out_specs=pl.BlockSpec((tm, D_MODEL), lambda i: (i, 0)),
        out_shape=jax.ShapeDtypeStruct((T, D_MODEL), F32),
        compiler_params=_cparams(("parallel",)),
        name="combine",
    )(h1, y_tok, y_tok, cw, nw)


def _tile_plan(cnt, n_tiles):
    tiles = (cnt + (MOE_TM - 1)) // MOE_TM
    ends = jnp.cumsum(tiles)
    starts = ends - tiles
    n_used = ends[-1]
    step = jnp.arange(n_tiles + 1, dtype=I32)
    tile = jnp.minimum(step, jnp.maximum(n_used - 1, 0))
    tile_expert = jnp.sum((ends[None, :] <= tile[:, None]).astype(I32), axis=1)
    tile_first = ((step == starts[tile_expert]) & (step < n_used)).astype(I32)
    nxt_tile = ends[tile_expert]
    nxt_expert = jnp.sum((ends[None, :] <= nxt_tile[:, None]).astype(I32), axis=1)
    next_expert = jnp.where(nxt_tile < n_used, nxt_expert, -1).astype(I32)
    pad_lo = starts * MOE_TM + cnt
    pad_hi = ends * MOE_TM
    return (tile_expert, tile_first, next_expert, n_used.reshape(1).astype(I32),
            pad_lo.astype(I32), pad_hi.astype(I32))


def kernel(x, positions, norm1_w, w_in, conv_w, conv_b, dt_bias, a_log, d_skip, ret_norm_w,
           ssm_norm_w, w_out, norm2_w, w_router_group, b_router_group, w_router_expert,
           b_router_expert, w_expert_gate, w_expert_up, w_expert_down, final_norm_w):
    B, T, D = x.shape
    assert B == 1 and D == D_MODEL and T % CHUNK == 0
    xf = x.reshape(T, D)
    pad_l = lambda v: jnp.pad(v, ((0, 0), (0, LANES - v.shape[-1])))

    w_in_t = jnp.swapaxes(w_in[0], 0, 1)
    w_main = _cast_rows_bf16(w_in_t, MAIN_PROJ, CAST_ROWS)
    w_dt = jnp.pad(w_in_t[MAIN_PROJ:], ((0, LANES - (w_in_t.shape[0] - MAIN_PROJ)), (0, 0))).astype(BF16)
    half = RET_HEAD_DIM // 2
    inv = (ROPE_THETA ** (-jnp.arange(half, dtype=F32) / half)).reshape(1, half)
    tri = (jnp.arange(CHUNK)[:, None] >= jnp.arange(CHUNK)[None, :]).astype(BF16)
    expand = (jnp.arange(LANES)[:, None] == jnp.arange(SSM_WIDTH)[None, :] // SSM_HEAD_DIM).astype(BF16)
    d_skip_e = jnp.repeat(d_skip[0], SSM_HEAD_DIM).reshape(1, SSM_WIDTH)
    wr_t = jnp.zeros((LANES, D), F32)
    wr_t = wr_t.at[0:N_GROUPS].set(w_router_group[0].T).at[8:8 + N_EXPERTS].set(w_router_expert[0].T)
    br = jnp.zeros((LANES,), F32)
    br = br.at[0:N_GROUPS].set(b_router_group[0]).at[8:8 + N_EXPERTS].set(b_router_expert[0])
    tri_u = (jnp.arange(SORT_BLK)[:, None] <= jnp.arange(SORT_BLK)[None, :]).astype(BF16)
    ltri = (jnp.arange(N_EXPERTS)[:, None] > jnp.arange(N_EXPERTS)[None, :]).astype(BF16)

    cos, sin = _rope_tables(positions.reshape(T, 1).astype(F32), inv)
    y_ret, y_ssm = _mixer(xf, norm1_w[0].reshape(1, D), w_main, w_dt, cos, sin,
                          ret_norm_w[0].reshape(1, RET_WIDTH), _retention_consts(),
                          conv_w[0], conv_b[0].reshape(1, -1), pad_l(dt_bias[0].reshape(1, -1)),
                          pad_l(a_log[0].reshape(1, -1)), d_skip_e, ssm_norm_w[0].reshape(1, -1), tri, expand)
    h1, h2t, cw, ids = _out_router(xf, y_ret, y_ssm, w_out[0].astype(BF16), norm2_w[0].reshape(1, D),
                                   wr_t.astype(BF16), br.reshape(LANES, 1))

    dest_blk, cnt = _sort_index(ids, tri_u, ltri)
    dest = dest_blk[:, 0:2, :].transpose(1, 0, 2).reshape(2 * T)
    n_tiles = (2 * T) // MOE_TM + N_EXPERTS
    tile_expert, tile_first, next_expert, n_used, pad_lo, pad_hi = _tile_plan(cnt[:, 0].astype(I32), n_tiles)
    y_tok = _moe(tile_expert, tile_first, next_expert, n_used, dest, pad_lo, pad_hi, h2t,
                 w_expert_gate[0], w_expert_up[0], w_expert_down[0])
    out = _combine(h1, y_tok, cw, final_norm_w.reshape(1, D))
    return out.reshape(B, T, D)
```

```python
import functools

import jax
import jax.numpy as jnp
from jax import lax
from jax.experimental import pallas as pl
from jax.experimental.pallas import tpu as pltpu

F32 = jnp.float32
BF16 = jnp.bfloat16
I32 = jnp.int32

D_MODEL = 2048
EPS = 1e-6
CHUNK = 128
RET_HEADS = 4
RET_HEAD_DIM = 256
RET_WIDTH = RET_HEADS * RET_HEAD_DIM
ROPE_THETA = 10000.0
SSM_WIDTH = 1024
SSM_HEAD_DIM = 64
SSM_HEADS = SSM_WIDTH // SSM_HEAD_DIM
SSM_GROUPS = 2
SSM_STATE = 128
SSM_CONV = 4
SSM_CONV_DIM = SSM_WIDTH + 2 * SSM_GROUPS * SSM_STATE
MAIN_PROJ = 4 * RET_WIDTH + SSM_WIDTH + SSM_CONV_DIM
N_GROUPS = 4
GROUP_EXPERTS = 8
N_EXPERTS = N_GROUPS * GROUP_EXPERTS
D_EXPERT = 512
LANES = 128
Y_ROWS = D_MODEL // LANES

VMEM_LIMIT = 56 * 1024 * 1024


def _cparams(sem, vmem=VMEM_LIMIT):
    return pltpu.CompilerParams(dimension_semantics=sem, vmem_limit_bytes=vmem)


def _silu(x):
    return x * (1.0 / (1.0 + jnp.exp(-x)))


def _dot(a, b):
    return jnp.dot(a, b, preferred_element_type=F32)


def _dot_nt(a, b):
    return lax.dot_general(a, b, (((1,), (1,)), ((), ())), preferred_element_type=F32)


def _split3(a):
    a1 = a.astype(BF16)
    r1 = a - a1.astype(F32)
    a2 = r1.astype(BF16)
    a3 = (r1 - a2.astype(F32)).astype(BF16)
    return a1, a2, a3


def _cast_body(w_ref, o_ref):
    o_ref[...] = w_ref[...].astype(BF16)


def _cast_rows_bf16(w, n_rows, tr):
    cols = w.shape[1]
    return pl.pallas_call(
        _cast_body,
        grid=(n_rows // tr,),
        in_specs=[pl.BlockSpec((tr, cols), lambda j: (j, 0))],
        out_specs=pl.BlockSpec((tr, cols), lambda j: (j, 0)),
        out_shape=jax.ShapeDtypeStruct((n_rows, cols), BF16),
        compiler_params=_cparams(("parallel",)),
        name="cast_bf16",
    )(w)


CAST_ROWS = 512


def _rope_body(pos_ref, inv_ref, cos_ref, sin_ref):
    ang = pos_ref[...] * inv_ref[...]
    cos_ref[...] = jnp.cos(ang)
    sin_ref[...] = jnp.sin(ang)


def _rope_tables(pos, inv):
    T = pos.shape[0]
    tm = min(1024, T)
    half = inv.shape[1]
    return pl.pallas_call(
        _rope_body,
        grid=(T // tm,),
        in_specs=[pl.BlockSpec((tm, 1), lambda i: (i, 0)),
                  pl.BlockSpec((1, half), lambda i: (0, 0))],
        out_specs=[pl.BlockSpec((tm, half), lambda i: (i, 0))] * 2,
        out_shape=[jax.ShapeDtypeStruct((T, half), F32)] * 2,
        compiler_params=_cparams(("parallel",)),
        name="rope_tables",
    )(pos, inv)


def _retention_chunks(cd_ref, q_ref, k_ref, v_ref, g_ref, cos_ref, sin_ref,
                      dec_ref, wq_ref, ws_ref, nw_ref, o_ref, st_ref, cb, tick):
    hd = RET_HEAD_DIM
    half = hd // 2
    for i in range(cb):
        sl = slice(i * CHUNK, (i + 1) * CHUNK)
        cos = cos_ref[sl, :]
        sin = sin_ref[sl, :]

        def rope(x):
            x1, x2 = x[:, :half], x[:, half:]
            return jnp.concatenate([x1 * cos - x2 * sin, x2 * cos + x1 * sin], axis=-1)

        for h in range(RET_HEADS):
            hs = slice(h * hd, (h + 1) * hd)
            q = rope(q_ref[sl, hs])
            k = rope(k_ref[sl, hs]) * (hd ** -0.5)
            v = v_ref[sl, hs]
            qb = q.astype(BF16)
            kb = k.astype(BF16)
            scores = _dot_nt(qb, kb) * dec_ref[h]
            y = _dot(scores.astype(BF16), v.astype(BF16))
            st = st_ref[h]
            y = y + _dot(qb, st.astype(BF16)) * wq_ref[h]
            vw = (v * ws_ref[h]).astype(BF16)
            new = _dot(k.T.astype(BF16), vw)
            st_ref[h] = st * cd_ref[h] + new
            mu = jnp.mean(y, axis=-1, keepdims=True)
            d = y - mu
            var = jnp.mean(d * d, axis=-1, keepdims=True)
            yn = d * lax.rsqrt(var + EPS)
            o_ref[sl, hs] = (yn * nw_ref[:, hs] * _silu(g_ref[sl, hs])).astype(BF16)
            tick()


def _retention_consts():
    H, L = RET_HEADS, CHUNK
    log_gamma = jnp.log1p(-(2.0 ** (-5.0 - jnp.arange(H, dtype=F32))))
    idx = jnp.arange(L, dtype=F32)
    diff = idx[:, None] - idx[None, :]
    causal = diff >= 0
    decay_intra = jnp.where(causal[None], jnp.exp(jnp.where(causal, diff, 0.0)[None] * log_gamma[:, None, None]), 0.0)
    w_state = jnp.exp((L - 1.0 - idx)[None, :] * log_gamma[:, None])
    w_query = jnp.exp((idx + 1.0)[None, :] * log_gamma[:, None])
    chunk_decay = jnp.exp(L * log_gamma)
    bc = lambda w: jnp.broadcast_to(w[:, :, None], (H, L, RET_HEAD_DIM))
    return chunk_decay, decay_intra, bc(w_query), bc(w_state)


CONV_PAD = 8
XBC_BLK = 512


def _ssd_chunks(xbc_ref, z_ref, dt_ref, cw_ref, cbias_ref, dtb_ref,
                alog_ref, dskip_ref, nw_ref, tri_ref, exp_ref, o_ref, xpad_ref, st_ref, cb, tick):
    L = CHUNK
    gw = SSM_WIDTH // SSM_GROUPS
    rows_i = lax.broadcasted_iota(I32, (L, L), 0)
    cols_i = lax.broadcasted_iota(I32, (L, L), 1)
    causal = rows_i >= cols_i
    lo_lane = lax.broadcasted_iota(I32, (L, LANES), 1) < SSM_HEAD_DIM
    tri = tri_ref[...]
    expand = exp_ref[...]
    a_neg = -jnp.exp(alog_ref[...])

    for i in range(cb):
        sl = slice(i * L, (i + 1) * L)
        xpad_ref[CONV_PAD:CONV_PAD + L, :] = xbc_ref[sl, :]
        u_parts = []
        for b in range(SSM_CONV_DIM // XBC_BLK):
            cs = slice(b * XBC_BLK, (b + 1) * XBC_BLK)
            acc = cbias_ref[:, cs]
            for t in range(SSM_CONV):
                r0 = CONV_PAD - (SSM_CONV - 1) + t
                acc = acc + xpad_ref[r0:r0 + L, cs] * cw_ref[t:t + 1, cs]
            u_parts.append(_silu(acc))
            tick()
        xpad_ref[0:CONV_PAD, :] = xpad_ref[L:L + CONV_PAD, :]
        xs = jnp.concatenate(u_parts[:2], axis=-1)
        bm = u_parts[2][:, :SSM_GROUPS * SSM_STATE]
        cm = u_parts[2][:, SSM_GROUPS * SSM_STATE:]

        dt_in = dt_ref[sl, :] + dtb_ref[...]
        dt = jnp.maximum(dt_in, 0.0) + jnp.log1p(jnp.exp(-jnp.abs(dt_in)))
        a = dt * a_neg
        a1, a2, a3 = _split3(a)
        a_cs = _dot(tri, a1) + _dot(tri, a2) + _dot(tri, a3)
        d1, d2, d3 = _split3(dt)
        dt_e = _dot(d1, expand) + _dot(d2, expand) + _dot(d3, expand)
        c1, c2, c3 = _split3(a_cs)
        acs_e = _dot(c1, expand) + _dot(c2, expand) + _dot(c3, expand)
        tick()
        last = acs_e[L - 1:L, :]
        to_end = jnp.exp(last - acs_e)
        from_start = jnp.exp(acs_e)
        chunk_decay = jnp.exp(last)
        x_dt = xs * dt_e
        xw = (x_dt * to_end).astype(BF16)
        acs_t = a_cs.T
        cmb = cm.astype(BF16)
        bmb = bm.astype(BF16)

        ys = []
        for g in range(SSM_GROUPS):
            ns = slice(g * SSM_STATE, (g + 1) * SSM_STATE)
            gs = slice(g * gw, (g + 1) * gw)
            cg = cmb[:, ns]
            cbg = _dot_nt(cg, bmb[:, ns])
            st = st_ref[g]
            y_off = _dot(cg, st.astype(BF16))
            new = _dot(bm[:, ns].T.astype(BF16), xw[:, gs])
            st_ref[g] = st * chunk_decay[:, gs] + new
            tick()
            for jp in range(gw // LANES):
                h0 = (g * gw + jp * LANES) // SSM_HEAD_DIM
                ms = []
                for hh in (h0, h0 + 1):
                    seg = a_cs[:, hh:hh + 1] - acs_t[hh:hh + 1, :]
                    dec = jnp.exp(jnp.where(causal, seg, -jnp.inf))
                    ms.append((cbg * dec).astype(BF16))
                lhs = jnp.concatenate(ms, axis=1)
                ls = slice(g * gw + jp * LANES, g * gw + (jp + 1) * LANES)
                xp = x_dt[:, ls]
                rhs = jnp.concatenate([jnp.where(lo_lane, xp, 0.0),
                                       jnp.where(lo_lane, 0.0, xp)], axis=0).astype(BF16)
                y_diag = _dot(lhs, rhs)
                ys.append(y_diag + y_off[:, jp * LANES:(jp + 1) * LANES] * from_start[:, ls])
                tick()
        y = jnp.concatenate(ys, axis=1) + xs * dskip_ref[...]
        y = y * _silu(z_ref[sl, :])
        outs = []
        for g in range(SSM_GROUPS):
            yg = y[:, g * gw:(g + 1) * gw]
            ms_ = jnp.mean(yg * yg, axis=-1, keepdims=True)
            outs.append(yg * lax.rsqrt(ms_ + EPS))
        o_ref[sl, :] = (jnp.concatenate(outs, axis=1) * nw_ref[...]).astype(BF16)


MIX_CB = 2
MIX_VMEM = 60 * 1024 * 1024
COL_Z = 4 * RET_WIDTH
COL_XBC = COL_Z + SSM_WIDTH
PROJ_WIDTHS = (RET_WIDTH, RET_WIDTH, RET_WIDTH, RET_WIDTH, SSM_WIDTH, SSM_CONV_DIM, LANES)
PROJ_SLAB = 256


def _mixer_body(cd_ref, x_ref, n1w_ref, wt_hbm, wdt_ref, cos_ref, sin_ref, dec_ref, wq_ref, ws_ref, rnw_ref,
                cw_ref, cbias_ref, dtb_ref, alog_ref, dskip_ref, snw_ref, tri_ref, exp_ref,
                yret_ref, yssm_ref, w_ref, hn_ref, pq, pk, pv, pg, pz, pxbc, pdt,
                st_ret, xpad_ref, st_ssd, sem, *, cb):
    @pl.when(pl.program_id(0) == 0)
    def _():
        cp = pltpu.make_async_copy(wt_hbm, w_ref, sem)
        cp.start()
        st_ret[...] = jnp.zeros_like(st_ret)
        xpad_ref[0:CONV_PAD, :] = jnp.zeros((CONV_PAD, SSM_CONV_DIM), F32)
        st_ssd[...] = jnp.zeros_like(st_ssd)
        cp.wait()

    def slabs(ref, col0):
        def slab(lo):
            def go():
                ref[:, lo:lo + PROJ_SLAB] = _dot_nt(hn_ref[...], w_ref[col0 + lo:col0 + lo + PROJ_SLAB, :])
            return go
        return [slab(lo) for lo in range(0, ref.shape[1], PROJ_SLAB)]

    x = x_ref[...]
    ms = jnp.mean(x * x, axis=-1, keepdims=True)
    hn_ref[...] = (x * lax.rsqrt(ms + EPS) * n1w_ref[...]).astype(BF16)
    for piece in slabs(pxbc, COL_XBC) + slabs(pz, COL_Z):
        piece()
    pdt[...] = _dot_nt(hn_ref[...], wdt_ref[...])

    pieces = slabs(pq, 0) + slabs(pk, RET_WIDTH) + slabs(pv, 2 * RET_WIDTH) + slabs(pg, 3 * RET_WIDTH)
    n_pieces = len(pieces)
    n_ticks = cb * (SSM_CONV_DIM // XBC_BLK + 1 + SSM_GROUPS + SSM_WIDTH // LANES)
    calls = [0]

    def tick():
        calls[0] += 1
        while pieces and (n_pieces - len(pieces)) * n_ticks < calls[0] * n_pieces:
            pieces.pop(0)()

    _ssd_chunks(pxbc, pz, pdt, cw_ref, cbias_ref, dtb_ref, alog_ref, dskip_ref, snw_ref, tri_ref,
                exp_ref, yssm_ref, xpad_ref, st_ssd, cb, tick)
    while pieces:
        pieces.pop(0)()
    _retention_chunks(cd_ref, pq, pk, pv, pg, cos_ref, sin_ref, dec_ref, wq_ref, ws_ref, rnw_ref,
                      yret_ref, st_ret, cb, lambda: None)


def _mixer(x, n1w, w_main_t, w_dt_t, cos, sin, ret_nw, ret_consts,
           conv_w, conv_b, dt_bias, a_log, d_skip_e, ssm_nw, tri, expand):
    T = x.shape[0]
    cb = min(MIX_CB, T // CHUNK)
    rows = cb * CHUNK
    chunk_decay, decay_intra, w_query, w_state = ret_consts
    hd = RET_HEAD_DIM
    half = hd // 2
    full = lambda shape: pl.BlockSpec(shape, lambda c, cd: (0,) * len(shape))
    rowblk = lambda width: pl.BlockSpec((rows, width), lambda c, cd: (c, 0))
    scratch = lambda width: pltpu.VMEM((rows, width), F32)
    grid_spec = pltpu.PrefetchScalarGridSpec(
        num_scalar_prefetch=1,
        grid=(T // rows,),
        in_specs=[
            rowblk(D_MODEL),
            full((1, D_MODEL)),
            pl.BlockSpec(memory_space=pl.ANY),
            full((LANES, D_MODEL)),
            rowblk(half), rowblk(half),
            full((RET_HEADS, CHUNK, CHUNK)),
            full((RET_HEADS, CHUNK, hd)),
            full((RET_HEADS, CHUNK, hd)),
            full((1, RET_WIDTH)),
            full((SSM_CONV, SSM_CONV_DIM)),
            full((1, SSM_CONV_DIM)),
            full((1, LANES)),
            full((1, LANES)),
            full((1, SSM_WIDTH)),
            full((1, SSM_WIDTH)),
            full((CHUNK, CHUNK)),
            full((LANES, SSM_WIDTH)),
        ],
        out_specs=[rowblk(RET_WIDTH), rowblk(SSM_WIDTH)],
        scratch_shapes=[
            pltpu.VMEM((MAIN_PROJ, D_MODEL), BF16),
            pltpu.VMEM((rows, D_MODEL), BF16),
            *[scratch(w) for w in PROJ_WIDTHS],
            pltpu.VMEM((RET_HEADS, hd, hd), F32),
            pltpu.VMEM((CHUNK + CONV_PAD, SSM_CONV_DIM), F32),
            pltpu.VMEM((SSM_GROUPS, SSM_STATE, SSM_WIDTH // SSM_GROUPS), F32),
            pltpu.SemaphoreType.DMA(()),
        ],
    )
    return pl.pallas_call(
        functools.partial(_mixer_body, cb=cb),
        grid_spec=grid_spec,
        out_shape=[jax.ShapeDtypeStruct((T, RET_WIDTH), BF16),
                   jax.ShapeDtypeStruct((T, SSM_WIDTH), BF16)],
        compiler_params=_cparams(("arbitrary",), MIX_VMEM),
        name="mixer",
    )(chunk_decay, x, n1w, w_main_t, w_dt_t, cos, sin, decay_intra, w_query, w_state, ret_nw,
      conv_w, conv_b, dt_bias, a_log, d_skip_e, ssm_nw, tri, expand)


OUT_TM = 512
ROUTE_ROWS = 8 + N_EXPERTS


def _out_router_body(x_ref, yr_ref, ys_ref, wo_ref, nw_ref, wr_ref, br_ref,
                     h1_ref, h2t_ref, cw_ref, ids_ref):
    tm = x_ref.shape[0]
    h1 = x_ref[...] + _dot(yr_ref[...], wo_ref[0:RET_WIDTH, :]) + _dot(ys_ref[...], wo_ref[RET_WIDTH:, :])
    h1_ref[...] = h1
    ms = jnp.mean(h1 * h1, axis=-1, keepdims=True)
    h2 = h1 * lax.rsqrt(ms + EPS) * nw_ref[...]
    for s in range(Y_ROWS):
        h2t_ref[:, s, :] = h2[:, s * LANES:(s + 1) * LANES]

    logits = _dot_nt(wr_ref[...], h2.astype(BF16)) + br_ref[...]
    row = lax.broadcasted_iota(I32, (8, tm), 0)
    lg = jnp.where(row < N_GROUPS, logits[0:8], -jnp.inf)
    m = jnp.max(lg, axis=0, keepdims=True)
    p_sel = 1.0 / jnp.sum(jnp.exp(lg - m), axis=0, keepdims=True)
    g_sel = jnp.min(jnp.where(lg == m, row, 8), axis=0, keepdims=True)
    le = jnp.zeros((GROUP_EXPERTS, tm), F32)
    for g in range(N_GROUPS):
        le = jnp.where(g_sel == g, logits[8 + g * GROUP_EXPERTS:8 + (g + 1) * GROUP_EXPERTS], le)
    m2 = jnp.max(le, axis=0, keepdims=True)
    ee = jnp.exp(le - m2)
    pe = ee / jnp.sum(ee, axis=0, keepdims=True)
    v1 = jnp.max(pe, axis=0, keepdims=True)
    i1 = jnp.min(jnp.where(pe == v1, row, 8), axis=0, keepdims=True)
    pe2 = jnp.where(row == i1, -1.0, pe)
    v2 = jnp.max(pe2, axis=0, keepdims=True)
    i2 = jnp.min(jnp.where(pe2 == v2, row, 8), axis=0, keepdims=True)
    tw = v1 + v2
    c1 = v1 / tw * p_sel
    c2 = v2 / tw * p_sel
    e1 = g_sel * GROUP_EXPERTS + i1
    e2 = g_sel * GROUP_EXPERTS + i2
    ids = jnp.where(row == 0, e1, jnp.where(row == 1, e2, 0))
    for b in range(tm // SORT_BLK):
        ids_ref[b] = ids[:, b * SORT_BLK:(b + 1) * SORT_BLK]
    cw8 = jnp.where(row == 0, c1, jnp.where(row == 1, c2, 0.0))
    cw = jnp.concatenate([cw8, jnp.zeros((LANES - 8, tm), F32)], axis=0)
    cw_ref[...] = cw.T


def _out_router(x, y_ret, y_ssm, w_out, nw, wr_t, br):
    T = x.shape[0]
    tm = min(OUT_TM, T)
    full = lambda shape: pl.BlockSpec(shape, lambda i: (0,) * len(shape))
    return pl.pallas_call(
        _out_router_body,
        grid=(T // tm,),
        in_specs=[
            pl.BlockSpec((tm, D_MODEL), lambda i: (i, 0)),
            pl.BlockSpec((tm, RET_WIDTH), lambda i: (i, 0)),
            pl.BlockSpec((tm, SSM_WIDTH), lambda i: (i, 0)),
            pl.BlockSpec((D_MODEL, D_MODEL), lambda i: (0, 0), pipeline_mode=pl.Buffered(1)),
            full((1, D_MODEL)),
            full((LANES, D_MODEL)),
            full((LANES, 1)),
        ],
        out_specs=[
            pl.BlockSpec((tm, D_MODEL), lambda i: (i, 0)),
            pl.BlockSpec((tm, Y_ROWS, LANES), lambda i: (i, 0, 0)),
            pl.BlockSpec((tm, LANES), lambda i: (i, 0)),
            pl.BlockSpec((tm // SORT_BLK, 8, SORT_BLK), lambda i: (i, 0, 0)),
        ],
        out_shape=[
            jax.ShapeDtypeStruct((T, D_MODEL), F32),
            jax.ShapeDtypeStruct((T, Y_ROWS, LANES), F32),
            jax.ShapeDtypeStruct((T, LANES), F32),
            jax.ShapeDtypeStruct((T // SORT_BLK, 8, SORT_BLK), I32),
        ],
        compiler_params=_cparams(("parallel",)),
        name="out_router",
    )(x, y_ret, y_ssm, w_out, nw, wr_t, br)


SORT_BLK = 256
MOE_TM = 256


def _sort_index_body(ids_ref, tri_ref, ltri_ref, dest_ref, cnt_ref, rank_ref):
    nblk = ids_ref.shape[0]
    row_e = lax.broadcasted_iota(I32, (N_EXPERTS, SORT_BLK), 0)
    row8 = lax.broadcasted_iota(I32, (8, SORT_BLK), 0)

    def onehots(b):
        ids = ids_ref[b]
        return row_e == ids[0:1], row_e == ids[1:2]

    def rank_blk(b, carry):
        oh1, oh2 = onehots(b)
        ohf = jnp.where(oh1 | oh2, 1.0, 0.0)
        incl = _dot(ohf.astype(BF16), tri_ref[...])
        base = carry + incl - 1.0
        r1 = jnp.sum(jnp.where(oh1, base, 0.0), axis=0, keepdims=True)
        r2 = jnp.sum(jnp.where(oh2, base, 0.0), axis=0, keepdims=True)
        rank_ref[b] = jnp.where(row8 == 0, r1, jnp.where(row8 == 1, r2, 0.0))
        return carry + jnp.sum(ohf, axis=1, keepdims=True)

    cnt = lax.fori_loop(0, nblk, rank_blk, jnp.zeros((N_EXPERTS, 1), F32))
    cnt_ref[...] = jnp.broadcast_to(cnt, cnt_ref.shape)
    tiles = jnp.floor((cnt + (MOE_TM - 1.0)) / MOE_TM)
    tiles_b = jnp.broadcast_to(tiles, (N_EXPERTS, LANES)).astype(BF16)
    off = _dot(ltri_ref[...], tiles_b)[:, 0:1] * MOE_TM

    def dest_blk(b, carry):
        oh1, oh2 = onehots(b)
        o1 = jnp.sum(jnp.where(oh1, off, 0.0), axis=0, keepdims=True)
        o2 = jnp.sum(jnp.where(oh2, off, 0.0), axis=0, keepdims=True)
        d = rank_ref[b] + jnp.where(row8 == 0, o1, jnp.where(row8 == 1, o2, 0.0))
        dest_ref[b] = d.astype(I32)
        return carry

    lax.fori_loop(0, nblk, dest_blk, 0)


def _sort_index(ids, tri_u, ltri):
    nblk = ids.shape[0]
    return pl.pallas_call(
        _sort_index_body,
        out_shape=[jax.ShapeDtypeStruct((nblk, 8, SORT_BLK), I32),
                   jax.ShapeDtypeStruct((N_EXPERTS, LANES), F32)],
        scratch_shapes=[pltpu.VMEM((nblk, 8, SORT_BLK), F32)],
        compiler_params=_cparams(None),
        name="sort_index",
    )(ids, tri_u, ltri)


ROW_DMA_PRIORITY = 1


def _moe_body(te_ref, tf_ref, nx_ref, nu_ref, dest_ref, lo_ref, hi_ref,
              h2t_ref, wg_ref, wu_ref, wd_ref, yt_ref,
              srow_ref, wcount, xbuf, ystage, wsg, wsu, wsd, wgb_ref, wub_ref, wdb_ref, gsem, ssem, wsem):
    i = pl.program_id(0)
    nu = nu_ref[0]
    T = h2t_ref.shape[0]

    def gather_row(tile, slot, r):
        tok = srow_ref[tile * MOE_TM + r] & (T - 1)
        return pltpu.make_async_copy(h2t_ref.at[tok], xbuf.at[slot, :, r, :], gsem.at[slot])

    def scatter_row(tile, slot, r):
        return pltpu.make_async_copy(ystage.at[slot, :, r, :], yt_ref.at[srow_ref[tile * MOE_TM + r]],
                                     ssem.at[slot])

    def gather_tile(slot):
        return pltpu.make_async_copy(h2t_ref.at[pl.ds(0, MOE_TM)], h2t_ref.at[pl.ds(0, MOE_TM)], gsem.at[slot])

    def scatter_tile(slot):
        return pltpu.make_async_copy(yt_ref.at[pl.ds(0, MOE_TM)], yt_ref.at[pl.ds(0, MOE_TM)], ssem.at[slot])

    def weight_copies(e, ws):
        return (pltpu.make_async_copy(wg_ref.at[e], wsg.at[ws], wsem.at[ws, 0]),
                pltpu.make_async_copy(wu_ref.at[e], wsu.at[ws], wsem.at[ws, 1]),
                pltpu.make_async_copy(wd_ref.at[e], wsd.at[ws], wsem.at[ws, 2]))

    @pl.when(i == 0)
    def _():
        wcount[0] = 0
        for c in weight_copies(te_ref[0], 0):
            c.start()

        def tok(t, c):
            srow_ref[dest_ref[t]] = t
            srow_ref[dest_ref[T + t]] = T + t
            return c

        lax.fori_loop(0, T, tok, 0, unroll=8)

        def seg(e, c):
            def pad(r, c2):
                srow_ref[r] = 2 * T + (r & (MOE_TM - 1))
                return c2

            lax.fori_loop(lo_ref[e], hi_ref[e], pad, 0)
            return c

        lax.fori_loop(0, lo_ref.shape[0], seg, 0)

        ystage[...] = jnp.zeros_like(ystage)

        def spare(r, c):
            pltpu.make_async_copy(ystage.at[0, :, r, :], yt_ref.at[2 * T + r], ssem.at[0]).start()
            return c

        lax.fori_loop(0, MOE_TM, spare, 0)
        scatter_tile(0).wait()
        for r in range(MOE_TM):
            gather_row(0, 0, r).start(priority=ROW_DMA_PRIORITY)

    @pl.when(tf_ref[i] == 1)
    def _():
        ws = lax.rem(wcount[0], 2)
        wcount[0] = wcount[0] + 1
        for c in weight_copies(te_ref[i], ws):
            c.wait()

        @pl.when(nx_ref[i] >= 0)
        def _():
            for c in weight_copies(nx_ref[i], 1 - ws):
                c.start()

        wgb_ref[...] = wsg[ws].astype(BF16)
        wub_ref[...] = wsu[ws].astype(BF16)
        wdb_ref[...] = wsd[ws].astype(BF16)

    slot = lax.rem(i, 2)
    other = 1 - slot

    @pl.when(i < nu)
    def _():
        gather_tile(slot).wait()

        @pl.when(i >= 1)
        def _():
            scatter_tile(slot).wait()

        nxt = jnp.minimum(i + 1, nu - 1)
        prv = jnp.maximum(i - 1, 0)
        for r in range(MOE_TM):
            gather_row(nxt, other, r).start(priority=r % 2)
            scatter_row(prv, other, r).start(priority=(r + 1) % 2)

        x = jnp.concatenate([xbuf[slot, s].astype(BF16) for s in range(Y_ROWS)], axis=1)
        a = _dot(x, wgb_ref[...])
        u = _dot(x, wub_ref[...])
        act = (_silu(a) * u).astype(BF16)
        y = _dot(act, wdb_ref[...])
        for s in range(Y_ROWS):
            ystage[slot, s] = y[:, s * LANES:(s + 1) * LANES]

    @pl.when(i == nu)
    def _():
        gather_tile(slot).wait()
        scatter_tile(slot).wait()

        def last(r, c):
            scatter_row(nu - 1, other, r).start()
            return c

        lax.fori_loop(0, MOE_TM, last, 0)
        scatter_tile(other).wait()


def _moe(tile_expert, tile_first, next_expert, n_used, dest, pad_lo, pad_hi, h2t, w_gate, w_up, w_down):
    T = h2t.shape[0]
    assert T & (T - 1) == 0 and T >= MOE_TM, "token index is recovered from the row table by masking"
    n_tiles = tile_expert.shape[0] - 1
    hbm = pl.BlockSpec(memory_space=pl.ANY)
    grid_spec = pltpu.PrefetchScalarGridSpec(
        num_scalar_prefetch=7,
        grid=(n_tiles + 1,),
        in_specs=[hbm, hbm, hbm, hbm],
        out_specs=hbm,
        scratch_shapes=[pltpu.SMEM((n_tiles * MOE_TM,), I32),
                        pltpu.SMEM((1,), I32),
                        pltpu.VMEM((2, Y_ROWS, MOE_TM, LANES), F32),
                        pltpu.VMEM((2, Y_ROWS, MOE_TM, LANES), F32),
                        pltpu.VMEM((2, D_MODEL, D_EXPERT), F32),
                        pltpu.VMEM((2, D_MODEL, D_EXPERT), F32),
                        pltpu.VMEM((2, D_EXPERT, D_MODEL), F32),
                        pltpu.VMEM((D_MODEL, D_EXPERT), BF16),
                        pltpu.VMEM((D_MODEL, D_EXPERT), BF16),
                        pltpu.VMEM((D_EXPERT, D_MODEL), BF16),
                        pltpu.SemaphoreType.DMA((2,)),
                        pltpu.SemaphoreType.DMA((2,)),
                        pltpu.SemaphoreType.DMA((2, 3))],
    )
    return pl.pallas_call(
        _moe_body,
        grid_spec=grid_spec,
        out_shape=jax.ShapeDtypeStruct((2 * T + MOE_TM, Y_ROWS, LANES), F32),
        compiler_params=_cparams(("arbitrary",)),
        name="moe",
    )(tile_expert, tile_first, next_expert, n_used, dest, pad_lo, pad_hi, h2t, w_gate, w_up, w_down)


COMB_TM = 512


def _combine_body(h1_ref, y0_ref, y1_ref, cw_ref, nw_ref, o_ref):
    tm = h1_ref.shape[0]
    c0 = cw_ref[:, 0:1]
    c1 = cw_ref[:, 1:2]
    cols = []
    for s in range(Y_ROWS):
        moe = c0 * y0_ref[:, s, :] + c1 * y1_ref[:, s, :]
        cols.append(h1_ref[:, s * LANES:(s + 1) * LANES] + moe)
    h = jnp.concatenate(cols, axis=1)
    ms = jnp.mean(h * h, axis=-1, keepdims=True)
    o_ref[...] = h * lax.rsqrt(ms + EPS) * nw_ref[...]


def _combine(h1, y_tok, cw, nw):
    T = h1.shape[0]
    tm = min(COMB_TM, T)
    nt = T // tm
    return pl.pallas_call(
        _combine_body,
        grid=(nt,),
        in_specs=[
            pl.BlockSpec((tm, D_MODEL), lambda i: (i, 0)),
            pl.BlockSpec((tm, Y_ROWS, LANES), lambda i: (i, 0, 0)),
            pl.BlockSpec((tm, Y_ROWS, LANES), lambda i: (i + nt, 0, 0)),
            pl.BlockSpec((tm, LANES), lambda i: (i, 0)),
            pl.BlockSpec((1, D_MODEL), lambda i: (0, 0)),
        ],
        out_specs=pl.BlockSpec((tm, D_MODEL), lambda i: (i, 0)),
        out_shape=jax.ShapeDtypeStruct((T, D_MODEL), F32),
        compiler_params=_cparams(("parallel",)),
        name="combine",
    )(h1, y_tok, y_tok, cw, nw)


def _tile_plan(cnt, n_tiles):
    tiles = (cnt + (MOE_TM - 1)) // MOE_TM
    ends = jnp.cumsum(tiles)
    starts = ends - tiles
    n_used = ends[-1]
    step = jnp.arange(n_tiles + 1, dtype=I32)
    tile = jnp.minimum(step, jnp.maximum(n_used - 1, 0))
    tile_expert = jnp.sum((ends[None, :] <= tile[:, None]).astype(I32), axis=1)
    tile_first = ((step == starts[tile_expert]) & (step < n_used)).astype(I32)
    nxt_tile = ends[tile_expert]
    nxt_expert = jnp.sum((ends[None, :] <= nxt_tile[:, None]).astype(I32), axis=1)
    next_expert = jnp.where(nxt_tile < n_used, nxt_expert, -1).astype(I32)
    pad_lo = starts * MOE_TM + cnt
    pad_hi = ends * MOE_TM
    return (tile_expert, tile_first, next_expert, n_used.reshape(1).astype(I32),
            pad_lo.astype(I32), pad_hi.astype(I32))


def kernel(x, positions, norm1_w, w_in, conv_w, conv_b, dt_bias, a_log, d_skip, ret_norm_w,
           ssm_norm_w, w_out, norm2_w, w_router_group, b_router_group, w_router_expert,
           b_router_expert, w_expert_gate, w_expert_up, w_expert_down, final_norm_w):
    B, T, D = x.shape
    assert B == 1 and D == D_MODEL and T % CHUNK == 0
    xf = x.reshape(T, D)
    pad_l = lambda v: jnp.pad(v, ((0, 0), (0, LANES - v.shape[-1])))

    w_in_t = jnp.swapaxes(w_in[0], 0, 1)
    w_main = _cast_rows_bf16(w_in_t, MAIN_PROJ, CAST_ROWS)
    w_dt = jnp.pad(w_in_t[MAIN_PROJ:], ((0, LANES - (w_in_t.shape[0] - MAIN_PROJ)), (0, 0))).astype(BF16)
    half = RET_HEAD_DIM // 2
    inv = (ROPE_THETA ** (-jnp.arange(half, dtype=F32) / half)).reshape(1, half)
    tri = (jnp.arange(CHUNK)[:, None] >= jnp.arange(CHUNK)[None, :]).astype(BF16)
    expand = (jnp.arange(LANES)[:, None] == jnp.arange(SSM_WIDTH)[None, :] // SSM_HEAD_DIM).astype(BF16)
    d_skip_e = jnp.repeat(d_skip[0], SSM_HEAD_DIM).reshape(1, SSM_WIDTH)
    wr_t = jnp.zeros((LANES, D), F32)
    wr_t = wr_t.at[0:N_GROUPS].set(w_router_group[0].T).at[8:8 + N_EXPERTS].set(w_router_expert[0].T)
    br = jnp.zeros((LANES,), F32)
    br = br.at[0:N_GROUPS].set(b_router_group[0]).at[8:8 + N_EXPERTS].set(b_router_expert[0])
    tri_u = (jnp.arange(SORT_BLK)[:, None] <= jnp.arange(SORT_BLK)[None, :]).astype(BF16)
    ltri = (jnp.arange(N_EXPERTS)[:, None] > jnp.arange(N_EXPERTS)[None, :]).astype(BF16)

    cos, sin = _rope_tables(positions.reshape(T, 1).astype(F32), inv)
    y_ret, y_ssm = _mixer(xf, norm1_w[0].reshape(1, D), w_main, w_dt, cos, sin,
                          ret_norm_w[0].reshape(1, RET_WIDTH), _retention_consts(),
                          conv_w[0], conv_b[0].reshape(1, -1), pad_l(dt_bias[0].reshape(1, -1)),
                          pad_l(a_log[0].reshape(1, -1)), d_skip_e, ssm_norm_w[0].reshape(1, -1), tri, expand)
    h1, h2t, cw, ids = _out_router(xf, y_ret, y_ssm, w_out[0].astype(BF16), norm2_w[0].reshape(1, D),
                                   wr_t.astype(BF16), br.reshape(LANES, 1))

    dest_blk, cnt = _sort_index(ids, tri_u, ltri)
    dest = dest_blk[:, 0:2, :].transpose(1, 0, 2).reshape(2 * T)
    n_tiles = (2 * T) // MOE_TM + N_EXPERTS
    tile_expert, tile_first, next_expert, n_used, pad_lo, pad_hi = _tile_plan(cnt[:, 0].astype(I32), n_tiles)
    y_tok = _moe(tile_expert, tile_first, next_expert, n_used, dest, pad_lo, pad_hi, h2t,
                 w_expert_gate[0], w_expert_up[0], w_expert_down[0])
    out = _combine(h1, y_tok, cw, final_norm_w.reshape(1, D))
    return out.reshape(B, T, D)
```

```python
import functools

import jax
import jax.numpy as jnp
from jax import lax
from jax.experimental import pallas as pl
from jax.experimental.pallas import tpu as pltpu

F32 = jnp.float32
BF16 = jnp.bfloat16
I32 = jnp.int32

D_MODEL = 2048
EPS = 1e-6
CHUNK = 128
RET_HEADS = 4
RET_HEAD_DIM = 256
RET_WIDTH = RET_HEADS * RET_HEAD_DIM
ROPE_THETA = 10000.0
SSM_WIDTH = 1024
SSM_HEAD_DIM = 64
SSM_HEADS = SSM_WIDTH // SSM_HEAD_DIM
SSM_GROUPS = 2
SSM_STATE = 128
SSM_CONV = 4
SSM_CONV_DIM = SSM_WIDTH + 2 * SSM_GROUPS * SSM_STATE
MAIN_PROJ = 4 * RET_WIDTH + SSM_WIDTH + SSM_CONV_DIM
N_GROUPS = 4
GROUP_EXPERTS = 8
N_EXPERTS = N_GROUPS * GROUP_EXPERTS
D_EXPERT = 512
LANES = 128
Y_ROWS = D_MODEL // LANES

VMEM_LIMIT = 56 * 1024 * 1024


def _cparams(sem, vmem=VMEM_LIMIT):
    return pltpu.CompilerParams(dimension_semantics=sem, vmem_limit_bytes=vmem)


def _silu(x):
    return x * (1.0 / (1.0 + jnp.exp(-x)))


def _dot(a, b):
    return jnp.dot(a, b, preferred_element_type=F32)


def _dot_nt(a, b):
    return lax.dot_general(a, b, (((1,), (1,)), ((), ())), preferred_element_type=F32)


def _split3(a):
    a1 = a.astype(BF16)
    r1 = a - a1.astype(F32)
    a2 = r1.astype(BF16)
    a3 = (r1 - a2.astype(F32)).astype(BF16)
    return a1, a2, a3


def _cast_body(w_ref, o_ref):
    o_ref[...] = w_ref[...].astype(BF16)


def _cast_rows_bf16(w, n_rows, tr):
    cols = w.shape[1]
    return pl.pallas_call(
        _cast_body,
        grid=(n_rows // tr,),
        in_specs=[pl.BlockSpec((tr, cols), lambda j: (j, 0))],
        out_specs=pl.BlockSpec((tr, cols), lambda j: (j, 0)),
        out_shape=jax.ShapeDtypeStruct((n_rows, cols), BF16),
        compiler_params=_cparams(("parallel",)),
        name="cast_bf16",
    )(w)


CAST_ROWS = 512


def _rope_body(pos_ref, inv_ref, cos_ref, sin_ref):
    ang = pos_ref[...] * inv_ref[...]
    cos_ref[...] = jnp.cos(ang)
    sin_ref[...] = jnp.sin(ang)


def _rope_tables(pos, inv):
    T = pos.shape[0]
    tm = min(1024, T)
    half = inv.shape[1]
    return pl.pallas_call(
        _rope_body,
        grid=(T // tm,),
        in_specs=[pl.BlockSpec((tm, 1), lambda i: (i, 0)),
                  pl.BlockSpec((1, half), lambda i: (0, 0))],
        out_specs=[pl.BlockSpec((tm, half), lambda i: (i, 0))] * 2,
        out_shape=[jax.ShapeDtypeStruct((T, half), F32)] * 2,
        compiler_params=_cparams(("parallel",)),
        name="rope_tables",
    )(pos, inv)


def _retention_chunks(cd_ref, q_ref, k_ref, v_ref, g_ref, cos_ref, sin_ref,
                      dec_ref, wq_ref, ws_ref, nw_ref, o_ref, st_ref, cb, tick):
    hd = RET_HEAD_DIM
    half = hd // 2
    for i in range(cb):
        sl = slice(i * CHUNK, (i + 1) * CHUNK)
        cos = cos_ref[sl, :]
        sin = sin_ref[sl, :]

        def rope(x):
            x1, x2 = x[:, :half], x[:, half:]
            return jnp.concatenate([x1 * cos - x2 * sin, x2 * cos + x1 * sin], axis=-1)

        for h in range(RET_HEADS):
            hs = slice(h * hd, (h + 1) * hd)
            q = rope(q_ref[sl, hs])
            k = rope(k_ref[sl, hs]) * (hd ** -0.5)
            v = v_ref[sl, hs]
            qb = q.astype(BF16)
            kb = k.astype(BF16)
            scores = _dot_nt(qb, kb) * dec_ref[h]
            y = _dot(scores.astype(BF16), v.astype(BF16))
            st = st_ref[h]
            y = y + _dot(qb, st.astype(BF16)) * wq_ref[h]
            vw = (v * ws_ref[h]).astype(BF16)
            new = _dot(k.T.astype(BF16), vw)
            st_ref[h] = st * cd_ref[h] + new
            mu = jnp.mean(y, axis=-1, keepdims=True)
            d = y - mu
            var = jnp.mean(d * d, axis=-1, keepdims=True)
            yn = d * lax.rsqrt(var + EPS)
            o_ref[sl, hs] = (yn * nw_ref[:, hs] * _silu(g_ref[sl, hs])).astype(BF16)
            tick()


def _retention_consts():
    H, L = RET_HEADS, CHUNK
    log_gamma = jnp.log1p(-(2.0 ** (-5.0 - jnp.arange(H, dtype=F32))))
    idx = jnp.arange(L, dtype=F32)
    diff = idx[:, None] - idx[None, :]
    causal = diff >= 0
    decay_intra = jnp.where(causal[None], jnp.exp(jnp.where(causal, diff, 0.0)[None] * log_gamma[:, None, None]), 0.0)
    w_state = jnp.exp((L - 1.0 - idx)[None, :] * log_gamma[:, None])
    w_query = jnp.exp((idx + 1.0)[None, :] * log_gamma[:, None])
    chunk_decay = jnp.exp(L * log_gamma)
    bc = lambda w: jnp.broadcast_to(w[:, :, None], (H, L, RET_HEAD_DIM))
    return chunk_decay, decay_intra, bc(w_query), bc(w_state)


CONV_PAD = 8
XBC_BLK = 512


def _ssd_chunks(xbc_ref, z_ref, dt_ref, cw_ref, cbias_ref, dtb_ref,
                alog_ref, dskip_ref, nw_ref, tri_ref, o_ref, xpad_ref, st_ref, cb, tick):
    L = CHUNK
    gw = SSM_WIDTH // SSM_GROUPS
    rows_i = lax.broadcasted_iota(I32, (L, L), 0)
    cols_i = lax.broadcasted_iota(I32, (L, L), 1)
    causal = rows_i >= cols_i
    lo_lane = lax.broadcasted_iota(I32, (L, LANES), 1) < SSM_HEAD_DIM
    tri = tri_ref[...]
    a_neg = -jnp.exp(alog_ref[...])

    for i in range(cb):
        sl = slice(i * L, (i + 1) * L)
        xpad_ref[CONV_PAD:CONV_PAD + L, :] = xbc_ref[sl, :]
        u_parts = []
        for b in range(SSM_CONV_DIM // XBC_BLK):
            cs = slice(b * XBC_BLK, (b + 1) * XBC_BLK)
            acc = cbias_ref[:, cs]
            for t in range(SSM_CONV):
                r0 = CONV_PAD - (SSM_CONV - 1) + t
                acc = acc + xpad_ref[r0:r0 + L, cs] * cw_ref[t:t + 1, cs]
            u_parts.append(_silu(acc))
            tick()
        xpad_ref[0:CONV_PAD, :] = xpad_ref[L:L + CONV_PAD, :]
        xs = jnp.concatenate(u_parts[:2], axis=-1)
        bm = u_parts[2][:, :SSM_GROUPS * SSM_STATE]
        cm = u_parts[2][:, SSM_GROUPS * SSM_STATE:]

        dt_in = dt_ref[sl, :] + dtb_ref[...]
        dt = jnp.maximum(dt_in, 0.0) + jnp.log1p(jnp.exp(-jnp.abs(dt_in)))
        a = dt * a_neg
        a1, a2, a3 = _split3(a)
        a_cs = _dot(tri, a1) + _dot(tri, a2) + _dot(tri, a3)
        last = a_cs[L - 1:L, :]

        def per_head_lanes(v):
            lo = lo_lane[0:v.shape[0], :]
            return jnp.concatenate([jnp.where(lo, v[:, 2 * p:2 * p + 1], v[:, 2 * p + 1:2 * p + 2])
                                    for p in range(SSM_HEADS // 2)], axis=1)

        dt_e = per_head_lanes(dt)
        to_end = per_head_lanes(jnp.exp(last - a_cs))
        from_start = per_head_lanes(jnp.exp(a_cs))
        chunk_decay = per_head_lanes(jnp.exp(last))
        tick()
        x_dt = xs * dt_e
        xw = (x_dt * to_end).astype(BF16)
        acs_t = a_cs.T
        cmb = cm.astype(BF16)
        bmb = bm.astype(BF16)

        ys = []
        for g in range(SSM_GROUPS):
            ns = slice(g * SSM_STATE, (g + 1) * SSM_STATE)
            gs = slice(g * gw, (g + 1) * gw)
            cg = cmb[:, ns]
            cbg = _dot_nt(cg, bmb[:, ns])
            st = st_ref[g]
            y_off = _dot(cg, st.astype(BF16))
            new = _dot(bm[:, ns].T.astype(BF16), xw[:, gs])
            st_ref[g] = st * chunk_decay[:, gs] + new
            tick()
            for jp in range(gw // LANES):
                h0 = (g * gw + jp * LANES) // SSM_HEAD_DIM
                ms = []
                for hh in (h0, h0 + 1):
                    seg = a_cs[:, hh:hh + 1] - acs_t[hh:hh + 1, :]
                    dec = jnp.exp(jnp.where(causal, seg, -jnp.inf))
                    ms.append((cbg * dec).astype(BF16))
                lhs = jnp.concatenate(ms, axis=1)
                ls = slice(g * gw + jp * LANES, g * gw + (jp + 1) * LANES)
                xp = x_dt[:, ls]
                rhs = jnp.concatenate([jnp.where(lo_lane, xp, 0.0),
                                       jnp.where(lo_lane, 0.0, xp)], axis=0).astype(BF16)
                y_diag = _dot(lhs, rhs)
                ys.append(y_diag + y_off[:, jp * LANES:(jp + 1) * LANES] * from_start[:, ls])
                tick()
        y = jnp.concatenate(ys, axis=1) + xs * dskip_ref[...]
        y = y * _silu(z_ref[sl, :])
        outs = []
        for g in range(SSM_GROUPS):
            yg = y[:, g * gw:(g + 1) * gw]
            ms_ = jnp.mean(yg * yg, axis=-1, keepdims=True)
            outs.append(yg * lax.rsqrt(ms_ + EPS))
        o_ref[sl, :] = (jnp.concatenate(outs, axis=1) * nw_ref[...]).astype(BF16)


MIX_CB = 2
MIX_VMEM = 60 * 1024 * 1024
COL_Z = 4 * RET_WIDTH
COL_XBC = COL_Z + SSM_WIDTH
PROJ_WIDTHS = (RET_WIDTH, RET_WIDTH, RET_WIDTH, RET_WIDTH, SSM_WIDTH, SSM_CONV_DIM, LANES)
PROJ_SLAB = 256


def _mixer_body(cd_ref, x_ref, n1w_ref, wt_hbm, wdt_ref, cos_ref, sin_ref, dec_ref, wq_ref, ws_ref, rnw_ref,
                cw_ref, cbias_ref, dtb_ref, alog_ref, dskip_ref, snw_ref, tri_ref,
                yret_ref, yssm_ref, w_ref, hn_ref, pq, pk, pv, pg, pz, pxbc, pdt,
                st_ret, xpad_ref, st_ssd, sem, *, cb):
    @pl.when(pl.program_id(0) == 0)
    def _():
        cp = pltpu.make_async_copy(wt_hbm, w_ref, sem)
        cp.start()
        st_ret[...] = jnp.zeros_like(st_ret)
        xpad_ref[0:CONV_PAD, :] = jnp.zeros((CONV_PAD, SSM_CONV_DIM), F32)
        st_ssd[...] = jnp.zeros_like(st_ssd)
        cp.wait()

    def slabs(ref, col0):
        def slab(lo):
            def go():
                ref[:, lo:lo + PROJ_SLAB] = _dot_nt(hn_ref[...], w_ref[col0 + lo:col0 + lo + PROJ_SLAB, :])
            return go
        return [slab(lo) for lo in range(0, ref.shape[1], PROJ_SLAB)]

    x = x_ref[...]
    ms = jnp.mean(x * x, axis=-1, keepdims=True)
    hn_ref[...] = (x * lax.rsqrt(ms + EPS) * n1w_ref[...]).astype(BF16)
    for piece in slabs(pxbc, COL_XBC) + slabs(pz, COL_Z):
        piece()
    pdt[...] = _dot_nt(hn_ref[...], wdt_ref[...])

    pieces = slabs(pq, 0) + slabs(pk, RET_WIDTH) + slabs(pv, 2 * RET_WIDTH) + slabs(pg, 3 * RET_WIDTH)
    n_pieces = len(pieces)
    n_ticks = cb * (SSM_CONV_DIM // XBC_BLK + 1 + SSM_GROUPS + SSM_WIDTH // LANES)
    calls = [0]

    def tick():
        calls[0] += 1
        while pieces and (n_pieces - len(pieces)) * n_ticks < calls[0] * n_pieces:
            pieces.pop(0)()

    _ssd_chunks(pxbc, pz, pdt, cw_ref, cbias_ref, dtb_ref, alog_ref, dskip_ref, snw_ref, tri_ref,
                yssm_ref, xpad_ref, st_ssd, cb, tick)
    while pieces:
        pieces.pop(0)()
    _retention_chunks(cd_ref, pq, pk, pv, pg, cos_ref, sin_ref, dec_ref, wq_ref, ws_ref, rnw_ref,
                      yret_ref, st_ret, cb, lambda: None)


def _mixer(x, n1w, w_main_t, w_dt_t, cos, sin, ret_nw, ret_consts,
           conv_w, conv_b, dt_bias, a_log, d_skip_e, ssm_nw, tri):
    T = x.shape[0]
    cb = min(MIX_CB, T // CHUNK)
    rows = cb * CHUNK
    chunk_decay, decay_intra, w_query, w_state = ret_consts
    hd = RET_HEAD_DIM
    half = hd // 2
    full = lambda shape: pl.BlockSpec(shape, lambda c, cd: (0,) * len(shape))
    rowblk = lambda width: pl.BlockSpec((rows, width), lambda c, cd: (c, 0))
    scratch = lambda width: pltpu.VMEM((rows, width), F32)
    grid_spec = pltpu.PrefetchScalarGridSpec(
        num_scalar_prefetch=1,
        grid=(T // rows,),
        in_specs=[
            rowblk(D_MODEL),
            full((1, D_MODEL)),
            pl.BlockSpec(memory_space=pl.ANY),
            full((LANES, D_MODEL)),
            rowblk(half), rowblk(half),
            full((RET_HEADS, CHUNK, CHUNK)),
            full((RET_HEADS, CHUNK, hd)),
            full((RET_HEADS, CHUNK, hd)),
            full((1, RET_WIDTH)),
            full((SSM_CONV, SSM_CONV_DIM)),
            full((1, SSM_CONV_DIM)),
            full((1, LANES)),
            full((1, LANES)),
            full((1, SSM_WIDTH)),
            full((1, SSM_WIDTH)),
            full((CHUNK, CHUNK)),
        ],
        out_specs=[rowblk(RET_WIDTH), rowblk(SSM_WIDTH)],
        scratch_shapes=[
            pltpu.VMEM((MAIN_PROJ, D_MODEL), BF16),
            pltpu.VMEM((rows, D_MODEL), BF16),
            *[scratch(w) for w in PROJ_WIDTHS],
            pltpu.VMEM((RET_HEADS, hd, hd), F32),
            pltpu.VMEM((CHUNK + CONV_PAD, SSM_CONV_DIM), F32),
            pltpu.VMEM((SSM_GROUPS, SSM_STATE, SSM_WIDTH // SSM_GROUPS), F32),
            pltpu.SemaphoreType.DMA(()),
        ],
    )
    return pl.pallas_call(
        functools.partial(_mixer_body, cb=cb),
        grid_spec=grid_spec,
        out_shape=[jax.ShapeDtypeStruct((T, RET_WIDTH), BF16),
                   jax.ShapeDtypeStruct((T, SSM_WIDTH), BF16)],
        compiler_params=_cparams(("arbitrary",), MIX_VMEM),
        name="mixer",
    )(chunk_decay, x, n1w, w_main_t, w_dt_t, cos, sin, decay_intra, w_query, w_state, ret_nw,
      conv_w, conv_b, dt_bias, a_log, d_skip_e, ssm_nw, tri)


OUT_TM = 512
ROUTE_ROWS = 8 + N_EXPERTS


def _out_router_body(x_ref, yr_ref, ys_ref, wo_ref, nw_ref, wr_ref, br_ref,
                     h1_ref, h2t_ref, cw_ref, ids_ref):
    tm = x_ref.shape[0]
    h1 = x_ref[...] + _dot(yr_ref[...], wo_ref[0:RET_WIDTH, :]) + _dot(ys_ref[...], wo_ref[RET_WIDTH:, :])
    h1_ref[...] = h1
    ms = jnp.mean(h1 * h1, axis=-1, keepdims=True)
    h2 = h1 * lax.rsqrt(ms + EPS) * nw_ref[...]
    for s in range(Y_ROWS):
        h2t_ref[:, s, :] = h2[:, s * LANES:(s + 1) * LANES]

    logits = _dot_nt(wr_ref[...], h2.astype(BF16)) + br_ref[...]
    row = lax.broadcasted_iota(I32, (8, tm), 0)
    lg = jnp.where(row < N_GROUPS, logits[0:8], -jnp.inf)
    m = jnp.max(lg, axis=0, keepdims=True)
    p_sel = 1.0 / jnp.sum(jnp.exp(lg - m), axis=0, keepdims=True)
    g_sel = jnp.min(jnp.where(lg == m, row, 8), axis=0, keepdims=True)
    le = jnp.zeros((GROUP_EXPERTS, tm), F32)
    for g in range(N_GROUPS):
        le = jnp.where(g_sel == g, logits[8 + g * GROUP_EXPERTS:8 + (g + 1) * GROUP_EXPERTS], le)
    m2 = jnp.max(le, axis=0, keepdims=True)
    ee = jnp.exp(le - m2)
    pe = ee / jnp.sum(ee, axis=0, keepdims=True)
    v1 = jnp.max(pe, axis=0, keepdims=True)
    i1 = jnp.min(jnp.where(pe == v1, row, 8), axis=0, keepdims=True)
    pe2 = jnp.where(row == i1, -1.0, pe)
    v2 = jnp.max(pe2, axis=0, keepdims=True)
    i2 = jnp.min(jnp.where(pe2 == v2, row, 8), axis=0, keepdims=True)
    tw = v1 + v2
    c1 = v1 / tw * p_sel
    c2 = v2 / tw * p_sel
    e1 = g_sel * GROUP_EXPERTS + i1
    e2 = g_sel * GROUP_EXPERTS + i2
    ids = jnp.where(row == 0, e1, jnp.where(row == 1, e2, 0))
    for b in range(tm // SORT_BLK):
        ids_ref[b] = ids[:, b * SORT_BLK:(b + 1) * SORT_BLK]
    cw8 = jnp.where(row == 0, c1, jnp.where(row == 1, c2, 0.0))
    cw = jnp.concatenate([cw8, jnp.zeros((LANES - 8, tm), F32)], axis=0)
    cw_ref[...] = cw.T


def _out_router(x, y_ret, y_ssm, w_out, nw, wr_t, br):
    T = x.shape[0]
    tm = min(OUT_TM, T)
    full = lambda shape: pl.BlockSpec(shape, lambda i: (0,) * len(shape))
    return pl.pallas_call(
        _out_router_body,
        grid=(T // tm,),
        in_specs=[
            pl.BlockSpec((tm, D_MODEL), lambda i: (i, 0)),
            pl.BlockSpec((tm, RET_WIDTH), lambda i: (i, 0)),
            pl.BlockSpec((tm, SSM_WIDTH), lambda i: (i, 0)),
            pl.BlockSpec((D_MODEL, D_MODEL), lambda i: (0, 0), pipeline_mode=pl.Buffered(1)),
            full((1, D_MODEL)),
            full((LANES, D_MODEL)),
            full((LANES, 1)),
        ],
        out_specs=[
            pl.BlockSpec((tm, D_MODEL), lambda i: (i, 0)),
            pl.BlockSpec((tm, Y_ROWS, LANES), lambda i: (i, 0, 0)),
            pl.BlockSpec((tm, LANES), lambda i: (i, 0)),
            pl.BlockSpec((tm // SORT_BLK, 8, SORT_BLK), lambda i: (i, 0, 0)),
        ],
        out_shape=[
            jax.ShapeDtypeStruct((T, D_MODEL), F32),
            jax.ShapeDtypeStruct((T, Y_ROWS, LANES), F32),
            jax.ShapeDtypeStruct((T, LANES), F32),
            jax.ShapeDtypeStruct((T // SORT_BLK, 8, SORT_BLK), I32),
        ],
        compiler_params=_cparams(("parallel",)),
        name="out_router",
    )(x, y_ret, y_ssm, w_out, nw, wr_t, br)


SORT_BLK = 256
MOE_TM = 256


def _sort_index_body(ids_ref, tri_ref, ltri_ref, dest_ref, cnt_ref, rank_ref):
    nblk = ids_ref.shape[0]
    row_e = lax.broadcasted_iota(I32, (N_EXPERTS, SORT_BLK), 0)
    row8 = lax.broadcasted_iota(I32, (8, SORT_BLK), 0)

    def onehots(b):
        ids = ids_ref[b]
        return row_e == ids[0:1], row_e == ids[1:2]

    def rank_blk(b, carry):
        oh1, oh2 = onehots(b)
        ohf = jnp.where(oh1 | oh2, 1.0, 0.0)
        incl = _dot(ohf.astype(BF16), tri_ref[...])
        base = carry + incl - 1.0
        r1 = jnp.sum(jnp.where(oh1, base, 0.0), axis=0, keepdims=True)
        r2 = jnp.sum(jnp.where(oh2, base, 0.0), axis=0, keepdims=True)
        rank_ref[b] = jnp.where(row8 == 0, r1, jnp.where(row8 == 1, r2, 0.0))
        return carry + jnp.sum(ohf, axis=1, keepdims=True)

    cnt = lax.fori_loop(0, nblk, rank_blk, jnp.zeros((N_EXPERTS, 1), F32))
    cnt_ref[...] = jnp.broadcast_to(cnt, cnt_ref.shape)
    tiles = jnp.floor((cnt + (MOE_TM - 1.0)) / MOE_TM)
    tiles_b = jnp.broadcast_to(tiles, (N_EXPERTS, LANES)).astype(BF16)
    off = _dot(ltri_ref[...], tiles_b)[:, 0:1] * MOE_TM

    def dest_blk(b, carry):
        oh1, oh2 = onehots(b)
        o1 = jnp.sum(jnp.where(oh1, off, 0.0), axis=0, keepdims=True)
        o2 = jnp.sum(jnp.where(oh2, off, 0.0), axis=0, keepdims=True)
        d = rank_ref[b] + jnp.where(row8 == 0, o1, jnp.where(row8 == 1, o2, 0.0))
        dest_ref[b] = d.astype(I32)
        return carry

    lax.fori_loop(0, nblk, dest_blk, 0)


def _sort_index(ids, tri_u, ltri):
    nblk = ids.shape[0]
    return pl.pallas_call(
        _sort_index_body,
        out_shape=[jax.ShapeDtypeStruct((nblk, 8, SORT_BLK), I32),
                   jax.ShapeDtypeStruct((N_EXPERTS, LANES), F32)],
        scratch_shapes=[pltpu.VMEM((nblk, 8, SORT_BLK), F32)],
        compiler_params=_cparams(None),
        name="sort_index",
    )(ids, tri_u, ltri)


ROW_DMA_PRIORITY = 1


def _moe_body(te_ref, tf_ref, nx_ref, nu_ref, dest_ref, lo_ref, hi_ref,
              h2t_ref, wg_ref, wu_ref, wd_ref, yt_ref,
              srow_ref, wcount, xbuf, ystage, wsg, wsu, wsd, wgb_ref, wub_ref, wdb_ref, gsem, ssem, wsem):
    i = pl.program_id(0)
    nu = nu_ref[0]
    T = h2t_ref.shape[0]

    def gather_row(tile, slot, r):
        tok = srow_ref[tile * MOE_TM + r] & (T - 1)
        return pltpu.make_async_copy(h2t_ref.at[tok], xbuf.at[slot, :, r, :], gsem.at[slot])

    def scatter_row(tile, slot, r):
        return pltpu.make_async_copy(ystage.at[slot, :, r, :], yt_ref.at[srow_ref[tile * MOE_TM + r]],
                                     ssem.at[slot])

    def gather_tile(slot):
        return pltpu.make_async_copy(h2t_ref.at[pl.ds(0, MOE_TM)], h2t_ref.at[pl.ds(0, MOE_TM)], gsem.at[slot])

    def scatter_tile(slot):
        return pltpu.make_async_copy(yt_ref.at[pl.ds(0, MOE_TM)], yt_ref.at[pl.ds(0, MOE_TM)], ssem.at[slot])

    def weight_copies(e, ws):
        return (pltpu.make_async_copy(wg_ref.at[e], wsg.at[ws], wsem.at[ws, 0]),
                pltpu.make_async_copy(wu_ref.at[e], wsu.at[ws], wsem.at[ws, 1]),
                pltpu.make_async_copy(wd_ref.at[e], wsd.at[ws], wsem.at[ws, 2]))

    @pl.when(i == 0)
    def _():
        wcount[0] = 0
        for c in weight_copies(te_ref[0], 0):
            c.start()

        def tok(t, c):
            srow_ref[dest_ref[t]] = t
            srow_ref[dest_ref[T + t]] = T + t
            return c

        lax.fori_loop(0, T, tok, 0, unroll=8)

        def seg(e, c):
            def pad(r, c2):
                srow_ref[r] = 2 * T + (r & (MOE_TM - 1))
                return c2

            lax.fori_loop(lo_ref[e], hi_ref[e], pad, 0)
            return c

        lax.fori_loop(0, lo_ref.shape[0], seg, 0)

        ystage[...] = jnp.zeros_like(ystage)

        def spare(r, c):
            pltpu.make_async_copy(ystage.at[0, :, r, :], yt_ref.at[2 * T + r], ssem.at[0]).start()
            return c

        lax.fori_loop(0, MOE_TM, spare, 0)
        scatter_tile(0).wait()
        for r in range(MOE_TM):
            gather_row(0, 0, r).start(priority=ROW_DMA_PRIORITY)

    @pl.when(tf_ref[i] == 1)
    def _():
        ws = lax.rem(wcount[0], 2)
        wcount[0] = wcount[0] + 1
        for c in weight_copies(te_ref[i], ws):
            c.wait()

        @pl.when(nx_ref[i] >= 0)
        def _():
            for c in weight_copies(nx_ref[i], 1 - ws):
                c.start()

        wgb_ref[...] = wsg[ws].astype(BF16)
        wub_ref[...] = wsu[ws].astype(BF16)
        wdb_ref[...] = wsd[ws].astype(BF16)

    slot = lax.rem(i, 2)
    other = 1 - slot

    @pl.when(i < nu)
    def _():
        gather_tile(slot).wait()

        @pl.when(i >= 1)
        def _():
            scatter_tile(slot).wait()

        nxt = jnp.minimum(i + 1, nu - 1)
        prv = jnp.maximum(i - 1, 0)
        for r in range(MOE_TM):
            gather_row(nxt, other, r).start(priority=r % 2)
            scatter_row(prv, other, r).start(priority=(r + 1) % 2)

        x = jnp.concatenate([xbuf[slot, s].astype(BF16) for s in range(Y_ROWS)], axis=1)
        a = _dot(x, wgb_ref[...])
        u = _dot(x, wub_ref[...])
        act = (_silu(a) * u).astype(BF16)
        y = _dot(act, wdb_ref[...])
        for s in range(Y_ROWS):
            ystage[slot, s] = y[:, s * LANES:(s + 1) * LANES]

    @pl.when(i == nu)
    def _():
        gather_tile(slot).wait()
        scatter_tile(slot).wait()

        def last(r, c):
            scatter_row(nu - 1, other, r).start()
            return c

        lax.fori_loop(0, MOE_TM, last, 0)
        scatter_tile(other).wait()


def _moe(tile_expert, tile_first, next_expert, n_used, dest, pad_lo, pad_hi, h2t, w_gate, w_up, w_down):
    T = h2t.shape[0]
    assert T & (T - 1) == 0 and T >= MOE_TM, "token index is recovered from the row table by masking"
    n_tiles = tile_expert.shape[0] - 1
    hbm = pl.BlockSpec(memory_space=pl.ANY)
    grid_spec = pltpu.PrefetchScalarGridSpec(
        num_scalar_prefetch=7,
        grid=(n_tiles + 1,),
        in_specs=[hbm, hbm, hbm, hbm],
        out_specs=hbm,
        scratch_shapes=[pltpu.SMEM((n_tiles * MOE_TM,), I32),
                        pltpu.SMEM((1,), I32),
                        pltpu.VMEM((2, Y_ROWS, MOE_TM, LANES), F32),
                        pltpu.VMEM((2, Y_ROWS, MOE_TM, LANES), F32),
                        pltpu.VMEM((2, D_MODEL, D_EXPERT), F32),
                        pltpu.VMEM((2, D_MODEL, D_EXPERT), F32),
                        pltpu.VMEM((2, D_EXPERT, D_MODEL), F32),
                        pltpu.VMEM((D_MODEL, D_EXPERT), BF16),
                        pltpu.VMEM((D_MODEL, D_EXPERT), BF16),
                        pltpu.VMEM((D_EXPERT, D_MODEL), BF16),
                        pltpu.SemaphoreType.DMA((2,)),
                        pltpu.SemaphoreType.DMA((2,)),
                        pltpu.SemaphoreType.DMA((2, 3))],
    )
    return pl.pallas_call(
        _moe_body,
        grid_spec=grid_spec,
        out_shape=jax.ShapeDtypeStruct((2 * T + MOE_TM, Y_ROWS, LANES), F32),
        compiler_params=_cparams(("arbitrary",)),
        name="moe",
    )(tile_expert, tile_first, next_expert, n_used, dest, pad_lo, pad_hi, h2t, w_gate, w_up, w_down)


COMB_TM = 512


def _combine_body(h1_ref, y0_ref, y1_ref, cw_ref, nw_ref, o_ref):
    tm = h1_ref.shape[0]
    c0 = cw_ref[:, 0:1]
    c1 = cw_ref[:, 1:2]
    cols = []
    for s in range(Y_ROWS):
        moe = c0 * y0_ref[:, s, :] + c1 * y1_ref[:, s, :]
        cols.append(h1_ref[:, s * LANES:(s + 1) * LANES] + moe)
    h = jnp.concatenate(cols, axis=1)
    ms = jnp.mean(h * h, axis=-1, keepdims=True)
    o_ref[...] = h * lax.rsqrt(ms + EPS) * nw_ref[...]


def _combine(h1, y_tok, cw, nw):
    T = h1.shape[0]
    tm = min(COMB_TM, T)
    nt = T // tm
    return pl.pallas_call(
        _combine_body,
        grid=(nt,),
        in_specs=[
            pl.BlockSpec((tm, D_MODEL), lambda i: (i, 0)),
            pl.BlockSpec((tm, Y_ROWS, LANES), lambda i: (i, 0, 0)),
            pl.BlockSpec((tm, Y_ROWS, LANES), lambda i: (i + nt, 0, 0)),
            pl.BlockSpec((tm, LANES), lambda i: (i, 0)),
            pl.BlockSpec((1, D_MODEL), lambda i: (0, 0)),
        ],
        out_specs=pl.BlockSpec((tm, D_MODEL), lambda i: (i, 0)),
        out_shape=jax.ShapeDtypeStruct((T, D_MODEL), F32),
        compiler_params=_cparams(("parallel",)),
        name="combine",
    )(h1, y_tok, y_tok, cw, nw)


def _tile_plan(cnt, n_tiles):
    tiles = (cnt + (MOE_TM - 1)) // MOE_TM
    ends = jnp.cumsum(tiles)
    starts = ends - tiles
    n_used = ends[-1]
    step = jnp.arange(n_tiles + 1, dtype=I32)
    tile = jnp.minimum(step, jnp.maximum(n_used - 1, 0))
    tile_expert = jnp.sum((ends[None, :] <= tile[:, None]).astype(I32), axis=1)
    tile_first = ((step == starts[tile_expert]) & (step < n_used)).astype(I32)
    nxt_tile = ends[tile_expert]
    nxt_expert = jnp.sum((ends[None, :] <= nxt_tile[:, None]).astype(I32), axis=1)
    next_expert = jnp.where(nxt_tile < n_used, nxt_expert, -1).astype(I32)
    pad_lo = starts * MOE_TM + cnt
    pad_hi = ends * MOE_TM
    return (tile_expert, tile_first, next_expert, n_used.reshape(1).astype(I32),
            pad_lo.astype(I32), pad_hi.astype(I32))


def kernel(x, positions, norm1_w, w_in, conv_w, conv_b, dt_bias, a_log, d_skip, ret_norm_w,
           ssm_norm_w, w_out, norm2_w, w_router_group, b_router_group, w_router_expert,
           b_router_expert, w_expert_gate, w_expert_up, w_expert_down, final_norm_w):
    B, T, D = x.shape
    assert B == 1 and D == D_MODEL and T % CHUNK == 0
    xf = x.reshape(T, D)
    pad_l = lambda v: jnp.pad(v, ((0, 0), (0, LANES - v.shape[-1])))

    w_in_t = jnp.swapaxes(w_in[0], 0, 1)
    w_main = _cast_rows_bf16(w_in_t, MAIN_PROJ, CAST_ROWS)
    w_dt = jnp.pad(w_in_t[MAIN_PROJ:], ((0, LANES - (w_in_t.shape[0] - MAIN_PROJ)), (0, 0))).astype(BF16)
    half = RET_HEAD_DIM // 2
    inv = (ROPE_THETA ** (-jnp.arange(half, dtype=F32) / half)).reshape(1, half)
    tri = (jnp.arange(CHUNK)[:, None] >= jnp.arange(CHUNK)[None, :]).astype(BF16)
    d_skip_e = jnp.repeat(d_skip[0], SSM_HEAD_DIM).reshape(1, SSM_WIDTH)
    wr_t = jnp.zeros((LANES, D), F32)
    wr_t = wr_t.at[0:N_GROUPS].set(w_router_group[0].T).at[8:8 + N_EXPERTS].set(w_router_expert[0].T)
    br = jnp.zeros((LANES,), F32)
    br = br.at[0:N_GROUPS].set(b_router_group[0]).at[8:8 + N_EXPERTS].set(b_router_expert[0])
    tri_u = (jnp.arange(SORT_BLK)[:, None] <= jnp.arange(SORT_BLK)[None, :]).astype(BF16)
    ltri = (jnp.arange(N_EXPERTS)[:, None] > jnp.arange(N_EXPERTS)[None, :]).astype(BF16)

    cos, sin = _rope_tables(positions.reshape(T, 1).astype(F32), inv)
    y_ret, y_ssm = _mixer(xf, norm1_w[0].reshape(1, D), w_main, w_dt, cos, sin,
                          ret_norm_w[0].reshape(1, RET_WIDTH), _retention_consts(),
                          conv_w[0], conv_b[0].reshape(1, -1), pad_l(dt_bias[0].reshape(1, -1)),
                          pad_l(a_log[0].reshape(1, -1)), d_skip_e, ssm_norm_w[0].reshape(1, -1), tri)
    h1, h2t, cw, ids = _out_router(xf, y_ret, y_ssm, w_out[0].astype(BF16), norm2_w[0].reshape(1, D),
                                   wr_t.astype(BF16), br.reshape(LANES, 1))

    dest_blk, cnt = _sort_index(ids, tri_u, ltri)
    dest = dest_blk[:, 0:2, :].transpose(1, 0, 2).reshape(2 * T)
    n_tiles = (2 * T) // MOE_TM + N_EXPERTS
    tile_expert, tile_first, next_expert, n_used, pad_lo, pad_hi = _tile_plan(cnt[:, 0].astype(I32), n_tiles)
    y_tok = _moe(tile_expert, tile_first, next_expert, n_used, dest, pad_lo, pad_hi, h2t,
                 w_expert_gate[0], w_expert_up[0], w_expert_down[0])
    out = _combine(h1, y_tok, cw, final_norm_w.reshape(1, D))
    return out.reshape(B, T, D)
```

```python
import functools

import jax
import jax.numpy as jnp
from jax import lax
from jax.experimental import pallas as pl
from jax.experimental.pallas import tpu as pltpu

F32 = jnp.float32
BF16 = jnp.bfloat16
I32 = jnp.int32

D_MODEL = 2048
EPS = 1e-6
CHUNK = 128
RET_HEADS = 4
RET_HEAD_DIM = 256
RET_WIDTH = RET_HEADS * RET_HEAD_DIM
ROPE_THETA = 10000.0
SSM_WIDTH = 1024
SSM_HEAD_DIM = 64
SSM_HEADS = SSM_WIDTH // SSM_HEAD_DIM
SSM_GROUPS = 2
SSM_STATE = 128
SSM_CONV = 4
SSM_CONV_DIM = SSM_WIDTH + 2 * SSM_GROUPS * SSM_STATE
MAIN_PROJ = 4 * RET_WIDTH + SSM_WIDTH + SSM_CONV_DIM
N_GROUPS = 4
GROUP_EXPERTS = 8
N_EXPERTS = N_GROUPS * GROUP_EXPERTS
D_EXPERT = 512
LANES = 128
Y_ROWS = D_MODEL // LANES

VMEM_LIMIT = 56 * 1024 * 1024


def _cparams(sem, vmem=VMEM_LIMIT):
    return pltpu.CompilerParams(dimension_semantics=sem, vmem_limit_bytes=vmem)


def _silu(x):
    return x * (1.0 / (1.0 + jnp.exp(-x)))


def _dot(a, b):
    return jnp.dot(a, b, preferred_element_type=F32)


def _dot_nt(a, b):
    return lax.dot_general(a, b, (((1,), (1,)), ((), ())), preferred_element_type=F32)


def _split3(a):
    a1 = a.astype(BF16)
    r1 = a - a1.astype(F32)
    a2 = r1.astype(BF16)
    a3 = (r1 - a2.astype(F32)).astype(BF16)
    return a1, a2, a3


def _cast_body(w_ref, o_ref):
    o_ref[...] = w_ref[...].astype(BF16)


def _cast_rows_bf16(w, n_rows, tr):
    cols = w.shape[1]
    return pl.pallas_call(
        _cast_body,
        grid=(n_rows // tr,),
        in_specs=[pl.BlockSpec((tr, cols), lambda j: (j, 0))],
        out_specs=pl.BlockSpec((tr, cols), lambda j: (j, 0)),
        out_shape=jax.ShapeDtypeStruct((n_rows, cols), BF16),
        compiler_params=_cparams(("parallel",)),
        name="cast_bf16",
    )(w)


CAST_ROWS = 512


def _rope_body(pos_ref, inv_ref, cos_ref, sin_ref):
    ang = pos_ref[...] * inv_ref[...]
    cos_ref[...] = jnp.cos(ang)
    sin_ref[...] = jnp.sin(ang)


def _rope_tables(pos, inv):
    T = pos.shape[0]
    tm = min(1024, T)
    half = inv.shape[1]
    return pl.pallas_call(
        _rope_body,
        grid=(T // tm,),
        in_specs=[pl.BlockSpec((tm, 1), lambda i: (i, 0)),
                  pl.BlockSpec((1, half), lambda i: (0, 0))],
        out_specs=[pl.BlockSpec((tm, half), lambda i: (i, 0))] * 2,
        out_shape=[jax.ShapeDtypeStruct((T, half), F32)] * 2,
        compiler_params=_cparams(("parallel",)),
        name="rope_tables",
    )(pos, inv)


def _retention_chunks(cd_ref, q_ref, k_ref, v_ref, g_ref, cos_ref, sin_ref,
                      dec_ref, wq_ref, ws_ref, nw_ref, o_ref, st_ref, cb, tick):
    hd = RET_HEAD_DIM
    half = hd // 2
    for i in range(cb):
        sl = slice(i * CHUNK, (i + 1) * CHUNK)
        cos = cos_ref[sl, :]
        sin = sin_ref[sl, :]

        def rope(x):
            x1, x2 = x[:, :half], x[:, half:]
            return jnp.concatenate([x1 * cos - x2 * sin, x2 * cos + x1 * sin], axis=-1)

        for h in range(RET_HEADS):
            hs = slice(h * hd, (h + 1) * hd)
            q = rope(q_ref[sl, hs])
            k = rope(k_ref[sl, hs]) * (hd ** -0.5)
            v = v_ref[sl, hs]
            qb = q.astype(BF16)
            kb = k.astype(BF16)
            scores = _dot_nt(qb, kb) * dec_ref[h]
            y = _dot(scores.astype(BF16), v.astype(BF16))
            st = st_ref[h]
            y = y + _dot(qb, st.astype(BF16)) * wq_ref[h]
            vw = (v * ws_ref[h]).astype(BF16)
            new = _dot(k.T.astype(BF16), vw)
            st_ref[h] = st * cd_ref[h] + new
            mu = jnp.mean(y, axis=-1, keepdims=True)
            d = y - mu
            var = jnp.mean(d * d, axis=-1, keepdims=True)
            yn = d * lax.rsqrt(var + EPS)
            o_ref[sl, hs] = (yn * nw_ref[:, hs] * _silu(g_ref[sl, hs])).astype(BF16)
            tick()


def _retention_consts():
    H, L = RET_HEADS, CHUNK
    log_gamma = jnp.log1p(-(2.0 ** (-5.0 - jnp.arange(H, dtype=F32))))
    idx = jnp.arange(L, dtype=F32)
    diff = idx[:, None] - idx[None, :]
    causal = diff >= 0
    decay_intra = jnp.where(causal[None], jnp.exp(jnp.where(causal, diff, 0.0)[None] * log_gamma[:, None, None]), 0.0)
    w_state = jnp.exp((L - 1.0 - idx)[None, :] * log_gamma[:, None])
    w_query = jnp.exp((idx + 1.0)[None, :] * log_gamma[:, None])
    chunk_decay = jnp.exp(L * log_gamma)
    bc = lambda w: jnp.broadcast_to(w[:, :, None], (H, L, RET_HEAD_DIM))
    return chunk_decay, decay_intra, bc(w_query), bc(w_state)


CONV_PAD = 8
XBC_BLK = 512


def _ssd_chunks(xbc_ref, z_ref, dt_ref, cw_ref, cbias_ref, dtb_ref,
                alog_ref, dskip_ref, nw_ref, tri_ref, o_ref, xpad_ref, st_ref, cb, tick):
    L = CHUNK
    gw = SSM_WIDTH // SSM_GROUPS
    rows_i = lax.broadcasted_iota(I32, (L, L), 0)
    cols_i = lax.broadcasted_iota(I32, (L, L), 1)
    causal = rows_i >= cols_i
    lo_lane = lax.broadcasted_iota(I32, (L, LANES), 1) < SSM_HEAD_DIM
    tri = tri_ref[...]
    a_neg = -jnp.exp(alog_ref[...])

    for i in range(cb):
        sl = slice(i * L, (i + 1) * L)
        xpad_ref[CONV_PAD:CONV_PAD + L, :] = xbc_ref[sl, :]
        u_parts = []
        for b in range(SSM_CONV_DIM // XBC_BLK):
            cs = slice(b * XBC_BLK, (b + 1) * XBC_BLK)
            acc = cbias_ref[:, cs]
            for t in range(SSM_CONV):
                r0 = CONV_PAD - (SSM_CONV - 1) + t
                acc = acc + xpad_ref[r0:r0 + L, cs] * cw_ref[t:t + 1, cs]
            u_parts.append(_silu(acc))
            tick()
        xpad_ref[0:CONV_PAD, :] = xpad_ref[L:L + CONV_PAD, :]
        xs = jnp.concatenate(u_parts[:2], axis=-1)
        bm = u_parts[2][:, :SSM_GROUPS * SSM_STATE]
        cm = u_parts[2][:, SSM_GROUPS * SSM_STATE:]

        dt_in = dt_ref[sl, :] + dtb_ref[...]
        dt = jnp.maximum(dt_in, 0.0) + jnp.log1p(jnp.exp(-jnp.abs(dt_in)))
        a = dt * a_neg
        a1, a2, a3 = _split3(a)
        a_cs = _dot(tri, a1) + _dot(tri, a2) + _dot(tri, a3)
        last = a_cs[L - 1:L, :]

        def per_head_lanes(v):
            lo = lo_lane[0:v.shape[0], :]
            return jnp.concatenate([jnp.where(lo, v[:, 2 * p:2 * p + 1], v[:, 2 * p + 1:2 * p + 2])
                                    for p in range(SSM_HEADS // 2)], axis=1)

        dt_e = per_head_lanes(dt)
        to_end = per_head_lanes(jnp.exp(last - a_cs))
        from_start = per_head_lanes(jnp.exp(a_cs))
        chunk_decay = per_head_lanes(jnp.exp(last))
        tick()
        x_dt = xs * dt_e
        xw = (x_dt * to_end).astype(BF16)
        acs_t = a_cs.T
        cmb = cm.astype(BF16)
        bmb = bm.astype(BF16)

        ys = []
        for g in range(SSM_GROUPS):
            ns = slice(g * SSM_STATE, (g + 1) * SSM_STATE)
            gs = slice(g * gw, (g + 1) * gw)
            cg = cmb[:, ns]
            cbg = _dot_nt(cg, bmb[:, ns])
            st = st_ref[g]
            y_off = _dot(cg, st.astype(BF16))
            new = _dot(bm[:, ns].T.astype(BF16), xw[:, gs])
            st_ref[g] = st * chunk_decay[:, gs] + new
            tick()
            for jp in range(gw // LANES):
                h0 = (g * gw + jp * LANES) // SSM_HEAD_DIM
                ms = []
                for hh in (h0, h0 + 1):
                    seg = a_cs[:, hh:hh + 1] - acs_t[hh:hh + 1, :]
                    dec = jnp.exp(jnp.where(causal, seg, -jnp.inf))
                    ms.append((cbg * dec).astype(BF16))
                lhs = jnp.concatenate(ms, axis=1)
                ls = slice(g * gw + jp * LANES, g * gw + (jp + 1) * LANES)
                xp = x_dt[:, ls]
                rhs = jnp.concatenate([jnp.where(lo_lane, xp, 0.0),
                                       jnp.where(lo_lane, 0.0, xp)], axis=0).astype(BF16)
                y_diag = _dot(lhs, rhs)
                ys.append(y_diag + y_off[:, jp * LANES:(jp + 1) * LANES] * from_start[:, ls])
                tick()
        y = jnp.concatenate(ys, axis=1) + xs * dskip_ref[...]
        y = y * _silu(z_ref[sl, :])
        outs = []
        for g in range(SSM_GROUPS):
            yg = y[:, g * gw:(g + 1) * gw]
            ms_ = jnp.mean(yg * yg, axis=-1, keepdims=True)
            outs.append(yg * lax.rsqrt(ms_ + EPS))
        o_ref[sl, :] = (jnp.concatenate(outs, axis=1) * nw_ref[...]).astype(BF16)


MIX_CB = 2
MIX_VMEM = 60 * 1024 * 1024
COL_Z = 4 * RET_WIDTH
COL_XBC = COL_Z + SSM_WIDTH
PROJ_WIDTHS = (RET_WIDTH, RET_WIDTH, RET_WIDTH, RET_WIDTH, SSM_WIDTH, SSM_CONV_DIM, LANES)
PROJ_SLAB = 256


def _mixer_body(cd_ref, x_ref, n1w_ref, wt_hbm, wdt_ref, cos_ref, sin_ref, dec_ref, wq_ref, ws_ref, rnw_ref,
                cw_ref, cbias_ref, dtb_ref, alog_ref, dskip_ref, snw_ref, tri_ref,
                yret_ref, yssm_ref, w_ref, hn_ref, pq, pk, pv, pg, pz, pxbc, pdt,
                st_ret, xpad_ref, st_ssd, sem, *, cb):
    @pl.when(pl.program_id(0) == 0)
    def _():
        cp = pltpu.make_async_copy(wt_hbm, w_ref, sem)
        cp.start()
        st_ret[...] = jnp.zeros_like(st_ret)
        xpad_ref[0:CONV_PAD, :] = jnp.zeros((CONV_PAD, SSM_CONV_DIM), F32)
        st_ssd[...] = jnp.zeros_like(st_ssd)
        cp.wait()

    def slabs(ref, col0):
        def slab(lo):
            def go():
                ref[:, lo:lo + PROJ_SLAB] = _dot_nt(hn_ref[...], w_ref[col0 + lo:col0 + lo + PROJ_SLAB, :])
            return go
        return [slab(lo) for lo in range(0, ref.shape[1], PROJ_SLAB)]

    x = x_ref[...]
    ms = jnp.mean(x * x, axis=-1, keepdims=True)
    hn_ref[...] = (x * lax.rsqrt(ms + EPS) * n1w_ref[...]).astype(BF16)
    for piece in slabs(pxbc, COL_XBC) + slabs(pz, COL_Z):
        piece()
    pdt[...] = _dot_nt(hn_ref[...], wdt_ref[...])

    pieces = slabs(pq, 0) + slabs(pk, RET_WIDTH) + slabs(pv, 2 * RET_WIDTH) + slabs(pg, 3 * RET_WIDTH)
    n_pieces = len(pieces)
    n_ticks = cb * (SSM_CONV_DIM // XBC_BLK + 1 + SSM_GROUPS + SSM_WIDTH // LANES)
    calls = [0]

    def tick():
        calls[0] += 1
        while pieces and (n_pieces - len(pieces)) * n_ticks < calls[0] * n_pieces:
            pieces.pop(0)()

    _ssd_chunks(pxbc, pz, pdt, cw_ref, cbias_ref, dtb_ref, alog_ref, dskip_ref, snw_ref, tri_ref,
                yssm_ref, xpad_ref, st_ssd, cb, tick)
    while pieces:
        pieces.pop(0)()
    _retention_chunks(cd_ref, pq, pk, pv, pg, cos_ref, sin_ref, dec_ref, wq_ref, ws_ref, rnw_ref,
                      yret_ref, st_ret, cb, lambda: None)


def _mixer(x, n1w, w_main_t, w_dt_t, cos, sin, ret_nw, ret_consts,
           conv_w, conv_b, dt_bias, a_log, d_skip_e, ssm_nw, tri):
    T = x.shape[0]
    cb = min(MIX_CB, T // CHUNK)
    rows = cb * CHUNK
    chunk_decay, decay_intra, w_query, w_state = ret_consts
    hd = RET_HEAD_DIM
    half = hd // 2
    full = lambda shape: pl.BlockSpec(shape, lambda c, cd: (0,) * len(shape))
    rowblk = lambda width: pl.BlockSpec((rows, width), lambda c, cd: (c, 0))
    scratch = lambda width: pltpu.VMEM((rows, width), F32)
    grid_spec = pltpu.PrefetchScalarGridSpec(
        num_scalar_prefetch=1,
        grid=(T // rows,),
        in_specs=[
            rowblk(D_MODEL),
            full((1, D_MODEL)),
            pl.BlockSpec(memory_space=pl.ANY),
            full((LANES, D_MODEL)),
            rowblk(half), rowblk(half),
            full((RET_HEADS, CHUNK, CHUNK)),
            full((RET_HEADS, CHUNK, hd)),
            full((RET_HEADS, CHUNK, hd)),
            full((1, RET_WIDTH)),
            full((SSM_CONV, SSM_CONV_DIM)),
            full((1, SSM_CONV_DIM)),
            full((1, LANES)),
            full((1, LANES)),
            full((1, SSM_WIDTH)),
            full((1, SSM_WIDTH)),
            full((CHUNK, CHUNK)),
        ],
        out_specs=[rowblk(RET_WIDTH), rowblk(SSM_WIDTH)],
        scratch_shapes=[
            pltpu.VMEM((MAIN_PROJ, D_MODEL), BF16),
            pltpu.VMEM((rows, D_MODEL), BF16),
            *[scratch(w) for w in PROJ_WIDTHS],
            pltpu.VMEM((RET_HEADS, hd, hd), F32),
            pltpu.VMEM((CHUNK + CONV_PAD, SSM_CONV_DIM), F32),
            pltpu.VMEM((SSM_GROUPS, SSM_STATE, SSM_WIDTH // SSM_GROUPS), F32),
            pltpu.SemaphoreType.DMA(()),
        ],
    )
    return pl.pallas_call(
        functools.partial(_mixer_body, cb=cb),
        grid_spec=grid_spec,
        out_shape=[jax.ShapeDtypeStruct((T, RET_WIDTH), BF16),
                   jax.ShapeDtypeStruct((T, SSM_WIDTH), BF16)],
        compiler_params=_cparams(("arbitrary",), MIX_VMEM),
        name="mixer",
    )(chunk_decay, x, n1w, w_main_t, w_dt_t, cos, sin, decay_intra, w_query, w_state, ret_nw,
      conv_w, conv_b, dt_bias, a_log, d_skip_e, ssm_nw, tri)


OUT_TM = 512
ROUTE_ROWS = 8 + N_EXPERTS


def _out_router_body(x_ref, yr_ref, ys_ref, wo_ref, nw_ref, wr_ref, br_ref,
                     h1_ref, h2t_ref, cw_ref, ids_ref):
    tm = x_ref.shape[0]
    h1 = x_ref[...] + _dot(yr_ref[...], wo_ref[0:RET_WIDTH, :]) + _dot(ys_ref[...], wo_ref[RET_WIDTH:, :])
    h1_ref[...] = h1
    ms = jnp.mean(h1 * h1, axis=-1, keepdims=True)
    h2 = h1 * lax.rsqrt(ms + EPS) * nw_ref[...]
    for s in range(Y_ROWS):
        h2t_ref[:, s, :] = h2[:, s * LANES:(s + 1) * LANES]

    logits = _dot_nt(wr_ref[...], h2.astype(BF16)) + br_ref[...]
    row = lax.broadcasted_iota(I32, (8, tm), 0)
    lg = jnp.where(row < N_GROUPS, logits[0:8], -jnp.inf)
    m = jnp.max(lg, axis=0, keepdims=True)
    p_sel = 1.0 / jnp.sum(jnp.exp(lg - m), axis=0, keepdims=True)
    g_sel = jnp.min(jnp.where(lg == m, row, 8), axis=0, keepdims=True)
    le = jnp.zeros((GROUP_EXPERTS, tm), F32)
    for g in range(N_GROUPS):
        le = jnp.where(g_sel == g, logits[8 + g * GROUP_EXPERTS:8 + (g + 1) * GROUP_EXPERTS], le)
    m2 = jnp.max(le, axis=0, keepdims=True)
    ee = jnp.exp(le - m2)
    pe = ee / jnp.sum(ee, axis=0, keepdims=True)
    v1 = jnp.max(pe, axis=0, keepdims=True)
    i1 = jnp.min(jnp.where(pe == v1, row, 8), axis=0, keepdims=True)
    pe2 = jnp.where(row == i1, -1.0, pe)
    v2 = jnp.max(pe2, axis=0, keepdims=True)
    i2 = jnp.min(jnp.where(pe2 == v2, row, 8), axis=0, keepdims=True)
    tw = v1 + v2
    c1 = v1 / tw * p_sel
    c2 = v2 / tw * p_sel
    e1 = g_sel * GROUP_EXPERTS + i1
    e2 = g_sel * GROUP_EXPERTS + i2
    ids = jnp.where(row == 0, e1, jnp.where(row == 1, e2, 0))
    for b in range(tm // SORT_BLK):
        ids_ref[b] = ids[:, b * SORT_BLK:(b + 1) * SORT_BLK]
    cw8 = jnp.where(row == 0, c1, jnp.where(row == 1, c2, 0.0))
    cw = jnp.concatenate([cw8, jnp.zeros((LANES - 8, tm), F32)], axis=0)
    cw_ref[...] = cw.T


def _out_router(x, y_ret, y_ssm, w_out, nw, wr_t, br):
    T = x.shape[0]
    tm = min(OUT_TM, T)
    full = lambda shape: pl.BlockSpec(shape, lambda i: (0,) * len(shape))
    return pl.pallas_call(
        _out_router_body,
        grid=(T // tm,),
        in_specs=[
            pl.BlockSpec((tm, D_MODEL), lambda i: (i, 0)),
            pl.BlockSpec((tm, RET_WIDTH), lambda i: (i, 0)),
            pl.BlockSpec((tm, SSM_WIDTH), lambda i: (i, 0)),
            pl.BlockSpec((D_MODEL, D_MODEL), lambda i: (0, 0), pipeline_mode=pl.Buffered(1)),
            full((1, D_MODEL)),
            full((LANES, D_MODEL)),
            full((LANES, 1)),
        ],
        out_specs=[
            pl.BlockSpec((tm, D_MODEL), lambda i: (i, 0)),
            pl.BlockSpec((tm, Y_ROWS, LANES), lambda i: (i, 0, 0)),
            pl.BlockSpec((tm, LANES), lambda i: (i, 0)),
            pl.BlockSpec((tm // SORT_BLK, 8, SORT_BLK), lambda i: (i, 0, 0)),
        ],
        out_shape=[
            jax.ShapeDtypeStruct((T, D_MODEL), F32),
            jax.ShapeDtypeStruct((T, Y_ROWS, LANES), F32),
            jax.ShapeDtypeStruct((T, LANES), F32),
            jax.ShapeDtypeStruct((T // SORT_BLK, 8, SORT_BLK), I32),
        ],
        compiler_params=_cparams(("parallel",)),
        name="out_router",
    )(x, y_ret, y_ssm, w_out, nw, wr_t, br)


SORT_BLK = 256
MOE_TM = 256


def _sort_index_body(ids_ref, tri_ref, ltri_ref, dest_ref, cnt_ref, rank_ref):
    nblk = ids_ref.shape[0]
    row_e = lax.broadcasted_iota(I32, (N_EXPERTS, SORT_BLK), 0)
    row8 = lax.broadcasted_iota(I32, (8, SORT_BLK), 0)

    def onehots(b):
        ids = ids_ref[b]
        return row_e == ids[0:1], row_e == ids[1:2]

    def rank_blk(b, carry):
        oh1, oh2 = onehots(b)
        ohf = jnp.where(oh1 | oh2, 1.0, 0.0)
        incl = _dot(ohf.astype(BF16), tri_ref[...])
        base = carry + incl - 1.0
        r1 = jnp.sum(jnp.where(oh1, base, 0.0), axis=0, keepdims=True)
        r2 = jnp.sum(jnp.where(oh2, base, 0.0), axis=0, keepdims=True)
        rank_ref[b] = jnp.where(row8 == 0, r1, jnp.where(row8 == 1, r2, 0.0))
        return carry + jnp.sum(ohf, axis=1, keepdims=True)

    cnt = lax.fori_loop(0, nblk, rank_blk, jnp.zeros((N_EXPERTS, 1), F32))
    cnt_ref[...] = jnp.broadcast_to(cnt, cnt_ref.shape)
    tiles = jnp.floor((cnt + (MOE_TM - 1.0)) / MOE_TM)
    tiles_b = jnp.broadcast_to(tiles, (N_EXPERTS, LANES)).astype(BF16)
    off = _dot(ltri_ref[...], tiles_b)[:, 0:1] * MOE_TM

    def dest_blk(b, carry):
        oh1, oh2 = onehots(b)
        o1 = jnp.sum(jnp.where(oh1, off, 0.0), axis=0, keepdims=True)
        o2 = jnp.sum(jnp.where(oh2, off, 0.0), axis=0, keepdims=True)
        d = rank_ref[b] + jnp.where(row8 == 0, o1, jnp.where(row8 == 1, o2, 0.0))
        dest_ref[b] = d.astype(I32)
        return carry

    lax.fori_loop(0, nblk, dest_blk, 0)


def _sort_index(ids, tri_u, ltri):
    nblk = ids.shape[0]
    return pl.pallas_call(
        _sort_index_body,
        out_shape=[jax.ShapeDtypeStruct((nblk, 8, SORT_BLK), I32),
                   jax.ShapeDtypeStruct((N_EXPERTS, LANES), F32)],
        scratch_shapes=[pltpu.VMEM((nblk, 8, SORT_BLK), F32)],
        compiler_params=_cparams(None),
        name="sort_index",
    )(ids, tri_u, ltri)


ROW_GROUP = 64


def _moe_body(te_ref, tf_ref, nx_ref, nu_ref, tv_ref, dest_ref, lo_ref, hi_ref,
              h2t_ref, wg_ref, wu_ref, wd_ref, yt_ref,
              srow_ref, wcount, xbuf, ystage, wsg, wsu, wsd, wgb_ref, wub_ref, wdb_ref, gsem, ssem, wsem):
    i = pl.program_id(0)
    nu = nu_ref[0]
    T = h2t_ref.shape[0]
    n_groups = MOE_TM // ROW_GROUP

    def gather_row(tile, slot, r):
        tok = srow_ref[tile * MOE_TM + r] & (T - 1)
        return pltpu.make_async_copy(h2t_ref.at[tok], xbuf.at[slot, :, r, :], gsem.at[slot])

    def scatter_row(tile, slot, r):
        return pltpu.make_async_copy(ystage.at[slot, :, r, :], yt_ref.at[srow_ref[tile * MOE_TM + r]],
                                     ssem.at[slot])

    def for_groups(tile, fn):
        used = lax.shift_right_logical(tv_ref[tile] + (ROW_GROUP - 1), ROW_GROUP.bit_length() - 1)
        for q in range(n_groups):
            @pl.when(q < used)
            def _():
                fn(q)

    def start_gathers(tile, slot):
        def fn(q):
            for r in range(q * ROW_GROUP, (q + 1) * ROW_GROUP):
                gather_row(tile, slot, r).start(priority=r % 2)
        for_groups(tile, fn)

    def start_scatters(tile, slot):
        def fn(q):
            for r in range(q * ROW_GROUP, (q + 1) * ROW_GROUP):
                scatter_row(tile, slot, r).start(priority=(r + 1) % 2)
        for_groups(tile, fn)

    def wait_gathers(tile, slot):
        grp = h2t_ref.at[pl.ds(0, ROW_GROUP)]
        for_groups(tile, lambda q: pltpu.make_async_copy(grp, grp, gsem.at[slot]).wait())

    def wait_scatters(tile, slot):
        grp = yt_ref.at[pl.ds(0, ROW_GROUP)]
        for_groups(tile, lambda q: pltpu.make_async_copy(grp, grp, ssem.at[slot]).wait())

    def scatter_tile(slot):
        return pltpu.make_async_copy(yt_ref.at[pl.ds(0, MOE_TM)], yt_ref.at[pl.ds(0, MOE_TM)], ssem.at[slot])

    def weight_copies(e, ws):
        return (pltpu.make_async_copy(wg_ref.at[e], wsg.at[ws], wsem.at[ws, 0]),
                pltpu.make_async_copy(wu_ref.at[e], wsu.at[ws], wsem.at[ws, 1]),
                pltpu.make_async_copy(wd_ref.at[e], wsd.at[ws], wsem.at[ws, 2]))

    @pl.when(i == 0)
    def _():
        wcount[0] = 0
        for c in weight_copies(te_ref[0], 0):
            c.start()

        def tok(t, c):
            srow_ref[dest_ref[t]] = t
            srow_ref[dest_ref[T + t]] = T + t
            return c

        lax.fori_loop(0, T, tok, 0, unroll=8)

        def seg(e, c):
            def pad(r, c2):
                srow_ref[r] = 2 * T + (r & (MOE_TM - 1))
                return c2

            lax.fori_loop(lo_ref[e], hi_ref[e], pad, 0)
            return c

        lax.fori_loop(0, lo_ref.shape[0], seg, 0)

        xbuf[...] = jnp.zeros_like(xbuf)
        ystage[...] = jnp.zeros_like(ystage)

        def spare(r, c):
            pltpu.make_async_copy(ystage.at[0, :, r, :], yt_ref.at[2 * T + r], ssem.at[0]).start()
            return c

        lax.fori_loop(0, MOE_TM, spare, 0)
        scatter_tile(0).wait()
        start_gathers(0, 0)

    @pl.when(tf_ref[i] == 1)
    def _():
        ws = lax.rem(wcount[0], 2)
        wcount[0] = wcount[0] + 1
        for c in weight_copies(te_ref[i], ws):
            c.wait()

        @pl.when(nx_ref[i] >= 0)
        def _():
            for c in weight_copies(nx_ref[i], 1 - ws):
                c.start()

        wgb_ref[...] = wsg[ws].astype(BF16)
        wub_ref[...] = wsu[ws].astype(BF16)
        wdb_ref[...] = wsd[ws].astype(BF16)

    slot = lax.rem(i, 2)
    other = 1 - slot

    @pl.when(i < nu)
    def _():
        wait_gathers(i, slot)

        @pl.when(i >= 1)
        def _():
            wait_scatters(jnp.maximum(i - 2, 0), slot)

        start_gathers(jnp.minimum(i + 1, nu - 1), other)
        start_scatters(jnp.maximum(i - 1, 0), other)

        x = jnp.concatenate([xbuf[slot, s].astype(BF16) for s in range(Y_ROWS)], axis=1)
        a = _dot(x, wgb_ref[...])
        u = _dot(x, wub_ref[...])
        act = (_silu(a) * u).astype(BF16)
        y = _dot(act, wdb_ref[...])
        for s in range(Y_ROWS):
            ystage[slot, s] = y[:, s * LANES:(s + 1) * LANES]

    @pl.when(i == nu)
    def _():
        wait_gathers(nu - 1, slot)
        wait_scatters(jnp.maximum(nu - 2, 0), slot)
        start_scatters(nu - 1, other)
        wait_scatters(nu - 1, other)


def _moe(tile_expert, tile_first, next_expert, n_used, tile_valid, dest, pad_lo, pad_hi,
         h2t, w_gate, w_up, w_down):
    T = h2t.shape[0]
    assert T & (T - 1) == 0 and T >= MOE_TM, "token index is recovered from the row table by masking"
    n_tiles = tile_expert.shape[0] - 1
    hbm = pl.BlockSpec(memory_space=pl.ANY)
    grid_spec = pltpu.PrefetchScalarGridSpec(
        num_scalar_prefetch=8,
        grid=(n_tiles + 1,),
        in_specs=[hbm, hbm, hbm, hbm],
        out_specs=hbm,
        scratch_shapes=[pltpu.SMEM((n_tiles * MOE_TM,), I32),
                        pltpu.SMEM((1,), I32),
                        pltpu.VMEM((2, Y_ROWS, MOE_TM, LANES), F32),
                        pltpu.VMEM((2, Y_ROWS, MOE_TM, LANES), F32),
                        pltpu.VMEM((2, D_MODEL, D_EXPERT), F32),
                        pltpu.VMEM((2, D_MODEL, D_EXPERT), F32),
                        pltpu.VMEM((2, D_EXPERT, D_MODEL), F32),
                        pltpu.VMEM((D_MODEL, D_EXPERT), BF16),
                        pltpu.VMEM((D_MODEL, D_EXPERT), BF16),
                        pltpu.VMEM((D_EXPERT, D_MODEL), BF16),
                        pltpu.SemaphoreType.DMA((2,)),
                        pltpu.SemaphoreType.DMA((2,)),
                        pltpu.SemaphoreType.DMA((2, 3))],
    )
    return pl.pallas_call(
        _moe_body,
        grid_spec=grid_spec,
        out_shape=jax.ShapeDtypeStruct((2 * T + MOE_TM, Y_ROWS, LANES), F32),
        compiler_params=_cparams(("arbitrary",)),
        name="moe",
    )(tile_expert, tile_first, next_expert, n_used, tile_valid, dest, pad_lo, pad_hi, h2t, w_gate, w_up, w_down)


COMB_TM = 512


def _combine_body(h1_ref, y0_ref, y1_ref, cw_ref, nw_ref, o_ref):
    tm = h1_ref.shape[0]
    c0 = cw_ref[:, 0:1]
    c1 = cw_ref[:, 1:2]
    cols = []
    for s in range(Y_ROWS):
        moe = c0 * y0_ref[:, s, :] + c1 * y1_ref[:, s, :]
        cols.append(h1_ref[:, s * LANES:(s + 1) * LANES] + moe)
    h = jnp.concatenate(cols, axis=1)
    ms = jnp.mean(h * h, axis=-1, keepdims=True)
    o_ref[...] = h * lax.rsqrt(ms + EPS) * nw_ref[...]


def _combine(h1, y_tok, cw, nw):
    T = h1.shape[0]
    tm = min(COMB_TM, T)
    nt = T // tm
    return pl.pallas_call(
        _combine_body,
        grid=(nt,),
        in_specs=[
            pl.BlockSpec((tm, D_MODEL), lambda i: (i, 0)),
            pl.BlockSpec((tm, Y_ROWS, LANES), lambda i: (i, 0, 0)),
            pl.BlockSpec((tm, Y_ROWS, LANES), lambda i: (i + nt, 0, 0)),
            pl.BlockSpec((tm, LANES), lambda i: (i, 0)),
            pl.BlockSpec((1, D_MODEL), lambda i: (0, 0)),
        ],
        out_specs=pl.BlockSpec((tm, D_MODEL), lambda i: (i, 0)),
        out_shape=jax.ShapeDtypeStruct((T, D_MODEL), F32),
        compiler_params=_cparams(("parallel",)),
        name="combine",
    )(h1, y_tok, y_tok, cw, nw)


def _tile_plan(cnt, n_tiles):
    tiles = (cnt + (MOE_TM - 1)) // MOE_TM
    ends = jnp.cumsum(tiles)
    starts = ends - tiles
    n_used = ends[-1]
    step = jnp.arange(n_tiles + 1, dtype=I32)
    tile = jnp.minimum(step, jnp.maximum(n_used - 1, 0))
    tile_expert = jnp.sum((ends[None, :] <= tile[:, None]).astype(I32), axis=1)
    tile_first = ((step == starts[tile_expert]) & (step < n_used)).astype(I32)
    nxt_tile = ends[tile_expert]
    nxt_expert = jnp.sum((ends[None, :] <= nxt_tile[:, None]).astype(I32), axis=1)
    next_expert = jnp.where(nxt_tile < n_used, nxt_expert, -1).astype(I32)
    pad_lo = starts * MOE_TM + cnt
    pad_hi = ends * MOE_TM
    tid = jnp.arange(n_tiles, dtype=I32)
    tid_expert = jnp.minimum(jnp.sum((ends[None, :] <= tid[:, None]).astype(I32), axis=1), N_EXPERTS - 1)
    tile_valid = jnp.clip(pad_lo[tid_expert] - tid * MOE_TM, 0, MOE_TM)
    return (tile_expert, tile_first, next_expert, n_used.reshape(1).astype(I32), tile_valid.astype(I32),
            pad_lo.astype(I32), pad_hi.astype(I32))


def kernel(x, positions, norm1_w, w_in, conv_w, conv_b, dt_bias, a_log, d_skip, ret_norm_w,
           ssm_norm_w, w_out, norm2_w, w_router_group, b_router_group, w_router_expert,
           b_router_expert, w_expert_gate, w_expert_up, w_expert_down, final_norm_w):
    B, T, D = x.shape
    assert B == 1 and D == D_MODEL and T % CHUNK == 0
    xf = x.reshape(T, D)
    pad_l = lambda v: jnp.pad(v, ((0, 0), (0, LANES - v.shape[-1])))

    w_in_t = jnp.swapaxes(w_in[0], 0, 1)
    w_main = _cast_rows_bf16(w_in_t, MAIN_PROJ, CAST_ROWS)
    w_dt = jnp.pad(w_in_t[MAIN_PROJ:], ((0, LANES - (w_in_t.shape[0] - MAIN_PROJ)), (0, 0))).astype(BF16)
    half = RET_HEAD_DIM // 2
    inv = (ROPE_THETA ** (-jnp.arange(half, dtype=F32) / half)).reshape(1, half)
    tri = (jnp.arange(CHUNK)[:, None] >= jnp.arange(CHUNK)[None, :]).astype(BF16)
    d_skip_e = jnp.repeat(d_skip[0], SSM_HEAD_DIM).reshape(1, SSM_WIDTH)
    wr_t = jnp.zeros((LANES, D), F32)
    wr_t = wr_t.at[0:N_GROUPS].set(w_router_group[0].T).at[8:8 + N_EXPERTS].set(w_router_expert[0].T)
    br = jnp.zeros((LANES,), F32)
    br = br.at[0:N_GROUPS].set(b_router_group[0]).at[8:8 + N_EXPERTS].set(b_router_expert[0])
    tri_u = (jnp.arange(SORT_BLK)[:, None] <= jnp.arange(SORT_BLK)[None, :]).astype(BF16)
    ltri = (jnp.arange(N_EXPERTS)[:, None] > jnp.arange(N_EXPERTS)[None, :]).astype(BF16)

    cos, sin = _rope_tables(positions.reshape(T, 1).astype(F32), inv)
    y_ret, y_ssm = _mixer(xf, norm1_w[0].reshape(1, D), w_main, w_dt, cos, sin,
                          ret_norm_w[0].reshape(1, RET_WIDTH), _retention_consts(),
                          conv_w[0], conv_b[0].reshape(1, -1), pad_l(dt_bias[0].reshape(1, -1)),
                          pad_l(a_log[0].reshape(1, -1)), d_skip_e, ssm_norm_w[0].reshape(1, -1), tri)
    h1, h2t, cw, ids = _out_router(xf, y_ret, y_ssm, w_out[0].astype(BF16), norm2_w[0].reshape(1, D),
                                   wr_t.astype(BF16), br.reshape(LANES, 1))

    dest_blk, cnt = _sort_index(ids, tri_u, ltri)
    dest = dest_blk[:, 0:2, :].transpose(1, 0, 2).reshape(2 * T)
    n_tiles = (2 * T) // MOE_TM + N_EXPERTS
    (tile_expert, tile_first, next_expert, n_used, tile_valid,
     pad_lo, pad_hi) = _tile_plan(cnt[:, 0].astype(I32), n_tiles)
    y_tok = _moe(tile_expert, tile_first, next_expert, n_used, tile_valid, dest, pad_lo, pad_hi, h2t,
                 w_expert_gate[0], w_expert_up[0], w_expert_down[0])
    out = _combine(h1, y_tok, cw, final_norm_w.reshape(1, D))
    return out.reshape(B, T, D)
```

```python
import functools

import jax
import jax.numpy as jnp
from jax import lax
from jax.experimental import pallas as pl
from jax.experimental.pallas import tpu as pltpu

F32 = jnp.float32
BF16 = jnp.bfloat16
I32 = jnp.int32

D_MODEL = 2048
EPS = 1e-6
CHUNK = 128
RET_HEADS = 4
RET_HEAD_DIM = 256
RET_WIDTH = RET_HEADS * RET_HEAD_DIM
ROPE_THETA = 10000.0
SSM_WIDTH = 1024
SSM_HEAD_DIM = 64
SSM_HEADS = SSM_WIDTH // SSM_HEAD_DIM
SSM_GROUPS = 2
SSM_STATE = 128
SSM_CONV = 4
SSM_CONV_DIM = SSM_WIDTH + 2 * SSM_GROUPS * SSM_STATE
MAIN_PROJ = 4 * RET_WIDTH + SSM_WIDTH + SSM_CONV_DIM
N_GROUPS = 4
GROUP_EXPERTS = 8
N_EXPERTS = N_GROUPS * GROUP_EXPERTS
D_EXPERT = 512
LANES = 128
Y_ROWS = D_MODEL // LANES

VMEM_LIMIT = 56 * 1024 * 1024


def _cparams(sem, vmem=VMEM_LIMIT):
    return pltpu.CompilerParams(dimension_semantics=sem, vmem_limit_bytes=vmem)


def _silu(x):
    return x * (1.0 / (1.0 + jnp.exp(-x)))


def _dot(a, b):
    return jnp.dot(a, b, preferred_element_type=F32)


def _dot_nt(a, b):
    return lax.dot_general(a, b, (((1,), (1,)), ((), ())), preferred_element_type=F32)


def _split3(a):
    a1 = a.astype(BF16)
    r1 = a - a1.astype(F32)
    a2 = r1.astype(BF16)
    a3 = (r1 - a2.astype(F32)).astype(BF16)
    return a1, a2, a3


CAST_ROWS = 512
ROPE_ROWS = 1024


def _prep_body(win_ref, wout_ref, pos_ref, inv_ref, winb_ref, woutb_ref, cos_ref, sin_ref, *, n_in, n_rope):
    j = pl.program_id(0)

    @pl.when(j < n_in)
    def _():
        winb_ref[...] = win_ref[...].astype(BF16)

    @pl.when(j >= n_in)
    def _():
        woutb_ref[...] = wout_ref[...].astype(BF16)

    @pl.when(j < n_rope)
    def _():
        ang = pos_ref[...] * inv_ref[...]
        cos_ref[...] = jnp.cos(ang)
        sin_ref[...] = jnp.sin(ang)


def _prep(w_in_t, w_out, pos, inv):
    T = pos.shape[0]
    cols = w_in_t.shape[1]
    half = inv.shape[1]
    n_in = MAIN_PROJ // CAST_ROWS
    n_out = w_out.shape[0] // CAST_ROWS
    rr = min(ROPE_ROWS, T)
    n_rope = T // rr
    assert n_rope <= n_in + n_out
    in_blk = lambda j: (jnp.minimum(j, n_in - 1), 0)
    out_blk = lambda j: (jnp.clip(j - n_in, 0, n_out - 1), 0)
    rope_blk = lambda j: (jnp.minimum(j, n_rope - 1), 0)
    return pl.pallas_call(
        functools.partial(_prep_body, n_in=n_in, n_rope=n_rope),
        grid=(n_in + n_out,),
        in_specs=[pl.BlockSpec((CAST_ROWS, cols), in_blk),
                  pl.BlockSpec((CAST_ROWS, w_out.shape[1]), out_blk),
                  pl.BlockSpec((rr, 1), rope_blk),
                  pl.BlockSpec((1, half), lambda j: (0, 0))],
        out_specs=[pl.BlockSpec((CAST_ROWS, cols), in_blk),
                   pl.BlockSpec((CAST_ROWS, w_out.shape[1]), out_blk),
                   pl.BlockSpec((rr, half), rope_blk),
                   pl.BlockSpec((rr, half), rope_blk)],
        out_shape=[jax.ShapeDtypeStruct((MAIN_PROJ, cols), BF16),
                   jax.ShapeDtypeStruct(w_out.shape, BF16),
                   jax.ShapeDtypeStruct((T, half), F32),
                   jax.ShapeDtypeStruct((T, half), F32)],
        compiler_params=_cparams(("arbitrary",)),
        name="prep",
    )(w_in_t, w_out, pos, inv)


def _retention_chunks(cd_ref, q_ref, k_ref, v_ref, g_ref, cos_ref, sin_ref,
                      dec_ref, wq_ref, ws_ref, nw_ref, o_ref, st_ref, cb, tick):
    hd = RET_HEAD_DIM
    half = hd // 2
    for i in range(cb):
        sl = slice(i * CHUNK, (i + 1) * CHUNK)
        cos = cos_ref[sl, :]
        sin = sin_ref[sl, :]

        def rope(x):
            x1, x2 = x[:, :half], x[:, half:]
            return jnp.concatenate([x1 * cos - x2 * sin, x2 * cos + x1 * sin], axis=-1)

        for h in range(RET_HEADS):
            hs = slice(h * hd, (h + 1) * hd)
            q = rope(q_ref[sl, hs])
            k = rope(k_ref[sl, hs]) * (hd ** -0.5)
            v = v_ref[sl, hs]
            qb = q.astype(BF16)
            kb = k.astype(BF16)
            scores = _dot_nt(qb, kb) * dec_ref[h]
            y = _dot(scores.astype(BF16), v.astype(BF16))
            st = st_ref[h]
            y = y + _dot(qb, st.astype(BF16)) * wq_ref[h]
            vw = (v * ws_ref[h]).astype(BF16)
            new = _dot(k.T.astype(BF16), vw)
            st_ref[h] = st * cd_ref[h] + new
            mu = jnp.mean(y, axis=-1, keepdims=True)
            d = y - mu
            var = jnp.mean(d * d, axis=-1, keepdims=True)
            yn = d * lax.rsqrt(var + EPS)
            o_ref[sl, hs] = (yn * nw_ref[:, hs] * _silu(g_ref[sl, hs])).astype(BF16)
            tick()


def _retention_consts():
    H, L = RET_HEADS, CHUNK
    log_gamma = jnp.log1p(-(2.0 ** (-5.0 - jnp.arange(H, dtype=F32))))
    idx = jnp.arange(L, dtype=F32)
    diff = idx[:, None] - idx[None, :]
    causal = diff >= 0
    decay_intra = jnp.where(causal[None], jnp.exp(jnp.where(causal, diff, 0.0)[None] * log_gamma[:, None, None]), 0.0)
    w_state = jnp.exp((L - 1.0 - idx)[None, :] * log_gamma[:, None])
    w_query = jnp.exp((idx + 1.0)[None, :] * log_gamma[:, None])
    chunk_decay = jnp.exp(L * log_gamma)
    bc = lambda w: jnp.broadcast_to(w[:, :, None], (H, L, RET_HEAD_DIM))
    return chunk_decay, decay_intra, bc(w_query), bc(w_state)


CONV_PAD = 8
XBC_BLK = 512


def _ssd_chunks(xbc_ref, z_ref, dt_ref, cw_ref, cbias_ref, dtb_ref,
                alog_ref, dskip_ref, nw_ref, tri_ref, o_ref, xpad_ref, st_ref, cb, tick):
    L = CHUNK
    gw = SSM_WIDTH // SSM_GROUPS
    rows_i = lax.broadcasted_iota(I32, (L, L), 0)
    cols_i = lax.broadcasted_iota(I32, (L, L), 1)
    causal = rows_i >= cols_i
    lo_lane = lax.broadcasted_iota(I32, (L, LANES), 1) < SSM_HEAD_DIM
    tri = tri_ref[...]
    a_neg = -jnp.exp(alog_ref[...])

    for i in range(cb):
        sl = slice(i * L, (i + 1) * L)
        xpad_ref[CONV_PAD:CONV_PAD + L, :] = xbc_ref[sl, :]
        u_parts = []
        for b in range(SSM_CONV_DIM // XBC_BLK):
            cs = slice(b * XBC_BLK, (b + 1) * XBC_BLK)
            acc = cbias_ref[:, cs]
            for t in range(SSM_CONV):
                r0 = CONV_PAD - (SSM_CONV - 1) + t
                acc = acc + xpad_ref[r0:r0 + L, cs] * cw_ref[t:t + 1, cs]
            u_parts.append(_silu(acc))
            tick()
        xpad_ref[0:CONV_PAD, :] = xpad_ref[L:L + CONV_PAD, :]
        xs = jnp.concatenate(u_parts[:2], axis=-1)
        bm = u_parts[2][:, :SSM_GROUPS * SSM_STATE]
        cm = u_parts[2][:, SSM_GROUPS * SSM_STATE:]

        dt_in = dt_ref[sl, :] + dtb_ref[...]
        dt = jnp.maximum(dt_in, 0.0) + jnp.log1p(jnp.exp(-jnp.abs(dt_in)))
        a = dt * a_neg
        a1, a2, a3 = _split3(a)
        a_cs = _dot(tri, a1) + _dot(tri, a2) + _dot(tri, a3)
        last = a_cs[L - 1:L, :]

        def per_head_lanes(v):
            lo = lo_lane[0:v.shape[0], :]
            return jnp.concatenate([jnp.where(lo, v[:, 2 * p:2 * p + 1], v[:, 2 * p + 1:2 * p + 2])
                                    for p in range(SSM_HEADS // 2)], axis=1)

        dt_e = per_head_lanes(dt)
        to_end = per_head_lanes(jnp.exp(last - a_cs))
        from_start = per_head_lanes(jnp.exp(a_cs))
        chunk_decay = per_head_lanes(jnp.exp(last))
        tick()
        x_dt = xs * dt_e
        xw = (x_dt * to_end).astype(BF16)
        acs_t = a_cs.T
        cmb = cm.astype(BF16)
        bmb = bm.astype(BF16)

        ys = []
        for g in range(SSM_GROUPS):
            ns = slice(g * SSM_STATE, (g + 1) * SSM_STATE)
            gs = slice(g * gw, (g + 1) * gw)
            cg = cmb[:, ns]
            cbg = _dot_nt(cg, bmb[:, ns])
            st = st_ref[g]
            y_off = _dot(cg, st.astype(BF16))
            new = _dot(bm[:, ns].T.astype(BF16), xw[:, gs])
            st_ref[g] = st * chunk_decay[:, gs] + new
            tick()
            for jp in range(gw // LANES):
                h0 = (g * gw + jp * LANES) // SSM_HEAD_DIM
                ms = []
                for hh in (h0, h0 + 1):
                    seg = a_cs[:, hh:hh + 1] - acs_t[hh:hh + 1, :]
                    dec = jnp.exp(jnp.where(causal, seg, -jnp.inf))
                    ms.append((cbg * dec).astype(BF16))
                lhs = jnp.concatenate(ms, axis=1)
                ls = slice(g * gw + jp * LANES, g * gw + (jp + 1) * LANES)
                xp = x_dt[:, ls]
                rhs = jnp.concatenate([jnp.where(lo_lane, xp, 0.0),
                                       jnp.where(lo_lane, 0.0, xp)], axis=0).astype(BF16)
                y_diag = _dot(lhs, rhs)
                ys.append(y_diag + y_off[:, jp * LANES:(jp + 1) * LANES] * from_start[:, ls])
                tick()
        y = jnp.concatenate(ys, axis=1) + xs * dskip_ref[...]
        y = y * _silu(z_ref[sl, :])
        outs = []
        for g in range(SSM_GROUPS):
            yg = y[:, g * gw:(g + 1) * gw]
            ms_ = jnp.mean(yg * yg, axis=-1, keepdims=True)
            outs.append(yg * lax.rsqrt(ms_ + EPS))
        o_ref[sl, :] = (jnp.concatenate(outs, axis=1) * nw_ref[...]).astype(BF16)


MIX_CB = 2
MIX_VMEM = 60 * 1024 * 1024
COL_Z = 4 * RET_WIDTH
COL_XBC = COL_Z + SSM_WIDTH
PROJ_WIDTHS = (RET_WIDTH, RET_WIDTH, RET_WIDTH, RET_WIDTH, SSM_WIDTH, SSM_CONV_DIM, LANES)
PROJ_SLAB = 256


def _mixer_body(cd_ref, x_ref, n1w_ref, wt_hbm, wdt_ref, cos_ref, sin_ref, dec_ref, wq_ref, ws_ref, rnw_ref,
                cw_ref, cbias_ref, dtb_ref, alog_ref, dskip_ref, snw_ref, tri_ref,
                yret_ref, yssm_ref, w_ref, hn_ref, pq, pk, pv, pg, pz, pxbc, pdt,
                st_ret, xpad_ref, st_ssd, sem, *, cb):
    @pl.when(pl.program_id(0) == 0)
    def _():
        cp = pltpu.make_async_copy(wt_hbm, w_ref, sem)
        cp.start()
        st_ret[...] = jnp.zeros_like(st_ret)
        xpad_ref[0:CONV_PAD, :] = jnp.zeros((CONV_PAD, SSM_CONV_DIM), F32)
        st_ssd[...] = jnp.zeros_like(st_ssd)
        cp.wait()

    def slabs(ref, col0):
        def slab(lo):
            def go():
                ref[:, lo:lo + PROJ_SLAB] = _dot_nt(hn_ref[...], w_ref[col0 + lo:col0 + lo + PROJ_SLAB, :])
            return go
        return [slab(lo) for lo in range(0, ref.shape[1], PROJ_SLAB)]

    x = x_ref[...]
    ms = jnp.mean(x * x, axis=-1, keepdims=True)
    hn_ref[...] = (x * lax.rsqrt(ms + EPS) * n1w_ref[...]).astype(BF16)
    for piece in slabs(pxbc, COL_XBC) + slabs(pz, COL_Z):
        piece()
    pdt[...] = _dot_nt(hn_ref[...], wdt_ref[...])

    pieces = slabs(pq, 0) + slabs(pk, RET_WIDTH) + slabs(pv, 2 * RET_WIDTH) + slabs(pg, 3 * RET_WIDTH)
    n_pieces = len(pieces)
    n_ticks = cb * (SSM_CONV_DIM // XBC_BLK + 1 + SSM_GROUPS + SSM_WIDTH // LANES)
    calls = [0]

    def tick():
        calls[0] += 1
        while pieces and (n_pieces - len(pieces)) * n_ticks < calls[0] * n_pieces:
            pieces.pop(0)()

    _ssd_chunks(pxbc, pz, pdt, cw_ref, cbias_ref, dtb_ref, alog_ref, dskip_ref, snw_ref, tri_ref,
                yssm_ref, xpad_ref, st_ssd, cb, tick)
    while pieces:
        pieces.pop(0)()
    _retention_chunks(cd_ref, pq, pk, pv, pg, cos_ref, sin_ref, dec_ref, wq_ref, ws_ref, rnw_ref,
                      yret_ref, st_ret, cb, lambda: None)


def _mixer(x, n1w, w_main_t, w_dt_t, cos, sin, ret_nw, ret_consts,
           conv_w, conv_b, dt_bias, a_log, d_skip_e, ssm_nw, tri):
    T = x.shape[0]
    cb = min(MIX_CB, T // CHUNK)
    rows = cb * CHUNK
    chunk_decay, decay_intra, w_query, w_state = ret_consts
    hd = RET_HEAD_DIM
    half = hd // 2
    full = lambda shape: pl.BlockSpec(shape, lambda c, cd: (0,) * len(shape))
    rowblk = lambda width: pl.BlockSpec((rows, width), lambda c, cd: (c, 0))
    scratch = lambda width: pltpu.VMEM((rows, width), F32)
    grid_spec = pltpu.PrefetchScalarGridSpec(
        num_scalar_prefetch=1,
        grid=(T // rows,),
        in_specs=[
            rowblk(D_MODEL),
            full((1, D_MODEL)),
            pl.BlockSpec(memory_space=pl.ANY),
            full((LANES, D_MODEL)),
            rowblk(half), rowblk(half),
            full((RET_HEADS, CHUNK, CHUNK)),
            full((RET_HEADS, CHUNK, hd)),
            full((RET_HEADS, CHUNK, hd)),
            full((1, RET_WIDTH)),
            full((SSM_CONV, SSM_CONV_DIM)),
            full((1, SSM_CONV_DIM)),
            full((1, LANES)),
            full((1, LANES)),
            full((1, SSM_WIDTH)),
            full((1, SSM_WIDTH)),
            full((CHUNK, CHUNK)),
        ],
        out_specs=[rowblk(RET_WIDTH), rowblk(SSM_WIDTH)],
        scratch_shapes=[
            pltpu.VMEM((MAIN_PROJ, D_MODEL), BF16),
            pltpu.VMEM((rows, D_MODEL), BF16),
            *[scratch(w) for w in PROJ_WIDTHS],
            pltpu.VMEM((RET_HEADS, hd, hd), F32),
            pltpu.VMEM((CHUNK + CONV_PAD, SSM_CONV_DIM), F32),
            pltpu.VMEM((SSM_GROUPS, SSM_STATE, SSM_WIDTH // SSM_GROUPS), F32),
            pltpu.SemaphoreType.DMA(()),
        ],
    )
    return pl.pallas_call(
        functools.partial(_mixer_body, cb=cb),
        grid_spec=grid_spec,
        out_shape=[jax.ShapeDtypeStruct((T, RET_WIDTH), BF16),
                   jax.ShapeDtypeStruct((T, SSM_WIDTH), BF16)],
        compiler_params=_cparams(("arbitrary",), MIX_VMEM),
        name="mixer",
    )(chunk_decay, x, n1w, w_main_t, w_dt_t, cos, sin, decay_intra, w_query, w_state, ret_nw,
      conv_w, conv_b, dt_bias, a_log, d_skip_e, ssm_nw, tri)


OUT_TM = 512
ROUTE_ROWS = 8 + N_EXPERTS


def _out_router_body(x_ref, yr_ref, ys_ref, wo_ref, nw_ref, wr_ref, br_ref,
                     h1_ref, h2t_ref, cw_ref, ids_ref):
    tm = x_ref.shape[0]
    h1 = x_ref[...] + _dot(yr_ref[...], wo_ref[0:RET_WIDTH, :]) + _dot(ys_ref[...], wo_ref[RET_WIDTH:, :])
    h1_ref[...] = h1
    ms = jnp.mean(h1 * h1, axis=-1, keepdims=True)
    h2 = h1 * lax.rsqrt(ms + EPS) * nw_ref[...]
    for s in range(Y_ROWS):
        h2t_ref[:, s, :] = h2[:, s * LANES:(s + 1) * LANES]

    logits = _dot_nt(wr_ref[...], h2.astype(BF16)) + br_ref[...]
    row = lax.broadcasted_iota(I32, (8, tm), 0)
    lg = jnp.where(row < N_GROUPS, logits[0:8], -jnp.inf)
    m = jnp.max(lg, axis=0, keepdims=True)
    p_sel = 1.0 / jnp.sum(jnp.exp(lg - m), axis=0, keepdims=True)
    g_sel = jnp.min(jnp.where(lg == m, row, 8), axis=0, keepdims=True)
    le = jnp.zeros((GROUP_EXPERTS, tm), F32)
    for g in range(N_GROUPS):
        le = jnp.where(g_sel == g, logits[8 + g * GROUP_EXPERTS:8 + (g + 1) * GROUP_EXPERTS], le)
    m2 = jnp.max(le, axis=0, keepdims=True)
    ee = jnp.exp(le - m2)
    pe = ee / jnp.sum(ee, axis=0, keepdims=True)
    v1 = jnp.max(pe, axis=0, keepdims=True)
    i1 = jnp.min(jnp.where(pe == v1, row, 8), axis=0, keepdims=True)
    pe2 = jnp.where(row == i1, -1.0, pe)
    v2 = jnp.max(pe2, axis=0, keepdims=True)
    i2 = jnp.min(jnp.where(pe2 == v2, row, 8), axis=0, keepdims=True)
    tw = v1 + v2
    c1 = v1 / tw * p_sel
    c2 = v2 / tw * p_sel
    e1 = g_sel * GROUP_EXPERTS + i1
    e2 = g_sel * GROUP_EXPERTS + i2
    ids = jnp.where(row == 0, e1, jnp.where(row == 1, e2, 0))
    for b in range(tm // SORT_BLK):
        ids_ref[b] = ids[:, b * SORT_BLK:(b + 1) * SORT_BLK]
    cw8 = jnp.where(row == 0, c1, jnp.where(row == 1, c2, 0.0))
    cw = jnp.concatenate([cw8, jnp.zeros((LANES - 8, tm), F32)], axis=0)
    cw_ref[...] = cw.T


def _out_router(x, y_ret, y_ssm, w_out, nw, wr_t, br):
    T = x.shape[0]
    tm = min(OUT_TM, T)
    full = lambda shape: pl.BlockSpec(shape, lambda i: (0,) * len(shape))
    return pl.pallas_call(
        _out_router_body,
        grid=(T // tm,),
        in_specs=[
            pl.BlockSpec((tm, D_MODEL), lambda i: (i, 0)),
            pl.BlockSpec((tm, RET_WIDTH), lambda i: (i, 0)),
            pl.BlockSpec((tm, SSM_WIDTH), lambda i: (i, 0)),
            pl.BlockSpec((D_MODEL, D_MODEL), lambda i: (0, 0), pipeline_mode=pl.Buffered(1)),
            full((1, D_MODEL)),
            full((LANES, D_MODEL)),
            full((LANES, 1)),
        ],
        out_specs=[
            pl.BlockSpec((tm, D_MODEL), lambda i: (i, 0)),
            pl.BlockSpec((tm, Y_ROWS, LANES), lambda i: (i, 0, 0)),
            pl.BlockSpec((tm, LANES), lambda i: (i, 0)),
            pl.BlockSpec((tm // SORT_BLK, 8, SORT_BLK), lambda i: (i, 0, 0)),
        ],
        out_shape=[
            jax.ShapeDtypeStruct((T, D_MODEL), F32),
            jax.ShapeDtypeStruct((T, Y_ROWS, LANES), F32),
            jax.ShapeDtypeStruct((T, LANES), F32),
            jax.ShapeDtypeStruct((T // SORT_BLK, 8, SORT_BLK), I32),
        ],
        compiler_params=_cparams(("parallel",)),
        name="out_router",
    )(x, y_ret, y_ssm, w_out, nw, wr_t, br)


SORT_BLK = 256
MOE_TM = 256


def _sort_index_body(ids_ref, tri_ref, ltri_ref, dest_ref, cnt_ref, rank_ref):
    nblk = ids_ref.shape[0]
    row_e = lax.broadcasted_iota(I32, (N_EXPERTS, SORT_BLK), 0)
    row8 = lax.broadcasted_iota(I32, (8, SORT_BLK), 0)

    def onehots(b):
        ids = ids_ref[b]
        return row_e == ids[0:1], row_e == ids[1:2]

    def rank_blk(b, carry):
        oh1, oh2 = onehots(b)
        ohf = jnp.where(oh1 | oh2, 1.0, 0.0)
        incl = _dot(ohf.astype(BF16), tri_ref[...])
        base = carry + incl - 1.0
        r1 = jnp.sum(jnp.where(oh1, base, 0.0), axis=0, keepdims=True)
        r2 = jnp.sum(jnp.where(oh2, base, 0.0), axis=0, keepdims=True)
        rank_ref[b] = jnp.where(row8 == 0, r1, jnp.where(row8 == 1, r2, 0.0))
        return carry + jnp.sum(ohf, axis=1, keepdims=True)

    cnt = lax.fori_loop(0, nblk, rank_blk, jnp.zeros((N_EXPERTS, 1), F32))
    cnt_ref[...] = jnp.broadcast_to(cnt, cnt_ref.shape)
    tiles = jnp.floor((cnt + (MOE_TM - 1.0)) / MOE_TM)
    tiles_b = jnp.broadcast_to(tiles, (N_EXPERTS, LANES)).astype(BF16)
    off = _dot(ltri_ref[...], tiles_b)[:, 0:1] * MOE_TM

    def dest_blk(b, carry):
        oh1, oh2 = onehots(b)
        o1 = jnp.sum(jnp.where(oh1, off, 0.0), axis=0, keepdims=True)
        o2 = jnp.sum(jnp.where(oh2, off, 0.0), axis=0, keepdims=True)
        d = rank_ref[b] + jnp.where(row8 == 0, o1, jnp.where(row8 == 1, o2, 0.0))
        dest_ref[b] = d.astype(I32)
        return carry

    lax.fori_loop(0, nblk, dest_blk, 0)


def _sort_index(ids, tri_u, ltri):
    nblk = ids.shape[0]
    return pl.pallas_call(
        _sort_index_body,
        out_shape=[jax.ShapeDtypeStruct((nblk, 8, SORT_BLK), I32),
                   jax.ShapeDtypeStruct((N_EXPERTS, LANES), F32)],
        scratch_shapes=[pltpu.VMEM((nblk, 8, SORT_BLK), F32)],
        compiler_params=_cparams(None),
        name="sort_index",
    )(ids, tri_u, ltri)


ROW_DMA_PRIORITY = 1


def _moe_body(te_ref, tf_ref, nx_ref, nu_ref, dest_ref, lo_ref, hi_ref,
              h2t_ref, wg_ref, wu_ref, wd_ref, yt_ref,
              srow_ref, wcount, xbuf, ystage, wsg, wsu, wsd, wgb_ref, wub_ref, wdb_ref, gsem, ssem, wsem):
    i = pl.program_id(0)
    nu = nu_ref[0]
    T = h2t_ref.shape[0]

    def gather_row(tile, slot, r):
        tok = srow_ref[tile * MOE_TM + r] & (T - 1)
        return pltpu.make_async_copy(h2t_ref.at[tok], xbuf.at[slot, :, r, :], gsem.at[slot])

    def scatter_row(tile, slot, r):
        return pltpu.make_async_copy(ystage.at[slot, :, r, :], yt_ref.at[srow_ref[tile * MOE_TM + r]],
                                     ssem.at[slot])

    def gather_tile(slot):
        return pltpu.make_async_copy(h2t_ref.at[pl.ds(0, MOE_TM)], h2t_ref.at[pl.ds(0, MOE_TM)], gsem.at[slot])

    def scatter_tile(slot):
        return pltpu.make_async_copy(yt_ref.at[pl.ds(0, MOE_TM)], yt_ref.at[pl.ds(0, MOE_TM)], ssem.at[slot])

    def weight_copies(e, ws):
        return (pltpu.make_async_copy(wg_ref.at[e], wsg.at[ws], wsem.at[ws, 0]),
                pltpu.make_async_copy(wu_ref.at[e], wsu.at[ws], wsem.at[ws, 1]),
                pltpu.make_async_copy(wd_ref.at[e], wsd.at[ws], wsem.at[ws, 2]))

    @pl.when(i == 0)
    def _():
        wcount[0] = 0
        for c in weight_copies(te_ref[0], 0):
            c.start()

        def tok(t, c):
            srow_ref[dest_ref[t]] = t
            srow_ref[dest_ref[T + t]] = T + t
            return c

        lax.fori_loop(0, T, tok, 0, unroll=8)

        def seg(e, c):
            def pad(r, c2):
                srow_ref[r] = 2 * T + (r & (MOE_TM - 1))
                return c2

            lax.fori_loop(lo_ref[e], hi_ref[e], pad, 0)
            return c

        lax.fori_loop(0, lo_ref.shape[0], seg, 0)

        ystage[...] = jnp.zeros_like(ystage)

        def spare(r, c):
            pltpu.make_async_copy(ystage.at[0, :, r, :], yt_ref.at[2 * T + r], ssem.at[0]).start()
            return c

        lax.fori_loop(0, MOE_TM, spare, 0)
        scatter_tile(0).wait()
        for r in range(MOE_TM):
            gather_row(0, 0, r).start(priority=ROW_DMA_PRIORITY)

    @pl.when(tf_ref[i] == 1)
    def _():
        ws = lax.rem(wcount[0], 2)
        wcount[0] = wcount[0] + 1
        for c in weight_copies(te_ref[i], ws):
            c.wait()

        @pl.when(nx_ref[i] >= 0)
        def _():
            for c in weight_copies(nx_ref[i], 1 - ws):
                c.start()

        wgb_ref[...] = wsg[ws].astype(BF16)
        wub_ref[...] = wsu[ws].astype(BF16)
        wdb_ref[...] = wsd[ws].astype(BF16)

    slot = lax.rem(i, 2)
    other = 1 - slot

    @pl.when(i < nu)
    def _():
        gather_tile(slot).wait()

        @pl.when(i >= 1)
        def _():
            scatter_tile(slot).wait()

        nxt = jnp.minimum(i + 1, nu - 1)
        prv = jnp.maximum(i - 1, 0)
        for r in range(MOE_TM):
            gather_row(nxt, other, r).start(priority=r % 2)
            scatter_row(prv, other, r).start(priority=(r + 1) % 2)

        x = jnp.concatenate([xbuf[slot, s].astype(BF16) for s in range(Y_ROWS)], axis=1)
        a = _dot(x, wgb_ref[...])
        u = _dot(x, wub_ref[...])
        act = (_silu(a) * u).astype(BF16)
        y = _dot(act, wdb_ref[...])
        for s in range(Y_ROWS):
            ystage[slot, s] = y[:, s * LANES:(s + 1) * LANES]

    @pl.when(i == nu)
    def _():
        gather_tile(slot).wait()
        scatter_tile(slot).wait()

        def last(r, c):
            scatter_row(nu - 1, other, r).start()
            return c

        lax.fori_loop(0, MOE_TM, last, 0)
        scatter_tile(other).wait()


def _moe(tile_expert, tile_first, next_expert, n_used, dest, pad_lo, pad_hi, h2t, w_gate, w_up, w_down):
    T = h2t.shape[0]
    assert T & (T - 1) == 0 and T >= MOE_TM, "token index is recovered from the row table by masking"
    n_tiles = tile_expert.shape[0] - 1
    hbm = pl.BlockSpec(memory_space=pl.ANY)
    grid_spec = pltpu.PrefetchScalarGridSpec(
        num_scalar_prefetch=7,
        grid=(n_tiles + 1,),
        in_specs=[hbm, hbm, hbm, hbm],
        out_specs=hbm,
        scratch_shapes=[pltpu.SMEM((n_tiles * MOE_TM,), I32),
                        pltpu.SMEM((1,), I32),
                        pltpu.VMEM((2, Y_ROWS, MOE_TM, LANES), F32),
                        pltpu.VMEM((2, Y_ROWS, MOE_TM, LANES), F32),
                        pltpu.VMEM((2, D_MODEL, D_EXPERT), F32),
                        pltpu.VMEM((2, D_MODEL, D_EXPERT), F32),
                        pltpu.VMEM((2, D_EXPERT, D_MODEL), F32),
                        pltpu.VMEM((D_MODEL, D_EXPERT), BF16),
                        pltpu.VMEM((D_MODEL, D_EXPERT), BF16),
                        pltpu.VMEM((D_EXPERT, D_MODEL), BF16),
                        pltpu.SemaphoreType.DMA((2,)),
                        pltpu.SemaphoreType.DMA((2,)),
                        pltpu.SemaphoreType.DMA((2, 3))],
    )
    return pl.pallas_call(
        _moe_body,
        grid_spec=grid_spec,
        out_shape=jax.ShapeDtypeStruct((2 * T + MOE_TM, Y_ROWS, LANES), F32),
        compiler_params=_cparams(("arbitrary",)),
        name="moe",
    )(tile_expert, tile_first, next_expert, n_used, dest, pad_lo, pad_hi, h2t, w_gate, w_up, w_down)


COMB_TM = 512


def _combine_body(h1_ref, y0_ref, y1_ref, cw_ref, nw_ref, o_ref):
    tm = h1_ref.shape[0]
    c0 = cw_ref[:, 0:1]
    c1 = cw_ref[:, 1:2]
    cols = []
    for s in range(Y_ROWS):
        moe = c0 * y0_ref[:, s, :] + c1 * y1_ref[:, s, :]
        cols.append(h1_ref[:, s * LANES:(s + 1) * LANES] + moe)
    h = jnp.concatenate(cols, axis=1)
    ms = jnp.mean(h * h, axis=-1, keepdims=True)
    o_ref[...] = h * lax.rsqrt(ms + EPS) * nw_ref[...]


def _combine(h1, y_tok, cw, nw):
    T = h1.shape[0]
    tm = min(COMB_TM, T)
    nt = T // tm
    return pl.pallas_call(
        _combine_body,
        grid=(nt,),
        in_specs=[
            pl.BlockSpec((tm, D_MODEL), lambda i: (i, 0)),
            pl.BlockSpec((tm, Y_ROWS, LANES), lambda i: (i, 0, 0)),
            pl.BlockSpec((tm, Y_ROWS, LANES), lambda i: (i + nt, 0, 0)),
            pl.BlockSpec((tm, LANES), lambda i: (i, 0)),
            pl.BlockSpec((1, D_MODEL), lambda i: (0, 0)),
        ],
        out_specs=pl.BlockSpec((tm, D_MODEL), lambda i: (i, 0)),
        out_shape=jax.ShapeDtypeStruct((T, D_MODEL), F32),
        compiler_params=_cparams(("parallel",)),
        name="combine",
    )(h1, y_tok, y_tok, cw, nw)


def _tile_plan(cnt, n_tiles):
    tiles = (cnt + (MOE_TM - 1)) // MOE_TM
    ends = jnp.cumsum(tiles)
    starts = ends - tiles
    n_used = ends[-1]
    step = jnp.arange(n_tiles + 1, dtype=I32)
    tile = jnp.minimum(step, jnp.maximum(n_used - 1, 0))
    tile_expert = jnp.sum((ends[None, :] <= tile[:, None]).astype(I32), axis=1)
    tile_first = ((step == starts[tile_expert]) & (step < n_used)).astype(I32)
    nxt_tile = ends[tile_expert]
    nxt_expert = jnp.sum((ends[None, :] <= nxt_tile[:, None]).astype(I32), axis=1)
    next_expert = jnp.where(nxt_tile < n_used, nxt_expert, -1).astype(I32)
    pad_lo = starts * MOE_TM + cnt
    pad_hi = ends * MOE_TM
    return (tile_expert, tile_first, next_expert, n_used.reshape(1).astype(I32),
            pad_lo.astype(I32), pad_hi.astype(I32))


def kernel(x, positions, norm1_w, w_in, conv_w, conv_b, dt_bias, a_log, d_skip, ret_norm_w,
           ssm_norm_w, w_out, norm2_w, w_router_group, b_router_group, w_router_expert,
           b_router_expert, w_expert_gate, w_expert_up, w_expert_down, final_norm_w):
    B, T, D = x.shape
    assert B == 1 and D == D_MODEL and T % CHUNK == 0
    xf = x.reshape(T, D)
    pad_l = lambda v: jnp.pad(v, ((0, 0), (0, LANES - v.shape[-1])))

    w_in_t = jnp.swapaxes(w_in[0], 0, 1)
    w_dt = jnp.pad(w_in_t[MAIN_PROJ:], ((0, LANES - (w_in_t.shape[0] - MAIN_PROJ)), (0, 0))).astype(BF16)
    half = RET_HEAD_DIM // 2
    inv = (ROPE_THETA ** (-jnp.arange(half, dtype=F32) / half)).reshape(1, half)
    tri = (jnp.arange(CHUNK)[:, None] >= jnp.arange(CHUNK)[None, :]).astype(BF16)
    d_skip_e = jnp.repeat(d_skip[0], SSM_HEAD_DIM).reshape(1, SSM_WIDTH)
    wr_t = jnp.zeros((LANES, D), F32)
    wr_t = wr_t.at[0:N_GROUPS].set(w_router_group[0].T).at[8:8 + N_EXPERTS].set(w_router_expert[0].T)
    br = jnp.zeros((LANES,), F32)
    br = br.at[0:N_GROUPS].set(b_router_group[0]).at[8:8 + N_EXPERTS].set(b_router_expert[0])
    tri_u = (jnp.arange(SORT_BLK)[:, None] <= jnp.arange(SORT_BLK)[None, :]).astype(BF16)
    ltri = (jnp.arange(N_EXPERTS)[:, None] > jnp.arange(N_EXPERTS)[None, :]).astype(BF16)

    w_main, w_out_b, cos, sin = _prep(w_in_t, w_out[0], positions.reshape(T, 1).astype(F32), inv)
    y_ret, y_ssm = _mixer(xf, norm1_w[0].reshape(1, D), w_main, w_dt, cos, sin,
                          ret_norm_w[0].reshape(1, RET_WIDTH), _retention_consts(),
                          conv_w[0], conv_b[0].reshape(1, -1), pad_l(dt_bias[0].reshape(1, -1)),
                          pad_l(a_log[0].reshape(1, -1)), d_skip_e, ssm_norm_w[0].reshape(1, -1), tri)
    h1, h2t, cw, ids = _out_router(xf, y_ret, y_ssm, w_out_b, norm2_w[0].reshape(1, D),
                                   wr_t.astype(BF16), br.reshape(LANES, 1))

    dest_blk, cnt = _sort_index(ids, tri_u, ltri)
    dest = dest_blk[:, 0:2, :].transpose(1, 0, 2).reshape(2 * T)
    n_tiles = (2 * T) // MOE_TM + N_EXPERTS
    tile_expert, tile_first, next_expert, n_used, pad_lo, pad_hi = _tile_plan(cnt[:, 0].astype(I32), n_tiles)
    y_tok = _moe(tile_expert, tile_first, next_expert, n_used, dest, pad_lo, pad_hi, h2t,
                 w_expert_gate[0], w_expert_up[0], w_expert_down[0])
    out = _combine(h1, y_tok, cw, final_norm_w.reshape(1, D))
    return out.reshape(B, T, D)
```

```python
import functools

import jax
import jax.numpy as jnp
from jax import lax
from jax.experimental import pallas as pl
from jax.experimental.pallas import tpu as pltpu

F32 = jnp.float32
BF16 = jnp.bfloat16
I32 = jnp.int32

D_MODEL = 2048
EPS = 1e-6
CHUNK = 128
RET_HEADS = 4
RET_HEAD_DIM = 256
RET_WIDTH = RET_HEADS * RET_HEAD_DIM
ROPE_THETA = 10000.0
SSM_WIDTH = 1024
SSM_HEAD_DIM = 64
SSM_HEADS = SSM_WIDTH // SSM_HEAD_DIM
SSM_GROUPS = 2
SSM_STATE = 128
SSM_CONV = 4
SSM_CONV_DIM = SSM_WIDTH + 2 * SSM_GROUPS * SSM_STATE
MAIN_PROJ = 4 * RET_WIDTH + SSM_WIDTH + SSM_CONV_DIM
N_GROUPS = 4
GROUP_EXPERTS = 8
N_EXPERTS = N_GROUPS * GROUP_EXPERTS
D_EXPERT = 512
LANES = 128
Y_ROWS = D_MODEL // LANES

VMEM_LIMIT = 56 * 1024 * 1024


def _cparams(sem, vmem=VMEM_LIMIT):
    return pltpu.CompilerParams(dimension_semantics=sem, vmem_limit_bytes=vmem)


def _silu(x):
    return x * (1.0 / (1.0 + jnp.exp(-x)))


def _dot(a, b):
    return jnp.dot(a, b, preferred_element_type=F32)


def _dot_nt(a, b):
    return lax.dot_general(a, b, (((1,), (1,)), ((), ())), preferred_element_type=F32)


def _split3(a):
    a1 = a.astype(BF16)
    r1 = a - a1.astype(F32)
    a2 = r1.astype(BF16)
    a3 = (r1 - a2.astype(F32)).astype(BF16)
    return a1, a2, a3


CAST_ROWS = 512
ROPE_ROWS = 1024


def _prep_body(win_ref, wout_ref, pos_ref, inv_ref, winb_ref, woutb_ref, cos_ref, sin_ref, *, n_in, n_rope):
    j = pl.program_id(0)

    @pl.when(j < n_in)
    def _():
        winb_ref[...] = win_ref[...].astype(BF16)

    @pl.when(j >= n_in)
    def _():
        woutb_ref[...] = wout_ref[...].astype(BF16)

    @pl.when(j < n_rope)
    def _():
        ang = pos_ref[...] * inv_ref[...]
        cos_ref[...] = jnp.cos(ang)
        sin_ref[...] = jnp.sin(ang)


def _prep(w_in_t, w_out, pos, inv):
    T = pos.shape[0]
    cols = w_in_t.shape[1]
    half = inv.shape[1]
    n_in = MAIN_PROJ // CAST_ROWS
    n_out = w_out.shape[0] // CAST_ROWS
    rr = min(ROPE_ROWS, T)
    n_rope = T // rr
    assert n_rope <= n_in + n_out
    in_blk = lambda j: (jnp.minimum(j, n_in - 1), 0)
    out_blk = lambda j: (jnp.clip(j - n_in, 0, n_out - 1), 0)
    rope_blk = lambda j: (jnp.minimum(j, n_rope - 1), 0)
    return pl.pallas_call(
        functools.partial(_prep_body, n_in=n_in, n_rope=n_rope),
        grid=(n_in + n_out,),
        in_specs=[pl.BlockSpec((CAST_ROWS, cols), in_blk),
                  pl.BlockSpec((CAST_ROWS, w_out.shape[1]), out_blk),
                  pl.BlockSpec((rr, 1), rope_blk),
                  pl.BlockSpec((1, half), lambda j: (0, 0))],
        out_specs=[pl.BlockSpec((CAST_ROWS, cols), in_blk),
                   pl.BlockSpec((CAST_ROWS, w_out.shape[1]), out_blk),
                   pl.BlockSpec((rr, half), rope_blk),
                   pl.BlockSpec((rr, half), rope_blk)],
        out_shape=[jax.ShapeDtypeStruct((MAIN_PROJ, cols), BF16),
                   jax.ShapeDtypeStruct(w_out.shape, BF16),
                   jax.ShapeDtypeStruct((T, half), F32),
                   jax.ShapeDtypeStruct((T, half), F32)],
        compiler_params=_cparams(("arbitrary",)),
        name="prep",
    )(w_in_t, w_out, pos, inv)


def _retention_chunks(cd_ref, q_ref, k_ref, v_ref, g_ref, cos_ref, sin_ref,
                      dec_ref, wq_ref, ws_ref, nw_ref, o_ref, st_ref, cb, tick):
    hd = RET_HEAD_DIM
    half = hd // 2
    for i in range(cb):
        sl = slice(i * CHUNK, (i + 1) * CHUNK)
        cos = cos_ref[sl, :]
        sin = sin_ref[sl, :]

        def rope(x):
            x1, x2 = x[:, :half], x[:, half:]
            return jnp.concatenate([x1 * cos - x2 * sin, x2 * cos + x1 * sin], axis=-1)

        for h in range(RET_HEADS):
            hs = slice(h * hd, (h + 1) * hd)
            q = rope(q_ref[sl, hs])
            k = rope(k_ref[sl, hs]) * (hd ** -0.5)
            v = v_ref[sl, hs]
            qb = q.astype(BF16)
            kb = k.astype(BF16)
            scores = _dot_nt(qb, kb) * dec_ref[h]
            y = _dot(scores.astype(BF16), v.astype(BF16))
            st = st_ref[h]
            y = y + _dot(qb, st.astype(BF16)) * wq_ref[h]
            vw = (v * ws_ref[h]).astype(BF16)
            new = _dot(k.T.astype(BF16), vw)
            st_ref[h] = st * cd_ref[h] + new
            mu = jnp.mean(y, axis=-1, keepdims=True)
            d = y - mu
            var = jnp.mean(d * d, axis=-1, keepdims=True)
            yn = d * lax.rsqrt(var + EPS)
            o_ref[sl, hs] = (yn * nw_ref[:, hs] * _silu(g_ref[sl, hs])).astype(BF16)
            tick()


def _retention_consts():
    H, L = RET_HEADS, CHUNK
    log_gamma = jnp.log1p(-(2.0 ** (-5.0 - jnp.arange(H, dtype=F32))))
    idx = jnp.arange(L, dtype=F32)
    diff = idx[:, None] - idx[None, :]
    causal = diff >= 0
    decay_intra = jnp.where(causal[None], jnp.exp(jnp.where(causal, diff, 0.0)[None] * log_gamma[:, None, None]), 0.0)
    w_state = jnp.exp((L - 1.0 - idx)[None, :] * log_gamma[:, None])
    w_query = jnp.exp((idx + 1.0)[None, :] * log_gamma[:, None])
    chunk_decay = jnp.exp(L * log_gamma)
    bc = lambda w: jnp.broadcast_to(w[:, :, None], (H, L, RET_HEAD_DIM))
    return chunk_decay, decay_intra, bc(w_query), bc(w_state)


CONV_PAD = 8
XBC_BLK = 512


def _ssd_chunks(xbc_ref, z_ref, dt_ref, cw_ref, cbias_ref, dtb_ref,
                alog_ref, dskip_ref, nw_ref, tri_ref, o_ref, xpad_ref, st_ref, cb, tick):
    L = CHUNK
    gw = SSM_WIDTH // SSM_GROUPS
    rows_i = lax.broadcasted_iota(I32, (L, L), 0)
    cols_i = lax.broadcasted_iota(I32, (L, L), 1)
    causal = rows_i >= cols_i
    lo_lane = lax.broadcasted_iota(I32, (L, LANES), 1) < SSM_HEAD_DIM
    tri = tri_ref[...]
    a_neg = -jnp.exp(alog_ref[...])

    for i in range(cb):
        sl = slice(i * L, (i + 1) * L)
        xpad_ref[CONV_PAD:CONV_PAD + L, :] = xbc_ref[sl, :]
        u_parts = []
        for b in range(SSM_CONV_DIM // XBC_BLK):
            cs = slice(b * XBC_BLK, (b + 1) * XBC_BLK)
            acc = cbias_ref[:, cs]
            for t in range(SSM_CONV):
                r0 = CONV_PAD - (SSM_CONV - 1) + t
                acc = acc + xpad_ref[r0:r0 + L, cs] * cw_ref[t:t + 1, cs]
            u_parts.append(_silu(acc))
            tick()
        xpad_ref[0:CONV_PAD, :] = xpad_ref[L:L + CONV_PAD, :]
        xs = jnp.concatenate(u_parts[:2], axis=-1)
        bm = u_parts[2][:, :SSM_GROUPS * SSM_STATE]
        cm = u_parts[2][:, SSM_GROUPS * SSM_STATE:]

        dt_in = dt_ref[sl, :] + dtb_ref[...]
        dt = jnp.maximum(dt_in, 0.0) + jnp.log1p(jnp.exp(-jnp.abs(dt_in)))
        a = dt * a_neg
        a1, a2, a3 = _split3(a)
        a_cs = _dot(tri, a1) + _dot(tri, a2) + _dot(tri, a3)
        last = a_cs[L - 1:L, :]

        def per_head_lanes(v):
            lo = lo_lane[0:v.shape[0], :]
            return jnp.concatenate([jnp.where(lo, v[:, 2 * p:2 * p + 1], v[:, 2 * p + 1:2 * p + 2])
                                    for p in range(SSM_HEADS // 2)], axis=1)

        dt_e = per_head_lanes(dt)
        to_end = per_head_lanes(jnp.exp(last - a_cs))
        from_start = per_head_lanes(jnp.exp(a_cs))
        chunk_decay = per_head_lanes(jnp.exp(last))
        tick()
        x_dt = xs * dt_e
        xw = (x_dt * to_end).astype(BF16)
        acs_t = a_cs.T
        cmb = cm.astype(BF16)
        bmb = bm.astype(BF16)

        ys = []
        for g in range(SSM_GROUPS):
            ns = slice(g * SSM_STATE, (g + 1) * SSM_STATE)
            gs = slice(g * gw, (g + 1) * gw)
            cg = cmb[:, ns]
            cbg = _dot_nt(cg, bmb[:, ns])
            st = st_ref[g]
            y_off = _dot(cg, st.astype(BF16))
            new = _dot(bm[:, ns].T.astype(BF16), xw[:, gs])
            st_ref[g] = st * chunk_decay[:, gs] + new
            tick()
            for jp in range(gw // LANES):
                h0 = (g * gw + jp * LANES) // SSM_HEAD_DIM
                ms = []
                for hh in (h0, h0 + 1):
                    seg = a_cs[:, hh:hh + 1] - acs_t[hh:hh + 1, :]
                    dec = jnp.exp(jnp.where(causal, seg, -jnp.inf))
                    ms.append((cbg * dec).astype(BF16))
                lhs = jnp.concatenate(ms, axis=1)
                ls = slice(g * gw + jp * LANES, g * gw + (jp + 1) * LANES)
                xp = x_dt[:, ls]
                rhs = jnp.concatenate([jnp.where(lo_lane, xp, 0.0),
                                       jnp.where(lo_lane, 0.0, xp)], axis=0).astype(BF16)
                y_diag = _dot(lhs, rhs)
                ys.append(y_diag + y_off[:, jp * LANES:(jp + 1) * LANES] * from_start[:, ls])
                tick()
        y = jnp.concatenate(ys, axis=1) + xs * dskip_ref[...]
        y = y * _silu(z_ref[sl, :])
        outs = []
        for g in range(SSM_GROUPS):
            yg = y[:, g * gw:(g + 1) * gw]
            ms_ = jnp.mean(yg * yg, axis=-1, keepdims=True)
            outs.append(yg * lax.rsqrt(ms_ + EPS))
        o_ref[sl, :] = (jnp.concatenate(outs, axis=1) * nw_ref[...]).astype(BF16)


MIX_CB = 2
MIX_VMEM = 60 * 1024 * 1024
COL_Z = 4 * RET_WIDTH
COL_XBC = COL_Z + SSM_WIDTH
PROJ_WIDTHS = (RET_WIDTH, RET_WIDTH, RET_WIDTH, RET_WIDTH, SSM_WIDTH, SSM_CONV_DIM, LANES)
PROJ_SLAB = 256


def _mixer_body(cd_ref, x_ref, n1w_ref, wt_hbm, wdt_ref, cos_ref, sin_ref, dec_ref, wq_ref, ws_ref, rnw_ref,
                cw_ref, cbias_ref, dtb_ref, alog_ref, dskip_ref, snw_ref, tri_ref,
                yret_ref, yssm_ref, w_ref, hn_ref, pq, pk, pv, pg, pz, pxbc, pdt,
                st_ret, xpad_ref, st_ssd, sem, *, cb):
    @pl.when(pl.program_id(0) == 0)
    def _():
        cp = pltpu.make_async_copy(wt_hbm, w_ref, sem)
        cp.start()
        st_ret[...] = jnp.zeros_like(st_ret)
        xpad_ref[0:CONV_PAD, :] = jnp.zeros((CONV_PAD, SSM_CONV_DIM), F32)
        st_ssd[...] = jnp.zeros_like(st_ssd)
        cp.wait()

    def slabs(ref, col0):
        def slab(lo):
            def go():
                ref[:, lo:lo + PROJ_SLAB] = _dot_nt(hn_ref[...], w_ref[col0 + lo:col0 + lo + PROJ_SLAB, :])
            return go
        return [slab(lo) for lo in range(0, ref.shape[1], PROJ_SLAB)]

    x = x_ref[...]
    ms = jnp.mean(x * x, axis=-1, keepdims=True)
    hn_ref[...] = (x * lax.rsqrt(ms + EPS) * n1w_ref[...]).astype(BF16)
    for piece in slabs(pxbc, COL_XBC) + slabs(pz, COL_Z):
        piece()
    pdt[...] = _dot_nt(hn_ref[...], wdt_ref[...])

    pieces = slabs(pq, 0) + slabs(pk, RET_WIDTH) + slabs(pv, 2 * RET_WIDTH) + slabs(pg, 3 * RET_WIDTH)
    n_pieces = len(pieces)
    n_ticks = cb * (SSM_CONV_DIM // XBC_BLK + 1 + SSM_GROUPS + SSM_WIDTH // LANES)
    calls = [0]

    def tick():
        calls[0] += 1
        while pieces and (n_pieces - len(pieces)) * n_ticks < calls[0] * n_pieces:
            pieces.pop(0)()

    _ssd_chunks(pxbc, pz, pdt, cw_ref, cbias_ref, dtb_ref, alog_ref, dskip_ref, snw_ref, tri_ref,
                yssm_ref, xpad_ref, st_ssd, cb, tick)
    while pieces:
        pieces.pop(0)()
    _retention_chunks(cd_ref, pq, pk, pv, pg, cos_ref, sin_ref, dec_ref, wq_ref, ws_ref, rnw_ref,
                      yret_ref, st_ret, cb, lambda: None)


def _mixer(x, n1w, w_main_t, w_dt_t, cos, sin, ret_nw, ret_consts,
           conv_w, conv_b, dt_bias, a_log, d_skip_e, ssm_nw, tri):
    T = x.shape[0]
    cb = min(MIX_CB, T // CHUNK)
    rows = cb * CHUNK
    chunk_decay, decay_intra, w_query, w_state = ret_consts
    hd = RET_HEAD_DIM
    half = hd // 2
    full = lambda shape: pl.BlockSpec(shape, lambda c, cd: (0,) * len(shape))
    rowblk = lambda width: pl.BlockSpec((rows, width), lambda c, cd: (c, 0))
    scratch = lambda width: pltpu.VMEM((rows, width), F32)
    grid_spec = pltpu.PrefetchScalarGridSpec(
        num_scalar_prefetch=1,
        grid=(T // rows,),
        in_specs=[
            rowblk(D_MODEL),
            full((1, D_MODEL)),
            pl.BlockSpec(memory_space=pl.ANY),
            full((LANES, D_MODEL)),
            rowblk(half), rowblk(half),
            full((RET_HEADS, CHUNK, CHUNK)),
            full((RET_HEADS, CHUNK, hd)),
            full((RET_HEADS, CHUNK, hd)),
            full((1, RET_WIDTH)),
            full((SSM_CONV, SSM_CONV_DIM)),
            full((1, SSM_CONV_DIM)),
            full((1, LANES)),
            full((1, LANES)),
            full((1, SSM_WIDTH)),
            full((1, SSM_WIDTH)),
            full((CHUNK, CHUNK)),
        ],
        out_specs=[rowblk(RET_WIDTH), rowblk(SSM_WIDTH)],
        scratch_shapes=[
            pltpu.VMEM((MAIN_PROJ, D_MODEL), BF16),
            pltpu.VMEM((rows, D_MODEL), BF16),
            *[scratch(w) for w in PROJ_WIDTHS],
            pltpu.VMEM((RET_HEADS, hd, hd), F32),
            pltpu.VMEM((CHUNK + CONV_PAD, SSM_CONV_DIM), F32),
            pltpu.VMEM((SSM_GROUPS, SSM_STATE, SSM_WIDTH // SSM_GROUPS), F32),
            pltpu.SemaphoreType.DMA(()),
        ],
    )
    return pl.pallas_call(
        functools.partial(_mixer_body, cb=cb),
        grid_spec=grid_spec,
        out_shape=[jax.ShapeDtypeStruct((T, RET_WIDTH), BF16),
                   jax.ShapeDtypeStruct((T, SSM_WIDTH), BF16)],
        compiler_params=_cparams(("arbitrary",), MIX_VMEM),
        name="mixer",
    )(chunk_decay, x, n1w, w_main_t, w_dt_t, cos, sin, decay_intra, w_query, w_state, ret_nw,
      conv_w, conv_b, dt_bias, a_log, d_skip_e, ssm_nw, tri)


OUT_TM = 512
ROUTE_ROWS = 8 + N_EXPERTS


def _out_router_body(x_ref, yr_ref, ys_ref, wo_ref, nw_ref, wr_ref, br_ref,
                     h1_ref, h2t_ref, cw_ref, ids_ref):
    tm = x_ref.shape[0]
    h1 = x_ref[...] + _dot(yr_ref[...], wo_ref[0:RET_WIDTH, :]) + _dot(ys_ref[...], wo_ref[RET_WIDTH:, :])
    h1_ref[...] = h1
    ms = jnp.mean(h1 * h1, axis=-1, keepdims=True)
    h2 = h1 * lax.rsqrt(ms + EPS) * nw_ref[...]
    for s in range(Y_ROWS):
        h2t_ref[:, s, :] = h2[:, s * LANES:(s + 1) * LANES]

    logits = _dot_nt(wr_ref[...], h2.astype(BF16)) + br_ref[...]
    row = lax.broadcasted_iota(I32, (8, tm), 0)
    lg = jnp.where(row < N_GROUPS, logits[0:8], -jnp.inf)
    m = jnp.max(lg, axis=0, keepdims=True)
    p_sel = 1.0 / jnp.sum(jnp.exp(lg - m), axis=0, keepdims=True)
    g_sel = jnp.min(jnp.where(lg == m, row, 8), axis=0, keepdims=True)
    le = jnp.zeros((GROUP_EXPERTS, tm), F32)
    for g in range(N_GROUPS):
        le = jnp.where(g_sel == g, logits[8 + g * GROUP_EXPERTS:8 + (g + 1) * GROUP_EXPERTS], le)
    m2 = jnp.max(le, axis=0, keepdims=True)
    ee = jnp.exp(le - m2)
    pe = ee / jnp.sum(ee, axis=0, keepdims=True)
    v1 = jnp.max(pe, axis=0, keepdims=True)
    i1 = jnp.min(jnp.where(pe == v1, row, 8), axis=0, keepdims=True)
    pe2 = jnp.where(row == i1, -1.0, pe)
    v2 = jnp.max(pe2, axis=0, keepdims=True)
    i2 = jnp.min(jnp.where(pe2 == v2, row, 8), axis=0, keepdims=True)
    tw = v1 + v2
    c1 = v1 / tw * p_sel
    c2 = v2 / tw * p_sel
    e1 = g_sel * GROUP_EXPERTS + i1
    e2 = g_sel * GROUP_EXPERTS + i2
    ids = jnp.where(row == 0, e1, jnp.where(row == 1, e2, 0))
    for b in range(tm // SORT_BLK):
        ids_ref[b] = ids[:, b * SORT_BLK:(b + 1) * SORT_BLK]
    cw8 = jnp.where(row == 0, c1, jnp.where(row == 1, c2, 0.0))
    cw = jnp.concatenate([cw8, jnp.zeros((LANES - 8, tm), F32)], axis=0)
    cw_ref[...] = cw.T


def _out_router(x, y_ret, y_ssm, w_out, nw, wr_t, br):
    T = x.shape[0]
    tm = min(OUT_TM, T)
    full = lambda shape: pl.BlockSpec(shape, lambda i: (0,) * len(shape))
    return pl.pallas_call(
        _out_router_body,
        grid=(T // tm,),
        in_specs=[
            pl.BlockSpec((tm, D_MODEL), lambda i: (i, 0)),
            pl.BlockSpec((tm, RET_WIDTH), lambda i: (i, 0)),
            pl.BlockSpec((tm, SSM_WIDTH), lambda i: (i, 0)),
            pl.BlockSpec((D_MODEL, D_MODEL), lambda i: (0, 0), pipeline_mode=pl.Buffered(1)),
            full((1, D_MODEL)),
            full((LANES, D_MODEL)),
            full((LANES, 1)),
        ],
        out_specs=[
            pl.BlockSpec((tm, D_MODEL), lambda i: (i, 0)),
            pl.BlockSpec((tm, Y_ROWS, LANES), lambda i: (i, 0, 0)),
            pl.BlockSpec((tm, LANES), lambda i: (i, 0)),
            pl.BlockSpec((tm // SORT_BLK, 8, SORT_BLK), lambda i: (i, 0, 0)),
        ],
        out_shape=[
            jax.ShapeDtypeStruct((T, D_MODEL), F32),
            jax.ShapeDtypeStruct((T, Y_ROWS, LANES), F32),
            jax.ShapeDtypeStruct((T, LANES), F32),
            jax.ShapeDtypeStruct((T // SORT_BLK, 8, SORT_BLK), I32),
        ],
        compiler_params=_cparams(("parallel",)),
        name="out_router",
    )(x, y_ret, y_ssm, w_out, nw, wr_t, br)


SORT_BLK = 256
MOE_TM = 256


def _sort_index_body(ids_ref, tri_ref, ltri_ref, dest_ref, cnt_ref, rank_ref):
    nblk = ids_ref.shape[0]
    row_e = lax.broadcasted_iota(I32, (N_EXPERTS, SORT_BLK), 0)
    row8 = lax.broadcasted_iota(I32, (8, SORT_BLK), 0)

    def onehots(b):
        ids = ids_ref[b]
        return row_e == ids[0:1], row_e == ids[1:2]

    def rank_blk(b, carry):
        oh1, oh2 = onehots(b)
        ohf = jnp.where(oh1 | oh2, 1.0, 0.0)
        incl = _dot(ohf.astype(BF16), tri_ref[...])
        base = carry + incl - 1.0
        r1 = jnp.sum(jnp.where(oh1, base, 0.0), axis=0, keepdims=True)
        r2 = jnp.sum(jnp.where(oh2, base, 0.0), axis=0, keepdims=True)
        rank_ref[b] = jnp.where(row8 == 0, r1, jnp.where(row8 == 1, r2, 0.0))
        return carry + jnp.sum(ohf, axis=1, keepdims=True)

    cnt = lax.fori_loop(0, nblk, rank_blk, jnp.zeros((N_EXPERTS, 1), F32))
    cnt_ref[...] = jnp.broadcast_to(cnt, cnt_ref.shape)
    tiles = jnp.floor((cnt + (MOE_TM - 1.0)) / MOE_TM)
    tiles_b = jnp.broadcast_to(tiles, (N_EXPERTS, LANES)).astype(BF16)
    off = _dot(ltri_ref[...], tiles_b)[:, 0:1] * MOE_TM

    def dest_blk(b, carry):
        oh1, oh2 = onehots(b)
        o1 = jnp.sum(jnp.where(oh1, off, 0.0), axis=0, keepdims=True)
        o2 = jnp.sum(jnp.where(oh2, off, 0.0), axis=0, keepdims=True)
        d = rank_ref[b] + jnp.where(row8 == 0, o1, jnp.where(row8 == 1, o2, 0.0))
        dest_ref[b] = d.astype(I32)
        return carry

    lax.fori_loop(0, nblk, dest_blk, 0)


def _sort_index(ids, tri_u, ltri):
    nblk = ids.shape[0]
    return pl.pallas_call(
        _sort_index_body,
        out_shape=[jax.ShapeDtypeStruct((nblk, 8, SORT_BLK), I32),
                   jax.ShapeDtypeStruct((N_EXPERTS, LANES), F32)],
        scratch_shapes=[pltpu.VMEM((nblk, 8, SORT_BLK), F32)],
        compiler_params=_cparams(None),
        name="sort_index",
    )(ids, tri_u, ltri)


ROW_DMA_PRIORITY = 1


def _moe_body(te_ref, tf_ref, nx_ref, nu_ref, dest_ref, lo_ref, hi_ref,
              h2t_ref, wg_ref, wu_ref, wd_ref, yt_ref,
              srow_ref, wcount, xbuf, ystage, wsg, wsu, wsd, wgb_ref, wub_ref, wdb_ref, gsem, ssem, wsem):
    i = pl.program_id(0)
    nu = nu_ref[0]
    T = h2t_ref.shape[0]

    def gather_row(tile, slot, r):
        tok = srow_ref[tile * MOE_TM + r] & (T - 1)
        return pltpu.make_async_copy(h2t_ref.at[tok], xbuf.at[slot, :, r, :], gsem.at[slot])

    def scatter_row(tile, slot, r):
        return pltpu.make_async_copy(ystage.at[slot, :, r, :], yt_ref.at[srow_ref[tile * MOE_TM + r]],
                                     ssem.at[slot])

    def gather_tile(slot):
        return pltpu.make_async_copy(h2t_ref.at[pl.ds(0, MOE_TM)], h2t_ref.at[pl.ds(0, MOE_TM)], gsem.at[slot])

    def scatter_tile(slot):
        return pltpu.make_async_copy(yt_ref.at[pl.ds(0, MOE_TM)], yt_ref.at[pl.ds(0, MOE_TM)], ssem.at[slot])

    def weight_copies(e, ws):
        return (pltpu.make_async_copy(wg_ref.at[e], wsg.at[ws], wsem.at[ws, 0]),
                pltpu.make_async_copy(wu_ref.at[e], wsu.at[ws], wsem.at[ws, 1]),
                pltpu.make_async_copy(wd_ref.at[e], wsd.at[ws], wsem.at[ws, 2]))

    @pl.when(i == 0)
    def _():
        wcount[0] = 0
        for c in weight_copies(te_ref[0], 0):
            c.start()

        def tok(t, c):
            srow_ref[dest_ref[t]] = t
            srow_ref[dest_ref[T + t]] = T + t
            return c

        lax.fori_loop(0, T, tok, 0, unroll=8)

        def seg(e, c):
            def pad(r, c2):
                srow_ref[r] = 2 * T + (r & (MOE_TM - 1))
                return c2

            lax.fori_loop(lo_ref[e], hi_ref[e], pad, 0)
            return c

        lax.fori_loop(0, lo_ref.shape[0], seg, 0)

        ystage[...] = jnp.zeros_like(ystage)

        def spare(r, c):
            pltpu.make_async_copy(ystage.at[0, :, r, :], yt_ref.at[2 * T + r], ssem.at[0]).start()
            return c

        lax.fori_loop(0, MOE_TM, spare, 0)
        scatter_tile(0).wait()
        second = jnp.minimum(1, nu - 1)
        for r in range(MOE_TM):
            gather_row(0, 0, r).start(priority=ROW_DMA_PRIORITY)
            gather_row(second, 1, r).start(priority=ROW_DMA_PRIORITY)

    @pl.when(tf_ref[i] == 1)
    def _():
        ws = lax.rem(wcount[0], 2)
        wcount[0] = wcount[0] + 1
        for c in weight_copies(te_ref[i], ws):
            c.wait()

        @pl.when(nx_ref[i] >= 0)
        def _():
            for c in weight_copies(nx_ref[i], 1 - ws):
                c.start()

        wgb_ref[...] = wsg[ws].astype(BF16)
        wub_ref[...] = wsu[ws].astype(BF16)
        wdb_ref[...] = wsd[ws].astype(BF16)

    slot = lax.rem(i, 2)
    other = 1 - slot
    xs = lax.rem(i, 3)

    @pl.when(i < nu)
    def _():
        gather_tile(xs).wait()

        @pl.when(i >= 1)
        def _():
            scatter_tile(slot).wait()

        nxt = jnp.minimum(i + 2, nu - 1)
        nxs = lax.rem(i + 2, 3)
        prv = jnp.maximum(i - 1, 0)
        for r in range(MOE_TM):
            gather_row(nxt, nxs, r).start(priority=r % 2)
            scatter_row(prv, other, r).start(priority=(r + 1) % 2)

        x = jnp.concatenate([xbuf[xs, s].astype(BF16) for s in range(Y_ROWS)], axis=1)
        a = _dot(x, wgb_ref[...])
        u = _dot(x, wub_ref[...])
        act = (_silu(a) * u).astype(BF16)
        y = _dot(act, wdb_ref[...])
        for s in range(Y_ROWS):
            ystage[slot, s] = y[:, s * LANES:(s + 1) * LANES]

    @pl.when(i == nu)
    def _():
        gather_tile(xs).wait()
        gather_tile(lax.rem(i + 1, 3)).wait()
        scatter_tile(slot).wait()

        def last(r, c):
            scatter_row(nu - 1, other, r).start()
            return c

        lax.fori_loop(0, MOE_TM, last, 0)
        scatter_tile(other).wait()


def _moe(tile_expert, tile_first, next_expert, n_used, dest, pad_lo, pad_hi, h2t, w_gate, w_up, w_down):
    T = h2t.shape[0]
    assert T & (T - 1) == 0 and T >= MOE_TM, "token index is recovered from the row table by masking"
    n_tiles = tile_expert.shape[0] - 1
    hbm = pl.BlockSpec(memory_space=pl.ANY)
    grid_spec = pltpu.PrefetchScalarGridSpec(
        num_scalar_prefetch=7,
        grid=(n_tiles + 1,),
        in_specs=[hbm, hbm, hbm, hbm],
        out_specs=hbm,
        scratch_shapes=[pltpu.SMEM((n_tiles * MOE_TM,), I32),
                        pltpu.SMEM((1,), I32),
                        pltpu.VMEM((3, Y_ROWS, MOE_TM, LANES), F32),
                        pltpu.VMEM((2, Y_ROWS, MOE_TM, LANES), F32),
                        pltpu.VMEM((2, D_MODEL, D_EXPERT), F32),
                        pltpu.VMEM((2, D_MODEL, D_EXPERT), F32),
                        pltpu.VMEM((2, D_EXPERT, D_MODEL), F32),
                        pltpu.VMEM((D_MODEL, D_EXPERT), BF16),
                        pltpu.VMEM((D_MODEL, D_EXPERT), BF16),
                        pltpu.VMEM((D_EXPERT, D_MODEL), BF16),
                        pltpu.SemaphoreType.DMA((3,)),
                        pltpu.SemaphoreType.DMA((2,)),
                        pltpu.SemaphoreType.DMA((2, 3))],
    )
    return pl.pallas_call(
        _moe_body,
        grid_spec=grid_spec,
        out_shape=jax.ShapeDtypeStruct((2 * T + MOE_TM, Y_ROWS, LANES), F32),
        compiler_params=_cparams(("arbitrary",)),
        name="moe",
    )(tile_expert, tile_first, next_expert, n_used, dest, pad_lo, pad_hi, h2t, w_gate, w_up, w_down)


COMB_TM = 512


def _combine_body(h1_ref, y0_ref, y1_ref, cw_ref, nw_ref, o_ref):
    tm = h1_ref.shape[0]
    c0 = cw_ref[:, 0:1]
    c1 = cw_ref[:, 1:2]
    cols = []
    for s in range(Y_ROWS):
        moe = c0 * y0_ref[:, s, :] + c1 * y1_ref[:, s, :]
        cols.append(h1_ref[:, s * LANES:(s + 1) * LANES] + moe)
    h = jnp.concatenate(cols, axis=1)
    ms = jnp.mean(h * h, axis=-1, keepdims=True)
    o_ref[...] = h * lax.rsqrt(ms + EPS) * nw_ref[...]


def _combine(h1, y_tok, cw, nw):
    T = h1.shape[0]
    tm = min(COMB_TM, T)
    nt = T // tm
    return pl.pallas_call(
        _combine_body,
        grid=(nt,),
        in_specs=[
            pl.BlockSpec((tm, D_MODEL), lambda i: (i, 0)),
            pl.BlockSpec((tm, Y_ROWS, LANES), lambda i: (i, 0, 0)),
            pl.BlockSpec((tm, Y_ROWS, LANES), lambda i: (i + nt, 0, 0)),
            pl.BlockSpec((tm, LANES), lambda i: (i, 0)),
            pl.BlockSpec((1, D_MODEL), lambda i: (0, 0)),
        ],
        out_specs=pl.BlockSpec((tm, D_MODEL), lambda i: (i, 0)),
        out_shape=jax.ShapeDtypeStruct((T, D_MODEL), F32),
        compiler_params=_cparams(("parallel",)),
        name="combine",
    )(h1, y_tok, y_tok, cw, nw)


def _tile_plan(cnt, n_tiles):
    tiles = (cnt + (MOE_TM - 1)) // MOE_TM
    ends = jnp.cumsum(tiles)
    starts = ends - tiles
    n_used = ends[-1]
    step = jnp.arange(n_tiles + 1, dtype=I32)
    tile = jnp.minimum(step, jnp.maximum(n_used - 1, 0))
    tile_expert = jnp.sum((ends[None, :] <= tile[:, None]).astype(I32), axis=1)
    tile_first = ((step == starts[tile_expert]) & (step < n_used)).astype(I32)
    nxt_tile = ends[tile_expert]
    nxt_expert = jnp.sum((ends[None, :] <= nxt_tile[:, None]).astype(I32), axis=1)
    next_expert = jnp.where(nxt_tile < n_used, nxt_expert, -1).astype(I32)
    pad_lo = starts * MOE_TM + cnt
    pad_hi = ends * MOE_TM
    return (tile_expert, tile_first, next_expert, n_used.reshape(1).astype(I32),
            pad_lo.astype(I32), pad_hi.astype(I32))


def kernel(x, positions, norm1_w, w_in, conv_w, conv_b, dt_bias, a_log, d_skip, ret_norm_w,
           ssm_norm_w, w_out, norm2_w, w_router_group, b_router_group, w_router_expert,
           b_router_expert, w_expert_gate, w_expert_up, w_expert_down, final_norm_w):
    B, T, D = x.shape
    assert B == 1 and D == D_MODEL and T % CHUNK == 0
    xf = x.reshape(T, D)
    pad_l = lambda v: jnp.pad(v, ((0, 0), (0, LANES - v.shape[-1])))

    w_in_t = jnp.swapaxes(w_in[0], 0, 1)
    w_dt = jnp.pad(w_in_t[MAIN_PROJ:], ((0, LANES - (w_in_t.shape[0] - MAIN_PROJ)), (0, 0))).astype(BF16)
    half = RET_HEAD_DIM // 2
    inv = (ROPE_THETA ** (-jnp.arange(half, dtype=F32) / half)).reshape(1, half)
    tri = (jnp.arange(CHUNK)[:, None] >= jnp.arange(CHUNK)[None, :]).astype(BF16)
    d_skip_e = jnp.repeat(d_skip[0], SSM_HEAD_DIM).reshape(1, SSM_WIDTH)
    wr_t = jnp.zeros((LANES, D), F32)
    wr_t = wr_t.at[0:N_GROUPS].set(w_router_group[0].T).at[8:8 + N_EXPERTS].set(w_router_expert[0].T)
    br = jnp.zeros((LANES,), F32)
    br = br.at[0:N_GROUPS].set(b_router_group[0]).at[8:8 + N_EXPERTS].set(b_router_expert[0])
    tri_u = (jnp.arange(SORT_BLK)[:, None] <= jnp.arange(SORT_BLK)[None, :]).astype(BF16)
    ltri = (jnp.arange(N_EXPERTS)[:, None] > jnp.arange(N_EXPERTS)[None, :]).astype(BF16)

    w_main, w_out_b, cos, sin = _prep(w_in_t, w_out[0], positions.reshape(T, 1).astype(F32), inv)
    y_ret, y_ssm = _mixer(xf, norm1_w[0].reshape(1, D), w_main, w_dt, cos, sin,
                          ret_norm_w[0].reshape(1, RET_WIDTH), _retention_consts(),
                          conv_w[0], conv_b[0].reshape(1, -1), pad_l(dt_bias[0].reshape(1, -1)),
                          pad_l(a_log[0].reshape(1, -1)), d_skip_e, ssm_norm_w[0].reshape(1, -1), tri)
    h1, h2t, cw, ids = _out_router(xf, y_ret, y_ssm, w_out_b, norm2_w[0].reshape(1, D),
                                   wr_t.astype(BF16), br.reshape(LANES, 1))

    dest_blk, cnt = _sort_index(ids, tri_u, ltri)
    dest = dest_blk[:, 0:2, :].transpose(1, 0, 2).reshape(2 * T)
    n_tiles = (2 * T) // MOE_TM + N_EXPERTS
    tile_expert, tile_first, next_expert, n_used, pad_lo, pad_hi = _tile_plan(cnt[:, 0].astype(I32), n_tiles)
    y_tok = _moe(tile_expert, tile_first, next_expert, n_used, dest, pad_lo, pad_hi, h2t,
                 w_expert_gate[0], w_expert_up[0], w_expert_down[0])
    out = _combine(h1, y_tok, cw, final_norm_w.reshape(1, D))
    return out.reshape(B, T, D)
```

```python
import functools

import jax
import jax.numpy as jnp
from jax import lax
from jax.experimental import pallas as pl
from jax.experimental.pallas import tpu as pltpu

F32 = jnp.float32
BF16 = jnp.bfloat16
I32 = jnp.int32

D_MODEL = 2048
EPS = 1e-6
CHUNK = 128
RET_HEADS = 4
RET_HEAD_DIM = 256
RET_WIDTH = RET_HEADS * RET_HEAD_DIM
ROPE_THETA = 10000.0
SSM_WIDTH = 1024
SSM_HEAD_DIM = 64
SSM_HEADS = SSM_WIDTH // SSM_HEAD_DIM
SSM_GROUPS = 2
SSM_STATE = 128
SSM_CONV = 4
SSM_CONV_DIM = SSM_WIDTH + 2 * SSM_GROUPS * SSM_STATE
MAIN_PROJ = 4 * RET_WIDTH + SSM_WIDTH + SSM_CONV_DIM
N_GROUPS = 4
GROUP_EXPERTS = 8
N_EXPERTS = N_GROUPS * GROUP_EXPERTS
D_EXPERT = 512
LANES = 128
Y_ROWS = D_MODEL // LANES

VMEM_LIMIT = 56 * 1024 * 1024


def _cparams(sem, vmem=VMEM_LIMIT):
    return pltpu.CompilerParams(dimension_semantics=sem, vmem_limit_bytes=vmem)


def _silu(x):
    return x * (1.0 / (1.0 + jnp.exp(-x)))


def _dot(a, b):
    return jnp.dot(a, b, preferred_element_type=F32)


def _dot_nt(a, b):
    return lax.dot_general(a, b, (((1,), (1,)), ((), ())), preferred_element_type=F32)


def _split3(a):
    a1 = a.astype(BF16)
    r1 = a - a1.astype(F32)
    a2 = r1.astype(BF16)
    a3 = (r1 - a2.astype(F32)).astype(BF16)
    return a1, a2, a3


CAST_ROWS = 512
ROPE_ROWS = 1024


def _prep_body(win_ref, wout_ref, pos_ref, inv_ref, winb_ref, woutb_ref, cos_ref, sin_ref, *, n_in, n_rope):
    j = pl.program_id(0)

    @pl.when(j < n_in)
    def _():
        winb_ref[...] = win_ref[...].astype(BF16)

    @pl.when(j >= n_in)
    def _():
        woutb_ref[...] = wout_ref[...].astype(BF16)

    @pl.when(j < n_rope)
    def _():
        ang = pos_ref[...] * inv_ref[...]
        cos_ref[...] = jnp.cos(ang)
        sin_ref[...] = jnp.sin(ang)


def _prep(w_in_t, w_out, pos, inv):
    T = pos.shape[0]
    cols = w_in_t.shape[1]
    half = inv.shape[1]
    n_in = MAIN_PROJ // CAST_ROWS
    n_out = w_out.shape[0] // CAST_ROWS
    rr = min(ROPE_ROWS, T)
    n_rope = T // rr
    assert n_rope <= n_in + n_out
    in_blk = lambda j: (jnp.minimum(j, n_in - 1), 0)
    out_blk = lambda j: (jnp.clip(j - n_in, 0, n_out - 1), 0)
    rope_blk = lambda j: (jnp.minimum(j, n_rope - 1), 0)
    return pl.pallas_call(
        functools.partial(_prep_body, n_in=n_in, n_rope=n_rope),
        grid=(n_in + n_out,),
        in_specs=[pl.BlockSpec((CAST_ROWS, cols), in_blk),
                  pl.BlockSpec((CAST_ROWS, w_out.shape[1]), out_blk),
                  pl.BlockSpec((rr, 1), rope_blk),
                  pl.BlockSpec((1, half), lambda j: (0, 0))],
        out_specs=[pl.BlockSpec((CAST_ROWS, cols), in_blk),
                   pl.BlockSpec((CAST_ROWS, w_out.shape[1]), out_blk),
                   pl.BlockSpec((rr, half), rope_blk),
                   pl.BlockSpec((rr, half), rope_blk)],
        out_shape=[jax.ShapeDtypeStruct((MAIN_PROJ, cols), BF16),
                   jax.ShapeDtypeStruct(w_out.shape, BF16),
                   jax.ShapeDtypeStruct((T, half), F32),
                   jax.ShapeDtypeStruct((T, half), F32)],
        compiler_params=_cparams(("arbitrary",)),
        name="prep",
    )(w_in_t, w_out, pos, inv)


def _retention_chunks(cd_ref, q_ref, k_ref, v_ref, g_ref, cos_ref, sin_ref,
                      dec_ref, wq_ref, ws_ref, nw_ref, o_ref, st_ref, cb, tick):
    hd = RET_HEAD_DIM
    half = hd // 2
    for i in range(cb):
        sl = slice(i * CHUNK, (i + 1) * CHUNK)
        cos = cos_ref[sl, :]
        sin = sin_ref[sl, :]

        def rope(x):
            x1, x2 = x[:, :half], x[:, half:]
            return jnp.concatenate([x1 * cos - x2 * sin, x2 * cos + x1 * sin], axis=-1)

        for h in range(RET_HEADS):
            hs = slice(h * hd, (h + 1) * hd)
            q = rope(q_ref[sl, hs])
            k = rope(k_ref[sl, hs]) * (hd ** -0.5)
            v = v_ref[sl, hs]
            qb = q.astype(BF16)
            kb = k.astype(BF16)
            scores = _dot_nt(qb, kb) * dec_ref[h]
            y = _dot(scores.astype(BF16), v.astype(BF16))
            st = st_ref[h]
            y = y + _dot(qb, st.astype(BF16)) * wq_ref[h]
            vw = (v * ws_ref[h]).astype(BF16)
            new = _dot(k.T.astype(BF16), vw)
            st_ref[h] = st * cd_ref[h] + new
            mu = jnp.mean(y, axis=-1, keepdims=True)
            d = y - mu
            var = jnp.mean(d * d, axis=-1, keepdims=True)
            yn = d * lax.rsqrt(var + EPS)
            o_ref[sl, hs] = (yn * nw_ref[:, hs] * _silu(g_ref[sl, hs])).astype(BF16)
            tick()


def _retention_consts():
    H, L = RET_HEADS, CHUNK
    log_gamma = jnp.log1p(-(2.0 ** (-5.0 - jnp.arange(H, dtype=F32))))
    idx = jnp.arange(L, dtype=F32)
    diff = idx[:, None] - idx[None, :]
    causal = diff >= 0
    decay_intra = jnp.where(causal[None], jnp.exp(jnp.where(causal, diff, 0.0)[None] * log_gamma[:, None, None]), 0.0)
    w_state = jnp.exp((L - 1.0 - idx)[None, :] * log_gamma[:, None])
    w_query = jnp.exp((idx + 1.0)[None, :] * log_gamma[:, None])
    chunk_decay = jnp.exp(L * log_gamma)
    bc = lambda w: jnp.broadcast_to(w[:, :, None], (H, L, RET_HEAD_DIM))
    return chunk_decay, decay_intra, bc(w_query), bc(w_state)


CONV_PAD = 8
XBC_BLK = 512


def _ssd_chunks(xbc_ref, z_ref, dt_ref, cw_ref, cbias_ref, dtb_ref,
                alog_ref, dskip_ref, nw_ref, tri_ref, o_ref, xpad_ref, st_ref, cb, tick):
    L = CHUNK
    gw = SSM_WIDTH // SSM_GROUPS
    rows_i = lax.broadcasted_iota(I32, (L, L), 0)
    cols_i = lax.broadcasted_iota(I32, (L, L), 1)
    causal = rows_i >= cols_i
    lo_lane = lax.broadcasted_iota(I32, (L, LANES), 1) < SSM_HEAD_DIM
    tri = tri_ref[...]
    a_neg = -jnp.exp(alog_ref[...])

    for i in range(cb):
        sl = slice(i * L, (i + 1) * L)
        xpad_ref[CONV_PAD:CONV_PAD + L, :] = xbc_ref[sl, :]
        u_parts = []
        for b in range(SSM_CONV_DIM // XBC_BLK):
            cs = slice(b * XBC_BLK, (b + 1) * XBC_BLK)
            acc = cbias_ref[:, cs]
            for t in range(SSM_CONV):
                r0 = CONV_PAD - (SSM_CONV - 1) + t
                acc = acc + xpad_ref[r0:r0 + L, cs] * cw_ref[t:t + 1, cs]
            u_parts.append(_silu(acc))
            tick()
        xpad_ref[0:CONV_PAD, :] = xpad_ref[L:L + CONV_PAD, :]
        xs = jnp.concatenate(u_parts[:2], axis=-1)
        bm = u_parts[2][:, :SSM_GROUPS * SSM_STATE]
        cm = u_parts[2][:, SSM_GROUPS * SSM_STATE:]

        dt_in = dt_ref[sl, :] + dtb_ref[...]
        dt = jnp.maximum(dt_in, 0.0) + jnp.log1p(jnp.exp(-jnp.abs(dt_in)))
        a = dt * a_neg
        a1, a2, a3 = _split3(a)
        a_cs = _dot(tri, a1) + _dot(tri, a2) + _dot(tri, a3)
        last = a_cs[L - 1:L, :]

        def per_head_lanes(v):
            lo = lo_lane[0:v.shape[0], :]
            return jnp.concatenate([jnp.where(lo, v[:, 2 * p:2 * p + 1], v[:, 2 * p + 1:2 * p + 2])
                                    for p in range(SSM_HEADS // 2)], axis=1)

        dt_e = per_head_lanes(dt)
        to_end = per_head_lanes(jnp.exp(last - a_cs))
        from_start = per_head_lanes(jnp.exp(a_cs))
        chunk_decay = per_head_lanes(jnp.exp(last))
        tick()
        x_dt = xs * dt_e
        xw = (x_dt * to_end).astype(BF16)
        acs_t = a_cs.T
        cmb = cm.astype(BF16)
        bmb = bm.astype(BF16)

        ys = []
        for g in range(SSM_GROUPS):
            ns = slice(g * SSM_STATE, (g + 1) * SSM_STATE)
            gs = slice(g * gw, (g + 1) * gw)
            cg = cmb[:, ns]
            cbg = _dot_nt(cg, bmb[:, ns])
            st = st_ref[g]
            y_off = _dot(cg, st.astype(BF16))
            new = _dot(bm[:, ns].T.astype(BF16), xw[:, gs])
            st_ref[g] = st * chunk_decay[:, gs] + new
            tick()
            for jp in range(gw // LANES):
                h0 = (g * gw + jp * LANES) // SSM_HEAD_DIM
                ms = []
                for hh in (h0, h0 + 1):
                    seg = a_cs[:, hh:hh + 1] - acs_t[hh:hh + 1, :]
                    dec = jnp.exp(jnp.where(causal, seg, -jnp.inf))
                    ms.append((cbg * dec).astype(BF16))
                lhs = jnp.concatenate(ms, axis=1)
                ls = slice(g * gw + jp * LANES, g * gw + (jp + 1) * LANES)
                xp = x_dt[:, ls]
                rhs = jnp.concatenate([jnp.where(lo_lane, xp, 0.0),
                                       jnp.where(lo_lane, 0.0, xp)], axis=0).astype(BF16)
                y_diag = _dot(lhs, rhs)
                ys.append(y_diag + y_off[:, jp * LANES:(jp + 1) * LANES] * from_start[:, ls])
                tick()
        y = jnp.concatenate(ys, axis=1) + xs * dskip_ref[...]
        y = y * _silu(z_ref[sl, :])
        outs = []
        for g in range(SSM_GROUPS):
            yg = y[:, g * gw:(g + 1) * gw]
            ms_ = jnp.mean(yg * yg, axis=-1, keepdims=True)
            outs.append(yg * lax.rsqrt(ms_ + EPS))
        o_ref[sl, :] = (jnp.concatenate(outs, axis=1) * nw_ref[...]).astype(BF16)


MIX_CB = 2
MIX_VMEM = 60 * 1024 * 1024
COL_Z = 4 * RET_WIDTH
COL_XBC = COL_Z + SSM_WIDTH
PROJ_WIDTHS = (RET_WIDTH, RET_WIDTH, RET_WIDTH, RET_WIDTH, SSM_WIDTH, SSM_CONV_DIM, LANES)
PROJ_SLAB = 256


def _mixer_body(cd_ref, x_ref, n1w_ref, wt_hbm, wdt_ref, cos_ref, sin_ref, dec_ref, wq_ref, ws_ref, rnw_ref,
                cw_ref, cbias_ref, dtb_ref, alog_ref, dskip_ref, snw_ref, tri_ref,
                yret_ref, yssm_ref, w_ref, hn_ref, pq, pk, pv, pg, pz, pxbc, pdt,
                st_ret, xpad_ref, st_ssd, sem, *, cb):
    @pl.when(pl.program_id(0) == 0)
    def _():
        cp = pltpu.make_async_copy(wt_hbm, w_ref, sem)
        cp.start()
        st_ret[...] = jnp.zeros_like(st_ret)
        xpad_ref[0:CONV_PAD, :] = jnp.zeros((CONV_PAD, SSM_CONV_DIM), F32)
        st_ssd[...] = jnp.zeros_like(st_ssd)
        cp.wait()

    def slabs(ref, col0):
        def slab(lo):
            def go():
                ref[:, lo:lo + PROJ_SLAB] = _dot_nt(hn_ref[...], w_ref[col0 + lo:col0 + lo + PROJ_SLAB, :])
            return go
        return [slab(lo) for lo in range(0, ref.shape[1], PROJ_SLAB)]

    x = x_ref[...]
    ms = jnp.mean(x * x, axis=-1, keepdims=True)
    hn_ref[...] = (x * lax.rsqrt(ms + EPS) * n1w_ref[...]).astype(BF16)
    for piece in slabs(pxbc, COL_XBC) + slabs(pz, COL_Z):
        piece()
    pdt[...] = _dot_nt(hn_ref[...], wdt_ref[...])

    pieces = slabs(pq, 0) + slabs(pk, RET_WIDTH) + slabs(pv, 2 * RET_WIDTH) + slabs(pg, 3 * RET_WIDTH)
    n_pieces = len(pieces)
    n_ticks = cb * (SSM_CONV_DIM // XBC_BLK + 1 + SSM_GROUPS + SSM_WIDTH // LANES)
    calls = [0]

    def tick():
        calls[0] += 1
        while pieces and (n_pieces - len(pieces)) * n_ticks < calls[0] * n_pieces:
            pieces.pop(0)()

    _ssd_chunks(pxbc, pz, pdt, cw_ref, cbias_ref, dtb_ref, alog_ref, dskip_ref, snw_ref, tri_ref,
                yssm_ref, xpad_ref, st_ssd, cb, tick)
    while pieces:
        pieces.pop(0)()
    _retention_chunks(cd_ref, pq, pk, pv, pg, cos_ref, sin_ref, dec_ref, wq_ref, ws_ref, rnw_ref,
                      yret_ref, st_ret, cb, lambda: None)


def _mixer(x, n1w, w_main_t, w_dt_t, cos, sin, ret_nw, ret_consts,
           conv_w, conv_b, dt_bias, a_log, d_skip_e, ssm_nw, tri):
    T = x.shape[0]
    cb = min(MIX_CB, T // CHUNK)
    rows = cb * CHUNK
    chunk_decay, decay_intra, w_query, w_state = ret_consts
    hd = RET_HEAD_DIM
    half = hd // 2
    full = lambda shape: pl.BlockSpec(shape, lambda c, cd: (0,) * len(shape))
    rowblk = lambda width: pl.BlockSpec((rows, width), lambda c, cd: (c, 0))
    scratch = lambda width: pltpu.VMEM((rows, width), F32)
    grid_spec = pltpu.PrefetchScalarGridSpec(
        num_scalar_prefetch=1,
        grid=(T // rows,),
        in_specs=[
            rowblk(D_MODEL),
            full((1, D_MODEL)),
            pl.BlockSpec(memory_space=pl.ANY),
            full((LANES, D_MODEL)),
            rowblk(half), rowblk(half),
            full((RET_HEADS, CHUNK, CHUNK)),
            full((RET_HEADS, CHUNK, hd)),
            full((RET_HEADS, CHUNK, hd)),
            full((1, RET_WIDTH)),
            full((SSM_CONV, SSM_CONV_DIM)),
            full((1, SSM_CONV_DIM)),
            full((1, LANES)),
            full((1, LANES)),
            full((1, SSM_WIDTH)),
            full((1, SSM_WIDTH)),
            full((CHUNK, CHUNK)),
        ],
        out_specs=[rowblk(RET_WIDTH), rowblk(SSM_WIDTH)],
        scratch_shapes=[
            pltpu.VMEM((MAIN_PROJ, D_MODEL), BF16),
            pltpu.VMEM((rows, D_MODEL), BF16),
            *[scratch(w) for w in PROJ_WIDTHS],
            pltpu.VMEM((RET_HEADS, hd, hd), F32),
            pltpu.VMEM((CHUNK + CONV_PAD, SSM_CONV_DIM), F32),
            pltpu.VMEM((SSM_GROUPS, SSM_STATE, SSM_WIDTH // SSM_GROUPS), F32),
            pltpu.SemaphoreType.DMA(()),
        ],
    )
    return pl.pallas_call(
        functools.partial(_mixer_body, cb=cb),
        grid_spec=grid_spec,
        out_shape=[jax.ShapeDtypeStruct((T, RET_WIDTH), BF16),
                   jax.ShapeDtypeStruct((T, SSM_WIDTH), BF16)],
        compiler_params=_cparams(("arbitrary",), MIX_VMEM),
        name="mixer",
    )(chunk_decay, x, n1w, w_main_t, w_dt_t, cos, sin, decay_intra, w_query, w_state, ret_nw,
      conv_w, conv_b, dt_bias, a_log, d_skip_e, ssm_nw, tri)


OUT_TM = 512
ROUTE_ROWS = 8 + N_EXPERTS


def _out_router_body(x_ref, yr_ref, ys_ref, wo_ref, nw_ref, wr_ref, br_ref,
                     h1_ref, h2t_ref, cw_ref, ids_ref):
    tm = x_ref.shape[0]
    h1 = x_ref[...] + _dot(yr_ref[...], wo_ref[0:RET_WIDTH, :]) + _dot(ys_ref[...], wo_ref[RET_WIDTH:, :])
    h1_ref[...] = h1
    ms = jnp.mean(h1 * h1, axis=-1, keepdims=True)
    h2 = h1 * lax.rsqrt(ms + EPS) * nw_ref[...]
    for s in range(Y_ROWS):
        h2t_ref[:, s, :] = h2[:, s * LANES:(s + 1) * LANES]

    logits = _dot_nt(wr_ref[...], h2.astype(BF16)) + br_ref[...]
    row = lax.broadcasted_iota(I32, (8, tm), 0)
    lg = jnp.where(row < N_GROUPS, logits[0:8], -jnp.inf)
    m = jnp.max(lg, axis=0, keepdims=True)
    p_sel = 1.0 / jnp.sum(jnp.exp(lg - m), axis=0, keepdims=True)
    g_sel = jnp.min(jnp.where(lg == m, row, 8), axis=0, keepdims=True)
    le = jnp.zeros((GROUP_EXPERTS, tm), F32)
    for g in range(N_GROUPS):
        le = jnp.where(g_sel == g, logits[8 + g * GROUP_EXPERTS:8 + (g + 1) * GROUP_EXPERTS], le)
    m2 = jnp.max(le, axis=0, keepdims=True)
    ee = jnp.exp(le - m2)
    pe = ee / jnp.sum(ee, axis=0, keepdims=True)
    v1 = jnp.max(pe, axis=0, keepdims=True)
    i1 = jnp.min(jnp.where(pe == v1, row, 8), axis=0, keepdims=True)
    pe2 = jnp.where(row == i1, -1.0, pe)
    v2 = jnp.max(pe2, axis=0, keepdims=True)
    i2 = jnp.min(jnp.where(pe2 == v2, row, 8), axis=0, keepdims=True)
    tw = v1 + v2
    c1 = v1 / tw * p_sel
    c2 = v2 / tw * p_sel
    e1 = g_sel * GROUP_EXPERTS + i1
    e2 = g_sel * GROUP_EXPERTS + i2
    ids = jnp.where(row == 0, e1, jnp.where(row == 1, e2, 0))
    for b in range(tm // SORT_BLK):
        ids_ref[b] = ids[:, b * SORT_BLK:(b + 1) * SORT_BLK]
    cw8 = jnp.where(row == 0, c1, jnp.where(row == 1, c2, 0.0))
    cw = jnp.concatenate([cw8, jnp.zeros((LANES - 8, tm), F32)], axis=0)
    cw_ref[...] = cw.T


def _out_router(x, y_ret, y_ssm, w_out, nw, wr_t, br):
    T = x.shape[0]
    tm = min(OUT_TM, T)
    full = lambda shape: pl.BlockSpec(shape, lambda i: (0,) * len(shape))
    return pl.pallas_call(
        _out_router_body,
        grid=(T // tm,),
        in_specs=[
            pl.BlockSpec((tm, D_MODEL), lambda i: (i, 0)),
            pl.BlockSpec((tm, RET_WIDTH), lambda i: (i, 0)),
            pl.BlockSpec((tm, SSM_WIDTH), lambda i: (i, 0)),
            pl.BlockSpec((D_MODEL, D_MODEL), lambda i: (0, 0), pipeline_mode=pl.Buffered(1)),
            full((1, D_MODEL)),
            full((LANES, D_MODEL)),
            full((LANES, 1)),
        ],
        out_specs=[
            pl.BlockSpec((tm, D_MODEL), lambda i: (i, 0)),
            pl.BlockSpec((tm, Y_ROWS, LANES), lambda i: (i, 0, 0)),
            pl.BlockSpec((tm, LANES), lambda i: (i, 0)),
            pl.BlockSpec((tm // SORT_BLK, 8, SORT_BLK), lambda i: (i, 0, 0)),
        ],
        out_shape=[
            jax.ShapeDtypeStruct((T, D_MODEL), F32),
            jax.ShapeDtypeStruct((T, Y_ROWS, LANES), F32),
            jax.ShapeDtypeStruct((T, LANES), F32),
            jax.ShapeDtypeStruct((T // SORT_BLK, 8, SORT_BLK), I32),
        ],
        compiler_params=_cparams(("parallel",)),
        name="out_router",
    )(x, y_ret, y_ssm, w_out, nw, wr_t, br)


SORT_BLK = 256
MOE_TM = 256


def _sort_index_body(ids_ref, tri_ref, ltri_ref, dest_ref, cnt_ref, rank_ref):
    nblk = ids_ref.shape[0]
    row_e = lax.broadcasted_iota(I32, (N_EXPERTS, SORT_BLK), 0)
    row8 = lax.broadcasted_iota(I32, (8, SORT_BLK), 0)

    def onehots(b):
        ids = ids_ref[b]
        return row_e == ids[0:1], row_e == ids[1:2]

    def rank_blk(b, carry):
        oh1, oh2 = onehots(b)
        ohf = jnp.where(oh1 | oh2, 1.0, 0.0)
        incl = _dot(ohf.astype(BF16), tri_ref[...])
        base = carry + incl - 1.0
        r1 = jnp.sum(jnp.where(oh1, base, 0.0), axis=0, keepdims=True)
        r2 = jnp.sum(jnp.where(oh2, base, 0.0), axis=0, keepdims=True)
        rank_ref[b] = jnp.where(row8 == 0, r1, jnp.where(row8 == 1, r2, 0.0))
        return carry + jnp.sum(ohf, axis=1, keepdims=True)

    cnt = lax.fori_loop(0, nblk, rank_blk, jnp.zeros((N_EXPERTS, 1), F32))
    cnt_ref[...] = jnp.broadcast_to(cnt, cnt_ref.shape)
    tiles = jnp.floor((cnt + (MOE_TM - 1.0)) / MOE_TM)
    tiles_b = jnp.broadcast_to(tiles, (N_EXPERTS, LANES)).astype(BF16)
    off = _dot(ltri_ref[...], tiles_b)[:, 0:1] * MOE_TM

    def dest_blk(b, carry):
        oh1, oh2 = onehots(b)
        o1 = jnp.sum(jnp.where(oh1, off, 0.0), axis=0, keepdims=True)
        o2 = jnp.sum(jnp.where(oh2, off, 0.0), axis=0, keepdims=True)
        d = rank_ref[b] + jnp.where(row8 == 0, o1, jnp.where(row8 == 1, o2, 0.0))
        dest_ref[b] = d.astype(I32)
        return carry

    lax.fori_loop(0, nblk, dest_blk, 0)


def _sort_index(ids, tri_u, ltri):
    nblk = ids.shape[0]
    return pl.pallas_call(
        _sort_index_body,
        out_shape=[jax.ShapeDtypeStruct((nblk, 8, SORT_BLK), I32),
                   jax.ShapeDtypeStruct((N_EXPERTS, LANES), F32)],
        scratch_shapes=[pltpu.VMEM((nblk, 8, SORT_BLK), F32)],
        compiler_params=_cparams(None),
        name="sort_index",
    )(ids, tri_u, ltri)


ROW_DMA_PRIORITY = 1
GATHER_AHEAD = 3


def _moe_body(te_ref, tf_ref, nx_ref, nu_ref, dest_ref, lo_ref, hi_ref,
              h2t_ref, wg_ref, wu_ref, wd_ref, yt_ref,
              srow_ref, wcount, xbuf, ystage, wsg, wsu, wsd, wgb_ref, wub_ref, wdb_ref, gsem, ssem, wsem):
    i = pl.program_id(0)
    nu = nu_ref[0]
    T = h2t_ref.shape[0]

    def gather_row(tile, slot, r):
        tok = srow_ref[tile * MOE_TM + r] & (T - 1)
        return pltpu.make_async_copy(h2t_ref.at[tok], xbuf.at[slot, :, r, :], gsem.at[slot])

    def scatter_row(tile, slot, r):
        return pltpu.make_async_copy(ystage.at[slot, :, r, :], yt_ref.at[srow_ref[tile * MOE_TM + r]],
                                     ssem.at[slot])

    def gather_tile(slot):
        return pltpu.make_async_copy(h2t_ref.at[pl.ds(0, MOE_TM)], h2t_ref.at[pl.ds(0, MOE_TM)], gsem.at[slot])

    def scatter_tile(slot):
        return pltpu.make_async_copy(yt_ref.at[pl.ds(0, MOE_TM)], yt_ref.at[pl.ds(0, MOE_TM)], ssem.at[slot])

    def weight_copies(e, ws):
        return (pltpu.make_async_copy(wg_ref.at[e], wsg.at[ws], wsem.at[ws, 0]),
                pltpu.make_async_copy(wu_ref.at[e], wsu.at[ws], wsem.at[ws, 1]),
                pltpu.make_async_copy(wd_ref.at[e], wsd.at[ws], wsem.at[ws, 2]))

    @pl.when(i == 0)
    def _():
        wcount[0] = 0
        for c in weight_copies(te_ref[0], 0):
            c.start()

        def tok(t, c):
            srow_ref[dest_ref[t]] = t
            srow_ref[dest_ref[T + t]] = T + t
            return c

        lax.fori_loop(0, T, tok, 0, unroll=8)

        def seg(e, c):
            def pad(r, c2):
                srow_ref[r] = 2 * T + (r & (MOE_TM - 1))
                return c2

            lax.fori_loop(lo_ref[e], hi_ref[e], pad, 0)
            return c

        lax.fori_loop(0, lo_ref.shape[0], seg, 0)

        ystage[...] = jnp.zeros_like(ystage)

        def spare(r, c):
            pltpu.make_async_copy(ystage.at[0, :, r, :], yt_ref.at[2 * T + r], ssem.at[0]).start()
            return c

        lax.fori_loop(0, MOE_TM, spare, 0)
        scatter_tile(0).wait()
        for d in range(GATHER_AHEAD):
            first = jnp.minimum(d, nu - 1)
            for r in range(MOE_TM):
                gather_row(first, d, r).start(priority=ROW_DMA_PRIORITY)

    @pl.when(tf_ref[i] == 1)
    def _():
        ws = lax.rem(wcount[0], 2)
        wcount[0] = wcount[0] + 1
        for c in weight_copies(te_ref[i], ws):
            c.wait()

        @pl.when(nx_ref[i] >= 0)
        def _():
            for c in weight_copies(nx_ref[i], 1 - ws):
                c.start()

        wgb_ref[...] = wsg[ws].astype(BF16)
        wub_ref[...] = wsu[ws].astype(BF16)
        wdb_ref[...] = wsd[ws].astype(BF16)

    slot = lax.rem(i, 2)
    other = 1 - slot
    n_xbuf = GATHER_AHEAD + 1
    xs = lax.rem(i, n_xbuf)

    @pl.when(i < nu)
    def _():
        gather_tile(xs).wait()

        @pl.when(i >= 1)
        def _():
            scatter_tile(slot).wait()

        nxt = jnp.minimum(i + GATHER_AHEAD, nu - 1)
        nxs = lax.rem(i + GATHER_AHEAD, n_xbuf)
        prv = jnp.maximum(i - 1, 0)
        for r in range(MOE_TM):
            gather_row(nxt, nxs, r).start(priority=r % 2)
            scatter_row(prv, other, r).start(priority=(r + 1) % 2)

        x = jnp.concatenate([xbuf[xs, s].astype(BF16) for s in range(Y_ROWS)], axis=1)
        a = _dot(x, wgb_ref[...])
        u = _dot(x, wub_ref[...])
        act = (_silu(a) * u).astype(BF16)
        y = _dot(act, wdb_ref[...])
        for s in range(Y_ROWS):
            ystage[slot, s] = y[:, s * LANES:(s + 1) * LANES]

    @pl.when(i == nu)
    def _():
        for d in range(GATHER_AHEAD):
            gather_tile(lax.rem(i + d, n_xbuf)).wait()
        scatter_tile(slot).wait()

        def last(r, c):
            scatter_row(nu - 1, other, r).start()
            return c

        lax.fori_loop(0, MOE_TM, last, 0)
        scatter_tile(other).wait()


def _moe(tile_expert, tile_first, next_expert, n_used, dest, pad_lo, pad_hi, h2t, w_gate, w_up, w_down):
    T = h2t.shape[0]
    assert T & (T - 1) == 0 and T >= MOE_TM, "token index is recovered from the row table by masking"
    n_tiles = tile_expert.shape[0] - 1
    hbm = pl.BlockSpec(memory_space=pl.ANY)
    grid_spec = pltpu.PrefetchScalarGridSpec(
        num_scalar_prefetch=7,
        grid=(n_tiles + 1,),
        in_specs=[hbm, hbm, hbm, hbm],
        out_specs=hbm,
        scratch_shapes=[pltpu.SMEM((n_tiles * MOE_TM,), I32),
                        pltpu.SMEM((1,), I32),
                        pltpu.VMEM((GATHER_AHEAD + 1, Y_ROWS, MOE_TM, LANES), F32),
                        pltpu.VMEM((2, Y_ROWS, MOE_TM, LANES), F32),
                        pltpu.VMEM((2, D_MODEL, D_EXPERT), F32),
                        pltpu.VMEM((2, D_MODEL, D_EXPERT), F32),
                        pltpu.VMEM((2, D_EXPERT, D_MODEL), F32),
                        pltpu.VMEM((D_MODEL, D_EXPERT), BF16),
                        pltpu.VMEM((D_MODEL, D_EXPERT), BF16),
                        pltpu.VMEM((D_EXPERT, D_MODEL), BF16),
                        pltpu.SemaphoreType.DMA((GATHER_AHEAD + 1,)),
                        pltpu.SemaphoreType.DMA((2,)),
                        pltpu.SemaphoreType.DMA((2, 3))],
    )
    return pl.pallas_call(
        _moe_body,
        grid_spec=grid_spec,
        out_shape=jax.ShapeDtypeStruct((2 * T + MOE_TM, Y_ROWS, LANES), F32),
        compiler_params=_cparams(("arbitrary",)),
        name="moe",
    )(tile_expert, tile_first, next_expert, n_used, dest, pad_lo, pad_hi, h2t, w_gate, w_up, w_down)


COMB_TM = 512


def _combine_body(h1_ref, y0_ref, y1_ref, cw_ref, nw_ref, o_ref):
    tm = h1_ref.shape[0]
    c0 = cw_ref[:, 0:1]
    c1 = cw_ref[:, 1:2]
    cols = []
    for s in range(Y_ROWS):
        moe = c0 * y0_ref[:, s, :] + c1 * y1_ref[:, s, :]
        cols.append(h1_ref[:, s * LANES:(s + 1) * LANES] + moe)
    h = jnp.concatenate(cols, axis=1)
    ms = jnp.mean(h * h, axis=-1, keepdims=True)
    o_ref[...] = h * lax.rsqrt(ms + EPS) * nw_ref[...]


def _combine(h1, y_tok, cw, nw):
    T = h1.shape[0]
    tm = min(COMB_TM, T)
    nt = T // tm
    return pl.pallas_call(
        _combine_body,
        grid=(nt,),
        in_specs=[
            pl.BlockSpec((tm, D_MODEL), lambda i: (i, 0)),
            pl.BlockSpec((tm, Y_ROWS, LANES), lambda i: (i, 0, 0)),
            pl.BlockSpec((tm, Y_ROWS, LANES), lambda i: (i + nt, 0, 0)),
            pl.BlockSpec((tm, LANES), lambda i: (i, 0)),
            pl.BlockSpec((1, D_MODEL), lambda i: (0, 0)),
        ],
        out_specs=pl.BlockSpec((tm, D_MODEL), lambda i: (i, 0)),
        out_shape=jax.ShapeDtypeStruct((T, D_MODEL), F32),
        compiler_params=_cparams(("parallel",)),
        name="combine",
    )(h1, y_tok, y_tok, cw, nw)


def _tile_plan(cnt, n_tiles):
    tiles = (cnt + (MOE_TM - 1)) // MOE_TM
    ends = jnp.cumsum(tiles)
    starts = ends - tiles
    n_used = ends[-1]
    step = jnp.arange(n_tiles + 1, dtype=I32)
    tile = jnp.minimum(step, jnp.maximum(n_used - 1, 0))
    tile_expert = jnp.sum((ends[None, :] <= tile[:, None]).astype(I32), axis=1)
    tile_first = ((step == starts[tile_expert]) & (step < n_used)).astype(I32)
    nxt_tile = ends[tile_expert]
    nxt_expert = jnp.sum((ends[None, :] <= nxt_tile[:, None]).astype(I32), axis=1)
    next_expert = jnp.where(nxt_tile < n_used, nxt_expert, -1).astype(I32)
    pad_lo = starts * MOE_TM + cnt
    pad_hi = ends * MOE_TM
    return (tile_expert, tile_first, next_expert, n_used.reshape(1).astype(I32),
            pad_lo.astype(I32), pad_hi.astype(I32))


def kernel(x, positions, norm1_w, w_in, conv_w, conv_b, dt_bias, a_log, d_skip, ret_norm_w,
           ssm_norm_w, w_out, norm2_w, w_router_group, b_router_group, w_router_expert,
           b_router_expert, w_expert_gate, w_expert_up, w_expert_down, final_norm_w):
    B, T, D = x.shape
    assert B == 1 and D == D_MODEL and T % CHUNK == 0
    xf = x.reshape(T, D)
    pad_l = lambda v: jnp.pad(v, ((0, 0), (0, LANES - v.shape[-1])))

    w_in_t = jnp.swapaxes(w_in[0], 0, 1)
    w_dt = jnp.pad(w_in_t[MAIN_PROJ:], ((0, LANES - (w_in_t.shape[0] - MAIN_PROJ)), (0, 0))).astype(BF16)
    half = RET_HEAD_DIM // 2
    inv = (ROPE_THETA ** (-jnp.arange(half, dtype=F32) / half)).reshape(1, half)
    tri = (jnp.arange(CHUNK)[:, None] >= jnp.arange(CHUNK)[None, :]).astype(BF16)
    d_skip_e = jnp.repeat(d_skip[0], SSM_HEAD_DIM).reshape(1, SSM_WIDTH)
    wr_t = jnp.zeros((LANES, D), F32)
    wr_t = wr_t.at[0:N_GROUPS].set(w_router_group[0].T).at[8:8 + N_EXPERTS].set(w_router_expert[0].T)
    br = jnp.zeros((LANES,), F32)
    br = br.at[0:N_GROUPS].set(b_router_group[0]).at[8:8 + N_EXPERTS].set(b_router_expert[0])
    tri_u = (jnp.arange(SORT_BLK)[:, None] <= jnp.arange(SORT_BLK)[None, :]).astype(BF16)
    ltri = (jnp.arange(N_EXPERTS)[:, None] > jnp.arange(N_EXPERTS)[None, :]).astype(BF16)

    w_main, w_out_b, cos, sin = _prep(w_in_t, w_out[0], positions.reshape(T, 1).astype(F32), inv)
    y_ret, y_ssm = _mixer(xf, norm1_w[0].reshape(1, D), w_main, w_dt, cos, sin,
                          ret_norm_w[0].reshape(1, RET_WIDTH), _retention_consts(),
                          conv_w[0], conv_b[0].reshape(1, -1), pad_l(dt_bias[0].reshape(1, -1)),
                          pad_l(a_log[0].reshape(1, -1)), d_skip_e, ssm_norm_w[0].reshape(1, -1), tri)
    h1, h2t, cw, ids = _out_router(xf, y_ret, y_ssm, w_out_b, norm2_w[0].reshape(1, D),
                                   wr_t.astype(BF16), br.reshape(LANES, 1))

    dest_blk, cnt = _sort_index(ids, tri_u, ltri)
    dest = dest_blk[:, 0:2, :].transpose(1, 0, 2).reshape(2 * T)
    n_tiles = (2 * T) // MOE_TM + N_EXPERTS
    tile_expert, tile_first, next_expert, n_used, pad_lo, pad_hi = _tile_plan(cnt[:, 0].astype(I32), n_tiles)
    y_tok = _moe(tile_expert, tile_first, next_expert, n_used, dest, pad_lo, pad_hi, h2t,
                 w_expert_gate[0], w_expert_up[0], w_expert_down[0])
    out = _combine(h1, y_tok, cw, final_norm_w.reshape(1, D))
    return out.reshape(B, T, D)
```

```python
import functools

import jax
import jax.numpy as jnp
from jax import lax
from jax.experimental import pallas as pl
from jax.experimental.pallas import tpu as pltpu

F32 = jnp.float32
BF16 = jnp.bfloat16
I32 = jnp.int32

D_MODEL = 2048
EPS = 1e-6
CHUNK = 128
RET_HEADS = 4
RET_HEAD_DIM = 256
RET_WIDTH = RET_HEADS * RET_HEAD_DIM
ROPE_THETA = 10000.0
SSM_WIDTH = 1024
SSM_HEAD_DIM = 64
SSM_HEADS = SSM_WIDTH // SSM_HEAD_DIM
SSM_GROUPS = 2
SSM_STATE = 128
SSM_CONV = 4
SSM_CONV_DIM = SSM_WIDTH + 2 * SSM_GROUPS * SSM_STATE
MAIN_PROJ = 4 * RET_WIDTH + SSM_WIDTH + SSM_CONV_DIM
N_GROUPS = 4
GROUP_EXPERTS = 8
N_EXPERTS = N_GROUPS * GROUP_EXPERTS
D_EXPERT = 512
LANES = 128
Y_ROWS = D_MODEL // LANES

VMEM_LIMIT = 56 * 1024 * 1024


def _cparams(sem, vmem=VMEM_LIMIT):
    return pltpu.CompilerParams(dimension_semantics=sem, vmem_limit_bytes=vmem)


def _silu(x):
    return x * (1.0 / (1.0 + jnp.exp(-x)))


def _dot(a, b):
    return jnp.dot(a, b, preferred_element_type=F32)


def _dot_nt(a, b):
    return lax.dot_general(a, b, (((1,), (1,)), ((), ())), preferred_element_type=F32)


def _split3(a):
    a1 = a.astype(BF16)
    r1 = a - a1.astype(F32)
    a2 = r1.astype(BF16)
    a3 = (r1 - a2.astype(F32)).astype(BF16)
    return a1, a2, a3


CAST_ROWS = 512
ROPE_ROWS = 1024


def _prep_body(win_ref, wout_ref, pos_ref, inv_ref, winb_ref, woutb_ref, cos_ref, sin_ref, *, n_in, n_rope):
    j = pl.program_id(0)

    @pl.when(j < n_in)
    def _():
        winb_ref[...] = win_ref[...].astype(BF16)

    @pl.when(j >= n_in)
    def _():
        woutb_ref[...] = wout_ref[...].astype(BF16)

    @pl.when(j < n_rope)
    def _():
        ang = pos_ref[...] * inv_ref[...]
        cos_ref[...] = jnp.cos(ang)
        sin_ref[...] = jnp.sin(ang)


def _prep(w_in_t, w_out, pos, inv):
    T = pos.shape[0]
    cols = w_in_t.shape[1]
    half = inv.shape[1]
    n_in = MAIN_PROJ // CAST_ROWS
    n_out = w_out.shape[0] // CAST_ROWS
    rr = min(ROPE_ROWS, T)
    n_rope = T // rr
    assert n_rope <= n_in + n_out
    in_blk = lambda j: (jnp.minimum(j, n_in - 1), 0)
    out_blk = lambda j: (jnp.clip(j - n_in, 0, n_out - 1), 0)
    rope_blk = lambda j: (jnp.minimum(j, n_rope - 1), 0)
    return pl.pallas_call(
        functools.partial(_prep_body, n_in=n_in, n_rope=n_rope),
        grid=(n_in + n_out,),
        in_specs=[pl.BlockSpec((CAST_ROWS, cols), in_blk),
                  pl.BlockSpec((CAST_ROWS, w_out.shape[1]), out_blk),
                  pl.BlockSpec((rr, 1), rope_blk),
                  pl.BlockSpec((1, half), lambda j: (0, 0))],
        out_specs=[pl.BlockSpec((CAST_ROWS, cols), in_blk),
                   pl.BlockSpec((CAST_ROWS, w_out.shape[1]), out_blk),
                   pl.BlockSpec((rr, half), rope_blk),
                   pl.BlockSpec((rr, half), rope_blk)],
        out_shape=[jax.ShapeDtypeStruct((MAIN_PROJ, cols), BF16),
                   jax.ShapeDtypeStruct(w_out.shape, BF16),
                   jax.ShapeDtypeStruct((T, half), F32),
                   jax.ShapeDtypeStruct((T, half), F32)],
        compiler_params=_cparams(("arbitrary",)),
        name="prep",
    )(w_in_t, w_out, pos, inv)


def _retention_chunks(cd_ref, q_ref, k_ref, v_ref, g_ref, cos_ref, sin_ref,
                      dec_ref, wq_ref, ws_ref, nw_ref, o_ref, st_ref, cb, tick):
    hd = RET_HEAD_DIM
    half = hd // 2
    for i in range(cb):
        sl = slice(i * CHUNK, (i + 1) * CHUNK)
        cos = cos_ref[sl, :]
        sin = sin_ref[sl, :]

        def rope(x):
            x1, x2 = x[:, :half], x[:, half:]
            return jnp.concatenate([x1 * cos - x2 * sin, x2 * cos + x1 * sin], axis=-1)

        for h in range(RET_HEADS):
            hs = slice(h * hd, (h + 1) * hd)
            q = rope(q_ref[sl, hs])
            k = rope(k_ref[sl, hs]) * (hd ** -0.5)
            v = v_ref[sl, hs]
            qb = q.astype(BF16)
            kb = k.astype(BF16)
            scores = _dot_nt(qb, kb) * dec_ref[h]
            y = _dot(scores.astype(BF16), v.astype(BF16))
            st = st_ref[h]
            y = y + _dot(qb, st.astype(BF16)) * wq_ref[h]
            vw = (v * ws_ref[h]).astype(BF16)
            new = _dot(k.T.astype(BF16), vw)
            st_ref[h] = st * cd_ref[h] + new
            mu = jnp.mean(y, axis=-1, keepdims=True)
            d = y - mu
            var = jnp.mean(d * d, axis=-1, keepdims=True)
            yn = d * lax.rsqrt(var + EPS)
            o_ref[sl, hs] = (yn * nw_ref[:, hs] * _silu(g_ref[sl, hs])).astype(BF16)
            tick()


def _retention_consts():
    H, L = RET_HEADS, CHUNK
    log_gamma = jnp.log1p(-(2.0 ** (-5.0 - jnp.arange(H, dtype=F32))))
    idx = jnp.arange(L, dtype=F32)
    diff = idx[:, None] - idx[None, :]
    causal = diff >= 0
    decay_intra = jnp.where(causal[None], jnp.exp(jnp.where(causal, diff, 0.0)[None] * log_gamma[:, None, None]), 0.0)
    w_state = jnp.exp((L - 1.0 - idx)[None, :] * log_gamma[:, None])
    w_query = jnp.exp((idx + 1.0)[None, :] * log_gamma[:, None])
    chunk_decay = jnp.exp(L * log_gamma)
    bc = lambda w: jnp.broadcast_to(w[:, :, None], (H, L, RET_HEAD_DIM))
    return chunk_decay, decay_intra, bc(w_query), bc(w_state)


CONV_PAD = 8
XBC_BLK = 512


def _ssd_chunks(xbc_ref, z_ref, dt_ref, cw_ref, cbias_ref, dtb_ref,
                alog_ref, dskip_ref, nw_ref, tri_ref, o_ref, xpad_ref, st_ref, cb, tick):
    L = CHUNK
    gw = SSM_WIDTH // SSM_GROUPS
    rows_i = lax.broadcasted_iota(I32, (L, L), 0)
    cols_i = lax.broadcasted_iota(I32, (L, L), 1)
    causal = rows_i >= cols_i
    lo_lane = lax.broadcasted_iota(I32, (L, LANES), 1) < SSM_HEAD_DIM
    tri = tri_ref[...]
    a_neg = -jnp.exp(alog_ref[...])

    for i in range(cb):
        sl = slice(i * L, (i + 1) * L)
        xpad_ref[CONV_PAD:CONV_PAD + L, :] = xbc_ref[sl, :]
        u_parts = []
        for b in range(SSM_CONV_DIM // XBC_BLK):
            cs = slice(b * XBC_BLK, (b + 1) * XBC_BLK)
            acc = cbias_ref[:, cs]
            for t in range(SSM_CONV):
                r0 = CONV_PAD - (SSM_CONV - 1) + t
                acc = acc + xpad_ref[r0:r0 + L, cs] * cw_ref[t:t + 1, cs]
            u_parts.append(_silu(acc))
            tick()
        xpad_ref[0:CONV_PAD, :] = xpad_ref[L:L + CONV_PAD, :]
        xs = jnp.concatenate(u_parts[:2], axis=-1)
        bm = u_parts[2][:, :SSM_GROUPS * SSM_STATE]
        cm = u_parts[2][:, SSM_GROUPS * SSM_STATE:]

        dt_in = dt_ref[sl, :] + dtb_ref[...]
        dt = jnp.maximum(dt_in, 0.0) + jnp.log1p(jnp.exp(-jnp.abs(dt_in)))
        a = dt * a_neg
        a1, a2, a3 = _split3(a)
        a_cs = _dot(tri, a1) + _dot(tri, a2) + _dot(tri, a3)
        last = a_cs[L - 1:L, :]

        def per_head_lanes(v):
            lo = lo_lane[0:v.shape[0], :]
            return jnp.concatenate([jnp.where(lo, v[:, 2 * p:2 * p + 1], v[:, 2 * p + 1:2 * p + 2])
                                    for p in range(SSM_HEADS // 2)], axis=1)

        dt_e = per_head_lanes(dt)
        to_end = per_head_lanes(jnp.exp(last - a_cs))
        from_start = per_head_lanes(jnp.exp(a_cs))
        chunk_decay = per_head_lanes(jnp.exp(last))
        tick()
        x_dt = xs * dt_e
        xw = (x_dt * to_end).astype(BF16)
        acs_t = a_cs.T
        cmb = cm.astype(BF16)
        bmb = bm.astype(BF16)

        ys = []
        for g in range(SSM_GROUPS):
            ns = slice(g * SSM_STATE, (g + 1) * SSM_STATE)
            gs = slice(g * gw, (g + 1) * gw)
            cg = cmb[:, ns]
            cbg = _dot_nt(cg, bmb[:, ns])
            st = st_ref[g]
            y_off = _dot(cg, st.astype(BF16))
            new = _dot(bm[:, ns].T.astype(BF16), xw[:, gs])
            st_ref[g] = st * chunk_decay[:, gs] + new
            tick()
            for jp in range(gw // LANES):
                h0 = (g * gw + jp * LANES) // SSM_HEAD_DIM
                ms = []
                for hh in (h0, h0 + 1):
                    seg = a_cs[:, hh:hh + 1] - acs_t[hh:hh + 1, :]
                    dec = jnp.exp(jnp.where(causal, seg, -jnp.inf))
                    ms.append((cbg * dec).astype(BF16))
                lhs = jnp.concatenate(ms, axis=1)
                ls = slice(g * gw + jp * LANES, g * gw + (jp + 1) * LANES)
                xp = x_dt[:, ls]
                rhs = jnp.concatenate([jnp.where(lo_lane, xp, 0.0),
                                       jnp.where(lo_lane, 0.0, xp)], axis=0).astype(BF16)
                y_diag = _dot(lhs, rhs)
                ys.append(y_diag + y_off[:, jp * LANES:(jp + 1) * LANES] * from_start[:, ls])
                tick()
        y = jnp.concatenate(ys, axis=1) + xs * dskip_ref[...]
        y = y * _silu(z_ref[sl, :])
        outs = []
        for g in range(SSM_GROUPS):
            yg = y[:, g * gw:(g + 1) * gw]
            ms_ = jnp.mean(yg * yg, axis=-1, keepdims=True)
            outs.append(yg * lax.rsqrt(ms_ + EPS))
        o_ref[sl, :] = (jnp.concatenate(outs, axis=1) * nw_ref[...]).astype(BF16)


MIX_CB = 4
MIX_VMEM = 60 * 1024 * 1024
COL_Z = 4 * RET_WIDTH
COL_XBC = COL_Z + SSM_WIDTH
PROJ_WIDTHS = (RET_WIDTH, RET_WIDTH, RET_WIDTH, RET_WIDTH, SSM_WIDTH, SSM_CONV_DIM, LANES)
PROJ_SLAB = 256


def _mixer_body(cd_ref, x_ref, n1w_ref, wt_hbm, wdt_ref, cos_ref, sin_ref, dec_ref, wq_ref, ws_ref, rnw_ref,
                cw_ref, cbias_ref, dtb_ref, alog_ref, dskip_ref, snw_ref, tri_ref,
                yret_ref, yssm_ref, w_ref, hn_ref, pq, pk, pv, pg, pz, pxbc, pdt,
                st_ret, xpad_ref, st_ssd, sem, *, cb):
    @pl.when(pl.program_id(0) == 0)
    def _():
        cp = pltpu.make_async_copy(wt_hbm, w_ref, sem)
        cp.start()
        st_ret[...] = jnp.zeros_like(st_ret)
        xpad_ref[0:CONV_PAD, :] = jnp.zeros((CONV_PAD, SSM_CONV_DIM), F32)
        st_ssd[...] = jnp.zeros_like(st_ssd)
        cp.wait()

    def slabs(ref, col0):
        def slab(lo):
            def go():
                ref[:, lo:lo + PROJ_SLAB] = _dot_nt(hn_ref[...], w_ref[col0 + lo:col0 + lo + PROJ_SLAB, :])
            return go
        return [slab(lo) for lo in range(0, ref.shape[1], PROJ_SLAB)]

    x = x_ref[...]
    ms = jnp.mean(x * x, axis=-1, keepdims=True)
    hn_ref[...] = (x * lax.rsqrt(ms + EPS) * n1w_ref[...]).astype(BF16)
    for piece in slabs(pxbc, COL_XBC) + slabs(pz, COL_Z):
        piece()
    pdt[...] = _dot_nt(hn_ref[...], wdt_ref[...])

    pieces = slabs(pq, 0) + slabs(pk, RET_WIDTH) + slabs(pv, 2 * RET_WIDTH) + slabs(pg, 3 * RET_WIDTH)
    n_pieces = len(pieces)
    n_ticks = cb * (SSM_CONV_DIM // XBC_BLK + 1 + SSM_GROUPS + SSM_WIDTH // LANES)
    calls = [0]

    def tick():
        calls[0] += 1
        while pieces and (n_pieces - len(pieces)) * n_ticks < calls[0] * n_pieces:
            pieces.pop(0)()

    _ssd_chunks(pxbc, pz, pdt, cw_ref, cbias_ref, dtb_ref, alog_ref, dskip_ref, snw_ref, tri_ref,
                yssm_ref, xpad_ref, st_ssd, cb, tick)
    while pieces:
        pieces.pop(0)()
    _retention_chunks(cd_ref, pq, pk, pv, pg, cos_ref, sin_ref, dec_ref, wq_ref, ws_ref, rnw_ref,
                      yret_ref, st_ret, cb, lambda: None)


def _mixer(x, n1w, w_main_t, w_dt_t, cos, sin, ret_nw, ret_consts,
           conv_w, conv_b, dt_bias, a_log, d_skip_e, ssm_nw, tri):
    T = x.shape[0]
    cb = min(MIX_CB, T // CHUNK)
    rows = cb * CHUNK
    chunk_decay, decay_intra, w_query, w_state = ret_consts
    hd = RET_HEAD_DIM
    half = hd // 2
    full = lambda shape: pl.BlockSpec(shape, lambda c, cd: (0,) * len(shape))
    rowblk = lambda width: pl.BlockSpec((rows, width), lambda c, cd: (c, 0))
    scratch = lambda width: pltpu.VMEM((rows, width), F32)
    grid_spec = pltpu.PrefetchScalarGridSpec(
        num_scalar_prefetch=1,
        grid=(T // rows,),
        in_specs=[
            rowblk(D_MODEL),
            full((1, D_MODEL)),
            pl.BlockSpec(memory_space=pl.ANY),
            full((LANES, D_MODEL)),
            rowblk(half), rowblk(half),
            full((RET_HEADS, CHUNK, CHUNK)),
            full((RET_HEADS, CHUNK, hd)),
            full((RET_HEADS, CHUNK, hd)),
            full((1, RET_WIDTH)),
            full((SSM_CONV, SSM_CONV_DIM)),
            full((1, SSM_CONV_DIM)),
            full((1, LANES)),
            full((1, LANES)),
            full((1, SSM_WIDTH)),
            full((1, SSM_WIDTH)),
            full((CHUNK, CHUNK)),
        ],
        out_specs=[rowblk(RET_WIDTH), rowblk(SSM_WIDTH)],
        scratch_shapes=[
            pltpu.VMEM((MAIN_PROJ, D_MODEL), BF16),
            pltpu.VMEM((rows, D_MODEL), BF16),
            *[scratch(w) for w in PROJ_WIDTHS],
            pltpu.VMEM((RET_HEADS, hd, hd), F32),
            pltpu.VMEM((CHUNK + CONV_PAD, SSM_CONV_DIM), F32),
            pltpu.VMEM((SSM_GROUPS, SSM_STATE, SSM_WIDTH // SSM_GROUPS), F32),
            pltpu.SemaphoreType.DMA(()),
        ],
    )
    return pl.pallas_call(
        functools.partial(_mixer_body, cb=cb),
        grid_spec=grid_spec,
        out_shape=[jax.ShapeDtypeStruct((T, RET_WIDTH), BF16),
                   jax.ShapeDtypeStruct((T, SSM_WIDTH), BF16)],
        compiler_params=_cparams(("arbitrary",), MIX_VMEM),
        name="mixer",
    )(chunk_decay, x, n1w, w_main_t, w_dt_t, cos, sin, decay_intra, w_query, w_state, ret_nw,
      conv_w, conv_b, dt_bias, a_log, d_skip_e, ssm_nw, tri)


OUT_TM = 512
ROUTE_ROWS = 8 + N_EXPERTS


def _out_router_body(x_ref, yr_ref, ys_ref, wo_ref, nw_ref, wr_ref, br_ref,
                     h1_ref, h2t_ref, cw_ref, ids_ref):
    tm = x_ref.shape[0]
    h1 = x_ref[...] + _dot(yr_ref[...], wo_ref[0:RET_WIDTH, :]) + _dot(ys_ref[...], wo_ref[RET_WIDTH:, :])
    h1_ref[...] = h1
    ms = jnp.mean(h1 * h1, axis=-1, keepdims=True)
    h2 = h1 * lax.rsqrt(ms + EPS) * nw_ref[...]
    for s in range(Y_ROWS):
        h2t_ref[:, s, :] = h2[:, s * LANES:(s + 1) * LANES]

    logits = _dot_nt(wr_ref[...], h2.astype(BF16)) + br_ref[...]
    row = lax.broadcasted_iota(I32, (8, tm), 0)
    lg = jnp.where(row < N_GROUPS, logits[0:8], -jnp.inf)
    m = jnp.max(lg, axis=0, keepdims=True)
    p_sel = 1.0 / jnp.sum(jnp.exp(lg - m), axis=0, keepdims=True)
    g_sel = jnp.min(jnp.where(lg == m, row, 8), axis=0, keepdims=True)
    le = jnp.zeros((GROUP_EXPERTS, tm), F32)
    for g in range(N_GROUPS):
        le = jnp.where(g_sel == g, logits[8 + g * GROUP_EXPERTS:8 + (g + 1) * GROUP_EXPERTS], le)
    m2 = jnp.max(le, axis=0, keepdims=True)
    ee = jnp.exp(le - m2)
    pe = ee / jnp.sum(ee, axis=0, keepdims=True)
    v1 = jnp.max(pe, axis=0, keepdims=True)
    i1 = jnp.min(jnp.where(pe == v1, row, 8), axis=0, keepdims=True)
    pe2 = jnp.where(row == i1, -1.0, pe)
    v2 = jnp.max(pe2, axis=0, keepdims=True)
    i2 = jnp.min(jnp.where(pe2 == v2, row, 8), axis=0, keepdims=True)
    tw = v1 + v2
    c1 = v1 / tw * p_sel
    c2 = v2 / tw * p_sel
    e1 = g_sel * GROUP_EXPERTS + i1
    e2 = g_sel * GROUP_EXPERTS + i2
    ids = jnp.where(row == 0, e1, jnp.where(row == 1, e2, 0))
    for b in range(tm // SORT_BLK):
        ids_ref[b] = ids[:, b * SORT_BLK:(b + 1) * SORT_BLK]
    cw8 = jnp.where(row == 0, c1, jnp.where(row == 1, c2, 0.0))
    cw = jnp.concatenate([cw8, jnp.zeros((LANES - 8, tm), F32)], axis=0)
    cw_ref[...] = cw.T


def _out_router(x, y_ret, y_ssm, w_out, nw, wr_t, br):
    T = x.shape[0]
    tm = min(OUT_TM, T)
    full = lambda shape: pl.BlockSpec(shape, lambda i: (0,) * len(shape))
    return pl.pallas_call(
        _out_router_body,
        grid=(T // tm,),
        in_specs=[
            pl.BlockSpec((tm, D_MODEL), lambda i: (i, 0)),
            pl.BlockSpec((tm, RET_WIDTH), lambda i: (i, 0)),
            pl.BlockSpec((tm, SSM_WIDTH), lambda i: (i, 0)),
            pl.BlockSpec((D_MODEL, D_MODEL), lambda i: (0, 0), pipeline_mode=pl.Buffered(1)),
            full((1, D_MODEL)),
            full((LANES, D_MODEL)),
            full((LANES, 1)),
        ],
        out_specs=[
            pl.BlockSpec((tm, D_MODEL), lambda i: (i, 0)),
            pl.BlockSpec((tm, Y_ROWS, LANES), lambda i: (i, 0, 0)),
            pl.BlockSpec((tm, LANES), lambda i: (i, 0)),
            pl.BlockSpec((tm // SORT_BLK, 8, SORT_BLK), lambda i: (i, 0, 0)),
        ],
        out_shape=[
            jax.ShapeDtypeStruct((T, D_MODEL), F32),
            jax.ShapeDtypeStruct((T, Y_ROWS, LANES), F32),
            jax.ShapeDtypeStruct((T, LANES), F32),
            jax.ShapeDtypeStruct((T // SORT_BLK, 8, SORT_BLK), I32),
        ],
        compiler_params=_cparams(("parallel",)),
        name="out_router",
    )(x, y_ret, y_ssm, w_out, nw, wr_t, br)


SORT_BLK = 256
MOE_TM = 256


def _sort_index_body(ids_ref, tri_ref, ltri_ref, dest_ref, cnt_ref, rank_ref):
    nblk = ids_ref.shape[0]
    row_e = lax.broadcasted_iota(I32, (N_EXPERTS, SORT_BLK), 0)
    row8 = lax.broadcasted_iota(I32, (8, SORT_BLK), 0)

    def onehots(b):
        ids = ids_ref[b]
        return row_e == ids[0:1], row_e == ids[1:2]

    def rank_blk(b, carry):
        oh1, oh2 = onehots(b)
        ohf = jnp.where(oh1 | oh2, 1.0, 0.0)
        incl = _dot(ohf.astype(BF16), tri_ref[...])
        base = carry + incl - 1.0
        r1 = jnp.sum(jnp.where(oh1, base, 0.0), axis=0, keepdims=True)
        r2 = jnp.sum(jnp.where(oh2, base, 0.0), axis=0, keepdims=True)
        rank_ref[b] = jnp.where(row8 == 0, r1, jnp.where(row8 == 1, r2, 0.0))
        return carry + jnp.sum(ohf, axis=1, keepdims=True)

    cnt = lax.fori_loop(0, nblk, rank_blk, jnp.zeros((N_EXPERTS, 1), F32))
    cnt_ref[...] = jnp.broadcast_to(cnt, cnt_ref.shape)
    tiles = jnp.floor((cnt + (MOE_TM - 1.0)) / MOE_TM)
    tiles_b = jnp.broadcast_to(tiles, (N_EXPERTS, LANES)).astype(BF16)
    off = _dot(ltri_ref[...], tiles_b)[:, 0:1] * MOE_TM

    def dest_blk(b, carry):
        oh1, oh2 = onehots(b)
        o1 = jnp.sum(jnp.where(oh1, off, 0.0), axis=0, keepdims=True)
        o2 = jnp.sum(jnp.where(oh2, off, 0.0), axis=0, keepdims=True)
        d = rank_ref[b] + jnp.where(row8 == 0, o1, jnp.where(row8 == 1, o2, 0.0))
        dest_ref[b] = d.astype(I32)
        return carry

    lax.fori_loop(0, nblk, dest_blk, 0)


def _sort_index(ids, tri_u, ltri):
    nblk = ids.shape[0]
    return pl.pallas_call(
        _sort_index_body,
        out_shape=[jax.ShapeDtypeStruct((nblk, 8, SORT_BLK), I32),
                   jax.ShapeDtypeStruct((N_EXPERTS, LANES), F32)],
        scratch_shapes=[pltpu.VMEM((nblk, 8, SORT_BLK), F32)],
        compiler_params=_cparams(None),
        name="sort_index",
    )(ids, tri_u, ltri)


ROW_DMA_PRIORITY = 1
GATHER_AHEAD = 2


def _moe_body(te_ref, tf_ref, nx_ref, nu_ref, dest_ref, lo_ref, hi_ref,
              h2t_ref, wg_ref, wu_ref, wd_ref, yt_ref,
              srow_ref, wcount, xbuf, ystage, wsg, wsu, wsd, wgb_ref, wub_ref, wdb_ref, gsem, ssem, wsem):
    i = pl.program_id(0)
    nu = nu_ref[0]
    T = h2t_ref.shape[0]

    def gather_row(tile, slot, r):
        tok = srow_ref[tile * MOE_TM + r] & (T - 1)
        return pltpu.make_async_copy(h2t_ref.at[tok], xbuf.at[slot, :, r, :], gsem.at[slot])

    def scatter_row(tile, slot, r):
        return pltpu.make_async_copy(ystage.at[slot, :, r, :], yt_ref.at[srow_ref[tile * MOE_TM + r]],
                                     ssem.at[slot])

    def gather_tile(slot):
        return pltpu.make_async_copy(h2t_ref.at[pl.ds(0, MOE_TM)], h2t_ref.at[pl.ds(0, MOE_TM)], gsem.at[slot])

    def scatter_tile(slot):
        return pltpu.make_async_copy(yt_ref.at[pl.ds(0, MOE_TM)], yt_ref.at[pl.ds(0, MOE_TM)], ssem.at[slot])

    def weight_copies(e, ws):
        return (pltpu.make_async_copy(wg_ref.at[e], wsg.at[ws], wsem.at[ws, 0]),
                pltpu.make_async_copy(wu_ref.at[e], wsu.at[ws], wsem.at[ws, 1]),
                pltpu.make_async_copy(wd_ref.at[e], wsd.at[ws], wsem.at[ws, 2]))

    @pl.when(i == 0)
    def _():
        wcount[0] = 0
        for c in weight_copies(te_ref[0], 0):
            c.start()

        def tok(t, c):
            srow_ref[dest_ref[t]] = t
            srow_ref[dest_ref[T + t]] = T + t
            return c

        lax.fori_loop(0, T, tok, 0, unroll=8)

        def seg(e, c):
            def pad(r, c2):
                srow_ref[r] = 2 * T + (r & (MOE_TM - 1))
                return c2

            lax.fori_loop(lo_ref[e], hi_ref[e], pad, 0)
            return c

        lax.fori_loop(0, lo_ref.shape[0], seg, 0)

        ystage[...] = jnp.zeros_like(ystage)

        def spare(r, c):
            pltpu.make_async_copy(ystage.at[0, :, r, :], yt_ref.at[2 * T + r], ssem.at[0]).start()
            return c

        lax.fori_loop(0, MOE_TM, spare, 0)
        scatter_tile(0).wait()
        for d in range(GATHER_AHEAD):
            first = jnp.minimum(d, nu - 1)
            for r in range(MOE_TM):
                gather_row(first, d, r).start(priority=ROW_DMA_PRIORITY)

    @pl.when(tf_ref[i] == 1)
    def _():
        ws = lax.rem(wcount[0], 2)
        wcount[0] = wcount[0] + 1
        for c in weight_copies(te_ref[i], ws):
            c.wait()

        @pl.when(nx_ref[i] >= 0)
        def _():
            for c in weight_copies(nx_ref[i], 1 - ws):
                c.start()

        wgb_ref[...] = wsg[ws].astype(BF16)
        wub_ref[...] = wsu[ws].astype(BF16)
        wdb_ref[...] = wsd[ws].astype(BF16)

    slot = lax.rem(i, 2)
    other = 1 - slot
    n_xbuf = GATHER_AHEAD + 1
    xs = lax.rem(i, n_xbuf)

    @pl.when(i < nu)
    def _():
        gather_tile(xs).wait()

        @pl.when(i >= 1)
        def _():
            scatter_tile(slot).wait()

        nxt = jnp.minimum(i + GATHER_AHEAD, nu - 1)
        nxs = lax.rem(i + GATHER_AHEAD, n_xbuf)
        prv = jnp.maximum(i - 1, 0)
        for r in range(MOE_TM):
            gather_row(nxt, nxs, r).start(priority=r % 2)
            scatter_row(prv, other, r).start(priority=(r + 1) % 2)

        x = jnp.concatenate([xbuf[xs, s].astype(BF16) for s in range(Y_ROWS)], axis=1)
        a = _dot(x, wgb_ref[...])
        u = _dot(x, wub_ref[...])
        act = (_silu(a) * u).astype(BF16)
        y = _dot(act, wdb_ref[...])
        for s in range(Y_ROWS):
            ystage[slot, s] = y[:, s * LANES:(s + 1) * LANES]

    @pl.when(i == nu)
    def _():
        for d in range(GATHER_AHEAD):
            gather_tile(lax.rem(i + d, n_xbuf)).wait()
        scatter_tile(slot).wait()

        def last(r, c):
            scatter_row(nu - 1, other, r).start()
            return c

        lax.fori_loop(0, MOE_TM, last, 0)
        scatter_tile(other).wait()


def _moe(tile_expert, tile_first, next_expert, n_used, dest, pad_lo, pad_hi, h2t, w_gate, w_up, w_down):
    T = h2t.shape[0]
    assert T & (T - 1) == 0 and T >= MOE_TM, "token index is recovered from the row table by masking"
    n_tiles = tile_expert.shape[0] - 1
    hbm = pl.BlockSpec(memory_space=pl.ANY)
    grid_spec = pltpu.PrefetchScalarGridSpec(
        num_scalar_prefetch=7,
        grid=(n_tiles + 1,),
        in_specs=[hbm, hbm, hbm, hbm],
        out_specs=hbm,
        scratch_shapes=[pltpu.SMEM((n_tiles * MOE_TM,), I32),
                        pltpu.SMEM((1,), I32),
                        pltpu.VMEM((GATHER_AHEAD + 1, Y_ROWS, MOE_TM, LANES), F32),
                        pltpu.VMEM((2, Y_ROWS, MOE_TM, LANES), F32),
                        pltpu.VMEM((2, D_MODEL, D_EXPERT), F32),
                        pltpu.VMEM((2, D_MODEL, D_EXPERT), F32),
                        pltpu.VMEM((2, D_EXPERT, D_MODEL), F32),
                        pltpu.VMEM((D_MODEL, D_EXPERT), BF16),
                        pltpu.VMEM((D_MODEL, D_EXPERT), BF16),
                        pltpu.VMEM((D_EXPERT, D_MODEL), BF16),
                        pltpu.SemaphoreType.DMA((GATHER_AHEAD + 1,)),
                        pltpu.SemaphoreType.DMA((2,)),
                        pltpu.SemaphoreType.DMA((2, 3))],
    )
    return pl.pallas_call(
        _moe_body,
        grid_spec=grid_spec,
        out_shape=jax.ShapeDtypeStruct((2 * T + MOE_TM, Y_ROWS, LANES), F32),
        compiler_params=_cparams(("arbitrary",)),
        name="moe",
    )(tile_expert, tile_first, next_expert, n_used, dest, pad_lo, pad_hi, h2t, w_gate, w_up, w_down)


COMB_TM = 512


def _combine_body(h1_ref, y0_ref, y1_ref, cw_ref, nw_ref, o_ref):
    tm = h1_ref.shape[0]
    c0 = cw_ref[:, 0:1]
    c1 = cw_ref[:, 1:2]
    cols = []
    for s in range(Y_ROWS):
        moe = c0 * y0_ref[:, s, :] + c1 * y1_ref[:, s, :]
        cols.append(h1_ref[:, s * LANES:(s + 1) * LANES] + moe)
    h = jnp.concatenate(cols, axis=1)
    ms = jnp.mean(h * h, axis=-1, keepdims=True)
    o_ref[...] = h * lax.rsqrt(ms + EPS) * nw_ref[...]


def _combine(h1, y_tok, cw, nw):
    T = h1.shape[0]
    tm = min(COMB_TM, T)
    nt = T // tm
    return pl.pallas_call(
        _combine_body,
        grid=(nt,),
        in_specs=[
            pl.BlockSpec((tm, D_MODEL), lambda i: (i, 0)),
            pl.BlockSpec((tm, Y_ROWS, LANES), lambda i: (i, 0, 0)),
            pl.BlockSpec((tm, Y_ROWS, LANES), lambda i: (i + nt, 0, 0)),
            pl.BlockSpec((tm, LANES), lambda i: (i, 0)),
            pl.BlockSpec((1, D_MODEL), lambda i: (0, 0)),
        ],
        out_specs=pl.BlockSpec((tm, D_MODEL), lambda i: (i, 0)),
        out_shape=jax.ShapeDtypeStruct((T, D_MODEL), F32),
        compiler_params=_cparams(("parallel",)),
        name="combine",
    )(h1, y_tok, y_tok, cw, nw)


def _tile_plan(cnt, n_tiles):
    tiles = (cnt + (MOE_TM - 1)) // MOE_TM
    ends = jnp.cumsum(tiles)
    starts = ends - tiles
    n_used = ends[-1]
    step = jnp.arange(n_tiles + 1, dtype=I32)
    tile = jnp.minimum(step, jnp.maximum(n_used - 1, 0))
    tile_expert = jnp.sum((ends[None, :] <= tile[:, None]).astype(I32), axis=1)
    tile_first = ((step == starts[tile_expert]) & (step < n_used)).astype(I32)
    nxt_tile = ends[tile_expert]
    nxt_expert = jnp.sum((ends[None, :] <= nxt_tile[:, None]).astype(I32), axis=1)
    next_expert = jnp.where(nxt_tile < n_used, nxt_expert, -1).astype(I32)
    pad_lo = starts * MOE_TM + cnt
    pad_hi = ends * MOE_TM
    return (tile_expert, tile_first, next_expert, n_used.reshape(1).astype(I32),
            pad_lo.astype(I32), pad_hi.astype(I32))


def kernel(x, positions, norm1_w, w_in, conv_w, conv_b, dt_bias, a_log, d_skip, ret_norm_w,
           ssm_norm_w, w_out, norm2_w, w_router_group, b_router_group, w_router_expert,
           b_router_expert, w_expert_gate, w_expert_up, w_expert_down, final_norm_w):
    B, T, D = x.shape
    assert B == 1 and D == D_MODEL and T % CHUNK == 0
    xf = x.reshape(T, D)
    pad_l = lambda v: jnp.pad(v, ((0, 0), (0, LANES - v.shape[-1])))

    w_in_t = jnp.swapaxes(w_in[0], 0, 1)
    w_dt = jnp.pad(w_in_t[MAIN_PROJ:], ((0, LANES - (w_in_t.shape[0] - MAIN_PROJ)), (0, 0))).astype(BF16)
    half = RET_HEAD_DIM // 2
    inv = (ROPE_THETA ** (-jnp.arange(half, dtype=F32) / half)).reshape(1, half)
    tri = (jnp.arange(CHUNK)[:, None] >= jnp.arange(CHUNK)[None, :]).astype(BF16)
    d_skip_e = jnp.repeat(d_skip[0], SSM_HEAD_DIM).reshape(1, SSM_WIDTH)
    wr_t = jnp.zeros((LANES, D), F32)
    wr_t = wr_t.at[0:N_GROUPS].set(w_router_group[0].T).at[8:8 + N_EXPERTS].set(w_router_expert[0].T)
    br = jnp.zeros((LANES,), F32)
    br = br.at[0:N_GROUPS].set(b_router_group[0]).at[8:8 + N_EXPERTS].set(b_router_expert[0])
    tri_u = (jnp.arange(SORT_BLK)[:, None] <= jnp.arange(SORT_BLK)[None, :]).astype(BF16)
    ltri = (jnp.arange(N_EXPERTS)[:, None] > jnp.arange(N_EXPERTS)[None, :]).astype(BF16)

    w_main, w_out_b, cos, sin = _prep(w_in_t, w_out[0], positions.reshape(T, 1).astype(F32), inv)
    y_ret, y_ssm = _mixer(xf, norm1_w[0].reshape(1, D), w_main, w_dt, cos, sin,
                          ret_norm_w[0].reshape(1, RET_WIDTH), _retention_consts(),
                          conv_w[0], conv_b[0].reshape(1, -1), pad_l(dt_bias[0].reshape(1, -1)),
                          pad_l(a_log[0].reshape(1, -1)), d_skip_e, ssm_norm_w[0].reshape(1, -1), tri)
    h1, h2t, cw, ids = _out_router(xf, y_ret, y_ssm, w_out_b, norm2_w[0].reshape(1, D),
                                   wr_t.astype(BF16), br.reshape(LANES, 1))

    dest_blk, cnt = _sort_index(ids, tri_u, ltri)
    dest = dest_blk[:, 0:2, :].transpose(1, 0, 2).reshape(2 * T)
    n_tiles = (2 * T) // MOE_TM + N_EXPERTS
    tile_expert, tile_first, next_expert, n_used, pad_lo, pad_hi = _tile_plan(cnt[:, 0].astype(I32), n_tiles)
    y_tok = _moe(tile_expert, tile_first, next_expert, n_used, dest, pad_lo, pad_hi, h2t,
                 w_expert_gate[0], w_expert_up[0], w_expert_down[0])
    out = _combine(h1, y_tok, cw, final_norm_w.reshape(1, D))
    return out.reshape(B, T, D)
```

```python
import functools

import jax
import jax.numpy as jnp
from jax import lax
from jax.experimental import pallas as pl
from jax.experimental.pallas import tpu as pltpu

F32 = jnp.float32
BF16 = jnp.bfloat16
I32 = jnp.int32

D_MODEL = 2048
EPS = 1e-6
CHUNK = 128
RET_HEADS = 4
RET_HEAD_DIM = 256
RET_WIDTH = RET_HEADS * RET_HEAD_DIM
ROPE_THETA = 10000.0
SSM_WIDTH = 1024
SSM_HEAD_DIM = 64
SSM_HEADS = SSM_WIDTH // SSM_HEAD_DIM
SSM_GROUPS = 2
SSM_STATE = 128
SSM_CONV = 4
SSM_CONV_DIM = SSM_WIDTH + 2 * SSM_GROUPS * SSM_STATE
MAIN_PROJ = 4 * RET_WIDTH + SSM_WIDTH + SSM_CONV_DIM
N_GROUPS = 4
GROUP_EXPERTS = 8
N_EXPERTS = N_GROUPS * GROUP_EXPERTS
D_EXPERT = 512
LANES = 128
Y_ROWS = D_MODEL // LANES

VMEM_LIMIT = 56 * 1024 * 1024


def _cparams(sem, vmem=VMEM_LIMIT):
    return pltpu.CompilerParams(dimension_semantics=sem, vmem_limit_bytes=vmem)


def _silu(x):
    return x * (1.0 / (1.0 + jnp.exp(-x)))


def _dot(a, b):
    return jnp.dot(a, b, preferred_element_type=F32)


def _dot_nt(a, b):
    return lax.dot_general(a, b, (((1,), (1,)), ((), ())), preferred_element_type=F32)


def _split3(a):
    a1 = a.astype(BF16)
    r1 = a - a1.astype(F32)
    a2 = r1.astype(BF16)
    a3 = (r1 - a2.astype(F32)).astype(BF16)
    return a1, a2, a3


CAST_ROWS = 512
ROPE_ROWS = 1024


def _prep_body(win_ref, wout_ref, pos_ref, inv_ref, winb_ref, woutb_ref, cos_ref, sin_ref, *, n_in, n_rope):
    j = pl.program_id(0)

    @pl.when(j < n_in)
    def _():
        winb_ref[...] = win_ref[...].astype(BF16)

    @pl.when(j >= n_in)
    def _():
        woutb_ref[...] = wout_ref[...].astype(BF16)

    @pl.when(j < n_rope)
    def _():
        ang = pos_ref[...] * inv_ref[...]
        cos_ref[...] = jnp.cos(ang)
        sin_ref[...] = jnp.sin(ang)


def _prep(w_in_t, w_out, pos, inv):
    T = pos.shape[0]
    cols = w_in_t.shape[1]
    half = inv.shape[1]
    n_in = MAIN_PROJ // CAST_ROWS
    n_out = w_out.shape[0] // CAST_ROWS
    rr = min(ROPE_ROWS, T)
    n_rope = T // rr
    assert n_rope <= n_in + n_out
    in_blk = lambda j: (jnp.minimum(j, n_in - 1), 0)
    out_blk = lambda j: (jnp.clip(j - n_in, 0, n_out - 1), 0)
    rope_blk = lambda j: (jnp.minimum(j, n_rope - 1), 0)
    return pl.pallas_call(
        functools.partial(_prep_body, n_in=n_in, n_rope=n_rope),
        grid=(n_in + n_out,),
        in_specs=[pl.BlockSpec((CAST_ROWS, cols), in_blk),
                  pl.BlockSpec((CAST_ROWS, w_out.shape[1]), out_blk),
                  pl.BlockSpec((rr, 1), rope_blk),
                  pl.BlockSpec((1, half), lambda j: (0, 0))],
        out_specs=[pl.BlockSpec((CAST_ROWS, cols), in_blk),
                   pl.BlockSpec((CAST_ROWS, w_out.shape[1]), out_blk),
                   pl.BlockSpec((rr, half), rope_blk),
                   pl.BlockSpec((rr, half), rope_blk)],
        out_shape=[jax.ShapeDtypeStruct((MAIN_PROJ, cols), BF16),
                   jax.ShapeDtypeStruct(w_out.shape, BF16),
                   jax.ShapeDtypeStruct((T, half), F32),
                   jax.ShapeDtypeStruct((T, half), F32)],
        compiler_params=_cparams(("arbitrary",)),
        name="prep",
    )(w_in_t, w_out, pos, inv)


def _retention_chunks(cd_ref, q_ref, k_ref, v_ref, g_ref, cos_ref, sin_ref,
                      dec_ref, wq_ref, ws_ref, nw_ref, o_ref, st_ref, cb, tick):
    hd = RET_HEAD_DIM
    half = hd // 2
    for i in range(cb):
        sl = slice(i * CHUNK, (i + 1) * CHUNK)
        cos = cos_ref[sl, :]
        sin = sin_ref[sl, :]

        def rope(x):
            x1, x2 = x[:, :half], x[:, half:]
            return jnp.concatenate([x1 * cos - x2 * sin, x2 * cos + x1 * sin], axis=-1)

        for h in range(RET_HEADS):
            hs = slice(h * hd, (h + 1) * hd)
            q = rope(q_ref[sl, hs])
            k = rope(k_ref[sl, hs]) * (hd ** -0.5)
            v = v_ref[sl, hs]
            qb = q.astype(BF16)
            kb = k.astype(BF16)
            scores = _dot_nt(qb, kb) * dec_ref[h]
            y = _dot(scores.astype(BF16), v.astype(BF16))
            st = st_ref[h]
            y = y + _dot(qb, st.astype(BF16)) * wq_ref[h]
            vw = (v * ws_ref[h]).astype(BF16)
            new = _dot(k.T.astype(BF16), vw)
            st_ref[h] = st * cd_ref[h] + new
            mu = jnp.mean(y, axis=-1, keepdims=True)
            d = y - mu
            var = jnp.mean(d * d, axis=-1, keepdims=True)
            yn = d * lax.rsqrt(var + EPS)
            o_ref[sl, hs] = (yn * nw_ref[:, hs] * _silu(g_ref[sl, hs])).astype(BF16)
            tick()


def _retention_consts():
    H, L = RET_HEADS, CHUNK
    log_gamma = jnp.log1p(-(2.0 ** (-5.0 - jnp.arange(H, dtype=F32))))
    idx = jnp.arange(L, dtype=F32)
    diff = idx[:, None] - idx[None, :]
    causal = diff >= 0
    decay_intra = jnp.where(causal[None], jnp.exp(jnp.where(causal, diff, 0.0)[None] * log_gamma[:, None, None]), 0.0)
    w_state = jnp.exp((L - 1.0 - idx)[None, :] * log_gamma[:, None])
    w_query = jnp.exp((idx + 1.0)[None, :] * log_gamma[:, None])
    chunk_decay = jnp.exp(L * log_gamma)
    bc = lambda w: jnp.broadcast_to(w[:, :, None], (H, L, RET_HEAD_DIM))
    return chunk_decay, decay_intra, bc(w_query), bc(w_state)


CONV_PAD = 8
XBC_BLK = 512


def _ssd_chunks(xbc_ref, z_ref, dt_ref, cw_ref, cbias_ref, dtb_ref,
                alog_ref, dskip_ref, nw_ref, tri_ref, o_ref, xpad_ref, st_ref, cb, tick):
    L = CHUNK
    gw = SSM_WIDTH // SSM_GROUPS
    rows_i = lax.broadcasted_iota(I32, (L, L), 0)
    cols_i = lax.broadcasted_iota(I32, (L, L), 1)
    causal = rows_i >= cols_i
    lo_lane = lax.broadcasted_iota(I32, (L, LANES), 1) < SSM_HEAD_DIM
    tri = tri_ref[...]
    a_neg = -jnp.exp(alog_ref[...])

    for i in range(cb):
        sl = slice(i * L, (i + 1) * L)
        xpad_ref[CONV_PAD:CONV_PAD + L, :] = xbc_ref[sl, :]
        u_parts = []
        for b in range(SSM_CONV_DIM // XBC_BLK):
            cs = slice(b * XBC_BLK, (b + 1) * XBC_BLK)
            acc = cbias_ref[:, cs]
            for t in range(SSM_CONV):
                r0 = CONV_PAD - (SSM_CONV - 1) + t
                acc = acc + xpad_ref[r0:r0 + L, cs] * cw_ref[t:t + 1, cs]
            u_parts.append(_silu(acc))
            tick()
        xpad_ref[0:CONV_PAD, :] = xpad_ref[L:L + CONV_PAD, :]
        xs = jnp.concatenate(u_parts[:2], axis=-1)
        bm = u_parts[2][:, :SSM_GROUPS * SSM_STATE]
        cm = u_parts[2][:, SSM_GROUPS * SSM_STATE:]

        dt_in = dt_ref[sl, :] + dtb_ref[...]
        dt = jnp.maximum(dt_in, 0.0) + jnp.log1p(jnp.exp(-jnp.abs(dt_in)))
        a = dt * a_neg
        a1, a2, a3 = _split3(a)
        a_cs = _dot(tri, a1) + _dot(tri, a2) + _dot(tri, a3)
        last = a_cs[L - 1:L, :]

        def per_head_lanes(v):
            lo = lo_lane[0:v.shape[0], :]
            return jnp.concatenate([jnp.where(lo, v[:, 2 * p:2 * p + 1], v[:, 2 * p + 1:2 * p + 2])
                                    for p in range(SSM_HEADS // 2)], axis=1)

        dt_e = per_head_lanes(dt)
        to_end = per_head_lanes(jnp.exp(last - a_cs))
        from_start = per_head_lanes(jnp.exp(a_cs))
        chunk_decay = per_head_lanes(jnp.exp(last))
        tick()
        x_dt = xs * dt_e
        xw = (x_dt * to_end).astype(BF16)
        acs_t = a_cs.T
        cmb = cm.astype(BF16)
        bmb = bm.astype(BF16)

        ys = []
        for g in range(SSM_GROUPS):
            ns = slice(g * SSM_STATE, (g + 1) * SSM_STATE)
            gs = slice(g * gw, (g + 1) * gw)
            cg = cmb[:, ns]
            cbg = _dot_nt(cg, bmb[:, ns])
            st = st_ref[g]
            y_off = _dot(cg, st.astype(BF16))
            new = _dot(bm[:, ns].T.astype(BF16), xw[:, gs])
            st_ref[g] = st * chunk_decay[:, gs] + new
            tick()
            for jp in range(gw // LANES):
                h0 = (g * gw + jp * LANES) // SSM_HEAD_DIM
                ms = []
                for hh in (h0, h0 + 1):
                    seg = a_cs[:, hh:hh + 1] - acs_t[hh:hh + 1, :]
                    dec = jnp.exp(jnp.where(causal, seg, -jnp.inf))
                    ms.append((cbg * dec).astype(BF16))
                lhs = jnp.concatenate(ms, axis=1)
                ls = slice(g * gw + jp * LANES, g * gw + (jp + 1) * LANES)
                xp = x_dt[:, ls]
                rhs = jnp.concatenate([jnp.where(lo_lane, xp, 0.0),
                                       jnp.where(lo_lane, 0.0, xp)], axis=0).astype(BF16)
                y_diag = _dot(lhs, rhs)
                ys.append(y_diag + y_off[:, jp * LANES:(jp + 1) * LANES] * from_start[:, ls])
                tick()
        y = jnp.concatenate(ys, axis=1) + xs * dskip_ref[...]
        y = y * _silu(z_ref[sl, :])
        outs = []
        for g in range(SSM_GROUPS):
            yg = y[:, g * gw:(g + 1) * gw]
            ms_ = jnp.mean(yg * yg, axis=-1, keepdims=True)
            outs.append(yg * lax.rsqrt(ms_ + EPS))
        o_ref[sl, :] = (jnp.concatenate(outs, axis=1) * nw_ref[...]).astype(BF16)


MIX_CB = 2
MIX_VMEM = 60 * 1024 * 1024
COL_Z = 4 * RET_WIDTH
COL_XBC = COL_Z + SSM_WIDTH
PROJ_WIDTHS = (RET_WIDTH, RET_WIDTH, RET_WIDTH, RET_WIDTH, SSM_WIDTH, SSM_CONV_DIM, LANES)
PROJ_SLAB = 256


def _mixer_body(cd_ref, x_ref, n1w_ref, wt_hbm, wdt_ref, cos_ref, sin_ref, dec_ref, wq_ref, ws_ref, rnw_ref,
                cw_ref, cbias_ref, dtb_ref, alog_ref, dskip_ref, snw_ref, tri_ref,
                yret_ref, yssm_ref, w_ref, hn_ref, pq, pk, pv, pg, pz, pxbc, pdt,
                st_ret, xpad_ref, st_ssd, sem, *, cb):
    @pl.when(pl.program_id(0) == 0)
    def _():
        cp = pltpu.make_async_copy(wt_hbm, w_ref, sem)
        cp.start()
        st_ret[...] = jnp.zeros_like(st_ret)
        xpad_ref[0:CONV_PAD, :] = jnp.zeros((CONV_PAD, SSM_CONV_DIM), F32)
        st_ssd[...] = jnp.zeros_like(st_ssd)
        cp.wait()

    def slabs(ref, col0):
        def slab(lo):
            def go():
                ref[:, lo:lo + PROJ_SLAB] = _dot_nt(hn_ref[...], w_ref[col0 + lo:col0 + lo + PROJ_SLAB, :])
            return go
        return [slab(lo) for lo in range(0, ref.shape[1], PROJ_SLAB)]

    x = x_ref[...]
    ms = jnp.mean(x * x, axis=-1, keepdims=True)
    hn_ref[...] = (x * lax.rsqrt(ms + EPS) * n1w_ref[...]).astype(BF16)
    for piece in slabs(pxbc, COL_XBC) + slabs(pz, COL_Z):
        piece()
    pdt[...] = _dot_nt(hn_ref[...], wdt_ref[...])

    pieces = slabs(pq, 0) + slabs(pk, RET_WIDTH) + slabs(pv, 2 * RET_WIDTH) + slabs(pg, 3 * RET_WIDTH)
    n_pieces = len(pieces)
    n_ticks = cb * (SSM_CONV_DIM // XBC_BLK + 1 + SSM_GROUPS + SSM_WIDTH // LANES)
    calls = [0]

    def tick():
        calls[0] += 1
        while pieces and (n_pieces - len(pieces)) * n_ticks < calls[0] * n_pieces:
            pieces.pop(0)()

    _ssd_chunks(pxbc, pz, pdt, cw_ref, cbias_ref, dtb_ref, alog_ref, dskip_ref, snw_ref, tri_ref,
                yssm_ref, xpad_ref, st_ssd, cb, tick)
    while pieces:
        pieces.pop(0)()
    _retention_chunks(cd_ref, pq, pk, pv, pg, cos_ref, sin_ref, dec_ref, wq_ref, ws_ref, rnw_ref,
                      yret_ref, st_ret, cb, lambda: None)


def _mixer(x, n1w, w_main_t, w_dt_t, cos, sin, ret_nw, ret_consts,
           conv_w, conv_b, dt_bias, a_log, d_skip_e, ssm_nw, tri):
    T = x.shape[0]
    cb = min(MIX_CB, T // CHUNK)
    rows = cb * CHUNK
    chunk_decay, decay_intra, w_query, w_state = ret_consts
    hd = RET_HEAD_DIM
    half = hd // 2
    full = lambda shape: pl.BlockSpec(shape, lambda c, cd: (0,) * len(shape))
    rowblk = lambda width: pl.BlockSpec((rows, width), lambda c, cd: (c, 0))
    scratch = lambda width: pltpu.VMEM((rows, width), F32)
    grid_spec = pltpu.PrefetchScalarGridSpec(
        num_scalar_prefetch=1,
        grid=(T // rows,),
        in_specs=[
            rowblk(D_MODEL),
            full((1, D_MODEL)),
            pl.BlockSpec(memory_space=pl.ANY),
            full((LANES, D_MODEL)),
            rowblk(half), rowblk(half),
            full((RET_HEADS, CHUNK, CHUNK)),
            full((RET_HEADS, CHUNK, hd)),
            full((RET_HEADS, CHUNK, hd)),
            full((1, RET_WIDTH)),
            full((SSM_CONV, SSM_CONV_DIM)),
            full((1, SSM_CONV_DIM)),
            full((1, LANES)),
            full((1, LANES)),
            full((1, SSM_WIDTH)),
            full((1, SSM_WIDTH)),
            full((CHUNK, CHUNK)),
        ],
        out_specs=[rowblk(RET_WIDTH), rowblk(SSM_WIDTH)],
        scratch_shapes=[
            pltpu.VMEM((MAIN_PROJ, D_MODEL), BF16),
            pltpu.VMEM((rows, D_MODEL), BF16),
            *[scratch(w) for w in PROJ_WIDTHS],
            pltpu.VMEM((RET_HEADS, hd, hd), F32),
            pltpu.VMEM((CHUNK + CONV_PAD, SSM_CONV_DIM), F32),
            pltpu.VMEM((SSM_GROUPS, SSM_STATE, SSM_WIDTH // SSM_GROUPS), F32),
            pltpu.SemaphoreType.DMA(()),
        ],
    )
    return pl.pallas_call(
        functools.partial(_mixer_body, cb=cb),
        grid_spec=grid_spec,
        out_shape=[jax.ShapeDtypeStruct((T, RET_WIDTH), BF16),
                   jax.ShapeDtypeStruct((T, SSM_WIDTH), BF16)],
        compiler_params=_cparams(("arbitrary",), MIX_VMEM),
        name="mixer",
    )(chunk_decay, x, n1w, w_main_t, w_dt_t, cos, sin, decay_intra, w_query, w_state, ret_nw,
      conv_w, conv_b, dt_bias, a_log, d_skip_e, ssm_nw, tri)


OUT_TM = 512
ROUTE_ROWS = 8 + N_EXPERTS


def _out_router_body(x_ref, yr_ref, ys_ref, wo_ref, nw_ref, wr_ref, br_ref,
                     h1_ref, h2t_ref, cw_ref, ids_ref):
    tm = x_ref.shape[0]
    h1 = x_ref[...] + _dot(yr_ref[...], wo_ref[0:RET_WIDTH, :]) + _dot(ys_ref[...], wo_ref[RET_WIDTH:, :])
    h1_ref[...] = h1
    ms = jnp.mean(h1 * h1, axis=-1, keepdims=True)
    h2 = h1 * lax.rsqrt(ms + EPS) * nw_ref[...]
    for s in range(Y_ROWS):
        h2t_ref[:, s, :] = h2[:, s * LANES:(s + 1) * LANES]

    logits = _dot_nt(wr_ref[...], h2.astype(BF16)) + br_ref[...]
    row = lax.broadcasted_iota(I32, (8, tm), 0)
    lg = jnp.where(row < N_GROUPS, logits[0:8], -jnp.inf)
    m = jnp.max(lg, axis=0, keepdims=True)
    p_sel = 1.0 / jnp.sum(jnp.exp(lg - m), axis=0, keepdims=True)
    g_sel = jnp.min(jnp.where(lg == m, row, 8), axis=0, keepdims=True)
    le = jnp.zeros((GROUP_EXPERTS, tm), F32)
    for g in range(N_GROUPS):
        le = jnp.where(g_sel == g, logits[8 + g * GROUP_EXPERTS:8 + (g + 1) * GROUP_EXPERTS], le)
    m2 = jnp.max(le, axis=0, keepdims=True)
    ee = jnp.exp(le - m2)
    pe = ee / jnp.sum(ee, axis=0, keepdims=True)
    v1 = jnp.max(pe, axis=0, keepdims=True)
    i1 = jnp.min(jnp.where(pe == v1, row, 8), axis=0, keepdims=True)
    pe2 = jnp.where(row == i1, -1.0, pe)
    v2 = jnp.max(pe2, axis=0, keepdims=True)
    i2 = jnp.min(jnp.where(pe2 == v2, row, 8), axis=0, keepdims=True)
    tw = v1 + v2
    c1 = v1 / tw * p_sel
    c2 = v2 / tw * p_sel
    e1 = g_sel * GROUP_EXPERTS + i1
    e2 = g_sel * GROUP_EXPERTS + i2
    ids = jnp.where(row == 0, e1, jnp.where(row == 1, e2, 0))
    for b in range(tm // SORT_BLK):
        ids_ref[b] = ids[:, b * SORT_BLK:(b + 1) * SORT_BLK]
    cw8 = jnp.where(row == 0, c1, jnp.where(row == 1, c2, 0.0))
    cw = jnp.concatenate([cw8, jnp.zeros((LANES - 8, tm), F32)], axis=0)
    cw_ref[...] = cw.T


def _out_router(x, y_ret, y_ssm, w_out, nw, wr_t, br):
    T = x.shape[0]
    tm = min(OUT_TM, T)
    full = lambda shape: pl.BlockSpec(shape, lambda i: (0,) * len(shape))
    return pl.pallas_call(
        _out_router_body,
        grid=(T // tm,),
        in_specs=[
            pl.BlockSpec((tm, D_MODEL), lambda i: (i, 0)),
            pl.BlockSpec((tm, RET_WIDTH), lambda i: (i, 0)),
            pl.BlockSpec((tm, SSM_WIDTH), lambda i: (i, 0)),
            pl.BlockSpec((D_MODEL, D_MODEL), lambda i: (0, 0), pipeline_mode=pl.Buffered(1)),
            full((1, D_MODEL)),
            full((LANES, D_MODEL)),
            full((LANES, 1)),
        ],
        out_specs=[
            pl.BlockSpec((tm, D_MODEL), lambda i: (i, 0)),
            pl.BlockSpec((tm, Y_ROWS, LANES), lambda i: (i, 0, 0)),
            pl.BlockSpec((tm, LANES), lambda i: (i, 0)),
            pl.BlockSpec((tm // SORT_BLK, 8, SORT_BLK), lambda i: (i, 0, 0)),
        ],
        out_shape=[
            jax.ShapeDtypeStruct((T, D_MODEL), F32),
            jax.ShapeDtypeStruct((T, Y_ROWS, LANES), F32),
            jax.ShapeDtypeStruct((T, LANES), F32),
            jax.ShapeDtypeStruct((T // SORT_BLK, 8, SORT_BLK), I32),
        ],
        compiler_params=_cparams(("parallel",)),
        name="out_router",
    )(x, y_ret, y_ssm, w_out, nw, wr_t, br)


SORT_BLK = 256
MOE_TM = 256


def _sort_index_body(ids_ref, tri_ref, ltri_ref, dest_ref, cnt_ref, rank_ref):
    nblk = ids_ref.shape[0]
    row_e = lax.broadcasted_iota(I32, (N_EXPERTS, SORT_BLK), 0)
    row8 = lax.broadcasted_iota(I32, (8, SORT_BLK), 0)

    def onehots(b):
        ids = ids_ref[b]
        return row_e == ids[0:1], row_e == ids[1:2]

    def rank_blk(b, carry):
        oh1, oh2 = onehots(b)
        ohf = jnp.where(oh1 | oh2, 1.0, 0.0)
        incl = _dot(ohf.astype(BF16), tri_ref[...])
        base = carry + incl - 1.0
        r1 = jnp.sum(jnp.where(oh1, base, 0.0), axis=0, keepdims=True)
        r2 = jnp.sum(jnp.where(oh2, base, 0.0), axis=0, keepdims=True)
        rank_ref[b] = jnp.where(row8 == 0, r1, jnp.where(row8 == 1, r2, 0.0))
        return carry + jnp.sum(ohf, axis=1, keepdims=True)

    cnt = lax.fori_loop(0, nblk, rank_blk, jnp.zeros((N_EXPERTS, 1), F32))
    cnt_ref[...] = jnp.broadcast_to(cnt, cnt_ref.shape)
    tiles = jnp.floor((cnt + (MOE_TM - 1.0)) / MOE_TM)
    tiles_b = jnp.broadcast_to(tiles, (N_EXPERTS, LANES)).astype(BF16)
    off = _dot(ltri_ref[...], tiles_b)[:, 0:1] * MOE_TM

    def dest_blk(b, carry):
        oh1, oh2 = onehots(b)
        o1 = jnp.sum(jnp.where(oh1, off, 0.0), axis=0, keepdims=True)
        o2 = jnp.sum(jnp.where(oh2, off, 0.0), axis=0, keepdims=True)
        d = rank_ref[b] + jnp.where(row8 == 0, o1, jnp.where(row8 == 1, o2, 0.0))
        dest_ref[b] = d.astype(I32)
        return carry

    lax.fori_loop(0, nblk, dest_blk, 0)


def _sort_index(ids, tri_u, ltri):
    nblk = ids.shape[0]
    return pl.pallas_call(
        _sort_index_body,
        out_shape=[jax.ShapeDtypeStruct((nblk, 8, SORT_BLK), I32),
                   jax.ShapeDtypeStruct((N_EXPERTS, LANES), F32)],
        scratch_shapes=[pltpu.VMEM((nblk, 8, SORT_BLK), F32)],
        compiler_params=_cparams(None),
        name="sort_index",
    )(ids, tri_u, ltri)


ROW_DMA_PRIORITY = 1
WEIGHT_AHEAD = 2
GATHER_AHEAD = 2


def _moe_body(te_ref, tf_ref, nx_ref, fe_ref, nu_ref, dest_ref, lo_ref, hi_ref,
              h2t_ref, wg_ref, wu_ref, wd_ref, yt_ref,
              srow_ref, wcount, xbuf, ystage, wsg, wsu, wsd, wgb_ref, wub_ref, wdb_ref, gsem, ssem, wsem):
    i = pl.program_id(0)
    nu = nu_ref[0]
    T = h2t_ref.shape[0]

    def gather_row(tile, slot, r):
        tok = srow_ref[tile * MOE_TM + r] & (T - 1)
        return pltpu.make_async_copy(h2t_ref.at[tok], xbuf.at[slot, :, r, :], gsem.at[slot])

    def scatter_row(tile, slot, r):
        return pltpu.make_async_copy(ystage.at[slot, :, r, :], yt_ref.at[srow_ref[tile * MOE_TM + r]],
                                     ssem.at[slot])

    def gather_tile(slot):
        return pltpu.make_async_copy(h2t_ref.at[pl.ds(0, MOE_TM)], h2t_ref.at[pl.ds(0, MOE_TM)], gsem.at[slot])

    def scatter_tile(slot):
        return pltpu.make_async_copy(yt_ref.at[pl.ds(0, MOE_TM)], yt_ref.at[pl.ds(0, MOE_TM)], ssem.at[slot])

    def weight_copies(e, ws):
        return (pltpu.make_async_copy(wg_ref.at[e], wsg.at[ws], wsem.at[ws, 0]),
                pltpu.make_async_copy(wu_ref.at[e], wsu.at[ws], wsem.at[ws, 1]),
                pltpu.make_async_copy(wd_ref.at[e], wsd.at[ws], wsem.at[ws, 2]))

    @pl.when(i == 0)
    def _():
        wcount[0] = 0
        for d in range(WEIGHT_AHEAD):
            @pl.when(fe_ref[d] >= 0)
            def _():
                for c in weight_copies(fe_ref[d], d):
                    c.start()

        def tok(t, c):
            srow_ref[dest_ref[t]] = t
            srow_ref[dest_ref[T + t]] = T + t
            return c

        lax.fori_loop(0, T, tok, 0, unroll=8)

        def seg(e, c):
            def pad(r, c2):
                srow_ref[r] = 2 * T + (r & (MOE_TM - 1))
                return c2

            lax.fori_loop(lo_ref[e], hi_ref[e], pad, 0)
            return c

        lax.fori_loop(0, lo_ref.shape[0], seg, 0)

        ystage[...] = jnp.zeros_like(ystage)

        def spare(r, c):
            pltpu.make_async_copy(ystage.at[0, :, r, :], yt_ref.at[2 * T + r], ssem.at[0]).start()
            return c

        lax.fori_loop(0, MOE_TM, spare, 0)
        scatter_tile(0).wait()
        for d in range(GATHER_AHEAD):
            first = jnp.minimum(d, nu - 1)
            for r in range(MOE_TM):
                gather_row(first, d, r).start(priority=ROW_DMA_PRIORITY)

    @pl.when(tf_ref[i] == 1)
    def _():
        ws = lax.rem(wcount[0], WEIGHT_AHEAD + 1)
        ahead_slot = lax.rem(wcount[0] + WEIGHT_AHEAD, WEIGHT_AHEAD + 1)
        wcount[0] = wcount[0] + 1
        for c in weight_copies(te_ref[i], ws):
            c.wait()

        @pl.when(nx_ref[i] >= 0)
        def _():
            for c in weight_copies(nx_ref[i], ahead_slot):
                c.start()

        wgb_ref[...] = wsg[ws].astype(BF16)
        wub_ref[...] = wsu[ws].astype(BF16)
        wdb_ref[...] = wsd[ws].astype(BF16)

    slot = lax.rem(i, 2)
    other = 1 - slot
    n_xbuf = GATHER_AHEAD + 1
    xs = lax.rem(i, n_xbuf)

    @pl.when(i < nu)
    def _():
        gather_tile(xs).wait()

        @pl.when(i >= 1)
        def _():
            scatter_tile(slot).wait()

        nxt = jnp.minimum(i + GATHER_AHEAD, nu - 1)
        nxs = lax.rem(i + GATHER_AHEAD, n_xbuf)
        prv = jnp.maximum(i - 1, 0)
        for r in range(MOE_TM):
            gather_row(nxt, nxs, r).start(priority=r % 2)
            scatter_row(prv, other, r).start(priority=(r + 1) % 2)

        x = jnp.concatenate([xbuf[xs, s].astype(BF16) for s in range(Y_ROWS)], axis=1)
        a = _dot(x, wgb_ref[...])
        u = _dot(x, wub_ref[...])
        act = (_silu(a) * u).astype(BF16)
        y = _dot(act, wdb_ref[...])
        for s in range(Y_ROWS):
            ystage[slot, s] = y[:, s * LANES:(s + 1) * LANES]

    @pl.when(i == nu)
    def _():
        for d in range(GATHER_AHEAD):
            gather_tile(lax.rem(i + d, n_xbuf)).wait()
        scatter_tile(slot).wait()

        def last(r, c):
            scatter_row(nu - 1, other, r).start()
            return c

        lax.fori_loop(0, MOE_TM, last, 0)
        scatter_tile(other).wait()


def _moe(tile_expert, tile_first, ahead_expert, first_experts, n_used, dest, pad_lo, pad_hi,
         h2t, w_gate, w_up, w_down):
    T = h2t.shape[0]
    assert T & (T - 1) == 0 and T >= MOE_TM, "token index is recovered from the row table by masking"
    n_tiles = tile_expert.shape[0] - 1
    hbm = pl.BlockSpec(memory_space=pl.ANY)
    n_ws = WEIGHT_AHEAD + 1
    grid_spec = pltpu.PrefetchScalarGridSpec(
        num_scalar_prefetch=8,
        grid=(n_tiles + 1,),
        in_specs=[hbm, hbm, hbm, hbm],
        out_specs=hbm,
        scratch_shapes=[pltpu.SMEM((n_tiles * MOE_TM,), I32),
                        pltpu.SMEM((1,), I32),
                        pltpu.VMEM((GATHER_AHEAD + 1, Y_ROWS, MOE_TM, LANES), F32),
                        pltpu.VMEM((2, Y_ROWS, MOE_TM, LANES), F32),
                        pltpu.VMEM((n_ws, D_MODEL, D_EXPERT), F32),
                        pltpu.VMEM((n_ws, D_MODEL, D_EXPERT), F32),
                        pltpu.VMEM((n_ws, D_EXPERT, D_MODEL), F32),
                        pltpu.VMEM((D_MODEL, D_EXPERT), BF16),
                        pltpu.VMEM((D_MODEL, D_EXPERT), BF16),
                        pltpu.VMEM((D_EXPERT, D_MODEL), BF16),
                        pltpu.SemaphoreType.DMA((GATHER_AHEAD + 1,)),
                        pltpu.SemaphoreType.DMA((2,)),
                        pltpu.SemaphoreType.DMA((n_ws, 3))],
    )
    return pl.pallas_call(
        _moe_body,
        grid_spec=grid_spec,
        out_shape=jax.ShapeDtypeStruct((2 * T + MOE_TM, Y_ROWS, LANES), F32),
        compiler_params=_cparams(("arbitrary",), MIX_VMEM),
        name="moe",
    )(tile_expert, tile_first, ahead_expert, first_experts, n_used, dest, pad_lo, pad_hi,
      h2t, w_gate, w_up, w_down)


COMB_TM = 512


def _combine_body(h1_ref, y0_ref, y1_ref, cw_ref, nw_ref, o_ref):
    tm = h1_ref.shape[0]
    c0 = cw_ref[:, 0:1]
    c1 = cw_ref[:, 1:2]
    cols = []
    for s in range(Y_ROWS):
        moe = c0 * y0_ref[:, s, :] + c1 * y1_ref[:, s, :]
        cols.append(h1_ref[:, s * LANES:(s + 1) * LANES] + moe)
    h = jnp.concatenate(cols, axis=1)
    ms = jnp.mean(h * h, axis=-1, keepdims=True)
    o_ref[...] = h * lax.rsqrt(ms + EPS) * nw_ref[...]


def _combine(h1, y_tok, cw, nw):
    T = h1.shape[0]
    tm = min(COMB_TM, T)
    nt = T // tm
    return pl.pallas_call(
        _combine_body,
        grid=(nt,),
        in_specs=[
            pl.BlockSpec((tm, D_MODEL), lambda i: (i, 0)),
            pl.BlockSpec((tm, Y_ROWS, LANES), lambda i: (i, 0, 0)),
            pl.BlockSpec((tm, Y_ROWS, LANES), lambda i: (i + nt, 0, 0)),
            pl.BlockSpec((tm, LANES), lambda i: (i, 0)),
            pl.BlockSpec((1, D_MODEL), lambda i: (0, 0)),
        ],
        out_specs=pl.BlockSpec((tm, D_MODEL), lambda i: (i, 0)),
        out_shape=jax.ShapeDtypeStruct((T, D_MODEL), F32),
        compiler_params=_cparams(("parallel",)),
        name="combine",
    )(h1, y_tok, y_tok, cw, nw)


def _tile_plan(cnt, n_tiles):
    tiles = (cnt + (MOE_TM - 1)) // MOE_TM
    ends = jnp.cumsum(tiles)
    starts = ends - tiles
    n_used = ends[-1]
    step = jnp.arange(n_tiles + 1, dtype=I32)
    tile = jnp.minimum(step, jnp.maximum(n_used - 1, 0))
    tile_expert = jnp.sum((ends[None, :] <= tile[:, None]).astype(I32), axis=1)
    tile_first = ((step == starts[tile_expert]) & (step < n_used)).astype(I32)
    used = tiles > 0
    ordinal = jnp.cumsum(used.astype(I32)) - 1
    n_experts_used = jnp.sum(used.astype(I32))
    rank = jnp.arange(N_EXPERTS, dtype=I32)
    by_ordinal = jnp.sum(jnp.where(used[None, :] & (ordinal[None, :] == rank[:, None]), rank[None, :], 0), axis=1)
    ahead = ordinal[tile_expert] + WEIGHT_AHEAD
    ahead_expert = jnp.where(ahead < n_experts_used, by_ordinal[jnp.minimum(ahead, N_EXPERTS - 1)], -1)
    first_experts = jnp.where(rank[:WEIGHT_AHEAD] < n_experts_used, by_ordinal[:WEIGHT_AHEAD], -1)
    pad_lo = starts * MOE_TM + cnt
    pad_hi = ends * MOE_TM
    return (tile_expert, tile_first, ahead_expert.astype(I32), first_experts.astype(I32),
            n_used.reshape(1).astype(I32), pad_lo.astype(I32), pad_hi.astype(I32))


def kernel(x, positions, norm1_w, w_in, conv_w, conv_b, dt_bias, a_log, d_skip, ret_norm_w,
           ssm_norm_w, w_out, norm2_w, w_router_group, b_router_group, w_router_expert,
           b_router_expert, w_expert_gate, w_expert_up, w_expert_down, final_norm_w):
    B, T, D = x.shape
    assert B == 1 and D == D_MODEL and T % CHUNK == 0
    xf = x.reshape(T, D)
    pad_l = lambda v: jnp.pad(v, ((0, 0), (0, LANES - v.shape[-1])))

    w_in_t = jnp.swapaxes(w_in[0], 0, 1)
    w_dt = jnp.pad(w_in_t[MAIN_PROJ:], ((0, LANES - (w_in_t.shape[0] - MAIN_PROJ)), (0, 0))).astype(BF16)
    half = RET_HEAD_DIM // 2
    inv = (ROPE_THETA ** (-jnp.arange(half, dtype=F32) / half)).reshape(1, half)
    tri = (jnp.arange(CHUNK)[:, None] >= jnp.arange(CHUNK)[None, :]).astype(BF16)
    d_skip_e = jnp.repeat(d_skip[0], SSM_HEAD_DIM).reshape(1, SSM_WIDTH)
    wr_t = jnp.zeros((LANES, D), F32)
    wr_t = wr_t.at[0:N_GROUPS].set(w_router_group[0].T).at[8:8 + N_EXPERTS].set(w_router_expert[0].T)
    br = jnp.zeros((LANES,), F32)
    br = br.at[0:N_GROUPS].set(b_router_group[0]).at[8:8 + N_EXPERTS].set(b_router_expert[0])
    tri_u = (jnp.arange(SORT_BLK)[:, None] <= jnp.arange(SORT_BLK)[None, :]).astype(BF16)
    ltri = (jnp.arange(N_EXPERTS)[:, None] > jnp.arange(N_EXPERTS)[None, :]).astype(BF16)

    w_main, w_out_b, cos, sin = _prep(w_in_t, w_out[0], positions.reshape(T, 1).astype(F32), inv)
    y_ret, y_ssm = _mixer(xf, norm1_w[0].reshape(1, D), w_main, w_dt, cos, sin,
                          ret_norm_w[0].reshape(1, RET_WIDTH), _retention_consts(),
                          conv_w[0], conv_b[0].reshape(1, -1), pad_l(dt_bias[0].reshape(1, -1)),
                          pad_l(a_log[0].reshape(1, -1)), d_skip_e, ssm_norm_w[0].reshape(1, -1), tri)
    h1, h2t, cw, ids = _out_router(xf, y_ret, y_ssm, w_out_b, norm2_w[0].reshape(1, D),
                                   wr_t.astype(BF16), br.reshape(LANES, 1))

    dest_blk, cnt = _sort_index(ids, tri_u, ltri)
    dest = dest_blk[:, 0:2, :].transpose(1, 0, 2).reshape(2 * T)
    n_tiles = (2 * T) // MOE_TM + N_EXPERTS
    (tile_expert, tile_first, ahead_expert, first_experts, n_used,
     pad_lo, pad_hi) = _tile_plan(cnt[:, 0].astype(I32), n_tiles)
    y_tok = _moe(tile_expert, tile_first, ahead_expert, first_experts, n_used, dest, pad_lo, pad_hi, h2t,
                 w_expert_gate[0], w_expert_up[0], w_expert_down[0])
    out = _combine(h1, y_tok, cw, final_norm_w.reshape(1, D))
    return out.reshape(B, T, D)
```

```python
import functools

import jax
import jax.numpy as jnp
import numpy as np
from jax import lax
from jax.experimental import pallas as pl
from jax.experimental.pallas import tpu as pltpu

F32 = jnp.float32
BF16 = jnp.bfloat16
I32 = jnp.int32

D_MODEL = 2048
EPS = 1e-6
CHUNK = 128
RET_HEADS = 4
RET_HEAD_DIM = 256
RET_WIDTH = RET_HEADS * RET_HEAD_DIM
ROPE_THETA = 10000.0
SSM_WIDTH = 1024
SSM_HEAD_DIM = 64
SSM_HEADS = SSM_WIDTH // SSM_HEAD_DIM
SSM_GROUPS = 2
SSM_STATE = 128
SSM_CONV = 4
SSM_CONV_DIM = SSM_WIDTH + 2 * SSM_GROUPS * SSM_STATE
MAIN_PROJ = 4 * RET_WIDTH + SSM_WIDTH + SSM_CONV_DIM
N_GROUPS = 4
GROUP_EXPERTS = 8
N_EXPERTS = N_GROUPS * GROUP_EXPERTS
D_EXPERT = 512
LANES = 128
Y_ROWS = D_MODEL // LANES

VMEM_LIMIT = 56 * 1024 * 1024


def _cparams(sem, vmem=VMEM_LIMIT):
    return pltpu.CompilerParams(dimension_semantics=sem, vmem_limit_bytes=vmem)


def _silu(x):
    return x * (1.0 / (1.0 + jnp.exp(-x)))


def _dot(a, b):
    return jnp.dot(a, b, preferred_element_type=F32)


def _dot_nt(a, b):
    return lax.dot_general(a, b, (((1,), (1,)), ((), ())), preferred_element_type=F32)


def _split3(a):
    a1 = a.astype(BF16)
    r1 = a - a1.astype(F32)
    a2 = r1.astype(BF16)
    a3 = (r1 - a2.astype(F32)).astype(BF16)
    return a1, a2, a3


CAST_ROWS = 512
ROPE_ROWS = 1024


def _prep_body(win_ref, wout_ref, pos_ref, inv_ref, winb_ref, woutb_ref, cos_ref, sin_ref, *, n_in, n_rope):
    j = pl.program_id(0)

    @pl.when(j < n_in)
    def _():
        winb_ref[...] = win_ref[...].astype(BF16)

    @pl.when(j >= n_in)
    def _():
        woutb_ref[...] = wout_ref[...].astype(BF16)

    @pl.when(j < n_rope)
    def _():
        ang = pos_ref[...] * inv_ref[...]
        cos_ref[...] = jnp.cos(ang)
        sin_ref[...] = jnp.sin(ang)


def _prep(w_in_t, w_out, pos, inv):
    T = pos.shape[0]
    cols = w_in_t.shape[1]
    half = inv.shape[1]
    n_in = MAIN_PROJ // CAST_ROWS
    n_out = w_out.shape[0] // CAST_ROWS
    rr = min(ROPE_ROWS, T)
    n_rope = T // rr
    assert n_rope <= n_in + n_out
    in_blk = lambda j: (jnp.minimum(j, n_in - 1), 0)
    out_blk = lambda j: (jnp.clip(j - n_in, 0, n_out - 1), 0)
    rope_blk = lambda j: (jnp.minimum(j, n_rope - 1), 0)
    return pl.pallas_call(
        functools.partial(_prep_body, n_in=n_in, n_rope=n_rope),
        grid=(n_in + n_out,),
        in_specs=[pl.BlockSpec((CAST_ROWS, cols), in_blk),
                  pl.BlockSpec((CAST_ROWS, w_out.shape[1]), out_blk),
                  pl.BlockSpec((rr, 1), rope_blk),
                  pl.BlockSpec((1, half), lambda j: (0, 0))],
        out_specs=[pl.BlockSpec((CAST_ROWS, cols), in_blk),
                   pl.BlockSpec((CAST_ROWS, w_out.shape[1]), out_blk),
                   pl.BlockSpec((rr, half), rope_blk),
                   pl.BlockSpec((rr, half), rope_blk)],
        out_shape=[jax.ShapeDtypeStruct((MAIN_PROJ, cols), BF16),
                   jax.ShapeDtypeStruct(w_out.shape, BF16),
                   jax.ShapeDtypeStruct((T, half), F32),
                   jax.ShapeDtypeStruct((T, half), F32)],
        compiler_params=_cparams(("arbitrary",)),
        name="prep",
    )(w_in_t, w_out, pos, inv)


def _retention_chunks(cd_ref, q_ref, k_ref, v_ref, g_ref, cos_ref, sin_ref,
                      dec_ref, wq_ref, ws_ref, nw_ref, o_ref, st_ref, cb, tick):
    hd = RET_HEAD_DIM
    half = hd // 2
    for i in range(cb):
        sl = slice(i * CHUNK, (i + 1) * CHUNK)
        cos = cos_ref[sl, :]
        sin = sin_ref[sl, :]

        def rope(x):
            x1, x2 = x[:, :half], x[:, half:]
            return jnp.concatenate([x1 * cos - x2 * sin, x2 * cos + x1 * sin], axis=-1)

        for h in range(RET_HEADS):
            hs = slice(h * hd, (h + 1) * hd)
            q = rope(q_ref[sl, hs])
            k = rope(k_ref[sl, hs]) * (hd ** -0.5)
            v = v_ref[sl, hs]
            qb = q.astype(BF16)
            kb = k.astype(BF16)
            scores = _dot_nt(qb, kb) * dec_ref[h]
            y = _dot(scores.astype(BF16), v.astype(BF16))
            st = st_ref[h]
            y = y + _dot(qb, st.astype(BF16)) * wq_ref[h]
            vw = (v * ws_ref[h]).astype(BF16)
            new = _dot(k.T.astype(BF16), vw)
            st_ref[h] = st * cd_ref[h] + new
            mu = jnp.mean(y, axis=-1, keepdims=True)
            d = y - mu
            var = jnp.mean(d * d, axis=-1, keepdims=True)
            yn = d * lax.rsqrt(var + EPS)
            o_ref[sl, hs] = (yn * nw_ref[:, hs] * _silu(g_ref[sl, hs])).astype(BF16)
            tick()


def _retention_consts():
    H, L = RET_HEADS, CHUNK
    f32 = np.float32
    log_gamma = np.log1p(-(f32(2.0) ** (f32(-5.0) - np.arange(H, dtype=f32)))).astype(f32)
    idx = np.arange(L, dtype=f32)
    diff = idx[:, None] - idx[None, :]
    causal = diff >= 0
    decay_intra = np.where(causal[None], np.exp(np.where(causal, diff, f32(0))[None] * log_gamma[:, None, None]), f32(0))
    w_state = np.exp((f32(L - 1) - idx)[None, :] * log_gamma[:, None])
    w_query = np.exp((idx + f32(1))[None, :] * log_gamma[:, None])
    chunk_decay = np.exp(f32(L) * log_gamma)
    bc = lambda w: np.ascontiguousarray(np.broadcast_to(w[:, :, None], (H, L, RET_HEAD_DIM)), dtype=f32)
    return (jnp.asarray(chunk_decay, F32), jnp.asarray(decay_intra, F32),
            jnp.asarray(bc(w_query), F32), jnp.asarray(bc(w_state), F32))


CONV_PAD = 8
XBC_BLK = 512


def _ssd_chunks(xbc_ref, z_ref, dt_ref, cw_ref, cbias_ref, dtb_ref,
                alog_ref, dskip_ref, nw_ref, tri_ref, o_ref, xpad_ref, st_ref, cb, tick):
    L = CHUNK
    gw = SSM_WIDTH // SSM_GROUPS
    rows_i = lax.broadcasted_iota(I32, (L, L), 0)
    cols_i = lax.broadcasted_iota(I32, (L, L), 1)
    causal = rows_i >= cols_i
    lo_lane = lax.broadcasted_iota(I32, (L, LANES), 1) < SSM_HEAD_DIM
    tri = tri_ref[...]
    a_neg = -jnp.exp(alog_ref[...])

    for i in range(cb):
        sl = slice(i * L, (i + 1) * L)
        xpad_ref[CONV_PAD:CONV_PAD + L, :] = xbc_ref[sl, :]
        u_parts = []
        for b in range(SSM_CONV_DIM // XBC_BLK):
            cs = slice(b * XBC_BLK, (b + 1) * XBC_BLK)
            acc = cbias_ref[:, cs]
            for t in range(SSM_CONV):
                r0 = CONV_PAD - (SSM_CONV - 1) + t
                acc = acc + xpad_ref[r0:r0 + L, cs] * cw_ref[t:t + 1, cs]
            u_parts.append(_silu(acc))
            tick()
        xpad_ref[0:CONV_PAD, :] = xpad_ref[L:L + CONV_PAD, :]
        xs = jnp.concatenate(u_parts[:2], axis=-1)
        bm = u_parts[2][:, :SSM_GROUPS * SSM_STATE]
        cm = u_parts[2][:, SSM_GROUPS * SSM_STATE:]

        dt_in = dt_ref[sl, :] + dtb_ref[...]
        dt = jnp.maximum(dt_in, 0.0) + jnp.log1p(jnp.exp(-jnp.abs(dt_in)))
        a = dt * a_neg
        a1, a2, a3 = _split3(a)
        a_cs = _dot(tri, a1) + _dot(tri, a2) + _dot(tri, a3)
        last = a_cs[L - 1:L, :]

        def per_head_lanes(v):
            lo = lo_lane[0:v.shape[0], :]
            return jnp.concatenate([jnp.where(lo, v[:, 2 * p:2 * p + 1], v[:, 2 * p + 1:2 * p + 2])
                                    for p in range(SSM_HEADS // 2)], axis=1)

        dt_e = per_head_lanes(dt)
        to_end = per_head_lanes(jnp.exp(last - a_cs))
        from_start = per_head_lanes(jnp.exp(a_cs))
        chunk_decay = per_head_lanes(jnp.exp(last))
        tick()
        x_dt = xs * dt_e
        xw = (x_dt * to_end).astype(BF16)
        acs_t = a_cs.T
        cmb = cm.astype(BF16)
        bmb = bm.astype(BF16)

        ys = []
        for g in range(SSM_GROUPS):
            ns = slice(g * SSM_STATE, (g + 1) * SSM_STATE)
            gs = slice(g * gw, (g + 1) * gw)
            cg = cmb[:, ns]
            cbg = _dot_nt(cg, bmb[:, ns])
            st = st_ref[g]
            y_off = _dot(cg, st.astype(BF16))
            new = _dot(bm[:, ns].T.astype(BF16), xw[:, gs])
            st_ref[g] = st * chunk_decay[:, gs] + new
            tick()
            for jp in range(gw // LANES):
                h0 = (g * gw + jp * LANES) // SSM_HEAD_DIM
                ms = []
                for hh in (h0, h0 + 1):
                    seg = a_cs[:, hh:hh + 1] - acs_t[hh:hh + 1, :]
                    dec = jnp.exp(jnp.where(causal, seg, -jnp.inf))
                    ms.append((cbg * dec).astype(BF16))
                lhs = jnp.concatenate(ms, axis=1)
                ls = slice(g * gw + jp * LANES, g * gw + (jp + 1) * LANES)
                xp = x_dt[:, ls]
                rhs = jnp.concatenate([jnp.where(lo_lane, xp, 0.0),
                                       jnp.where(lo_lane, 0.0, xp)], axis=0).astype(BF16)
                y_diag = _dot(lhs, rhs)
                ys.append(y_diag + y_off[:, jp * LANES:(jp + 1) * LANES] * from_start[:, ls])
                tick()
        y = jnp.concatenate(ys, axis=1) + xs * dskip_ref[...]
        y = y * _silu(z_ref[sl, :])
        outs = []
        for g in range(SSM_GROUPS):
            yg = y[:, g * gw:(g + 1) * gw]
            ms_ = jnp.mean(yg * yg, axis=-1, keepdims=True)
            outs.append(yg * lax.rsqrt(ms_ + EPS))
        o_ref[sl, :] = (jnp.concatenate(outs, axis=1) * nw_ref[...]).astype(BF16)


MIX_CB = 2
MIX_VMEM = 60 * 1024 * 1024
COL_Z = 4 * RET_WIDTH
COL_XBC = COL_Z + SSM_WIDTH
PROJ_WIDTHS = (RET_WIDTH, RET_WIDTH, RET_WIDTH, RET_WIDTH, SSM_WIDTH, SSM_CONV_DIM, LANES)
PROJ_SLAB = 256


def _mixer_body(cd_ref, x_ref, n1w_ref, wt_hbm, wdt_ref, cos_ref, sin_ref, dec_ref, wq_ref, ws_ref, rnw_ref,
                cw_ref, cbias_ref, dtb_ref, alog_ref, dskip_ref, snw_ref, tri_ref,
                yret_ref, yssm_ref, w_ref, hn_ref, pq, pk, pv, pg, pz, pxbc, pdt,
                st_ret, xpad_ref, st_ssd, sem, *, cb):
    @pl.when(pl.program_id(0) == 0)
    def _():
        cp = pltpu.make_async_copy(wt_hbm, w_ref, sem)
        cp.start()
        st_ret[...] = jnp.zeros_like(st_ret)
        xpad_ref[0:CONV_PAD, :] = jnp.zeros((CONV_PAD, SSM_CONV_DIM), F32)
        st_ssd[...] = jnp.zeros_like(st_ssd)
        cp.wait()

    def slabs(ref, col0):
        def slab(lo):
            def go():
                ref[:, lo:lo + PROJ_SLAB] = _dot_nt(hn_ref[...], w_ref[col0 + lo:col0 + lo + PROJ_SLAB, :])
            return go
        return [slab(lo) for lo in range(0, ref.shape[1], PROJ_SLAB)]

    x = x_ref[...]
    ms = jnp.mean(x * x, axis=-1, keepdims=True)
    hn_ref[...] = (x * lax.rsqrt(ms + EPS) * n1w_ref[...]).astype(BF16)
    for piece in slabs(pxbc, COL_XBC) + slabs(pz, COL_Z):
        piece()
    pdt[...] = _dot_nt(hn_ref[...], wdt_ref[...])

    pieces = slabs(pq, 0) + slabs(pk, RET_WIDTH) + slabs(pv, 2 * RET_WIDTH) + slabs(pg, 3 * RET_WIDTH)
    n_pieces = len(pieces)
    n_ticks = cb * (SSM_CONV_DIM // XBC_BLK + 1 + SSM_GROUPS + SSM_WIDTH // LANES)
    calls = [0]

    def tick():
        calls[0] += 1
        while pieces and (n_pieces - len(pieces)) * n_ticks < calls[0] * n_pieces:
            pieces.pop(0)()

    _ssd_chunks(pxbc, pz, pdt, cw_ref, cbias_ref, dtb_ref, alog_ref, dskip_ref, snw_ref, tri_ref,
                yssm_ref, xpad_ref, st_ssd, cb, tick)
    while pieces:
        pieces.pop(0)()
    _retention_chunks(cd_ref, pq, pk, pv, pg, cos_ref, sin_ref, dec_ref, wq_ref, ws_ref, rnw_ref,
                      yret_ref, st_ret, cb, lambda: None)


def _mixer(x, n1w, w_main_t, w_dt_t, cos, sin, ret_nw, ret_consts,
           conv_w, conv_b, dt_bias, a_log, d_skip_e, ssm_nw, tri):
    T = x.shape[0]
    cb = min(MIX_CB, T // CHUNK)
    rows = cb * CHUNK
    chunk_decay, decay_intra, w_query, w_state = ret_consts
    hd = RET_HEAD_DIM
    half = hd // 2
    full = lambda shape: pl.BlockSpec(shape, lambda c, cd: (0,) * len(shape))
    rowblk = lambda width: pl.BlockSpec((rows, width), lambda c, cd: (c, 0))
    scratch = lambda width: pltpu.VMEM((rows, width), F32)
    grid_spec = pltpu.PrefetchScalarGridSpec(
        num_scalar_prefetch=1,
        grid=(T // rows,),
        in_specs=[
            rowblk(D_MODEL),
            full((1, D_MODEL)),
            pl.BlockSpec(memory_space=pl.ANY),
            full((LANES, D_MODEL)),
            rowblk(half), rowblk(half),
            full((RET_HEADS, CHUNK, CHUNK)),
            full((RET_HEADS, CHUNK, hd)),
            full((RET_HEADS, CHUNK, hd)),
            full((1, RET_WIDTH)),
            full((SSM_CONV, SSM_CONV_DIM)),
            full((1, SSM_CONV_DIM)),
            full((1, LANES)),
            full((1, LANES)),
            full((1, SSM_WIDTH)),
            full((1, SSM_WIDTH)),
            full((CHUNK, CHUNK)),
        ],
        out_specs=[rowblk(RET_WIDTH), rowblk(SSM_WIDTH)],
        scratch_shapes=[
            pltpu.VMEM((MAIN_PROJ, D_MODEL), BF16),
            pltpu.VMEM((rows, D_MODEL), BF16),
            *[scratch(w) for w in PROJ_WIDTHS],
            pltpu.VMEM((RET_HEADS, hd, hd), F32),
            pltpu.VMEM((CHUNK + CONV_PAD, SSM_CONV_DIM), F32),
            pltpu.VMEM((SSM_GROUPS, SSM_STATE, SSM_WIDTH // SSM_GROUPS), F32),
            pltpu.SemaphoreType.DMA(()),
        ],
    )
    return pl.pallas_call(
        functools.partial(_mixer_body, cb=cb),
        grid_spec=grid_spec,
        out_shape=[jax.ShapeDtypeStruct((T, RET_WIDTH), BF16),
                   jax.ShapeDtypeStruct((T, SSM_WIDTH), BF16)],
        compiler_params=_cparams(("arbitrary",), MIX_VMEM),
        name="mixer",
    )(chunk_decay, x, n1w, w_main_t, w_dt_t, cos, sin, decay_intra, w_query, w_state, ret_nw,
      conv_w, conv_b, dt_bias, a_log, d_skip_e, ssm_nw, tri)


OUT_TM = 512
ROUTE_ROWS = 8 + N_EXPERTS


def _out_router_body(x_ref, yr_ref, ys_ref, wo_ref, nw_ref, wr_ref, br_ref,
                     h1_ref, h2t_ref, cw_ref, ids_ref):
    tm = x_ref.shape[0]
    h1 = x_ref[...] + _dot(yr_ref[...], wo_ref[0:RET_WIDTH, :]) + _dot(ys_ref[...], wo_ref[RET_WIDTH:, :])
    h1_ref[...] = h1
    ms = jnp.mean(h1 * h1, axis=-1, keepdims=True)
    h2 = h1 * lax.rsqrt(ms + EPS) * nw_ref[...]
    for s in range(Y_ROWS):
        h2t_ref[:, s, :] = h2[:, s * LANES:(s + 1) * LANES]

    logits = _dot_nt(wr_ref[...], h2.astype(BF16)) + br_ref[...]
    row = lax.broadcasted_iota(I32, (8, tm), 0)
    lg = jnp.where(row < N_GROUPS, logits[0:8], -jnp.inf)
    m = jnp.max(lg, axis=0, keepdims=True)
    p_sel = 1.0 / jnp.sum(jnp.exp(lg - m), axis=0, keepdims=True)
    g_sel = jnp.min(jnp.where(lg == m, row, 8), axis=0, keepdims=True)
    le = jnp.zeros((GROUP_EXPERTS, tm), F32)
    for g in range(N_GROUPS):
        le = jnp.where(g_sel == g, logits[8 + g * GROUP_EXPERTS:8 + (g + 1) * GROUP_EXPERTS], le)
    m2 = jnp.max(le, axis=0, keepdims=True)
    ee = jnp.exp(le - m2)
    pe = ee / jnp.sum(ee, axis=0, keepdims=True)
    v1 = jnp.max(pe, axis=0, keepdims=True)
    i1 = jnp.min(jnp.where(pe == v1, row, 8), axis=0, keepdims=True)
    pe2 = jnp.where(row == i1, -1.0, pe)
    v2 = jnp.max(pe2, axis=0, keepdims=True)
    i2 = jnp.min(jnp.where(pe2 == v2, row, 8), axis=0, keepdims=True)
    tw = v1 + v2
    c1 = v1 / tw * p_sel
    c2 = v2 / tw * p_sel
    e1 = g_sel * GROUP_EXPERTS + i1
    e2 = g_sel * GROUP_EXPERTS + i2
    ids = jnp.where(row == 0, e1, jnp.where(row == 1, e2, 0))
    for b in range(tm // SORT_BLK):
        ids_ref[b] = ids[:, b * SORT_BLK:(b + 1) * SORT_BLK]
    cw8 = jnp.where(row == 0, c1, jnp.where(row == 1, c2, 0.0))
    cw = jnp.concatenate([cw8, jnp.zeros((LANES - 8, tm), F32)], axis=0)
    cw_ref[...] = cw.T


def _out_router(x, y_ret, y_ssm, w_out, nw, wr_t, br):
    T = x.shape[0]
    tm = min(OUT_TM, T)
    full = lambda shape: pl.BlockSpec(shape, lambda i: (0,) * len(shape))
    return pl.pallas_call(
        _out_router_body,
        grid=(T // tm,),
        in_specs=[
            pl.BlockSpec((tm, D_MODEL), lambda i: (i, 0)),
            pl.BlockSpec((tm, RET_WIDTH), lambda i: (i, 0)),
            pl.BlockSpec((tm, SSM_WIDTH), lambda i: (i, 0)),
            pl.BlockSpec((D_MODEL, D_MODEL), lambda i: (0, 0), pipeline_mode=pl.Buffered(1)),
            full((1, D_MODEL)),
            full((LANES, D_MODEL)),
            full((LANES, 1)),
        ],
        out_specs=[
            pl.BlockSpec((tm, D_MODEL), lambda i: (i, 0)),
            pl.BlockSpec((tm, Y_ROWS, LANES), lambda i: (i, 0, 0)),
            pl.BlockSpec((tm, LANES), lambda i: (i, 0)),
            pl.BlockSpec((tm // SORT_BLK, 8, SORT_BLK), lambda i: (i, 0, 0)),
        ],
        out_shape=[
            jax.ShapeDtypeStruct((T, D_MODEL), F32),
            jax.ShapeDtypeStruct((T, Y_ROWS, LANES), F32),
            jax.ShapeDtypeStruct((T, LANES), F32),
            jax.ShapeDtypeStruct((T // SORT_BLK, 8, SORT_BLK), I32),
        ],
        compiler_params=_cparams(("parallel",)),
        name="out_router",
    )(x, y_ret, y_ssm, w_out, nw, wr_t, br)


SORT_BLK = 256
MOE_TM = 256


def _sort_index_body(ids_ref, tri_ref, ltri_ref, dest_ref, cnt_ref, rank_ref):
    nblk = ids_ref.shape[0]
    row_e = lax.broadcasted_iota(I32, (N_EXPERTS, SORT_BLK), 0)
    row8 = lax.broadcasted_iota(I32, (8, SORT_BLK), 0)

    def onehots(b):
        ids = ids_ref[b]
        return row_e == ids[0:1], row_e == ids[1:2]

    def rank_blk(b, carry):
        oh1, oh2 = onehots(b)
        ohf = jnp.where(oh1 | oh2, 1.0, 0.0)
        incl = _dot(ohf.astype(BF16), tri_ref[...])
        base = carry + incl - 1.0
        r1 = jnp.sum(jnp.where(oh1, base, 0.0), axis=0, keepdims=True)
        r2 = jnp.sum(jnp.where(oh2, base, 0.0), axis=0, keepdims=True)
        rank_ref[b] = jnp.where(row8 == 0, r1, jnp.where(row8 == 1, r2, 0.0))
        return carry + jnp.sum(ohf, axis=1, keepdims=True)

    cnt = lax.fori_loop(0, nblk, rank_blk, jnp.zeros((N_EXPERTS, 1), F32))
    cnt_ref[...] = jnp.broadcast_to(cnt, cnt_ref.shape)
    tiles = jnp.floor((cnt + (MOE_TM - 1.0)) / MOE_TM)
    tiles_b = jnp.broadcast_to(tiles, (N_EXPERTS, LANES)).astype(BF16)
    off = _dot(ltri_ref[...], tiles_b)[:, 0:1] * MOE_TM

    def dest_blk(b, carry):
        oh1, oh2 = onehots(b)
        o1 = jnp.sum(jnp.where(oh1, off, 0.0), axis=0, keepdims=True)
        o2 = jnp.sum(jnp.where(oh2, off, 0.0), axis=0, keepdims=True)
        d = rank_ref[b] + jnp.where(row8 == 0, o1, jnp.where(row8 == 1, o2, 0.0))
        dest_ref[b] = d.astype(I32)
        return carry

    lax.fori_loop(0, nblk, dest_blk, 0)


def _sort_index(ids, tri_u, ltri):
    nblk = ids.shape[0]
    return pl.pallas_call(
        _sort_index_body,
        out_shape=[jax.ShapeDtypeStruct((nblk, 8, SORT_BLK), I32),
                   jax.ShapeDtypeStruct((N_EXPERTS, LANES), F32)],
        scratch_shapes=[pltpu.VMEM((nblk, 8, SORT_BLK), F32)],
        compiler_params=_cparams(None),
        name="sort_index",
    )(ids, tri_u, ltri)


ROW_DMA_PRIORITY = 1
GATHER_AHEAD = 2


def _moe_body(te_ref, tf_ref, nx_ref, nu_ref, dest_ref, lo_ref, hi_ref,
              h2t_ref, wg_ref, wu_ref, wd_ref, yt_ref,
              srow_ref, wcount, xbuf, ystage, wsg, wsu, wsd, wgb_ref, wub_ref, wdb_ref, gsem, ssem, wsem):
    i = pl.program_id(0)
    nu = nu_ref[0]
    T = h2t_ref.shape[0]

    def gather_row(tile, slot, r):
        tok = srow_ref[tile * MOE_TM + r] & (T - 1)
        return pltpu.make_async_copy(h2t_ref.at[tok], xbuf.at[slot, :, r, :], gsem.at[slot])

    def scatter_row(tile, slot, r):
        return pltpu.make_async_copy(ystage.at[slot, :, r, :], yt_ref.at[srow_ref[tile * MOE_TM + r]],
                                     ssem.at[slot])

    def gather_tile(slot):
        return pltpu.make_async_copy(h2t_ref.at[pl.ds(0, MOE_TM)], h2t_ref.at[pl.ds(0, MOE_TM)], gsem.at[slot])

    def scatter_tile(slot):
        return pltpu.make_async_copy(yt_ref.at[pl.ds(0, MOE_TM)], yt_ref.at[pl.ds(0, MOE_TM)], ssem.at[slot])

    def weight_copies(e, ws):
        return (pltpu.make_async_copy(wg_ref.at[e], wsg.at[ws], wsem.at[ws, 0]),
                pltpu.make_async_copy(wu_ref.at[e], wsu.at[ws], wsem.at[ws, 1]),
                pltpu.make_async_copy(wd_ref.at[e], wsd.at[ws], wsem.at[ws, 2]))

    @pl.when(i == 0)
    def _():
        wcount[0] = 0
        for c in weight_copies(te_ref[0], 0):
            c.start()

        def tok(t, c):
            srow_ref[dest_ref[t]] = t
            srow_ref[dest_ref[T + t]] = T + t
            return c

        lax.fori_loop(0, T, tok, 0, unroll=8)

        def seg(e, c):
            def pad(r, c2):
                srow_ref[r] = 2 * T + (r & (MOE_TM - 1))
                return c2

            lax.fori_loop(lo_ref[e], hi_ref[e], pad, 0)
            return c

        lax.fori_loop(0, lo_ref.shape[0], seg, 0)

        ystage[...] = jnp.zeros_like(ystage)

        def spare(r, c):
            pltpu.make_async_copy(ystage.at[0, :, r, :], yt_ref.at[2 * T + r], ssem.at[0]).start()
            return c

        lax.fori_loop(0, MOE_TM, spare, 0)
        scatter_tile(0).wait()
        for d in range(GATHER_AHEAD):
            first = jnp.minimum(d, nu - 1)
            for r in range(MOE_TM):
                gather_row(first, d, r).start(priority=ROW_DMA_PRIORITY)

    @pl.when(tf_ref[i] == 1)
    def _():
        ws = lax.rem(wcount[0], 2)
        wcount[0] = wcount[0] + 1
        for c in weight_copies(te_ref[i], ws):
            c.wait()

        @pl.when(nx_ref[i] >= 0)
        def _():
            for c in weight_copies(nx_ref[i], 1 - ws):
                c.start()

        wgb_ref[...] = wsg[ws].astype(BF16)
        wub_ref[...] = wsu[ws].astype(BF16)
        wdb_ref[...] = wsd[ws].astype(BF16)

    slot = lax.rem(i, 2)
    other = 1 - slot
    n_xbuf = GATHER_AHEAD + 1
    xs = lax.rem(i, n_xbuf)

    @pl.when(i < nu)
    def _():
        gather_tile(xs).wait()

        @pl.when(i >= 1)
        def _():
            scatter_tile(slot).wait()

        nxt = jnp.minimum(i + GATHER_AHEAD, nu - 1)
        nxs = lax.rem(i + GATHER_AHEAD, n_xbuf)
        prv = jnp.maximum(i - 1, 0)
        for r in range(MOE_TM):
            gather_row(nxt, nxs, r).start(priority=r % 2)
            scatter_row(prv, other, r).start(priority=(r + 1) % 2)

        x = jnp.concatenate([xbuf[xs, s].astype(BF16) for s in range(Y_ROWS)], axis=1)
        a = _dot(x, wgb_ref[...])
        u = _dot(x, wub_ref[...])
        act = (_silu(a) * u).astype(BF16)
        y = _dot(act, wdb_ref[...])
        for s in range(Y_ROWS):
            ystage[slot, s] = y[:, s * LANES:(s + 1) * LANES]

    @pl.when(i == nu)
    def _():
        for d in range(GATHER_AHEAD):
            gather_tile(lax.rem(i + d, n_xbuf)).wait()
        scatter_tile(slot).wait()

        def last(r, c):
            scatter_row(nu - 1, other, r).start()
            return c

        lax.fori_loop(0, MOE_TM, last, 0)
        scatter_tile(other).wait()


def _moe(tile_expert, tile_first, next_expert, n_used, dest, pad_lo, pad_hi, h2t, w_gate, w_up, w_down):
    T = h2t.shape[0]
    assert T & (T - 1) == 0 and T >= MOE_TM, "token index is recovered from the row table by masking"
    n_tiles = tile_expert.shape[0] - 1
    hbm = pl.BlockSpec(memory_space=pl.ANY)
    grid_spec = pltpu.PrefetchScalarGridSpec(
        num_scalar_prefetch=7,
        grid=(n_tiles + 1,),
        in_specs=[hbm, hbm, hbm, hbm],
        out_specs=hbm,
        scratch_shapes=[pltpu.SMEM((n_tiles * MOE_TM,), I32),
                        pltpu.SMEM((1,), I32),
                        pltpu.VMEM((GATHER_AHEAD + 1, Y_ROWS, MOE_TM, LANES), F32),
                        pltpu.VMEM((2, Y_ROWS, MOE_TM, LANES), F32),
                        pltpu.VMEM((2, D_MODEL, D_EXPERT), F32),
                        pltpu.VMEM((2, D_MODEL, D_EXPERT), F32),
                        pltpu.VMEM((2, D_EXPERT, D_MODEL), F32),
                        pltpu.VMEM((D_MODEL, D_EXPERT), BF16),
                        pltpu.VMEM((D_MODEL, D_EXPERT), BF16),
                        pltpu.VMEM((D_EXPERT, D_MODEL), BF16),
                        pltpu.SemaphoreType.DMA((GATHER_AHEAD + 1,)),
                        pltpu.SemaphoreType.DMA((2,)),
                        pltpu.SemaphoreType.DMA((2, 3))],
    )
    return pl.pallas_call(
        _moe_body,
        grid_spec=grid_spec,
        out_shape=jax.ShapeDtypeStruct((2 * T + MOE_TM, Y_ROWS, LANES), F32),
        compiler_params=_cparams(("arbitrary",)),
        name="moe",
    )(tile_expert, tile_first, next_expert, n_used, dest, pad_lo, pad_hi, h2t, w_gate, w_up, w_down)


COMB_TM = 512


def _combine_body(h1_ref, y0_ref, y1_ref, cw_ref, nw_ref, o_ref):
    tm = h1_ref.shape[0]
    c0 = cw_ref[:, 0:1]
    c1 = cw_ref[:, 1:2]
    cols = []
    for s in range(Y_ROWS):
        moe = c0 * y0_ref[:, s, :] + c1 * y1_ref[:, s, :]
        cols.append(h1_ref[:, s * LANES:(s + 1) * LANES] + moe)
    h = jnp.concatenate(cols, axis=1)
    ms = jnp.mean(h * h, axis=-1, keepdims=True)
    o_ref[...] = h * lax.rsqrt(ms + EPS) * nw_ref[...]


def _combine(h1, y_tok, cw, nw):
    T = h1.shape[0]
    tm = min(COMB_TM, T)
    nt = T // tm
    return pl.pallas_call(
        _combine_body,
        grid=(nt,),
        in_specs=[
            pl.BlockSpec((tm, D_MODEL), lambda i: (i, 0)),
            pl.BlockSpec((tm, Y_ROWS, LANES), lambda i: (i, 0, 0)),
            pl.BlockSpec((tm, Y_ROWS, LANES), lambda i: (i + nt, 0, 0)),
            pl.BlockSpec((tm, LANES), lambda i: (i, 0)),
            pl.BlockSpec((1, D_MODEL), lambda i: (0, 0)),
        ],
        out_specs=pl.BlockSpec((tm, D_MODEL), lambda i: (i, 0)),
        out_shape=jax.ShapeDtypeStruct((T, D_MODEL), F32),
        compiler_params=_cparams(("parallel",)),
        name="combine",
    )(h1, y_tok, y_tok, cw, nw)


def _tile_plan(cnt, n_tiles):
    tiles = (cnt + (MOE_TM - 1)) // MOE_TM
    ends = jnp.cumsum(tiles)
    starts = ends - tiles
    n_used = ends[-1]
    step = jnp.arange(n_tiles + 1, dtype=I32)
    tile = jnp.minimum(step, jnp.maximum(n_used - 1, 0))
    tile_expert = jnp.sum((ends[None, :] <= tile[:, None]).astype(I32), axis=1)
    tile_first = ((step == starts[tile_expert]) & (step < n_used)).astype(I32)
    nxt_tile = ends[tile_expert]
    nxt_expert = jnp.sum((ends[None, :] <= nxt_tile[:, None]).astype(I32), axis=1)
    next_expert = jnp.where(nxt_tile < n_used, nxt_expert, -1).astype(I32)
    pad_lo = starts * MOE_TM + cnt
    pad_hi = ends * MOE_TM
    return (tile_expert, tile_first, next_expert, n_used.reshape(1).astype(I32),
            pad_lo.astype(I32), pad_hi.astype(I32))


def kernel(x, positions, norm1_w, w_in, conv_w, conv_b, dt_bias, a_log, d_skip, ret_norm_w,
           ssm_norm_w, w_out, norm2_w, w_router_group, b_router_group, w_router_expert,
           b_router_expert, w_expert_gate, w_expert_up, w_expert_down, final_norm_w):
    B, T, D = x.shape
    assert B == 1 and D == D_MODEL and T % CHUNK == 0
    xf = x.reshape(T, D)
    pad_l = lambda v: jnp.pad(v, ((0, 0), (0, LANES - v.shape[-1])))

    w_in_t = jnp.swapaxes(w_in[0], 0, 1)
    w_dt = jnp.pad(w_in_t[MAIN_PROJ:], ((0, LANES - (w_in_t.shape[0] - MAIN_PROJ)), (0, 0))).astype(BF16)
    half = RET_HEAD_DIM // 2
    inv = (ROPE_THETA ** (-jnp.arange(half, dtype=F32) / half)).reshape(1, half)
    tri = jnp.asarray(np.tril(np.ones((CHUNK, CHUNK), np.float32)), BF16)
    d_skip_e = jnp.repeat(d_skip[0], SSM_HEAD_DIM).reshape(1, SSM_WIDTH)
    wr_t = jnp.concatenate([w_router_group[0].T, jnp.zeros((8 - N_GROUPS, D), F32), w_router_expert[0].T,
                            jnp.zeros((LANES - 8 - N_EXPERTS, D), F32)], axis=0)
    br = jnp.concatenate([b_router_group[0], jnp.zeros((8 - N_GROUPS,), F32), b_router_expert[0],
                          jnp.zeros((LANES - 8 - N_EXPERTS,), F32)])
    tri_u = jnp.asarray(np.triu(np.ones((SORT_BLK, SORT_BLK), np.float32)), BF16)
    ltri = jnp.asarray(np.tril(np.ones((N_EXPERTS, N_EXPERTS), np.float32), k=-1), BF16)

    w_main, w_out_b, cos, sin = _prep(w_in_t, w_out[0], positions.reshape(T, 1).astype(F32), inv)
    y_ret, y_ssm = _mixer(xf, norm1_w[0].reshape(1, D), w_main, w_dt, cos, sin,
                          ret_norm_w[0].reshape(1, RET_WIDTH), _retention_consts(),
                          conv_w[0], conv_b[0].reshape(1, -1), pad_l(dt_bias[0].reshape(1, -1)),
                          pad_l(a_log[0].reshape(1, -1)), d_skip_e, ssm_norm_w[0].reshape(1, -1), tri)
    h1, h2t, cw, ids = _out_router(xf, y_ret, y_ssm, w_out_b, norm2_w[0].reshape(1, D),
                                   wr_t.astype(BF16), br.reshape(LANES, 1))

    dest_blk, cnt = _sort_index(ids, tri_u, ltri)
    dest = dest_blk[:, 0:2, :].transpose(1, 0, 2).reshape(2 * T)
    n_tiles = (2 * T) // MOE_TM + N_EXPERTS
    tile_expert, tile_first, next_expert, n_used, pad_lo, pad_hi = _tile_plan(cnt[:, 0].astype(I32), n_tiles)
    y_tok = _moe(tile_expert, tile_first, next_expert, n_used, dest, pad_lo, pad_hi, h2t,
                 w_expert_gate[0], w_expert_up[0], w_expert_down[0])
    out = _combine(h1, y_tok, cw, final_norm_w.reshape(1, D))
    return out.reshape(B, T, D)
```

```python
import functools

import jax
import jax.numpy as jnp
import numpy as np
from jax import lax
from jax.experimental import pallas as pl
from jax.experimental.pallas import tpu as pltpu

F32 = jnp.float32
BF16 = jnp.bfloat16
I32 = jnp.int32

D_MODEL = 2048
EPS = 1e-6
CHUNK = 128
RET_HEADS = 4
RET_HEAD_DIM = 256
RET_WIDTH = RET_HEADS * RET_HEAD_DIM
ROPE_THETA = 10000.0
SSM_WIDTH = 1024
SSM_HEAD_DIM = 64
SSM_HEADS = SSM_WIDTH // SSM_HEAD_DIM
SSM_GROUPS = 2
SSM_STATE = 128
SSM_CONV = 4
SSM_CONV_DIM = SSM_WIDTH + 2 * SSM_GROUPS * SSM_STATE
MAIN_PROJ = 4 * RET_WIDTH + SSM_WIDTH + SSM_CONV_DIM
N_GROUPS = 4
GROUP_EXPERTS = 8
N_EXPERTS = N_GROUPS * GROUP_EXPERTS
D_EXPERT = 512
LANES = 128
Y_ROWS = D_MODEL // LANES

VMEM_LIMIT = 56 * 1024 * 1024


def _cparams(sem, vmem=VMEM_LIMIT):
    return pltpu.CompilerParams(dimension_semantics=sem, vmem_limit_bytes=vmem)


def _silu(x):
    return x * (1.0 / (1.0 + jnp.exp(-x)))


def _dot(a, b):
    return jnp.dot(a, b, preferred_element_type=F32)


def _dot_nt(a, b):
    return lax.dot_general(a, b, (((1,), (1,)), ((), ())), preferred_element_type=F32)


def _split3(a):
    a1 = a.astype(BF16)
    r1 = a - a1.astype(F32)
    a2 = r1.astype(BF16)
    a3 = (r1 - a2.astype(F32)).astype(BF16)
    return a1, a2, a3


CAST_ROWS = 512
ROPE_ROWS = 1024


def _prep_body(win_ref, wout_ref, pos_ref, inv_ref, winb_ref, woutb_ref, cos_ref, sin_ref, *, n_in, n_rope):
    j = pl.program_id(0)

    @pl.when(j < n_in)
    def _():
        winb_ref[...] = win_ref[...].astype(BF16)

    @pl.when(j >= n_in)
    def _():
        woutb_ref[...] = wout_ref[...].astype(BF16)

    @pl.when(j < n_rope)
    def _():
        ang = pos_ref[...] * inv_ref[...]
        cos_ref[...] = jnp.cos(ang)
        sin_ref[...] = jnp.sin(ang)


def _prep(w_in_t, w_out, pos, inv):
    T = pos.shape[0]
    cols = w_in_t.shape[1]
    half = inv.shape[1]
    n_in = MAIN_PROJ // CAST_ROWS
    n_out = w_out.shape[0] // CAST_ROWS
    rr = min(ROPE_ROWS, T)
    n_rope = T // rr
    assert n_rope <= n_in + n_out
    in_blk = lambda j: (jnp.minimum(j, n_in - 1), 0)
    out_blk = lambda j: (jnp.clip(j - n_in, 0, n_out - 1), 0)
    rope_blk = lambda j: (jnp.minimum(j, n_rope - 1), 0)
    return pl.pallas_call(
        functools.partial(_prep_body, n_in=n_in, n_rope=n_rope),
        grid=(n_in + n_out,),
        in_specs=[pl.BlockSpec((CAST_ROWS, cols), in_blk),
                  pl.BlockSpec((CAST_ROWS, w_out.shape[1]), out_blk),
                  pl.BlockSpec((rr, 1), rope_blk),
                  pl.BlockSpec((1, half), lambda j: (0, 0))],
        out_specs=[pl.BlockSpec((CAST_ROWS, cols), in_blk),
                   pl.BlockSpec((CAST_ROWS, w_out.shape[1]), out_blk),
                   pl.BlockSpec((rr, half), rope_blk),
                   pl.BlockSpec((rr, half), rope_blk)],
        out_shape=[jax.ShapeDtypeStruct((MAIN_PROJ, cols), BF16),
                   jax.ShapeDtypeStruct(w_out.shape, BF16),
                   jax.ShapeDtypeStruct((T, half), F32),
                   jax.ShapeDtypeStruct((T, half), F32)],
        compiler_params=_cparams(("arbitrary",)),
        name="prep",
    )(w_in_t, w_out, pos, inv)


def _retention_chunks(cd_ref, q_ref, k_ref, v_ref, g_ref, cos_ref, sin_ref,
                      dec_ref, wq_ref, ws_ref, nw_ref, o_ref, st_ref, cb, tick):
    hd = RET_HEAD_DIM
    half = hd // 2
    for i in range(cb):
        sl = slice(i * CHUNK, (i + 1) * CHUNK)
        cos = cos_ref[sl, :]
        sin = sin_ref[sl, :]

        def rope(x):
            x1, x2 = x[:, :half], x[:, half:]
            return jnp.concatenate([x1 * cos - x2 * sin, x2 * cos + x1 * sin], axis=-1)

        for h in range(RET_HEADS):
            hs = slice(h * hd, (h + 1) * hd)
            q = rope(q_ref[sl, hs])
            k = rope(k_ref[sl, hs]) * (hd ** -0.5)
            v = v_ref[sl, hs]
            qb = q.astype(BF16)
            kb = k.astype(BF16)
            scores = _dot_nt(qb, kb) * dec_ref[h]
            y = _dot(scores.astype(BF16), v.astype(BF16))
            st = st_ref[h]
            y = y + _dot(qb, st.astype(BF16)) * wq_ref[h]
            vw = (v * ws_ref[h]).astype(BF16)
            new = _dot(k.T.astype(BF16), vw)
            st_ref[h] = st * cd_ref[h] + new
            mu = jnp.mean(y, axis=-1, keepdims=True)
            d = y - mu
            var = jnp.mean(d * d, axis=-1, keepdims=True)
            yn = d * lax.rsqrt(var + EPS)
            o_ref[sl, hs] = (yn * nw_ref[:, hs] * _silu(g_ref[sl, hs])).astype(BF16)
            tick()


def _retention_consts():
    H, L = RET_HEADS, CHUNK
    f32 = np.float32
    log_gamma = np.log1p(-(f32(2.0) ** (f32(-5.0) - np.arange(H, dtype=f32)))).astype(f32)
    idx = np.arange(L, dtype=f32)
    diff = idx[:, None] - idx[None, :]
    causal = diff >= 0
    decay_intra = np.where(causal[None], np.exp(np.where(causal, diff, f32(0))[None] * log_gamma[:, None, None]), f32(0))
    w_state = np.exp((f32(L - 1) - idx)[None, :] * log_gamma[:, None])
    w_query = np.exp((idx + f32(1))[None, :] * log_gamma[:, None])
    chunk_decay = np.exp(f32(L) * log_gamma)
    bc = lambda w: np.ascontiguousarray(np.broadcast_to(w[:, :, None], (H, L, RET_HEAD_DIM)), dtype=f32)
    return (jnp.asarray(chunk_decay, F32), jnp.asarray(decay_intra, F32),
            jnp.asarray(bc(w_query), F32), jnp.asarray(bc(w_state), F32))


CONV_PAD = 8
XBC_BLK = 512


def _ssd_chunks(xbc_ref, z_ref, dt_ref, cw_ref, cbias_ref, dtb_ref,
                alog_ref, dskip_ref, nw_ref, tri_ref, o_ref, xpad_ref, st_ref, cb, tick):
    L = CHUNK
    gw = SSM_WIDTH // SSM_GROUPS
    rows_i = lax.broadcasted_iota(I32, (L, L), 0)
    cols_i = lax.broadcasted_iota(I32, (L, L), 1)
    causal = rows_i >= cols_i
    lo_lane = lax.broadcasted_iota(I32, (L, LANES), 1) < SSM_HEAD_DIM
    tri = tri_ref[...]
    a_neg = -jnp.exp(alog_ref[...])

    for i in range(cb):
        sl = slice(i * L, (i + 1) * L)
        xpad_ref[CONV_PAD:CONV_PAD + L, :] = xbc_ref[sl, :]
        u_parts = []
        for b in range(SSM_CONV_DIM // XBC_BLK):
            cs = slice(b * XBC_BLK, (b + 1) * XBC_BLK)
            acc = cbias_ref[:, cs]
            for t in range(SSM_CONV):
                r0 = CONV_PAD - (SSM_CONV - 1) + t
                acc = acc + xpad_ref[r0:r0 + L, cs] * cw_ref[t:t + 1, cs]
            u_parts.append(_silu(acc))
            tick()
        xpad_ref[0:CONV_PAD, :] = xpad_ref[L:L + CONV_PAD, :]
        xs = jnp.concatenate(u_parts[:2], axis=-1)
        bm = u_parts[2][:, :SSM_GROUPS * SSM_STATE]
        cm = u_parts[2][:, SSM_GROUPS * SSM_STATE:]

        dt_in = dt_ref[sl, :] + dtb_ref[...]
        dt = jnp.maximum(dt_in, 0.0) + jnp.log1p(jnp.exp(-jnp.abs(dt_in)))
        a = dt * a_neg
        a1, a2, a3 = _split3(a)
        a_cs = _dot(tri, a1) + _dot(tri, a2) + _dot(tri, a3)
        last = a_cs[L - 1:L, :]

        def per_head_lanes(v):
            lo = lo_lane[0:v.shape[0], :]
            return jnp.concatenate([jnp.where(lo, v[:, 2 * p:2 * p + 1], v[:, 2 * p + 1:2 * p + 2])
                                    for p in range(SSM_HEADS // 2)], axis=1)

        dt_e = per_head_lanes(dt)
        to_end = per_head_lanes(jnp.exp(last - a_cs))
        from_start = per_head_lanes(jnp.exp(a_cs))
        chunk_decay = per_head_lanes(jnp.exp(last))
        tick()
        x_dt = xs * dt_e
        xw = (x_dt * to_end).astype(BF16)
        acs_t = a_cs.T
        cmb = cm.astype(BF16)
        bmb = bm.astype(BF16)

        ys = []
        for g in range(SSM_GROUPS):
            ns = slice(g * SSM_STATE, (g + 1) * SSM_STATE)
            gs = slice(g * gw, (g + 1) * gw)
            cg = cmb[:, ns]
            cbg = _dot_nt(cg, bmb[:, ns])
            st = st_ref[g]
            y_off = _dot(cg, st.astype(BF16))
            new = _dot(bm[:, ns].T.astype(BF16), xw[:, gs])
            st_ref[g] = st * chunk_decay[:, gs] + new
            tick()
            for jp in range(gw // LANES):
                h0 = (g * gw + jp * LANES) // SSM_HEAD_DIM
                ms = []
                for hh in (h0, h0 + 1):
                    seg = a_cs[:, hh:hh + 1] - acs_t[hh:hh + 1, :]
                    dec = jnp.exp(jnp.where(causal, seg, -jnp.inf))
                    ms.append((cbg * dec).astype(BF16))
                lhs = jnp.concatenate(ms, axis=1)
                ls = slice(g * gw + jp * LANES, g * gw + (jp + 1) * LANES)
                xp = x_dt[:, ls]
                rhs = jnp.concatenate([jnp.where(lo_lane, xp, 0.0),
                                       jnp.where(lo_lane, 0.0, xp)], axis=0).astype(BF16)
                y_diag = _dot(lhs, rhs)
                ys.append(y_diag + y_off[:, jp * LANES:(jp + 1) * LANES] * from_start[:, ls])
                tick()
        y = jnp.concatenate(ys, axis=1) + xs * dskip_ref[...]
        y = y * _silu(z_ref[sl, :])
        outs = []
        for g in range(SSM_GROUPS):
            yg = y[:, g * gw:(g + 1) * gw]
            ms_ = jnp.mean(yg * yg, axis=-1, keepdims=True)
            outs.append(yg * lax.rsqrt(ms_ + EPS))
        o_ref[sl, :] = (jnp.concatenate(outs, axis=1) * nw_ref[...]).astype(BF16)


MIX_CB = 2
MIX_VMEM = 60 * 1024 * 1024
COL_Z = 4 * RET_WIDTH
COL_XBC = COL_Z + SSM_WIDTH
PROJ_WIDTHS = (RET_WIDTH, RET_WIDTH, RET_WIDTH, RET_WIDTH, SSM_WIDTH, SSM_CONV_DIM, LANES)
PROJ_SLAB = 256


def _mixer_body(cd_ref, x_ref, n1w_ref, wt_hbm, wdt_ref, cos_ref, sin_ref, dec_ref, wq_ref, ws_ref, rnw_ref,
                cw_ref, cbias_ref, dtb_ref, alog_ref, dskip_ref, snw_ref, tri_ref,
                yret_ref, yssm_ref, w_ref, hn_ref, pq, pk, pv, pg, pz, pxbc, pdt,
                st_ret, xpad_ref, st_ssd, sem, *, cb):
    @pl.when(pl.program_id(0) == 0)
    def _():
        cp = pltpu.make_async_copy(wt_hbm, w_ref, sem)
        cp.start()
        st_ret[...] = jnp.zeros_like(st_ret)
        xpad_ref[0:CONV_PAD, :] = jnp.zeros((CONV_PAD, SSM_CONV_DIM), F32)
        st_ssd[...] = jnp.zeros_like(st_ssd)
        cp.wait()

    def slabs(ref, col0):
        def slab(lo):
            def go():
                ref[:, lo:lo + PROJ_SLAB] = _dot_nt(hn_ref[...], w_ref[col0 + lo:col0 + lo + PROJ_SLAB, :])
            return go
        return [slab(lo) for lo in range(0, ref.shape[1], PROJ_SLAB)]

    x = x_ref[...]
    ms = jnp.mean(x * x, axis=-1, keepdims=True)
    hn_ref[...] = (x * lax.rsqrt(ms + EPS) * n1w_ref[...]).astype(BF16)
    for piece in slabs(pxbc, COL_XBC) + slabs(pz, COL_Z):
        piece()
    pdt[...] = _dot_nt(hn_ref[...], wdt_ref[...])

    pieces = slabs(pq, 0) + slabs(pk, RET_WIDTH) + slabs(pv, 2 * RET_WIDTH) + slabs(pg, 3 * RET_WIDTH)
    n_pieces = len(pieces)
    n_ticks = cb * (SSM_CONV_DIM // XBC_BLK + 1 + SSM_GROUPS + SSM_WIDTH // LANES)
    calls = [0]

    def tick():
        calls[0] += 1
        while pieces and (n_pieces - len(pieces)) * n_ticks < calls[0] * n_pieces:
            pieces.pop(0)()

    _ssd_chunks(pxbc, pz, pdt, cw_ref, cbias_ref, dtb_ref, alog_ref, dskip_ref, snw_ref, tri_ref,
                yssm_ref, xpad_ref, st_ssd, cb, tick)
    while pieces:
        pieces.pop(0)()
    _retention_chunks(cd_ref, pq, pk, pv, pg, cos_ref, sin_ref, dec_ref, wq_ref, ws_ref, rnw_ref,
                      yret_ref, st_ret, cb, lambda: None)


def _mixer(x, n1w, w_main_t, w_dt_t, cos, sin, ret_nw, ret_consts,
           conv_w, conv_b, dt_bias, a_log, d_skip_e, ssm_nw, tri):
    T = x.shape[0]
    cb = min(MIX_CB, T // CHUNK)
    rows = cb * CHUNK
    chunk_decay, decay_intra, w_query, w_state = ret_consts
    hd = RET_HEAD_DIM
    half = hd // 2
    full = lambda shape: pl.BlockSpec(shape, lambda c, cd: (0,) * len(shape))
    rowblk = lambda width: pl.BlockSpec((rows, width), lambda c, cd: (c, 0))
    scratch = lambda width: pltpu.VMEM((rows, width), F32)
    grid_spec = pltpu.PrefetchScalarGridSpec(
        num_scalar_prefetch=1,
        grid=(T // rows,),
        in_specs=[
            rowblk(D_MODEL),
            full((1, D_MODEL)),
            pl.BlockSpec(memory_space=pl.ANY),
            full((LANES, D_MODEL)),
            rowblk(half), rowblk(half),
            full((RET_HEADS, CHUNK, CHUNK)),
            full((RET_HEADS, CHUNK, hd)),
            full((RET_HEADS, CHUNK, hd)),
            full((1, RET_WIDTH)),
            full((SSM_CONV, SSM_CONV_DIM)),
            full((1, SSM_CONV_DIM)),
            full((1, LANES)),
            full((1, LANES)),
            full((1, SSM_WIDTH)),
            full((1, SSM_WIDTH)),
            full((CHUNK, CHUNK)),
        ],
        out_specs=[rowblk(RET_WIDTH), rowblk(SSM_WIDTH)],
        scratch_shapes=[
            pltpu.VMEM((MAIN_PROJ, D_MODEL), BF16),
            pltpu.VMEM((rows, D_MODEL), BF16),
            *[scratch(w) for w in PROJ_WIDTHS],
            pltpu.VMEM((RET_HEADS, hd, hd), F32),
            pltpu.VMEM((CHUNK + CONV_PAD, SSM_CONV_DIM), F32),
            pltpu.VMEM((SSM_GROUPS, SSM_STATE, SSM_WIDTH // SSM_GROUPS), F32),
            pltpu.SemaphoreType.DMA(()),
        ],
    )
    return pl.pallas_call(
        functools.partial(_mixer_body, cb=cb),
        grid_spec=grid_spec,
        out_shape=[jax.ShapeDtypeStruct((T, RET_WIDTH), BF16),
                   jax.ShapeDtypeStruct((T, SSM_WIDTH), BF16)],
        compiler_params=_cparams(("arbitrary",), MIX_VMEM),
        name="mixer",
    )(chunk_decay, x, n1w, w_main_t, w_dt_t, cos, sin, decay_intra, w_query, w_state, ret_nw,
      conv_w, conv_b, dt_bias, a_log, d_skip_e, ssm_nw, tri)


OUT_TM = 512
ROUTE_ROWS = 8 + N_EXPERTS


def _out_router_body(x_ref, yr_ref, ys_ref, wo_ref, nw_ref, wr_ref, br_ref,
                     h1_ref, h2t_ref, cw_ref, ids_ref, hstage, hsem):
    tm = x_ref.shape[0]
    i = pl.program_id(0)
    last = pl.num_programs(0) - 1
    slot = lax.rem(i, 2)

    def tile_wait(sl):
        blk = h2t_ref.at[pl.ds(0, tm)]
        pltpu.make_async_copy(blk, blk, hsem.at[sl]).wait()

    h1 = x_ref[...] + _dot(yr_ref[...], wo_ref[0:RET_WIDTH, :]) + _dot(ys_ref[...], wo_ref[RET_WIDTH:, :])
    h1_ref[...] = h1
    ms = jnp.mean(h1 * h1, axis=-1, keepdims=True)
    h2 = h1 * lax.rsqrt(ms + EPS) * nw_ref[...]

    @pl.when(i >= 2)
    def _():
        tile_wait(slot)

    for s in range(Y_ROWS):
        hstage[slot, s] = h2[:, s * LANES:(s + 1) * LANES]
    row0 = pl.multiple_of(i * tm, tm)
    for s in range(Y_ROWS):
        pltpu.make_async_copy(hstage.at[slot, s], h2t_ref.at[pl.ds(row0, tm), s, :], hsem.at[slot]).start()

    @pl.when(i == last)
    def _():
        tile_wait(slot)

        @pl.when(i >= 1)
        def _():
            tile_wait(1 - slot)

    logits = _dot_nt(wr_ref[...], h2.astype(BF16)) + br_ref[...]
    row = lax.broadcasted_iota(I32, (8, tm), 0)
    lg = jnp.where(row < N_GROUPS, logits[0:8], -jnp.inf)
    m = jnp.max(lg, axis=0, keepdims=True)
    p_sel = 1.0 / jnp.sum(jnp.exp(lg - m), axis=0, keepdims=True)
    g_sel = jnp.min(jnp.where(lg == m, row, 8), axis=0, keepdims=True)
    le = jnp.zeros((GROUP_EXPERTS, tm), F32)
    for g in range(N_GROUPS):
        le = jnp.where(g_sel == g, logits[8 + g * GROUP_EXPERTS:8 + (g + 1) * GROUP_EXPERTS], le)
    m2 = jnp.max(le, axis=0, keepdims=True)
    ee = jnp.exp(le - m2)
    pe = ee / jnp.sum(ee, axis=0, keepdims=True)
    v1 = jnp.max(pe, axis=0, keepdims=True)
    i1 = jnp.min(jnp.where(pe == v1, row, 8), axis=0, keepdims=True)
    pe2 = jnp.where(row == i1, -1.0, pe)
    v2 = jnp.max(pe2, axis=0, keepdims=True)
    i2 = jnp.min(jnp.where(pe2 == v2, row, 8), axis=0, keepdims=True)
    tw = v1 + v2
    c1 = v1 / tw * p_sel
    c2 = v2 / tw * p_sel
    e1 = g_sel * GROUP_EXPERTS + i1
    e2 = g_sel * GROUP_EXPERTS + i2
    ids = jnp.where(row == 0, e1, jnp.where(row == 1, e2, 0))
    for b in range(tm // SORT_BLK):
        ids_ref[b] = ids[:, b * SORT_BLK:(b + 1) * SORT_BLK]
    cw8 = jnp.where(row == 0, c1, jnp.where(row == 1, c2, 0.0))
    cw = jnp.concatenate([cw8, jnp.zeros((LANES - 8, tm), F32)], axis=0)
    cw_ref[...] = cw.T


def _out_router(x, y_ret, y_ssm, w_out, nw, wr_t, br):
    T = x.shape[0]
    tm = min(OUT_TM, T)
    full = lambda shape: pl.BlockSpec(shape, lambda i: (0,) * len(shape))
    return pl.pallas_call(
        _out_router_body,
        grid=(T // tm,),
        in_specs=[
            pl.BlockSpec((tm, D_MODEL), lambda i: (i, 0)),
            pl.BlockSpec((tm, RET_WIDTH), lambda i: (i, 0)),
            pl.BlockSpec((tm, SSM_WIDTH), lambda i: (i, 0)),
            pl.BlockSpec((D_MODEL, D_MODEL), lambda i: (0, 0), pipeline_mode=pl.Buffered(1)),
            full((1, D_MODEL)),
            full((LANES, D_MODEL)),
            full((LANES, 1)),
        ],
        out_specs=[
            pl.BlockSpec((tm, D_MODEL), lambda i: (i, 0)),
            pl.BlockSpec(memory_space=pl.ANY),
            pl.BlockSpec((tm, LANES), lambda i: (i, 0)),
            pl.BlockSpec((tm // SORT_BLK, 8, SORT_BLK), lambda i: (i, 0, 0)),
        ],
        out_shape=[
            jax.ShapeDtypeStruct((T, D_MODEL), F32),
            jax.ShapeDtypeStruct((T, Y_ROWS, LANES), F32),
            jax.ShapeDtypeStruct((T, LANES), F32),
            jax.ShapeDtypeStruct((T // SORT_BLK, 8, SORT_BLK), I32),
        ],
        scratch_shapes=[pltpu.VMEM((2, Y_ROWS, tm, LANES), F32), pltpu.SemaphoreType.DMA((2,))],
        compiler_params=_cparams(("arbitrary",)),
        name="out_router",
    )(x, y_ret, y_ssm, w_out, nw, wr_t, br)


SORT_BLK = 256
MOE_TM = 256


def _sort_index_body(ids_ref, tri_ref, ltri_ref, dest_ref, cnt_ref, rank_ref):
    nblk = ids_ref.shape[0]
    row_e = lax.broadcasted_iota(I32, (N_EXPERTS, SORT_BLK), 0)
    row8 = lax.broadcasted_iota(I32, (8, SORT_BLK), 0)

    def onehots(b):
        ids = ids_ref[b]
        return row_e == ids[0:1], row_e == ids[1:2]

    def rank_blk(b, carry):
        oh1, oh2 = onehots(b)
        ohf = jnp.where(oh1 | oh2, 1.0, 0.0)
        incl = _dot(ohf.astype(BF16), tri_ref[...])
        base = carry + incl - 1.0
        r1 = jnp.sum(jnp.where(oh1, base, 0.0), axis=0, keepdims=True)
        r2 = jnp.sum(jnp.where(oh2, base, 0.0), axis=0, keepdims=True)
        rank_ref[b] = jnp.where(row8 == 0, r1, jnp.where(row8 == 1, r2, 0.0))
        return carry + jnp.sum(ohf, axis=1, keepdims=True)

    cnt = lax.fori_loop(0, nblk, rank_blk, jnp.zeros((N_EXPERTS, 1), F32))
    cnt_ref[...] = jnp.broadcast_to(cnt, cnt_ref.shape)
    tiles = jnp.floor((cnt + (MOE_TM - 1.0)) / MOE_TM)
    tiles_b = jnp.broadcast_to(tiles, (N_EXPERTS, LANES)).astype(BF16)
    off = _dot(ltri_ref[...], tiles_b)[:, 0:1] * MOE_TM

    def dest_blk(b, carry):
        oh1, oh2 = onehots(b)
        o1 = jnp.sum(jnp.where(oh1, off, 0.0), axis=0, keepdims=True)
        o2 = jnp.sum(jnp.where(oh2, off, 0.0), axis=0, keepdims=True)
        d = rank_ref[b] + jnp.where(row8 == 0, o1, jnp.where(row8 == 1, o2, 0.0))
        dest_ref[b] = d.astype(I32)
        return carry

    lax.fori_loop(0, nblk, dest_blk, 0)


def _sort_index(ids, tri_u, ltri):
    nblk = ids.shape[0]
    return pl.pallas_call(
        _sort_index_body,
        out_shape=[jax.ShapeDtypeStruct((nblk, 8, SORT_BLK), I32),
                   jax.ShapeDtypeStruct((N_EXPERTS, LANES), F32)],
        scratch_shapes=[pltpu.VMEM((nblk, 8, SORT_BLK), F32)],
        compiler_params=_cparams(None),
        name="sort_index",
    )(ids, tri_u, ltri)


ROW_DMA_PRIORITY = 1
GATHER_AHEAD = 2


def _moe_body(te_ref, tf_ref, nx_ref, nu_ref, dest_ref, lo_ref, hi_ref,
              h2t_ref, wg_ref, wu_ref, wd_ref, yt_ref,
              srow_ref, wcount, xbuf, ystage, wsg, wsu, wsd, wgb_ref, wub_ref, wdb_ref, gsem, ssem, wsem):
    i = pl.program_id(0)
    nu = nu_ref[0]
    T = h2t_ref.shape[0]

    def gather_row(tile, slot, r):
        tok = srow_ref[tile * MOE_TM + r] & (T - 1)
        return pltpu.make_async_copy(h2t_ref.at[tok], xbuf.at[slot, :, r, :], gsem.at[slot])

    def scatter_row(tile, slot, r):
        return pltpu.make_async_copy(ystage.at[slot, :, r, :], yt_ref.at[srow_ref[tile * MOE_TM + r]],
                                     ssem.at[slot])

    def gather_tile(slot):
        return pltpu.make_async_copy(h2t_ref.at[pl.ds(0, MOE_TM)], h2t_ref.at[pl.ds(0, MOE_TM)], gsem.at[slot])

    def scatter_tile(slot):
        return pltpu.make_async_copy(yt_ref.at[pl.ds(0, MOE_TM)], yt_ref.at[pl.ds(0, MOE_TM)], ssem.at[slot])

    def weight_copies(e, ws):
        return (pltpu.make_async_copy(wg_ref.at[e], wsg.at[ws], wsem.at[ws, 0]),
                pltpu.make_async_copy(wu_ref.at[e], wsu.at[ws], wsem.at[ws, 1]),
                pltpu.make_async_copy(wd_ref.at[e], wsd.at[ws], wsem.at[ws, 2]))

    @pl.when(i == 0)
    def _():
        wcount[0] = 0
        for c in weight_copies(te_ref[0], 0):
            c.start()

        def tok(t, c):
            srow_ref[dest_ref[t]] = t
            srow_ref[dest_ref[T + t]] = T + t
            return c

        lax.fori_loop(0, T, tok, 0, unroll=8)

        def seg(e, c):
            def pad(r, c2):
                srow_ref[r] = 2 * T + (r & (MOE_TM - 1))
                return c2

            lax.fori_loop(lo_ref[e], hi_ref[e], pad, 0)
            return c

        lax.fori_loop(0, lo_ref.shape[0], seg, 0)

        ystage[...] = jnp.zeros_like(ystage)

        def spare(r, c):
            pltpu.make_async_copy(ystage.at[0, :, r, :], yt_ref.at[2 * T + r], ssem.at[0]).start()
            return c

        lax.fori_loop(0, MOE_TM, spare, 0)
        scatter_tile(0).wait()
        for d in range(GATHER_AHEAD):
            first = jnp.minimum(d, nu - 1)
            for r in range(MOE_TM):
                gather_row(first, d, r).start(priority=ROW_DMA_PRIORITY)

    @pl.when(tf_ref[i] == 1)
    def _():
        ws = lax.rem(wcount[0], 2)
        wcount[0] = wcount[0] + 1
        for c in weight_copies(te_ref[i], ws):
            c.wait()

        @pl.when(nx_ref[i] >= 0)
        def _():
            for c in weight_copies(nx_ref[i], 1 - ws):
                c.start()

        wgb_ref[...] = wsg[ws].astype(BF16)
        wub_ref[...] = wsu[ws].astype(BF16)
        wdb_ref[...] = wsd[ws].astype(BF16)

    slot = lax.rem(i, 2)
    other = 1 - slot
    n_xbuf = GATHER_AHEAD + 1
    xs = lax.rem(i, n_xbuf)

    @pl.when(i < nu)
    def _():
        gather_tile(xs).wait()

        @pl.when(i >= 1)
        def _():
            scatter_tile(slot).wait()

        nxt = jnp.minimum(i + GATHER_AHEAD, nu - 1)
        nxs = lax.rem(i + GATHER_AHEAD, n_xbuf)
        prv = jnp.maximum(i - 1, 0)
        for r in range(MOE_TM):
            gather_row(nxt, nxs, r).start(priority=r % 2)
            scatter_row(prv, other, r).start(priority=(r + 1) % 2)

        x = jnp.concatenate([xbuf[xs, s].astype(BF16) for s in range(Y_ROWS)], axis=1)
        a = _dot(x, wgb_ref[...])
        u = _dot(x, wub_ref[...])
        act = (_silu(a) * u).astype(BF16)
        y = _dot(act, wdb_ref[...])
        for s in range(Y_ROWS):
            ystage[slot, s] = y[:, s * LANES:(s + 1) * LANES]

    @pl.when(i == nu)
    def _():
        for d in range(GATHER_AHEAD):
            gather_tile(lax.rem(i + d, n_xbuf)).wait()
        scatter_tile(slot).wait()

        def last(r, c):
            scatter_row(nu - 1, other, r).start()
            return c

        lax.fori_loop(0, MOE_TM, last, 0)
        scatter_tile(other).wait()


def _moe(tile_expert, tile_first, next_expert, n_used, dest, pad_lo, pad_hi, h2t, w_gate, w_up, w_down):
    T = h2t.shape[0]
    assert T & (T - 1) == 0 and T >= MOE_TM, "token index is recovered from the row table by masking"
    n_tiles = tile_expert.shape[0] - 1
    hbm = pl.BlockSpec(memory_space=pl.ANY)
    grid_spec = pltpu.PrefetchScalarGridSpec(
        num_scalar_prefetch=7,
        grid=(n_tiles + 1,),
        in_specs=[hbm, hbm, hbm, hbm],
        out_specs=hbm,
        scratch_shapes=[pltpu.SMEM((n_tiles * MOE_TM,), I32),
                        pltpu.SMEM((1,), I32),
                        pltpu.VMEM((GATHER_AHEAD + 1, Y_ROWS, MOE_TM, LANES), F32),
                        pltpu.VMEM((2, Y_ROWS, MOE_TM, LANES), F32),
                        pltpu.VMEM((2, D_MODEL, D_EXPERT), F32),
                        pltpu.VMEM((2, D_MODEL, D_EXPERT), F32),
                        pltpu.VMEM((2, D_EXPERT, D_MODEL), F32),
                        pltpu.VMEM((D_MODEL, D_EXPERT), BF16),
                        pltpu.VMEM((D_MODEL, D_EXPERT), BF16),
                        pltpu.VMEM((D_EXPERT, D_MODEL), BF16),
                        pltpu.SemaphoreType.DMA((GATHER_AHEAD + 1,)),
                        pltpu.SemaphoreType.DMA((2,)),
                        pltpu.SemaphoreType.DMA((2, 3))],
    )
    return pl.pallas_call(
        _moe_body,
        grid_spec=grid_spec,
        out_shape=jax.ShapeDtypeStruct((2 * T + MOE_TM, Y_ROWS, LANES), F32),
        compiler_params=_cparams(("arbitrary",)),
        name="moe",
    )(tile_expert, tile_first, next_expert, n_used, dest, pad_lo, pad_hi, h2t, w_gate, w_up, w_down)


COMB_TM = 512


def _combine_body(h1_ref, y0_ref, y1_ref, cw_ref, nw_ref, o_ref):
    tm = h1_ref.shape[0]
    c0 = cw_ref[:, 0:1]
    c1 = cw_ref[:, 1:2]
    cols = []
    for s in range(Y_ROWS):
        moe = c0 * y0_ref[:, s, :] + c1 * y1_ref[:, s, :]
        cols.append(h1_ref[:, s * LANES:(s + 1) * LANES] + moe)
    h = jnp.concatenate(cols, axis=1)
    ms = jnp.mean(h * h, axis=-1, keepdims=True)
    o_ref[...] = h * lax.rsqrt(ms + EPS) * nw_ref[...]


def _combine(h1, y_tok, cw, nw):
    T = h1.shape[0]
    tm = min(COMB_TM, T)
    nt = T // tm
    return pl.pallas_call(
        _combine_body,
        grid=(nt,),
        in_specs=[
            pl.BlockSpec((tm, D_MODEL), lambda i: (i, 0)),
            pl.BlockSpec((tm, Y_ROWS, LANES), lambda i: (i, 0, 0)),
            pl.BlockSpec((tm, Y_ROWS, LANES), lambda i: (i + nt, 0, 0)),
            pl.BlockSpec((tm, LANES), lambda i: (i, 0)),
            pl.BlockSpec((1, D_MODEL), lambda i: (0, 0)),
        ],
        out_specs=pl.BlockSpec((tm, D_MODEL), lambda i: (i, 0)),
        out_shape=jax.ShapeDtypeStruct((T, D_MODEL), F32),
        compiler_params=_cparams(("parallel",)),
        name="combine",
    )(h1, y_tok, y_tok, cw, nw)


def _tile_plan(cnt, n_tiles):
    tiles = (cnt + (MOE_TM - 1)) // MOE_TM
    ends = jnp.cumsum(tiles)
    starts = ends - tiles
    n_used = ends[-1]
    step = jnp.arange(n_tiles + 1, dtype=I32)
    tile = jnp.minimum(step, jnp.maximum(n_used - 1, 0))
    tile_expert = jnp.sum((ends[None, :] <= tile[:, None]).astype(I32), axis=1)
    tile_first = ((step == starts[tile_expert]) & (step < n_used)).astype(I32)
    nxt_tile = ends[tile_expert]
    nxt_expert = jnp.sum((ends[None, :] <= nxt_tile[:, None]).astype(I32), axis=1)
    next_expert = jnp.where(nxt_tile < n_used, nxt_expert, -1).astype(I32)
    pad_lo = starts * MOE_TM + cnt
    pad_hi = ends * MOE_TM
    return (tile_expert, tile_first, next_expert, n_used.reshape(1).astype(I32),
            pad_lo.astype(I32), pad_hi.astype(I32))


def kernel(x, positions, norm1_w, w_in, conv_w, conv_b, dt_bias, a_log, d_skip, ret_norm_w,
           ssm_norm_w, w_out, norm2_w, w_router_group, b_router_group, w_router_expert,
           b_router_expert, w_expert_gate, w_expert_up, w_expert_down, final_norm_w):
    B, T, D = x.shape
    assert B == 1 and D == D_MODEL and T % CHUNK == 0
    xf = x.reshape(T, D)
    pad_l = lambda v: jnp.pad(v, ((0, 0), (0, LANES - v.shape[-1])))

    w_in_t = jnp.swapaxes(w_in[0], 0, 1)
    w_dt = jnp.pad(w_in_t[MAIN_PROJ:], ((0, LANES - (w_in_t.shape[0] - MAIN_PROJ)), (0, 0))).astype(BF16)
    half = RET_HEAD_DIM // 2
    inv = (ROPE_THETA ** (-jnp.arange(half, dtype=F32) / half)).reshape(1, half)
    tri = jnp.asarray(np.tril(np.ones((CHUNK, CHUNK), np.float32)), BF16)
    d_skip_e = jnp.repeat(d_skip[0], SSM_HEAD_DIM).reshape(1, SSM_WIDTH)
    wr_t = jnp.concatenate([w_router_group[0].T, jnp.zeros((8 - N_GROUPS, D), F32), w_router_expert[0].T,
                            jnp.zeros((LANES - 8 - N_EXPERTS, D), F32)], axis=0)
    br = jnp.concatenate([b_router_group[0], jnp.zeros((8 - N_GROUPS,), F32), b_router_expert[0],
                          jnp.zeros((LANES - 8 - N_EXPERTS,), F32)])
    tri_u = jnp.asarray(np.triu(np.ones((SORT_BLK, SORT_BLK), np.float32)), BF16)
    ltri = jnp.asarray(np.tril(np.ones((N_EXPERTS, N_EXPERTS), np.float32), k=-1), BF16)

    w_main, w_out_b, cos, sin = _prep(w_in_t, w_out[0], positions.reshape(T, 1).astype(F32), inv)
    y_ret, y_ssm = _mixer(xf, norm1_w[0].reshape(1, D), w_main, w_dt, cos, sin,
                          ret_norm_w[0].reshape(1, RET_WIDTH), _retention_consts(),
                          conv_w[0], conv_b[0].reshape(1, -1), pad_l(dt_bias[0].reshape(1, -1)),
                          pad_l(a_log[0].reshape(1, -1)), d_skip_e, ssm_norm_w[0].reshape(1, -1), tri)
    h1, h2t, cw, ids = _out_router(xf, y_ret, y_ssm, w_out_b, norm2_w[0].reshape(1, D),
                                   wr_t.astype(BF16), br.reshape(LANES, 1))

    dest_blk, cnt = _sort_index(ids, tri_u, ltri)
    dest = dest_blk[:, 0:2, :].transpose(1, 0, 2).reshape(2 * T)
    n_tiles = (2 * T) // MOE_TM + N_EXPERTS
    tile_expert, tile_first, next_expert, n_used, pad_lo, pad_hi = _tile_plan(cnt[:, 0].astype(I32), n_tiles)
    y_tok = _moe(tile_expert, tile_first, next_expert, n_used, dest, pad_lo, pad_hi, h2t,
                 w_expert_gate[0], w_expert_up[0], w_expert_down[0])
    out = _combine(h1, y_tok, cw, final_norm_w.reshape(1, D))
    return out.reshape(B, T, D)
```

```python
import functools

import jax
import jax.numpy as jnp
import numpy as np
from jax import lax
from jax.experimental import pallas as pl
from jax.experimental.pallas import tpu as pltpu

F32 = jnp.float32
BF16 = jnp.bfloat16
I32 = jnp.int32

D_MODEL = 2048
EPS = 1e-6
CHUNK = 128
RET_HEADS = 4
RET_HEAD_DIM = 256
RET_WIDTH = RET_HEADS * RET_HEAD_DIM
ROPE_THETA = 10000.0
SSM_WIDTH = 1024
SSM_HEAD_DIM = 64
SSM_HEADS = SSM_WIDTH // SSM_HEAD_DIM
SSM_GROUPS = 2
SSM_STATE = 128
SSM_CONV = 4
SSM_CONV_DIM = SSM_WIDTH + 2 * SSM_GROUPS * SSM_STATE
MAIN_PROJ = 4 * RET_WIDTH + SSM_WIDTH + SSM_CONV_DIM
N_GROUPS = 4
GROUP_EXPERTS = 8
N_EXPERTS = N_GROUPS * GROUP_EXPERTS
D_EXPERT = 512
LANES = 128
Y_ROWS = D_MODEL // LANES

VMEM_LIMIT = 56 * 1024 * 1024


def _cparams(sem, vmem=VMEM_LIMIT):
    return pltpu.CompilerParams(dimension_semantics=sem, vmem_limit_bytes=vmem)


def _silu(x):
    return x * (1.0 / (1.0 + jnp.exp(-x)))


def _dot(a, b):
    return jnp.dot(a, b, preferred_element_type=F32)


def _dot_nt(a, b):
    return lax.dot_general(a, b, (((1,), (1,)), ((), ())), preferred_element_type=F32)


def _split3(a):
    a1 = a.astype(BF16)
    r1 = a - a1.astype(F32)
    a2 = r1.astype(BF16)
    a3 = (r1 - a2.astype(F32)).astype(BF16)
    return a1, a2, a3


CAST_ROWS = 512
ROPE_ROWS = 1024


def _prep_body(win_ref, wout_ref, pos_ref, inv_ref, winb_ref, woutb_ref, cos_ref, sin_ref, *, n_in, n_rope):
    j = pl.program_id(0)

    @pl.when(j < n_in)
    def _():
        winb_ref[...] = win_ref[...].astype(BF16)

    @pl.when(j >= n_in)
    def _():
        woutb_ref[...] = wout_ref[...].astype(BF16)

    @pl.when(j < n_rope)
    def _():
        ang = pos_ref[...] * inv_ref[...]
        cos_ref[...] = jnp.cos(ang)
        sin_ref[...] = jnp.sin(ang)


def _prep(w_in_t, w_out, pos, inv):
    T = pos.shape[0]
    cols = w_in_t.shape[1]
    half = inv.shape[1]
    n_in = MAIN_PROJ // CAST_ROWS
    n_out = w_out.shape[0] // CAST_ROWS
    rr = min(ROPE_ROWS, T)
    n_rope = T // rr
    assert n_rope <= n_in + n_out
    in_blk = lambda j: (jnp.minimum(j, n_in - 1), 0)
    out_blk = lambda j: (jnp.clip(j - n_in, 0, n_out - 1), 0)
    rope_blk = lambda j: (jnp.minimum(j, n_rope - 1), 0)
    return pl.pallas_call(
        functools.partial(_prep_body, n_in=n_in, n_rope=n_rope),
        grid=(n_in + n_out,),
        in_specs=[pl.BlockSpec((CAST_ROWS, cols), in_blk),
                  pl.BlockSpec((CAST_ROWS, w_out.shape[1]), out_blk),
                  pl.BlockSpec((rr, 1), rope_blk),
                  pl.BlockSpec((1, half), lambda j: (0, 0))],
        out_specs=[pl.BlockSpec((CAST_ROWS, cols), in_blk),
                   pl.BlockSpec((CAST_ROWS, w_out.shape[1]), out_blk),
                   pl.BlockSpec((rr, half), rope_blk),
                   pl.BlockSpec((rr, half), rope_blk)],
        out_shape=[jax.ShapeDtypeStruct((MAIN_PROJ, cols), BF16),
                   jax.ShapeDtypeStruct(w_out.shape, BF16),
                   jax.ShapeDtypeStruct((T, half), F32),
                   jax.ShapeDtypeStruct((T, half), F32)],
        compiler_params=_cparams(("arbitrary",)),
        name="prep",
    )(w_in_t, w_out, pos, inv)


def _retention_chunks(cd_ref, q_ref, k_ref, v_ref, g_ref, cos_ref, sin_ref,
                      dec_ref, wq_ref, ws_ref, nw_ref, o_ref, st_ref, cb, tick):
    hd = RET_HEAD_DIM
    half = hd // 2
    for i in range(cb):
        sl = slice(i * CHUNK, (i + 1) * CHUNK)
        cos = cos_ref[sl, :]
        sin = sin_ref[sl, :]

        def rope(x):
            x1, x2 = x[:, :half], x[:, half:]
            return jnp.concatenate([x1 * cos - x2 * sin, x2 * cos + x1 * sin], axis=-1)

        for h in range(RET_HEADS):
            hs = slice(h * hd, (h + 1) * hd)
            q = rope(q_ref[sl, hs])
            k = rope(k_ref[sl, hs]) * (hd ** -0.5)
            v = v_ref[sl, hs]
            qb = q.astype(BF16)
            kb = k.astype(BF16)
            scores = _dot_nt(qb, kb) * dec_ref[h]
            y = _dot(scores.astype(BF16), v.astype(BF16))
            st = st_ref[h]
            y = y + _dot(qb, st.astype(BF16)) * wq_ref[h]
            vw = (v * ws_ref[h]).astype(BF16)
            new = _dot(k.T.astype(BF16), vw)
            st_ref[h] = st * cd_ref[h] + new
            mu = jnp.mean(y, axis=-1, keepdims=True)
            d = y - mu
            var = jnp.mean(d * d, axis=-1, keepdims=True)
            yn = d * lax.rsqrt(var + EPS)
            o_ref[sl, hs] = (yn * nw_ref[:, hs] * _silu(g_ref[sl, hs])).astype(BF16)
            tick()


def _retention_consts():
    H, L = RET_HEADS, CHUNK
    f32 = np.float32
    log_gamma = np.log1p(-(f32(2.0) ** (f32(-5.0) - np.arange(H, dtype=f32)))).astype(f32)
    idx = np.arange(L, dtype=f32)
    diff = idx[:, None] - idx[None, :]
    causal = diff >= 0
    decay_intra = np.where(causal[None], np.exp(np.where(causal, diff, f32(0))[None] * log_gamma[:, None, None]), f32(0))
    w_state = np.exp((f32(L - 1) - idx)[None, :] * log_gamma[:, None])
    w_query = np.exp((idx + f32(1))[None, :] * log_gamma[:, None])
    chunk_decay = np.exp(f32(L) * log_gamma)
    bc = lambda w: np.ascontiguousarray(np.broadcast_to(w[:, :, None], (H, L, RET_HEAD_DIM)), dtype=f32)
    return (jnp.asarray(chunk_decay, F32), jnp.asarray(decay_intra, F32),
            jnp.asarray(bc(w_query), F32), jnp.asarray(bc(w_state), F32))


CONV_PAD = 8
XBC_BLK = 512


def _ssd_chunks(xbc_ref, z_ref, dt_ref, cw_ref, cbias_ref, dtb_ref,
                alog_ref, dskip_ref, nw_ref, tri_ref, o_ref, xpad_ref, st_ref, cb, tick):
    L = CHUNK
    gw = SSM_WIDTH // SSM_GROUPS
    rows_i = lax.broadcasted_iota(I32, (L, L), 0)
    cols_i = lax.broadcasted_iota(I32, (L, L), 1)
    causal = rows_i >= cols_i
    lo_lane = lax.broadcasted_iota(I32, (L, LANES), 1) < SSM_HEAD_DIM
    tri = tri_ref[...]
    a_neg = -jnp.exp(alog_ref[...])

    for i in range(cb):
        sl = slice(i * L, (i + 1) * L)
        xpad_ref[CONV_PAD:CONV_PAD + L, :] = xbc_ref[sl, :]
        u_parts = []
        for b in range(SSM_CONV_DIM // XBC_BLK):
            cs = slice(b * XBC_BLK, (b + 1) * XBC_BLK)
            acc = cbias_ref[:, cs]
            for t in range(SSM_CONV):
                r0 = CONV_PAD - (SSM_CONV - 1) + t
                acc = acc + xpad_ref[r0:r0 + L, cs] * cw_ref[t:t + 1, cs]
            u_parts.append(_silu(acc))
            tick()
        xpad_ref[0:CONV_PAD, :] = xpad_ref[L:L + CONV_PAD, :]
        xs = jnp.concatenate(u_parts[:2], axis=-1)
        bm = u_parts[2][:, :SSM_GROUPS * SSM_STATE]
        cm = u_parts[2][:, SSM_GROUPS * SSM_STATE:]

        dt_in = dt_ref[sl, :] + dtb_ref[...]
        dt = jnp.maximum(dt_in, 0.0) + jnp.log1p(jnp.exp(-jnp.abs(dt_in)))
        a = dt * a_neg
        a1, a2, a3 = _split3(a)
        a_cs = _dot(tri, a1) + _dot(tri, a2) + _dot(tri, a3)
        last = a_cs[L - 1:L, :]

        def per_head_lanes(v):
            lo = lo_lane[0:v.shape[0], :]
            return jnp.concatenate([jnp.where(lo, v[:, 2 * p:2 * p + 1], v[:, 2 * p + 1:2 * p + 2])
                                    for p in range(SSM_HEADS // 2)], axis=1)

        dt_e = per_head_lanes(dt)
        to_end = per_head_lanes(jnp.exp(last - a_cs))
        from_start = per_head_lanes(jnp.exp(a_cs))
        chunk_decay = per_head_lanes(jnp.exp(last))
        tick()
        x_dt = xs * dt_e
        xw = (x_dt * to_end).astype(BF16)
        acs_t = a_cs.T
        cmb = cm.astype(BF16)
        bmb = bm.astype(BF16)

        ys = []
        for g in range(SSM_GROUPS):
            ns = slice(g * SSM_STATE, (g + 1) * SSM_STATE)
            gs = slice(g * gw, (g + 1) * gw)
            cg = cmb[:, ns]
            cbg = _dot_nt(cg, bmb[:, ns])
            st = st_ref[g]
            y_off = _dot(cg, st.astype(BF16))
            new = _dot(bm[:, ns].T.astype(BF16), xw[:, gs])
            st_ref[g] = st * chunk_decay[:, gs] + new
            tick()
            for jp in range(gw // LANES):
                h0 = (g * gw + jp * LANES) // SSM_HEAD_DIM
                ms = []
                for hh in (h0, h0 + 1):
                    seg = a_cs[:, hh:hh + 1] - acs_t[hh:hh + 1, :]
                    dec = jnp.exp(jnp.where(causal, seg, -jnp.inf))
                    ms.append((cbg * dec).astype(BF16))
                lhs = jnp.concatenate(ms, axis=1)
                ls = slice(g * gw + jp * LANES, g * gw + (jp + 1) * LANES)
                xp = x_dt[:, ls]
                rhs = jnp.concatenate([jnp.where(lo_lane, xp, 0.0),
                                       jnp.where(lo_lane, 0.0, xp)], axis=0).astype(BF16)
                y_diag = _dot(lhs, rhs)
                ys.append(y_diag + y_off[:, jp * LANES:(jp + 1) * LANES] * from_start[:, ls])
                tick()
        y = jnp.concatenate(ys, axis=1) + xs * dskip_ref[...]
        y = y * _silu(z_ref[sl, :])
        outs = []
        for g in range(SSM_GROUPS):
            yg = y[:, g * gw:(g + 1) * gw]
            ms_ = jnp.mean(yg * yg, axis=-1, keepdims=True)
            outs.append(yg * lax.rsqrt(ms_ + EPS))
        o_ref[sl, :] = (jnp.concatenate(outs, axis=1) * nw_ref[...]).astype(BF16)


MIX_CB = 2
MIX_VMEM = 60 * 1024 * 1024
COL_Z = 4 * RET_WIDTH
COL_XBC = COL_Z + SSM_WIDTH
PROJ_WIDTHS = (RET_WIDTH, RET_WIDTH, RET_WIDTH, RET_WIDTH, SSM_WIDTH, SSM_CONV_DIM, LANES)
PROJ_SLAB = 256


def _mixer_body(cd_ref, x_ref, n1w_ref, wt_hbm, wdt_ref, cos_ref, sin_ref, dec_ref, wq_ref, ws_ref, rnw_ref,
                cw_ref, cbias_ref, dtb_ref, alog_ref, dskip_ref, snw_ref, tri_ref,
                yret_ref, yssm_ref, w_ref, hn_ref, pq, pk, pv, pg, pz, pxbc, pdt,
                st_ret, xpad_ref, st_ssd, sem, *, cb):
    @pl.when(pl.program_id(0) == 0)
    def _():
        cp = pltpu.make_async_copy(wt_hbm, w_ref, sem)
        cp.start()
        st_ret[...] = jnp.zeros_like(st_ret)
        xpad_ref[0:CONV_PAD, :] = jnp.zeros((CONV_PAD, SSM_CONV_DIM), F32)
        st_ssd[...] = jnp.zeros_like(st_ssd)
        cp.wait()

    def slabs(ref, col0):
        def slab(lo):
            def go():
                ref[:, lo:lo + PROJ_SLAB] = _dot_nt(hn_ref[...], w_ref[col0 + lo:col0 + lo + PROJ_SLAB, :])
            return go
        return [slab(lo) for lo in range(0, ref.shape[1], PROJ_SLAB)]

    x = x_ref[...]
    ms = jnp.mean(x * x, axis=-1, keepdims=True)
    hn_ref[...] = (x * lax.rsqrt(ms + EPS) * n1w_ref[...]).astype(BF16)
    for piece in slabs(pxbc, COL_XBC) + slabs(pz, COL_Z):
        piece()
    pdt[...] = _dot_nt(hn_ref[...], wdt_ref[...])

    pieces = slabs(pq, 0) + slabs(pk, RET_WIDTH) + slabs(pv, 2 * RET_WIDTH) + slabs(pg, 3 * RET_WIDTH)
    n_pieces = len(pieces)
    n_ticks = cb * (SSM_CONV_DIM // XBC_BLK + 1 + SSM_GROUPS + SSM_WIDTH // LANES)
    calls = [0]

    def tick():
        calls[0] += 1
        while pieces and (n_pieces - len(pieces)) * n_ticks < calls[0] * n_pieces:
            pieces.pop(0)()

    _ssd_chunks(pxbc, pz, pdt, cw_ref, cbias_ref, dtb_ref, alog_ref, dskip_ref, snw_ref, tri_ref,
                yssm_ref, xpad_ref, st_ssd, cb, tick)
    while pieces:
        pieces.pop(0)()
    _retention_chunks(cd_ref, pq, pk, pv, pg, cos_ref, sin_ref, dec_ref, wq_ref, ws_ref, rnw_ref,
                      yret_ref, st_ret, cb, lambda: None)


def _mixer(x, n1w, w_main_t, w_dt_t, cos, sin, ret_nw, ret_consts,
           conv_w, conv_b, dt_bias, a_log, d_skip_e, ssm_nw, tri):
    T = x.shape[0]
    cb = min(MIX_CB, T // CHUNK)
    rows = cb * CHUNK
    chunk_decay, decay_intra, w_query, w_state = ret_consts
    hd = RET_HEAD_DIM
    half = hd // 2
    full = lambda shape: pl.BlockSpec(shape, lambda c, cd: (0,) * len(shape))
    rowblk = lambda width: pl.BlockSpec((rows, width), lambda c, cd: (c, 0))
    scratch = lambda width: pltpu.VMEM((rows, width), F32)
    grid_spec = pltpu.PrefetchScalarGridSpec(
        num_scalar_prefetch=1,
        grid=(T // rows,),
        in_specs=[
            rowblk(D_MODEL),
            full((1, D_MODEL)),
            pl.BlockSpec(memory_space=pl.ANY),
            full((LANES, D_MODEL)),
            rowblk(half), rowblk(half),
            full((RET_HEADS, CHUNK, CHUNK)),
            full((RET_HEADS, CHUNK, hd)),
            full((RET_HEADS, CHUNK, hd)),
            full((1, RET_WIDTH)),
            full((SSM_CONV, SSM_CONV_DIM)),
            full((1, SSM_CONV_DIM)),
            full((1, LANES)),
            full((1, LANES)),
            full((1, SSM_WIDTH)),
            full((1, SSM_WIDTH)),
            full((CHUNK, CHUNK)),
        ],
        out_specs=[rowblk(RET_WIDTH), rowblk(SSM_WIDTH)],
        scratch_shapes=[
            pltpu.VMEM((MAIN_PROJ, D_MODEL), BF16),
            pltpu.VMEM((rows, D_MODEL), BF16),
            *[scratch(w) for w in PROJ_WIDTHS],
            pltpu.VMEM((RET_HEADS, hd, hd), F32),
            pltpu.VMEM((CHUNK + CONV_PAD, SSM_CONV_DIM), F32),
            pltpu.VMEM((SSM_GROUPS, SSM_STATE, SSM_WIDTH // SSM_GROUPS), F32),
            pltpu.SemaphoreType.DMA(()),
        ],
    )
    return pl.pallas_call(
        functools.partial(_mixer_body, cb=cb),
        grid_spec=grid_spec,
        out_shape=[jax.ShapeDtypeStruct((T, RET_WIDTH), BF16),
                   jax.ShapeDtypeStruct((T, SSM_WIDTH), BF16)],
        compiler_params=_cparams(("arbitrary",), MIX_VMEM),
        name="mixer",
    )(chunk_decay, x, n1w, w_main_t, w_dt_t, cos, sin, decay_intra, w_query, w_state, ret_nw,
      conv_w, conv_b, dt_bias, a_log, d_skip_e, ssm_nw, tri)


OUT_TM = 512


def _out_router_body(x_ref, yr_ref, ys_ref, wo_ref, nw_ref, wr_ref, br_ref,
                     h1_ref, h2t_ref, cw_ref, ids_ref, hstage, hsem):
    tm = x_ref.shape[0]
    i = pl.program_id(0)
    last = pl.num_programs(0) - 1
    slot = lax.rem(i, 2)

    def tile_wait(sl):
        blk = h2t_ref.at[pl.ds(0, tm)]
        pltpu.make_async_copy(blk, blk, hsem.at[sl]).wait()

    h1 = x_ref[...] + _dot(yr_ref[...], wo_ref[0:RET_WIDTH, :]) + _dot(ys_ref[...], wo_ref[RET_WIDTH:, :])
    h1_ref[...] = h1
    ms = jnp.mean(h1 * h1, axis=-1, keepdims=True)
    h2 = h1 * lax.rsqrt(ms + EPS) * nw_ref[...]

    @pl.when(i >= 2)
    def _():
        tile_wait(slot)

    for s in range(Y_ROWS):
        hstage[slot, s] = h2[:, s * LANES:(s + 1) * LANES]
    row0 = pl.multiple_of(i * tm, tm)
    for s in range(Y_ROWS):
        pltpu.make_async_copy(hstage.at[slot, s], h2t_ref.at[pl.ds(row0, tm), s, :], hsem.at[slot]).start()

    @pl.when(i == last)
    def _():
        tile_wait(slot)

        @pl.when(i >= 1)
        def _():
            tile_wait(1 - slot)

    logits = _dot_nt(wr_ref[...], h2.astype(BF16)) + br_ref[...]
    row = lax.broadcasted_iota(I32, (8, tm), 0)
    lg = jnp.where(row < N_GROUPS, logits[0:8], -jnp.inf)
    m = jnp.max(lg, axis=0, keepdims=True)
    p_sel = 1.0 / jnp.sum(jnp.exp(lg - m), axis=0, keepdims=True)
    g_sel = jnp.min(jnp.where(lg == m, row, 8), axis=0, keepdims=True)
    le = jnp.zeros((GROUP_EXPERTS, tm), F32)
    for g in range(N_GROUPS):
        le = jnp.where(g_sel == g, logits[8 + g * GROUP_EXPERTS:8 + (g + 1) * GROUP_EXPERTS], le)
    m2 = jnp.max(le, axis=0, keepdims=True)
    ee = jnp.exp(le - m2)
    pe = ee / jnp.sum(ee, axis=0, keepdims=True)
    v1 = jnp.max(pe, axis=0, keepdims=True)
    i1 = jnp.min(jnp.where(pe == v1, row, 8), axis=0, keepdims=True)
    pe2 = jnp.where(row == i1, -1.0, pe)
    v2 = jnp.max(pe2, axis=0, keepdims=True)
    i2 = jnp.min(jnp.where(pe2 == v2, row, 8), axis=0, keepdims=True)
    tw = v1 + v2
    c1 = v1 / tw * p_sel
    c2 = v2 / tw * p_sel
    e1 = g_sel * GROUP_EXPERTS + i1
    e2 = g_sel * GROUP_EXPERTS + i2
    ids = jnp.where(row == 0, e1, jnp.where(row == 1, e2, 0))
    for b in range(tm // SORT_BLK):
        ids_ref[b] = ids[:, b * SORT_BLK:(b + 1) * SORT_BLK]
    cw8 = jnp.where(row == 0, c1, jnp.where(row == 1, c2, 0.0))
    cw = jnp.concatenate([cw8, jnp.zeros((LANES - 8, tm), F32)], axis=0)
    cw_ref[...] = cw.T


def _out_router(x, y_ret, y_ssm, w_out, nw, wr_t, br):
    T = x.shape[0]
    tm = min(OUT_TM, T)
    full = lambda shape: pl.BlockSpec(shape, lambda i: (0,) * len(shape))
    return pl.pallas_call(
        _out_router_body,
        grid=(T // tm,),
        in_specs=[
            pl.BlockSpec((tm, D_MODEL), lambda i: (i, 0)),
            pl.BlockSpec((tm, RET_WIDTH), lambda i: (i, 0)),
            pl.BlockSpec((tm, SSM_WIDTH), lambda i: (i, 0)),
            pl.BlockSpec((D_MODEL, D_MODEL), lambda i: (0, 0), pipeline_mode=pl.Buffered(1)),
            full((1, D_MODEL)),
            full((LANES, D_MODEL)),
            full((LANES, 1)),
        ],
        out_specs=[
            pl.BlockSpec((tm, D_MODEL), lambda i: (i, 0)),
            pl.BlockSpec(memory_space=pl.ANY),
            pl.BlockSpec((tm, LANES), lambda i: (i, 0)),
            pl.BlockSpec((tm // SORT_BLK, 8, SORT_BLK), lambda i: (i, 0, 0)),
        ],
        out_shape=[
            jax.ShapeDtypeStruct((T, D_MODEL), F32),
            jax.ShapeDtypeStruct((T, Y_ROWS, LANES), F32),
            jax.ShapeDtypeStruct((T, LANES), F32),
            jax.ShapeDtypeStruct((T // SORT_BLK, 8, SORT_BLK), I32),
        ],
        scratch_shapes=[pltpu.VMEM((2, Y_ROWS, tm, LANES), F32), pltpu.SemaphoreType.DMA((2,))],
        compiler_params=_cparams(("arbitrary",)),
        name="out_router",
    )(x, y_ret, y_ssm, w_out, nw, wr_t, br)


SORT_BLK = 256
MOE_TM = 256


def _sort_index_body(ids_ref, tri_ref, ltri_ref, dest_ref, cnt_ref, rank_ref):
    nblk = ids_ref.shape[0]
    row_e = lax.broadcasted_iota(I32, (N_EXPERTS, SORT_BLK), 0)
    row8 = lax.broadcasted_iota(I32, (8, SORT_BLK), 0)

    def onehots(b):
        ids = ids_ref[b]
        return row_e == ids[0:1], row_e == ids[1:2]

    def rank_blk(b, carry):
        oh1, oh2 = onehots(b)
        ohf = jnp.where(oh1 | oh2, 1.0, 0.0)
        incl = _dot(ohf.astype(BF16), tri_ref[...])
        base = carry + incl - 1.0
        r1 = jnp.sum(jnp.where(oh1, base, 0.0), axis=0, keepdims=True)
        r2 = jnp.sum(jnp.where(oh2, base, 0.0), axis=0, keepdims=True)
        rank_ref[b] = jnp.where(row8 == 0, r1, jnp.where(row8 == 1, r2, 0.0))
        return carry + jnp.sum(ohf, axis=1, keepdims=True)

    cnt = lax.fori_loop(0, nblk, rank_blk, jnp.zeros((N_EXPERTS, 1), F32))
    cnt_ref[...] = jnp.broadcast_to(cnt, cnt_ref.shape)
    tiles = jnp.floor((cnt + (MOE_TM - 1.0)) / MOE_TM)
    tiles_b = jnp.broadcast_to(tiles, (N_EXPERTS, LANES)).astype(BF16)
    off = _dot(ltri_ref[...], tiles_b)[:, 0:1] * MOE_TM

    def dest_blk(b, carry):
        oh1, oh2 = onehots(b)
        o1 = jnp.sum(jnp.where(oh1, off, 0.0), axis=0, keepdims=True)
        o2 = jnp.sum(jnp.where(oh2, off, 0.0), axis=0, keepdims=True)
        d = rank_ref[b] + jnp.where(row8 == 0, o1, jnp.where(row8 == 1, o2, 0.0))
        dest_ref[b] = d.astype(I32)
        return carry

    lax.fori_loop(0, nblk, dest_blk, 0)


def _sort_index(ids, tri_u, ltri):
    nblk = ids.shape[0]
    return pl.pallas_call(
        _sort_index_body,
        out_shape=[jax.ShapeDtypeStruct((nblk, 8, SORT_BLK), I32),
                   jax.ShapeDtypeStruct((N_EXPERTS, LANES), F32)],
        scratch_shapes=[pltpu.VMEM((nblk, 8, SORT_BLK), F32)],
        compiler_params=_cparams(None),
        name="sort_index",
    )(ids, tri_u, ltri)


ROW_DMA_PRIORITY = 1
GATHER_AHEAD = 2


def _moe_body(te_ref, tf_ref, nx_ref, nu_ref, dest_ref, lo_ref, hi_ref,
              h2t_ref, wg_ref, wu_ref, wd_ref, yt_ref,
              srow_ref, wcount, xbuf, ystage, wsg, wsu, wsd, wgb_ref, wub_ref, wdb_ref, gsem, ssem, wsem):
    i = pl.program_id(0)
    nu = nu_ref[0]
    T = h2t_ref.shape[0]

    def gather_row(tile, slot, r):
        tok = srow_ref[tile * MOE_TM + r] & (T - 1)
        return pltpu.make_async_copy(h2t_ref.at[tok], xbuf.at[slot, :, r, :], gsem.at[slot])

    def scatter_row(tile, slot, r):
        return pltpu.make_async_copy(ystage.at[slot, :, r, :], yt_ref.at[srow_ref[tile * MOE_TM + r]],
                                     ssem.at[slot])

    def gather_tile(slot):
        return pltpu.make_async_copy(h2t_ref.at[pl.ds(0, MOE_TM)], h2t_ref.at[pl.ds(0, MOE_TM)], gsem.at[slot])

    def scatter_tile(slot):
        return pltpu.make_async_copy(yt_ref.at[pl.ds(0, MOE_TM)], yt_ref.at[pl.ds(0, MOE_TM)], ssem.at[slot])

    def weight_copies(e, ws):
        return (pltpu.make_async_copy(wg_ref.at[e], wsg.at[ws], wsem.at[ws, 0]),
                pltpu.make_async_copy(wu_ref.at[e], wsu.at[ws], wsem.at[ws, 1]),
                pltpu.make_async_copy(wd_ref.at[e], wsd.at[ws], wsem.at[ws, 2]))

    @pl.when(i == 0)
    def _():
        wcount[0] = 0
        for c in weight_copies(te_ref[0], 0):
            c.start()

        def tok(t, c):
            srow_ref[dest_ref[t]] = t
            srow_ref[dest_ref[T + t]] = T + t
            return c

        lax.fori_loop(0, T, tok, 0, unroll=8)

        def seg(e, c):
            def pad(r, c2):
                srow_ref[r] = 2 * T + (r & (MOE_TM - 1))
                return c2

            lax.fori_loop(lo_ref[e], hi_ref[e], pad, 0)
            return c

        lax.fori_loop(0, lo_ref.shape[0], seg, 0)

        ystage[...] = jnp.zeros_like(ystage)

        def spare(r, c):
            pltpu.make_async_copy(ystage.at[0, :, r, :], yt_ref.at[2 * T + r], ssem.at[0]).start()
            return c

        lax.fori_loop(0, MOE_TM, spare, 0)
        scatter_tile(0).wait()
        for d in range(GATHER_AHEAD):
            first = jnp.minimum(d, nu - 1)
            for r in range(MOE_TM):
                gather_row(first, d, r).start(priority=ROW_DMA_PRIORITY)

    @pl.when(tf_ref[i] == 1)
    def _():
        ws = lax.rem(wcount[0], 2)
        wcount[0] = wcount[0] + 1
        for c in weight_copies(te_ref[i], ws):
            c.wait()

        @pl.when(nx_ref[i] >= 0)
        def _():
            for c in weight_copies(nx_ref[i], 1 - ws):
                c.start()

        wgb_ref[...] = wsg[ws].astype(BF16)
        wub_ref[...] = wsu[ws].astype(BF16)
        wdb_ref[...] = wsd[ws].astype(BF16)

    slot = lax.rem(i, 2)
    other = 1 - slot
    n_xbuf = GATHER_AHEAD + 1
    xs = lax.rem(i, n_xbuf)

    @pl.when(i < nu)
    def _():
        gather_tile(xs).wait()

        @pl.when(i >= 1)
        def _():
            scatter_tile(slot).wait()

        nxt = jnp.minimum(i + GATHER_AHEAD, nu - 1)
        nxs = lax.rem(i + GATHER_AHEAD, n_xbuf)
        prv = jnp.maximum(i - 1, 0)
        for r in range(MOE_TM):
            gather_row(nxt, nxs, r).start(priority=r % 2)
            scatter_row(prv, other, r).start(priority=(r + 1) % 2)

        x = jnp.concatenate([xbuf[xs, s].astype(BF16) for s in range(Y_ROWS)], axis=1)
        a = _dot(x, wgb_ref[...])
        u = _dot(x, wub_ref[...])
        act = (_silu(a) * u).astype(BF16)
        y = _dot(act, wdb_ref[...])
        for s in range(Y_ROWS):
            ystage[slot, s] = y[:, s * LANES:(s + 1) * LANES]

    @pl.when(i == nu)
    def _():
        for d in range(GATHER_AHEAD):
            gather_tile(lax.rem(i + d, n_xbuf)).wait()
        scatter_tile(slot).wait()

        def last(r, c):
            scatter_row(nu - 1, other, r).start()
            return c

        lax.fori_loop(0, MOE_TM, last, 0)
        scatter_tile(other).wait()


def _moe(tile_expert, tile_first, next_expert, n_used, dest, pad_lo, pad_hi, h2t, w_gate, w_up, w_down):
    T = h2t.shape[0]
    assert T & (T - 1) == 0 and T >= MOE_TM, "token index is recovered from the row table by masking"
    n_tiles = tile_expert.shape[0] - 1
    hbm = pl.BlockSpec(memory_space=pl.ANY)
    grid_spec = pltpu.PrefetchScalarGridSpec(
        num_scalar_prefetch=7,
        grid=(n_tiles + 1,),
        in_specs=[hbm, hbm, hbm, hbm],
        out_specs=hbm,
        scratch_shapes=[pltpu.SMEM((n_tiles * MOE_TM,), I32),
                        pltpu.SMEM((1,), I32),
                        pltpu.VMEM((GATHER_AHEAD + 1, Y_ROWS, MOE_TM, LANES), F32),
                        pltpu.VMEM((2, Y_ROWS, MOE_TM, LANES), F32),
                        pltpu.VMEM((2, D_MODEL, D_EXPERT), F32),
                        pltpu.VMEM((2, D_MODEL, D_EXPERT), F32),
                        pltpu.VMEM((2, D_EXPERT, D_MODEL), F32),
                        pltpu.VMEM((D_MODEL, D_EXPERT), BF16),
                        pltpu.VMEM((D_MODEL, D_EXPERT), BF16),
                        pltpu.VMEM((D_EXPERT, D_MODEL), BF16),
                        pltpu.SemaphoreType.DMA((GATHER_AHEAD + 1,)),
                        pltpu.SemaphoreType.DMA((2,)),
                        pltpu.SemaphoreType.DMA((2, 3))],
    )
    return pl.pallas_call(
        _moe_body,
        grid_spec=grid_spec,
        out_shape=jax.ShapeDtypeStruct((2 * T + MOE_TM, Y_ROWS, LANES), F32),
        compiler_params=_cparams(("arbitrary",)),
        name="moe",
    )(tile_expert, tile_first, next_expert, n_used, dest, pad_lo, pad_hi, h2t, w_gate, w_up, w_down)


COMB_TM = 512


def _combine_body(h1_ref, y0_ref, y1_ref, cw_ref, nw_ref, o_ref):
    tm = h1_ref.shape[0]
    c0 = cw_ref[:, 0:1]
    c1 = cw_ref[:, 1:2]
    cols = []
    for s in range(Y_ROWS):
        moe = c0 * y0_ref[:, s, :] + c1 * y1_ref[:, s, :]
        cols.append(h1_ref[:, s * LANES:(s + 1) * LANES] + moe)
    h = jnp.concatenate(cols, axis=1)
    ms = jnp.mean(h * h, axis=-1, keepdims=True)
    o_ref[...] = h * lax.rsqrt(ms + EPS) * nw_ref[...]


def _combine(h1, y_tok, cw, nw):
    T = h1.shape[0]
    tm = min(COMB_TM, T)
    nt = T // tm
    return pl.pallas_call(
        _combine_body,
        grid=(nt,),
        in_specs=[
            pl.BlockSpec((tm, D_MODEL), lambda i: (i, 0)),
            pl.BlockSpec((tm, Y_ROWS, LANES), lambda i: (i, 0, 0)),
            pl.BlockSpec((tm, Y_ROWS, LANES), lambda i: (i + nt, 0, 0)),
            pl.BlockSpec((tm, LANES), lambda i: (i, 0)),
            pl.BlockSpec((1, D_MODEL), lambda i: (0, 0)),
        ],
        out_specs=pl.BlockSpec((tm, D_MODEL), lambda i: (i, 0)),
        out_shape=jax.ShapeDtypeStruct((T, D_MODEL), F32),
        compiler_params=_cparams(("parallel",)),
        name="combine",
    )(h1, y_tok, y_tok, cw, nw)


def _tile_plan(cnt, n_tiles):
    tiles = (cnt + (MOE_TM - 1)) // MOE_TM
    ends = jnp.cumsum(tiles)
    starts = ends - tiles
    n_used = ends[-1]
    step = jnp.arange(n_tiles + 1, dtype=I32)
    tile = jnp.minimum(step, jnp.maximum(n_used - 1, 0))
    tile_expert = jnp.sum((ends[None, :] <= tile[:, None]).astype(I32), axis=1)
    tile_first = ((step == starts[tile_expert]) & (step < n_used)).astype(I32)
    nxt_tile = ends[tile_expert]
    nxt_expert = jnp.sum((ends[None, :] <= nxt_tile[:, None]).astype(I32), axis=1)
    next_expert = jnp.where(nxt_tile < n_used, nxt_expert, -1).astype(I32)
    pad_lo = starts * MOE_TM + cnt
    pad_hi = ends * MOE_TM
    return (tile_expert, tile_first, next_expert, n_used.reshape(1).astype(I32),
            pad_lo.astype(I32), pad_hi.astype(I32))


def kernel(x, positions, norm1_w, w_in, conv_w, conv_b, dt_bias, a_log, d_skip, ret_norm_w,
           ssm_norm_w, w_out, norm2_w, w_router_group, b_router_group, w_router_expert,
           b_router_expert, w_expert_gate, w_expert_up, w_expert_down, final_norm_w):
    B, T, D = x.shape
    assert B == 1 and D == D_MODEL and T % CHUNK == 0
    xf = x.reshape(T, D)
    pad_l = lambda v: jnp.pad(v, ((0, 0), (0, LANES - v.shape[-1])))

    w_in_t = jnp.swapaxes(w_in[0], 0, 1)
    w_dt = jnp.pad(w_in_t[MAIN_PROJ:], ((0, LANES - (w_in_t.shape[0] - MAIN_PROJ)), (0, 0))).astype(BF16)
    half = RET_HEAD_DIM // 2
    inv = (ROPE_THETA ** (-jnp.arange(half, dtype=F32) / half)).reshape(1, half)
    tri = jnp.asarray(np.tril(np.ones((CHUNK, CHUNK), np.float32)), BF16)
    d_skip_e = jnp.repeat(d_skip[0], SSM_HEAD_DIM).reshape(1, SSM_WIDTH)
    wr_t = jnp.concatenate([w_router_group[0].T, jnp.zeros((8 - N_GROUPS, D), F32), w_router_expert[0].T,
                            jnp.zeros((LANES - 8 - N_EXPERTS, D), F32)], axis=0)
    br = jnp.concatenate([b_router_group[0], jnp.zeros((8 - N_GROUPS,), F32), b_router_expert[0],
                          jnp.zeros((LANES - 8 - N_EXPERTS,), F32)])
    tri_u = jnp.asarray(np.triu(np.ones((SORT_BLK, SORT_BLK), np.float32)), BF16)
    ltri = jnp.asarray(np.tril(np.ones((N_EXPERTS, N_EXPERTS), np.float32), k=-1), BF16)

    w_main, w_out_b, cos, sin = _prep(w_in_t, w_out[0], positions.reshape(T, 1).astype(F32), inv)
    y_ret, y_ssm = _mixer(xf, norm1_w[0].reshape(1, D), w_main, w_dt, cos, sin,
                          ret_norm_w[0].reshape(1, RET_WIDTH), _retention_consts(),
                          conv_w[0], conv_b[0].reshape(1, -1), pad_l(dt_bias[0].reshape(1, -1)),
                          pad_l(a_log[0].reshape(1, -1)), d_skip_e, ssm_norm_w[0].reshape(1, -1), tri)
    h1, h2t, cw, ids = _out_router(xf, y_ret, y_ssm, w_out_b, norm2_w[0].reshape(1, D),
                                   wr_t.astype(BF16), br.reshape(LANES, 1))

    dest_blk, cnt = _sort_index(ids, tri_u, ltri)
    dest = dest_blk[:, 0:2, :].transpose(1, 0, 2).reshape(2 * T)
    n_tiles = (2 * T) // MOE_TM + N_EXPERTS
    tile_expert, tile_first, next_expert, n_used, pad_lo, pad_hi = _tile_plan(cnt[:, 0].astype(I32), n_tiles)
    y_tok = _moe(tile_expert, tile_first, next_expert, n_used, dest, pad_lo, pad_hi, h2t,
                 w_expert_gate[0], w_expert_up[0], w_expert_down[0])
    out = _combine(h1, y_tok, cw, final_norm_w.reshape(1, D))
    return out.reshape(B, T, D)
```

```python
import functools

import jax
import jax.numpy as jnp
import numpy as np
from jax import lax
from jax.experimental import pallas as pl
from jax.experimental.pallas import tpu as pltpu

F32 = jnp.float32
BF16 = jnp.bfloat16
I32 = jnp.int32

D_MODEL = 2048
EPS = 1e-6
CHUNK = 128
RET_HEADS = 4
RET_HEAD_DIM = 256
RET_WIDTH = RET_HEADS * RET_HEAD_DIM
ROPE_THETA = 10000.0
SSM_WIDTH = 1024
SSM_HEAD_DIM = 64
SSM_HEADS = SSM_WIDTH // SSM_HEAD_DIM
SSM_GROUPS = 2
SSM_STATE = 128
SSM_CONV = 4
SSM_CONV_DIM = SSM_WIDTH + 2 * SSM_GROUPS * SSM_STATE
MAIN_PROJ = 4 * RET_WIDTH + SSM_WIDTH + SSM_CONV_DIM
N_GROUPS = 4
GROUP_EXPERTS = 8
N_EXPERTS = N_GROUPS * GROUP_EXPERTS
D_EXPERT = 512
LANES = 128
Y_ROWS = D_MODEL // LANES

VMEM_LIMIT = 56 * 1024 * 1024


def _cparams(sem, vmem=VMEM_LIMIT):
    return pltpu.CompilerParams(dimension_semantics=sem, vmem_limit_bytes=vmem)


def _silu(x):
    return x * (1.0 / (1.0 + jnp.exp(-x)))


def _dot(a, b):
    return jnp.dot(a, b, preferred_element_type=F32)


def _dot_nt(a, b):
    return lax.dot_general(a, b, (((1,), (1,)), ((), ())), preferred_element_type=F32)


def _split3(a):
    a1 = a.astype(BF16)
    r1 = a - a1.astype(F32)
    a2 = r1.astype(BF16)
    a3 = (r1 - a2.astype(F32)).astype(BF16)
    return a1, a2, a3


CAST_ROWS = 512
ROPE_ROWS = 1024


def _prep_body(win_ref, wout_ref, pos_ref, inv_ref, winb_ref, woutb_ref, cos_ref, sin_ref, *, n_in, n_rope):
    j = pl.program_id(0)

    @pl.when(j < n_in)
    def _():
        winb_ref[...] = win_ref[...].astype(BF16)

    @pl.when(j >= n_in)
    def _():
        woutb_ref[...] = wout_ref[...].astype(BF16)

    @pl.when(j < n_rope)
    def _():
        ang = pos_ref[...] * inv_ref[...]
        cos_ref[...] = jnp.cos(ang)
        sin_ref[...] = jnp.sin(ang)


def _prep(w_in_t, w_out, pos, inv):
    T = pos.shape[0]
    cols = w_in_t.shape[1]
    half = inv.shape[1]
    n_in = MAIN_PROJ // CAST_ROWS
    n_out = w_out.shape[0] // CAST_ROWS
    rr = min(ROPE_ROWS, T)
    n_rope = T // rr
    assert n_rope <= n_in + n_out
    in_blk = lambda j: (jnp.minimum(j, n_in - 1), 0)
    out_blk = lambda j: (jnp.clip(j - n_in, 0, n_out - 1), 0)
    rope_blk = lambda j: (jnp.minimum(j, n_rope - 1), 0)
    return pl.pallas_call(
        functools.partial(_prep_body, n_in=n_in, n_rope=n_rope),
        grid=(n_in + n_out,),
        in_specs=[pl.BlockSpec((CAST_ROWS, cols), in_blk),
                  pl.BlockSpec((CAST_ROWS, w_out.shape[1]), out_blk),
                  pl.BlockSpec((rr, 1), rope_blk),
                  pl.BlockSpec((1, half), lambda j: (0, 0))],
        out_specs=[pl.BlockSpec((CAST_ROWS, cols), in_blk),
                   pl.BlockSpec((CAST_ROWS, w_out.shape[1]), out_blk),
                   pl.BlockSpec((rr, half), rope_blk),
                   pl.BlockSpec((rr, half), rope_blk)],
        out_shape=[jax.ShapeDtypeStruct((MAIN_PROJ, cols), BF16),
                   jax.ShapeDtypeStruct(w_out.shape, BF16),
                   jax.ShapeDtypeStruct((T, half), F32),
                   jax.ShapeDtypeStruct((T, half), F32)],
        compiler_params=_cparams(("arbitrary",)),
        name="prep",
    )(w_in_t, w_out, pos, inv)


def _retention_chunks(cd_ref, q_ref, k_ref, v_ref, g_ref, cos_ref, sin_ref,
                      dec_ref, wq_ref, ws_ref, nw_ref, o_ref, st_ref, cb, tick):
    hd = RET_HEAD_DIM
    half = hd // 2
    for i in range(cb):
        sl = slice(i * CHUNK, (i + 1) * CHUNK)
        cos = cos_ref[sl, :]
        sin = sin_ref[sl, :]

        def rope(x):
            x1, x2 = x[:, :half], x[:, half:]
            return jnp.concatenate([x1 * cos - x2 * sin, x2 * cos + x1 * sin], axis=-1)

        for h in range(RET_HEADS):
            hs = slice(h * hd, (h + 1) * hd)
            q = rope(q_ref[sl, hs])
            k = rope(k_ref[sl, hs]) * (hd ** -0.5)
            v = v_ref[sl, hs]
            qb = q.astype(BF16)
            kb = k.astype(BF16)
            scores = _dot_nt(qb, kb) * dec_ref[h]
            y = _dot(scores.astype(BF16), v.astype(BF16))
            st = st_ref[h]
            y = y + _dot(qb, st.astype(BF16)) * wq_ref[h]
            vw = (v * ws_ref[h]).astype(BF16)
            new = _dot(k.T.astype(BF16), vw)
            st_ref[h] = st * cd_ref[h] + new
            mu = jnp.mean(y, axis=-1, keepdims=True)
            d = y - mu
            var = jnp.mean(d * d, axis=-1, keepdims=True)
            yn = d * lax.rsqrt(var + EPS)
            o_ref[sl, hs] = (yn * nw_ref[:, hs] * _silu(g_ref[sl, hs])).astype(BF16)
            tick()


def _retention_consts():
    H, L = RET_HEADS, CHUNK
    f32 = np.float32
    log_gamma = np.log1p(-(f32(2.0) ** (f32(-5.0) - np.arange(H, dtype=f32)))).astype(f32)
    idx = np.arange(L, dtype=f32)
    diff = idx[:, None] - idx[None, :]
    causal = diff >= 0
    decay_intra = np.where(causal[None], np.exp(np.where(causal, diff, f32(0))[None] * log_gamma[:, None, None]), f32(0))
    w_state = np.exp((f32(L - 1) - idx)[None, :] * log_gamma[:, None])
    w_query = np.exp((idx + f32(1))[None, :] * log_gamma[:, None])
    chunk_decay = np.exp(f32(L) * log_gamma)
    bc = lambda w: np.ascontiguousarray(np.broadcast_to(w[:, :, None], (H, L, RET_HEAD_DIM)), dtype=f32)
    return (jnp.asarray(chunk_decay, F32), jnp.asarray(decay_intra, F32),
            jnp.asarray(bc(w_query), F32), jnp.asarray(bc(w_state), F32))


CONV_PAD = 8
XBC_BLK = 512


def _ssd_chunks(xbc_ref, z_ref, dt_ref, cw_ref, cbias_ref, dtb_ref,
                alog_ref, dskip_ref, nw_ref, tri_ref, o_ref, xpad_ref, st_ref, cb, tick):
    L = CHUNK
    gw = SSM_WIDTH // SSM_GROUPS
    rows_i = lax.broadcasted_iota(I32, (L, L), 0)
    cols_i = lax.broadcasted_iota(I32, (L, L), 1)
    causal = rows_i >= cols_i
    lo_lane = lax.broadcasted_iota(I32, (L, LANES), 1) < SSM_HEAD_DIM
    tri = tri_ref[...]
    a_neg = -jnp.exp(alog_ref[...])

    for i in range(cb):
        sl = slice(i * L, (i + 1) * L)
        xpad_ref[CONV_PAD:CONV_PAD + L, :] = xbc_ref[sl, :]
        u_parts = []
        for b in range(SSM_CONV_DIM // XBC_BLK):
            cs = slice(b * XBC_BLK, (b + 1) * XBC_BLK)
            acc = cbias_ref[:, cs]
            for t in range(SSM_CONV):
                r0 = CONV_PAD - (SSM_CONV - 1) + t
                acc = acc + xpad_ref[r0:r0 + L, cs] * cw_ref[t:t + 1, cs]
            u_parts.append(_silu(acc))
            tick()
        xpad_ref[0:CONV_PAD, :] = xpad_ref[L:L + CONV_PAD, :]
        xs = jnp.concatenate(u_parts[:2], axis=-1)
        bm = u_parts[2][:, :SSM_GROUPS * SSM_STATE]
        cm = u_parts[2][:, SSM_GROUPS * SSM_STATE:]

        dt_in = dt_ref[sl, :] + dtb_ref[...]
        dt = jnp.maximum(dt_in, 0.0) + jnp.log1p(jnp.exp(-jnp.abs(dt_in)))
        a = dt * a_neg
        a1, a2, a3 = _split3(a)
        a_cs = _dot(tri, a1) + _dot(tri, a2) + _dot(tri, a3)
        last = a_cs[L - 1:L, :]

        def per_head_lanes(v):
            lo = lo_lane[0:v.shape[0], :]
            return jnp.concatenate([jnp.where(lo, v[:, 2 * p:2 * p + 1], v[:, 2 * p + 1:2 * p + 2])
                                    for p in range(SSM_HEADS // 2)], axis=1)

        dt_e = per_head_lanes(dt)
        to_end = per_head_lanes(jnp.exp(last - a_cs))
        from_start = per_head_lanes(jnp.exp(a_cs))
        chunk_decay = per_head_lanes(jnp.exp(last))
        tick()
        x_dt = xs * dt_e
        xw = (x_dt * to_end).astype(BF16)
        acs_t = a_cs.T
        cmb = cm.astype(BF16)
        bmb = bm.astype(BF16)

        ys = []
        for g in range(SSM_GROUPS):
            ns = slice(g * SSM_STATE, (g + 1) * SSM_STATE)
            gs = slice(g * gw, (g + 1) * gw)
            cg = cmb[:, ns]
            cbg = _dot_nt(cg, bmb[:, ns])
            st = st_ref[g]
            y_off = _dot(cg, st.astype(BF16))
            new = _dot(bm[:, ns].T.astype(BF16), xw[:, gs])
            st_ref[g] = st * chunk_decay[:, gs] + new
            tick()
            for jp in range(gw // LANES):
                h0 = (g * gw + jp * LANES) // SSM_HEAD_DIM
                ms = []
                for hh in (h0, h0 + 1):
                    seg = a_cs[:, hh:hh + 1] - acs_t[hh:hh + 1, :]
                    dec = jnp.exp(jnp.where(causal, seg, -jnp.inf))
                    ms.append((cbg * dec).astype(BF16))
                lhs = jnp.concatenate(ms, axis=1)
                ls = slice(g * gw + jp * LANES, g * gw + (jp + 1) * LANES)
                xp = x_dt[:, ls]
                rhs = jnp.concatenate([jnp.where(lo_lane, xp, 0.0),
                                       jnp.where(lo_lane, 0.0, xp)], axis=0).astype(BF16)
                y_diag = _dot(lhs, rhs)
                ys.append(y_diag + y_off[:, jp * LANES:(jp + 1) * LANES] * from_start[:, ls])
                tick()
        y = jnp.concatenate(ys, axis=1) + xs * dskip_ref[...]
        y = y * _silu(z_ref[sl, :])
        outs = []
        for g in range(SSM_GROUPS):
            yg = y[:, g * gw:(g + 1) * gw]
            ms_ = jnp.mean(yg * yg, axis=-1, keepdims=True)
            outs.append(yg * lax.rsqrt(ms_ + EPS))
        o_ref[sl, :] = (jnp.concatenate(outs, axis=1) * nw_ref[...]).astype(BF16)


MIX_CB = 2
MIX_VMEM = 60 * 1024 * 1024
COL_Z = 4 * RET_WIDTH
COL_XBC = COL_Z + SSM_WIDTH
PROJ_WIDTHS = (RET_WIDTH, RET_WIDTH, RET_WIDTH, RET_WIDTH, SSM_WIDTH, SSM_CONV_DIM, LANES)
PROJ_SLAB = 256


def _mixer_body(cd_ref, x_ref, n1w_ref, wt_hbm, wdt_ref, cos_ref, sin_ref, dec_ref, wq_ref, ws_ref, rnw_ref,
                cw_ref, cbias_ref, dtb_ref, alog_ref, dskip_ref, snw_ref, tri_ref,
                yret_ref, yssm_ref, w_ref, hn_ref, pq, pk, pv, pg, pz, pxbc, pdt,
                st_ret, xpad_ref, st_ssd, sem, *, cb):
    @pl.when(pl.program_id(0) == 0)
    def _():
        cp = pltpu.make_async_copy(wt_hbm, w_ref, sem)
        cp.start()
        st_ret[...] = jnp.zeros_like(st_ret)
        xpad_ref[0:CONV_PAD, :] = jnp.zeros((CONV_PAD, SSM_CONV_DIM), F32)
        st_ssd[...] = jnp.zeros_like(st_ssd)
        cp.wait()

    def slabs(ref, col0):
        def slab(lo):
            def go():
                ref[:, lo:lo + PROJ_SLAB] = _dot_nt(hn_ref[...], w_ref[col0 + lo:col0 + lo + PROJ_SLAB, :])
            return go
        return [slab(lo) for lo in range(0, ref.shape[1], PROJ_SLAB)]

    x = x_ref[...]
    ms = jnp.mean(x * x, axis=-1, keepdims=True)
    hn_ref[...] = (x * lax.rsqrt(ms + EPS) * n1w_ref[...]).astype(BF16)
    for piece in slabs(pxbc, COL_XBC) + slabs(pz, COL_Z):
        piece()
    pdt[...] = _dot_nt(hn_ref[...], wdt_ref[...])

    pieces = slabs(pq, 0) + slabs(pk, RET_WIDTH) + slabs(pv, 2 * RET_WIDTH) + slabs(pg, 3 * RET_WIDTH)
    n_pieces = len(pieces)
    n_ticks = cb * (SSM_CONV_DIM // XBC_BLK + 1 + SSM_GROUPS + SSM_WIDTH // LANES)
    calls = [0]

    def tick():
        calls[0] += 1
        while pieces and (n_pieces - len(pieces)) * n_ticks < calls[0] * n_pieces:
            pieces.pop(0)()

    _ssd_chunks(pxbc, pz, pdt, cw_ref, cbias_ref, dtb_ref, alog_ref, dskip_ref, snw_ref, tri_ref,
                yssm_ref, xpad_ref, st_ssd, cb, tick)
    while pieces:
        pieces.pop(0)()
    _retention_chunks(cd_ref, pq, pk, pv, pg, cos_ref, sin_ref, dec_ref, wq_ref, ws_ref, rnw_ref,
                      yret_ref, st_ret, cb, lambda: None)


def _mixer(x, n1w, w_main_t, w_dt_t, cos, sin, ret_nw, ret_consts,
           conv_w, conv_b, dt_bias, a_log, d_skip_e, ssm_nw, tri):
    T = x.shape[0]
    cb = min(MIX_CB, T // CHUNK)
    rows = cb * CHUNK
    chunk_decay, decay_intra, w_query, w_state = ret_consts
    hd = RET_HEAD_DIM
    half = hd // 2
    full = lambda shape: pl.BlockSpec(shape, lambda c, cd: (0,) * len(shape))
    rowblk = lambda width: pl.BlockSpec((rows, width), lambda c, cd: (c, 0))
    scratch = lambda width: pltpu.VMEM((rows, width), F32)
    grid_spec = pltpu.PrefetchScalarGridSpec(
        num_scalar_prefetch=1,
        grid=(T // rows,),
        in_specs=[
            rowblk(D_MODEL),
            full((1, D_MODEL)),
            pl.BlockSpec(memory_space=pl.ANY),
            full((LANES, D_MODEL)),
            rowblk(half), rowblk(half),
            full((RET_HEADS, CHUNK, CHUNK)),
            full((RET_HEADS, CHUNK, hd)),
            full((RET_HEADS, CHUNK, hd)),
            full((1, RET_WIDTH)),
            full((SSM_CONV, SSM_CONV_DIM)),
            full((1, SSM_CONV_DIM)),
            full((1, LANES)),
            full((1, LANES)),
            full((1, SSM_WIDTH)),
            full((1, SSM_WIDTH)),
            full((CHUNK, CHUNK)),
        ],
        out_specs=[rowblk(RET_WIDTH), rowblk(SSM_WIDTH)],
        scratch_shapes=[
            pltpu.VMEM((MAIN_PROJ, D_MODEL), BF16),
            pltpu.VMEM((rows, D_MODEL), BF16),
            *[scratch(w) for w in PROJ_WIDTHS],
            pltpu.VMEM((RET_HEADS, hd, hd), F32),
            pltpu.VMEM((CHUNK + CONV_PAD, SSM_CONV_DIM), F32),
            pltpu.VMEM((SSM_GROUPS, SSM_STATE, SSM_WIDTH // SSM_GROUPS), F32),
            pltpu.SemaphoreType.DMA(()),
        ],
    )
    return pl.pallas_call(
        functools.partial(_mixer_body, cb=cb),
        grid_spec=grid_spec,
        out_shape=[jax.ShapeDtypeStruct((T, RET_WIDTH), BF16),
                   jax.ShapeDtypeStruct((T, SSM_WIDTH), BF16)],
        compiler_params=_cparams(("arbitrary",), MIX_VMEM),
        name="mixer",
    )(chunk_decay, x, n1w, w_main_t, w_dt_t, cos, sin, decay_intra, w_query, w_state, ret_nw,
      conv_w, conv_b, dt_bias, a_log, d_skip_e, ssm_nw, tri)


OUT_TM = 512


def _out_router_body(x_ref, yr_ref, ys_ref, wo_ref, nw_ref, wr_ref, br_ref,
                     h1_ref, h2t_ref, cw_ref, ids_ref, hstage, hsem):
    tm = x_ref.shape[0]
    i = pl.program_id(0)
    last = pl.num_programs(0) - 1
    slot = lax.rem(i, 2)

    def tile_wait(sl):
        blk = h2t_ref.at[pl.ds(0, tm)]
        pltpu.make_async_copy(blk, blk, hsem.at[sl]).wait()

    h1 = x_ref[...] + _dot(yr_ref[...], wo_ref[0:RET_WIDTH, :]) + _dot(ys_ref[...], wo_ref[RET_WIDTH:, :])
    h1_ref[...] = h1
    ms = jnp.mean(h1 * h1, axis=-1, keepdims=True)
    h2 = h1 * lax.rsqrt(ms + EPS) * nw_ref[...]

    @pl.when(i >= 2)
    def _():
        tile_wait(slot)

    for s in range(Y_ROWS):
        hstage[slot, s] = h2[:, s * LANES:(s + 1) * LANES]
    row0 = pl.multiple_of(i * tm, tm)
    for s in range(Y_ROWS):
        pltpu.make_async_copy(hstage.at[slot, s], h2t_ref.at[pl.ds(row0, tm), s, :], hsem.at[slot]).start()

    @pl.when(i == last)
    def _():
        tile_wait(slot)

        @pl.when(i >= 1)
        def _():
            tile_wait(1 - slot)

    logits = _dot_nt(wr_ref[...], h2.astype(BF16)) + br_ref[...]
    row = lax.broadcasted_iota(I32, (8, tm), 0)
    lg = jnp.where(row < N_GROUPS, logits[0:8], -jnp.inf)
    m = jnp.max(lg, axis=0, keepdims=True)
    p_sel = 1.0 / jnp.sum(jnp.exp(lg - m), axis=0, keepdims=True)
    g_sel = jnp.min(jnp.where(lg == m, row, 8), axis=0, keepdims=True)
    le = jnp.zeros((GROUP_EXPERTS, tm), F32)
    for g in range(N_GROUPS):
        le = jnp.where(g_sel == g, logits[8 + g * GROUP_EXPERTS:8 + (g + 1) * GROUP_EXPERTS], le)
    m2 = jnp.max(le, axis=0, keepdims=True)
    ee = jnp.exp(le - m2)
    pe = ee / jnp.sum(ee, axis=0, keepdims=True)
    v1 = jnp.max(pe, axis=0, keepdims=True)
    i1 = jnp.min(jnp.where(pe == v1, row, 8), axis=0, keepdims=True)
    pe2 = jnp.where(row == i1, -1.0, pe)
    v2 = jnp.max(pe2, axis=0, keepdims=True)
    i2 = jnp.min(jnp.where(pe2 == v2, row, 8), axis=0, keepdims=True)
    tw = v1 + v2
    c1 = v1 / tw * p_sel
    c2 = v2 / tw * p_sel
    e1 = g_sel * GROUP_EXPERTS + i1
    e2 = g_sel * GROUP_EXPERTS + i2
    ids = jnp.where(row == 0, e1, jnp.where(row == 1, e2, 0))
    for b in range(tm // SORT_BLK):
        ids_ref[b] = ids[:, b * SORT_BLK:(b + 1) * SORT_BLK]
    cw8 = jnp.where(row == 0, c1, jnp.where(row == 1, c2, 0.0))
    cw = jnp.concatenate([cw8, jnp.zeros((LANES - 8, tm), F32)], axis=0)
    cw_ref[...] = cw.T


def _out_router(x, y_ret, y_ssm, w_out, nw, wr_t, br):
    T = x.shape[0]
    tm = min(OUT_TM, T)
    full = lambda shape: pl.BlockSpec(shape, lambda i: (0,) * len(shape))
    return pl.pallas_call(
        _out_router_body,
        grid=(T // tm,),
        in_specs=[
            pl.BlockSpec((tm, D_MODEL), lambda i: (i, 0)),
            pl.BlockSpec((tm, RET_WIDTH), lambda i: (i, 0)),
            pl.BlockSpec((tm, SSM_WIDTH), lambda i: (i, 0)),
            pl.BlockSpec((D_MODEL, D_MODEL), lambda i: (0, 0), pipeline_mode=pl.Buffered(1)),
            full((1, D_MODEL)),
            full((LANES, D_MODEL)),
            full((LANES, 1)),
        ],
        out_specs=[
            pl.BlockSpec((tm, D_MODEL), lambda i: (i, 0)),
            pl.BlockSpec(memory_space=pl.ANY),
            pl.BlockSpec((tm, LANES), lambda i: (i, 0)),
            pl.BlockSpec((tm // SORT_BLK, 8, SORT_BLK), lambda i: (i, 0, 0)),
        ],
        out_shape=[
            jax.ShapeDtypeStruct((T, D_MODEL), F32),
            jax.ShapeDtypeStruct((T, Y_ROWS, LANES), F32),
            jax.ShapeDtypeStruct((T, LANES), F32),
            jax.ShapeDtypeStruct((T // SORT_BLK, 8, SORT_BLK), I32),
        ],
        scratch_shapes=[pltpu.VMEM((2, Y_ROWS, tm, LANES), F32), pltpu.SemaphoreType.DMA((2,))],
        compiler_params=_cparams(("arbitrary",)),
        name="out_router",
    )(x, y_ret, y_ssm, w_out, nw, wr_t, br)


SORT_BLK = 256
MOE_TM = 256


def _sort_index_body(ids_ref, tri_ref, ltri_ref, dest_ref, cnt_ref, rank_ref):
    nblk = ids_ref.shape[0]
    row_e = lax.broadcasted_iota(I32, (N_EXPERTS, SORT_BLK), 0)
    row8 = lax.broadcasted_iota(I32, (8, SORT_BLK), 0)

    def onehots(b):
        ids = ids_ref[b]
        return row_e == ids[0:1], row_e == ids[1:2]

    def rank_blk(b, carry):
        oh1, oh2 = onehots(b)
        ohf = jnp.where(oh1 | oh2, 1.0, 0.0)
        incl = _dot(ohf.astype(BF16), tri_ref[...])
        base = carry + incl - 1.0
        r1 = jnp.sum(jnp.where(oh1, base, 0.0), axis=0, keepdims=True)
        r2 = jnp.sum(jnp.where(oh2, base, 0.0), axis=0, keepdims=True)
        rank_ref[b] = jnp.where(row8 == 0, r1, jnp.where(row8 == 1, r2, 0.0))
        return carry + jnp.sum(ohf, axis=1, keepdims=True)

    cnt = lax.fori_loop(0, nblk, rank_blk, jnp.zeros((N_EXPERTS, 1), F32))
    cnt_ref[...] = jnp.broadcast_to(cnt, cnt_ref.shape)
    tiles = jnp.floor((cnt + (MOE_TM - 1.0)) / MOE_TM)
    tiles_b = jnp.broadcast_to(tiles, (N_EXPERTS, LANES)).astype(BF16)
    off = _dot(ltri_ref[...], tiles_b)[:, 0:1] * MOE_TM

    def dest_blk(b, carry):
        oh1, oh2 = onehots(b)
        o1 = jnp.sum(jnp.where(oh1, off, 0.0), axis=0, keepdims=True)
        o2 = jnp.sum(jnp.where(oh2, off, 0.0), axis=0, keepdims=True)
        d = rank_ref[b] + jnp.where(row8 == 0, o1, jnp.where(row8 == 1, o2, 0.0))
        dest_ref[b] = d.astype(I32)
        return carry

    lax.fori_loop(0, nblk, dest_blk, 0)


def _sort_index(ids, tri_u, ltri):
    nblk = ids.shape[0]
    return pl.pallas_call(
        _sort_index_body,
        out_shape=[jax.ShapeDtypeStruct((nblk, 8, SORT_BLK), I32),
                   jax.ShapeDtypeStruct((N_EXPERTS, LANES), F32)],
        scratch_shapes=[pltpu.VMEM((nblk, 8, SORT_BLK), F32)],
        compiler_params=_cparams(None),
        name="sort_index",
    )(ids, tri_u, ltri)


ROW_DMA_PRIORITY = 1
GATHER_AHEAD = 2


def _moe_body(te_ref, tf_ref, nx_ref, nu_ref, dest_ref, lo_ref, hi_ref,
              h2t_ref, wg_ref, wu_ref, wd_ref, yt_ref,
              srow_ref, wcount, xbuf, ystage, wsg, wsu, wsd, wgb_ref, wub_ref, wdb_ref, gsem, ssem, wsem):
    i = pl.program_id(0)
    nu = nu_ref[0]
    T = h2t_ref.shape[0]

    def gather_row(tile, slot, r):
        tok = srow_ref[tile * MOE_TM + r] & (T - 1)
        return pltpu.make_async_copy(h2t_ref.at[tok], xbuf.at[slot, :, r, :], gsem.at[slot])

    def scatter_row(tile, slot, r):
        return pltpu.make_async_copy(ystage.at[slot, :, r, :], yt_ref.at[srow_ref[tile * MOE_TM + r]],
                                     ssem.at[slot])

    def gather_tile(slot):
        return pltpu.make_async_copy(h2t_ref.at[pl.ds(0, MOE_TM)], h2t_ref.at[pl.ds(0, MOE_TM)], gsem.at[slot])

    def scatter_tile(slot):
        return pltpu.make_async_copy(yt_ref.at[pl.ds(0, MOE_TM)], yt_ref.at[pl.ds(0, MOE_TM)], ssem.at[slot])

    def weight_copies(e, ws):
        return (pltpu.make_async_copy(wg_ref.at[e], wsg.at[ws], wsem.at[ws, 0]),
                pltpu.make_async_copy(wu_ref.at[e], wsu.at[ws], wsem.at[ws, 1]),
                pltpu.make_async_copy(wd_ref.at[e], wsd.at[ws], wsem.at[ws, 2]))

    @pl.when(i == 0)
    def _():
        wcount[0] = 0
        for c in weight_copies(te_ref[0], 0):
            c.start()

        def seg(e, c):
            base = jnp.maximum(hi_ref[e] - MOE_TM, 0)

            def pad(j, c2):
                srow_ref[base + j] = 2 * T + j
                return c2

            lax.fori_loop(0, MOE_TM, pad, 0, unroll=8)
            return c

        lax.fori_loop(0, hi_ref.shape[0], seg, 0)

        def tok(t, c):
            srow_ref[dest_ref[t]] = t
            srow_ref[dest_ref[T + t]] = T + t
            return c

        lax.fori_loop(0, T, tok, 0, unroll=8)

        ystage[...] = jnp.zeros_like(ystage)

        def spare(r, c):
            pltpu.make_async_copy(ystage.at[0, :, r, :], yt_ref.at[2 * T + r], ssem.at[0]).start()
            return c

        lax.fori_loop(0, MOE_TM, spare, 0)
        scatter_tile(0).wait()
        for d in range(GATHER_AHEAD):
            first = jnp.minimum(d, nu - 1)
            for r in range(MOE_TM):
                gather_row(first, d, r).start(priority=ROW_DMA_PRIORITY)

    @pl.when(tf_ref[i] == 1)
    def _():
        ws = lax.rem(wcount[0], 2)
        wcount[0] = wcount[0] + 1
        for c in weight_copies(te_ref[i], ws):
            c.wait()

        @pl.when(nx_ref[i] >= 0)
        def _():
            for c in weight_copies(nx_ref[i], 1 - ws):
                c.start()

        wgb_ref[...] = wsg[ws].astype(BF16)
        wub_ref[...] = wsu[ws].astype(BF16)
        wdb_ref[...] = wsd[ws].astype(BF16)

    slot = lax.rem(i, 2)
    other = 1 - slot
    n_xbuf = GATHER_AHEAD + 1
    xs = lax.rem(i, n_xbuf)

    @pl.when(i < nu)
    def _():
        gather_tile(xs).wait()

        @pl.when(i >= 1)
        def _():
            scatter_tile(slot).wait()

        nxt = jnp.minimum(i + GATHER_AHEAD, nu - 1)
        nxs = lax.rem(i + GATHER_AHEAD, n_xbuf)
        prv = jnp.maximum(i - 1, 0)
        for r in range(MOE_TM):
            gather_row(nxt, nxs, r).start(priority=r % 2)
            scatter_row(prv, other, r).start(priority=(r + 1) % 2)

        x = jnp.concatenate([xbuf[xs, s].astype(BF16) for s in range(Y_ROWS)], axis=1)
        a = _dot(x, wgb_ref[...])
        u = _dot(x, wub_ref[...])
        act = (_silu(a) * u).astype(BF16)
        y = _dot(act, wdb_ref[...])
        for s in range(Y_ROWS):
            ystage[slot, s] = y[:, s * LANES:(s + 1) * LANES]

    @pl.when(i == nu)
    def _():
        for d in range(GATHER_AHEAD):
            gather_tile(lax.rem(i + d, n_xbuf)).wait()
        scatter_tile(slot).wait()

        def last(r, c):
            scatter_row(nu - 1, other, r).start()
            return c

        lax.fori_loop(0, MOE_TM, last, 0)
        scatter_tile(other).wait()


def _moe(tile_expert, tile_first, next_expert, n_used, dest, pad_lo, pad_hi, h2t, w_gate, w_up, w_down):
    T = h2t.shape[0]
    assert T & (T - 1) == 0 and T >= MOE_TM, "token index is recovered from the row table by masking"
    n_tiles = tile_expert.shape[0] - 1
    hbm = pl.BlockSpec(memory_space=pl.ANY)
    grid_spec = pltpu.PrefetchScalarGridSpec(
        num_scalar_prefetch=7,
        grid=(n_tiles + 1,),
        in_specs=[hbm, hbm, hbm, hbm],
        out_specs=hbm,
        scratch_shapes=[pltpu.SMEM((n_tiles * MOE_TM,), I32),
                        pltpu.SMEM((1,), I32),
                        pltpu.VMEM((GATHER_AHEAD + 1, Y_ROWS, MOE_TM, LANES), F32),
                        pltpu.VMEM((2, Y_ROWS, MOE_TM, LANES), F32),
                        pltpu.VMEM((2, D_MODEL, D_EXPERT), F32),
                        pltpu.VMEM((2, D_MODEL, D_EXPERT), F32),
                        pltpu.VMEM((2, D_EXPERT, D_MODEL), F32),
                        pltpu.VMEM((D_MODEL, D_EXPERT), BF16),
                        pltpu.VMEM((D_MODEL, D_EXPERT), BF16),
                        pltpu.VMEM((D_EXPERT, D_MODEL), BF16),
                        pltpu.SemaphoreType.DMA((GATHER_AHEAD + 1,)),
                        pltpu.SemaphoreType.DMA((2,)),
                        pltpu.SemaphoreType.DMA((2, 3))],
    )
    return pl.pallas_call(
        _moe_body,
        grid_spec=grid_spec,
        out_shape=jax.ShapeDtypeStruct((2 * T + MOE_TM, Y_ROWS, LANES), F32),
        compiler_params=_cparams(("arbitrary",)),
        name="moe",
    )(tile_expert, tile_first, next_expert, n_used, dest, pad_lo, pad_hi, h2t, w_gate, w_up, w_down)


COMB_TM = 512


def _combine_body(h1_ref, y0_ref, y1_ref, cw_ref, nw_ref, o_ref):
    tm = h1_ref.shape[0]
    c0 = cw_ref[:, 0:1]
    c1 = cw_ref[:, 1:2]
    cols = []
    for s in range(Y_ROWS):
        moe = c0 * y0_ref[:, s, :] + c1 * y1_ref[:, s, :]
        cols.append(h1_ref[:, s * LANES:(s + 1) * LANES] + moe)
    h = jnp.concatenate(cols, axis=1)
    ms = jnp.mean(h * h, axis=-1, keepdims=True)
    o_ref[...] = h * lax.rsqrt(ms + EPS) * nw_ref[...]


def _combine(h1, y_tok, cw, nw):
    T = h1.shape[0]
    tm = min(COMB_TM, T)
    nt = T // tm
    return pl.pallas_call(
        _combine_body,
        grid=(nt,),
        in_specs=[
            pl.BlockSpec((tm, D_MODEL), lambda i: (i, 0)),
            pl.BlockSpec((tm, Y_ROWS, LANES), lambda i: (i, 0, 0)),
            pl.BlockSpec((tm, Y_ROWS, LANES), lambda i: (i + nt, 0, 0)),
            pl.BlockSpec((tm, LANES), lambda i: (i, 0)),
            pl.BlockSpec((1, D_MODEL), lambda i: (0, 0)),
        ],
        out_specs=pl.BlockSpec((tm, D_MODEL), lambda i: (i, 0)),
        out_shape=jax.ShapeDtypeStruct((T, D_MODEL), F32),
        compiler_params=_cparams(("parallel",)),
        name="combine",
    )(h1, y_tok, y_tok, cw, nw)


def _tile_plan(cnt, n_tiles):
    tiles = (cnt + (MOE_TM - 1)) // MOE_TM
    ends = jnp.cumsum(tiles)
    starts = ends - tiles
    n_used = ends[-1]
    step = jnp.arange(n_tiles + 1, dtype=I32)
    tile = jnp.minimum(step, jnp.maximum(n_used - 1, 0))
    tile_expert = jnp.sum((ends[None, :] <= tile[:, None]).astype(I32), axis=1)
    tile_first = ((step == starts[tile_expert]) & (step < n_used)).astype(I32)
    nxt_tile = ends[tile_expert]
    nxt_expert = jnp.sum((ends[None, :] <= nxt_tile[:, None]).astype(I32), axis=1)
    next_expert = jnp.where(nxt_tile < n_used, nxt_expert, -1).astype(I32)
    pad_lo = starts * MOE_TM + cnt
    pad_hi = ends * MOE_TM
    return (tile_expert, tile_first, next_expert, n_used.reshape(1).astype(I32),
            pad_lo.astype(I32), pad_hi.astype(I32))


def kernel(x, positions, norm1_w, w_in, conv_w, conv_b, dt_bias, a_log, d_skip, ret_norm_w,
           ssm_norm_w, w_out, norm2_w, w_router_group, b_router_group, w_router_expert,
           b_router_expert, w_expert_gate, w_expert_up, w_expert_down, final_norm_w):
    B, T, D = x.shape
    assert B == 1 and D == D_MODEL and T % CHUNK == 0
    xf = x.reshape(T, D)
    pad_l = lambda v: jnp.pad(v, ((0, 0), (0, LANES - v.shape[-1])))

    w_in_t = jnp.swapaxes(w_in[0], 0, 1)
    w_dt = jnp.pad(w_in_t[MAIN_PROJ:], ((0, LANES - (w_in_t.shape[0] - MAIN_PROJ)), (0, 0))).astype(BF16)
    half = RET_HEAD_DIM // 2
    inv = (ROPE_THETA ** (-jnp.arange(half, dtype=F32) / half)).reshape(1, half)
    tri = jnp.asarray(np.tril(np.ones((CHUNK, CHUNK), np.float32)), BF16)
    d_skip_e = jnp.repeat(d_skip[0], SSM_HEAD_DIM).reshape(1, SSM_WIDTH)
    wr_t = jnp.concatenate([w_router_group[0].T, jnp.zeros((8 - N_GROUPS, D), F32), w_router_expert[0].T,
                            jnp.zeros((LANES - 8 - N_EXPERTS, D), F32)], axis=0)
    br = jnp.concatenate([b_router_group[0], jnp.zeros((8 - N_GROUPS,), F32), b_router_expert[0],
                          jnp.zeros((LANES - 8 - N_EXPERTS,), F32)])
    tri_u = jnp.asarray(np.triu(np.ones((SORT_BLK, SORT_BLK), np.float32)), BF16)
    ltri = jnp.asarray(np.tril(np.ones((N_EXPERTS, N_EXPERTS), np.float32), k=-1), BF16)

    w_main, w_out_b, cos, sin = _prep(w_in_t, w_out[0], positions.reshape(T, 1).astype(F32), inv)
    y_ret, y_ssm = _mixer(xf, norm1_w[0].reshape(1, D), w_main, w_dt, cos, sin,
                          ret_norm_w[0].reshape(1, RET_WIDTH), _retention_consts(),
                          conv_w[0], conv_b[0].reshape(1, -1), pad_l(dt_bias[0].reshape(1, -1)),
                          pad_l(a_log[0].reshape(1, -1)), d_skip_e, ssm_norm_w[0].reshape(1, -1), tri)
    h1, h2t, cw, ids = _out_router(xf, y_ret, y_ssm, w_out_b, norm2_w[0].reshape(1, D),
                                   wr_t.astype(BF16), br.reshape(LANES, 1))

    dest_blk, cnt = _sort_index(ids, tri_u, ltri)
    dest = dest_blk[:, 0:2, :].transpose(1, 0, 2).reshape(2 * T)
    n_tiles = (2 * T) // MOE_TM + N_EXPERTS
    tile_expert, tile_first, next_expert, n_used, pad_lo, pad_hi = _tile_plan(cnt[:, 0].astype(I32), n_tiles)
    y_tok = _moe(tile_expert, tile_first, next_expert, n_used, dest, pad_lo, pad_hi, h2t,
                 w_expert_gate[0], w_expert_up[0], w_expert_down[0])
    out = _combine(h1, y_tok, cw, final_norm_w.reshape(1, D))
    return out.reshape(B, T, D)
```
